```python
import math
import jax, jax.numpy as jnp
from jax import lax
import numpy as np

D_MODEL = 1024
BATCH = 8
SEQ = 2048
DEPTH = 1
DEC_BATCH = 128
DEC_SEQ = 8
PAST_LEN = 16384
PAGE_SIZE = 128

GLA_HEADS = 4
GLA_DK = D_MODEL // 8
GLA_DV = D_MODEL // 4
GLA_KEY = GLA_HEADS * GLA_DK
GLA_VAL = GLA_HEADS * GLA_DV
GATE_RANK = 16
GATE_NORMALIZER = 16.0
GLA_CHUNK = 64
RMS_EPS = 1e-6
POOL_GROUPS = 4
POOL_WIDTH = D_MODEL // 2
POOL_GC = POOL_WIDTH // POOL_GROUPS
POOL_WINDOWS = (2, 4, 8, 16)
POOL_WMAX = 16
POOL_BUF = POOL_WMAX - 1
N_EXPERTS = 32
TOP_K = 4
D_FF = D_MODEL
SWIGLU_LIMIT = 7.0
GLU_ALPHA = 1.702
MOE_BLOCK = 128
LN_EPS = 1e-5
DEEPNORM_ALPHA = (2.0 * DEPTH) ** 0.25
DEEPNORM_BETA = (8.0 * DEPTH) ** -0.25
IN_SIZES = (GLA_KEY, GLA_KEY, GLA_VAL, GATE_RANK, GLA_VAL, POOL_WIDTH, D_MODEL, D_MODEL)
N_IN = sum(IN_SIZES)

kernel_name = 'gla_pool_gated_moe_deepnorm_step'


def _split_in(h):
    parts = []
    off = 0
    for n in IN_SIZES:
        parts.append(h[..., off:off + n])
        off += n
    return parts


def _layer_norm(x, g, b):
    xf = x.astype(jnp.float32)
    mu = jnp.mean(xf, -1, keepdims=True)
    var = jnp.mean(jnp.square(xf - mu), -1, keepdims=True)
    y = (xf - mu) * lax.rsqrt(var + LN_EPS) * g.astype(jnp.float32) + b.astype(jnp.float32)
    return y.astype(x.dtype)


def _gla_chunked(q, k, v, glog, s0):
    B, L = q.shape[:2]
    C = math.gcd(L, GLA_CHUNK)
    n = L // C

    def to_chunks(t):
        return jnp.moveaxis(t.reshape(B, n, C, *t.shape[2:]), 1, 0)

    mask = jnp.tril(jnp.ones((C, C), dtype=bool))[None, None]

    def step(s, inp):
        qc, kc, vc, gc = inp
        b = jnp.cumsum(gc, axis=1)
        b_ref = b[:, C // 2:C // 2 + 1]
        a = jnp.einsum('bihd,bjhd->bhij', qc * jnp.exp(b - b_ref), kc * jnp.exp(b_ref - b))
        a = jnp.where(mask, a, 0.0)
        o = (jnp.einsum('bhij,bjhv->bihv', a, vc)
             + jnp.einsum('bihd,bhdv->bihv', qc * jnp.exp(b), s))
        b_last = b[:, -1]
        s = (jnp.exp(b_last)[..., None] * s
             + jnp.einsum('bjhd,bjhv->bhdv', kc * jnp.exp(b_last[:, None] - b), vc))
        return s, o

    s, o = lax.scan(step, s0, (to_chunks(q), to_chunks(k), to_chunks(v), to_chunks(glog)))
    o = jnp.moveaxis(o, 0, 1).reshape(B, L, GLA_HEADS, GLA_DV)
    return o, s


def _pool_mix(u, buf, w_grp, scale):
    L = u.shape[1]
    ext = jnp.concatenate([buf.astype(jnp.float32), u.astype(jnp.float32)], axis=1)
    N = ext.shape[1]
    W = POOL_WMAX
    cs = jnp.cumsum(jnp.pad(ext, ((0, 0), (W, 0), (0, 0))), axis=1)
    cs = jnp.pad(cs, ((0, 0), (1, 0), (0, 0)))
    e = jnp.arange(N - L, N)
    outs = []
    for gi, w in enumerate(POOL_WINDOWS):
        sl = slice(gi * POOL_GC, (gi + 1) * POOL_GC)
        win_sum = cs[:, W + 1 + N - L:W + 1 + N, sl] - cs[:, W + 1 - w + N - L:W + 1 - w + N, sl]
        cnt = jnp.minimum(w, e + 1).astype(jnp.float32)[None, :, None]
        outs.append(win_sum / cnt - ext[:, N - L:, sl])
    pooled = jnp.concatenate(outs, axis=-1)
    B = pooled.shape[0]
    pooled = jnp.einsum('blgc,gcd->blgd', pooled.reshape(B, L, POOL_GROUPS, POOL_GC),
                        w_grp.astype(jnp.float32)).reshape(B, L, POOL_WIDTH)
    pooled = pooled * scale.astype(jnp.float32)
    return pooled, ext[:, N - POOL_BUF:]


def _mixer(x, s0, buf0, w_in, w_gk2, b_gk, gla_norm_w, w_branch_gla, w_pool_grp, pool_scale,
           w_branch_pool, b_gates, w_out):
    B, L, _ = x.shape
    f32 = jnp.float32
    h = x @ w_in
    q, k, v, lr, g, u, ga, gb = _split_in(h)
    q = q.astype(f32).reshape(B, L, GLA_HEADS, GLA_DK) * (GLA_DK ** -0.5)
    k = k.astype(f32).reshape(B, L, GLA_HEADS, GLA_DK)
    v = v.astype(f32).reshape(B, L, GLA_HEADS, GLA_DV)
    glog = jax.nn.log_sigmoid((lr @ w_gk2 + b_gk).astype(f32)) / GATE_NORMALIZER
    glog = glog.reshape(B, L, GLA_HEADS, GLA_DK)
    o, s_new = _gla_chunked(q, k, v, glog, s0.astype(f32))
    o = o * lax.rsqrt(jnp.mean(jnp.square(o), -1, keepdims=True) + RMS_EPS) * gla_norm_w.astype(f32)
    o = o * jax.nn.silu(g.astype(f32).reshape(B, L, GLA_HEADS, GLA_DV))
    branch_a = o.reshape(B, L, GLA_VAL).astype(x.dtype) @ w_branch_gla
    pooled, buf_new = _pool_mix(u, buf0, w_pool_grp, pool_scale)
    branch_b = pooled.astype(x.dtype) @ w_branch_pool
    gate_a = jax.nn.sigmoid(ga + b_gates[:D_MODEL])
    gate_b = jax.nn.sigmoid(gb + b_gates[D_MODEL:])
    merged = gate_a * branch_a + gate_b * branch_b
    return merged @ w_out, s_new, buf_new


def _moe(x2, w_router, b_router, w_gu, b_gu, w_down, b_down):
    M, D = x2.shape
    MK = M * TOP_K
    logits = (x2 @ w_router).astype(jnp.float32) + b_router.astype(jnp.float32)
    top_v, top_i = lax.top_k(logits, TOP_K)
    gate_w = jax.nn.softmax(top_v, axis=-1)
    flat_e = top_i.reshape(-1)
    flat_tok = jnp.arange(MK, dtype=jnp.int32) // TOP_K
    order = jnp.argsort(flat_e)
    se = flat_e[order]
    stok = flat_tok[order]
    sw = gate_w.reshape(-1)[order]
    counts = jnp.bincount(flat_e, length=N_EXPERTS)
    padded = (counts + MOE_BLOCK - 1) // MOE_BLOCK * MOE_BLOCK
    pad_end = jnp.cumsum(padded)
    pad_start = pad_end - padded
    start = jnp.cumsum(counts) - counts
    dest = pad_start[se] + jnp.arange(MK, dtype=jnp.int32) - start[se]
    n_blocks = -(-(MK + N_EXPERTS * (MOE_BLOCK - 1)) // MOE_BLOCK)
    n_rows = n_blocks * MOE_BLOCK
    row_tok = jnp.full((n_rows,), M, dtype=jnp.int32).at[dest].set(stok)
    x_pad = jnp.concatenate([x2, jnp.zeros((1, D), x2.dtype)], axis=0)
    xb = x_pad[row_tok].reshape(n_blocks, MOE_BLOCK, D)
    block_e = jnp.minimum(jnp.searchsorted(pad_end, jnp.arange(n_blocks) * MOE_BLOCK, side='right'),
                          N_EXPERTS - 1)

    def expert_block(args):
        xblk, e = args
        hh = xblk @ w_gu[e] + b_gu[e]
        gate = jnp.minimum(hh[:, :D_FF], SWIGLU_LIMIT)
        up = jnp.clip(hh[:, D_FF:], -SWIGLU_LIMIT, SWIGLU_LIMIT)
        act = (up + 1.0) * (gate * jax.nn.sigmoid(GLU_ALPHA * gate))
        return act @ w_down[e] + b_down[e]

    res = lax.map(expert_block, (xb, block_e)).reshape(n_rows, D)
    y = jnp.zeros((M, D), jnp.float32).at[stok].add(sw[:, None] * res[dest].astype(jnp.float32))
    return y.astype(x2.dtype)


def _layer(x, s0, buf0, mix_w, ln1_g, ln1_b, moe_w, ln2_g, ln2_b):
    B, L, D = x.shape
    mix, s_new, buf_new = _mixer(x, s0, buf0, *mix_w)
    h = _layer_norm(DEEPNORM_ALPHA * x + mix, ln1_g, ln1_b)
    f = _moe(h.reshape(B * L, D), *moe_w).reshape(B, L, D)
    y = _layer_norm(DEEPNORM_ALPHA * h + f, ln2_g, ln2_b)
    return y, s_new, buf_new


def setup_inputs(seed: int = 0) -> dict:
    key = jax.random.key(seed)
    ks = jax.random.split(key, 32)
    nrm = jax.random.normal
    f32 = jnp.float32
    col_scale = jnp.concatenate([jnp.full((n,), DEEPNORM_BETA if i == 2 else 1.0, f32)
                                 for i, n in enumerate(IN_SIZES)])
    return {
        'x_prompt': nrm(ks[0], (BATCH, SEQ, D_MODEL), f32),
        'x_sample': nrm(ks[1], (DEC_BATCH, DEC_SEQ, D_MODEL), f32),
        'state_gla': 0.3 * nrm(ks[2], (DEPTH, DEC_BATCH, GLA_HEADS, GLA_DK, GLA_DV), f32),
        'state_pool': nrm(ks[3], (DEPTH, DEC_BATCH, POOL_BUF, POOL_WIDTH), f32),
        'w_in': nrm(ks[4], (DEPTH, D_MODEL, N_IN), f32) * D_MODEL ** -0.5 * col_scale,
        'w_gk2': nrm(ks[5], (DEPTH, GATE_RANK, GLA_KEY), f32) * GATE_RANK ** -0.5,
        'b_gk': 0.1 * nrm(ks[6], (DEPTH, GLA_KEY), f32),
        'gla_norm_w': 1.0 + 0.02 * nrm(ks[7], (DEPTH, GLA_DV), f32),
        'w_branch_gla': nrm(ks[8], (DEPTH, GLA_VAL, D_MODEL), f32) * GLA_VAL ** -0.5,
        'w_pool_grp': nrm(ks[9], (DEPTH, POOL_GROUPS, POOL_GC, POOL_GC), f32) * POOL_GC ** -0.5,
        'pool_scale': 1.0 + 0.1 * nrm(ks[10], (DEPTH, POOL_WIDTH), f32),
        'w_branch_pool': nrm(ks[11], (DEPTH, POOL_WIDTH, D_MODEL), f32) * POOL_WIDTH ** -0.5,
        'b_gates': 0.1 * nrm(ks[12], (DEPTH, 2 * D_MODEL), f32),
        'w_out': nrm(ks[13], (DEPTH, D_MODEL, D_MODEL), f32) * D_MODEL ** -0.5 * DEEPNORM_BETA,
        'ln1_g': 1.0 + 0.02 * nrm(ks[14], (DEPTH, D_MODEL), f32),
        'ln1_b': 0.02 * nrm(ks[15], (DEPTH, D_MODEL), f32),
        'w_router': nrm(ks[16], (DEPTH, D_MODEL, N_EXPERTS), f32) * D_MODEL ** -0.5,
        'b_router': 0.01 * nrm(ks[17], (DEPTH, N_EXPERTS), f32),
        'w_gu': nrm(ks[18], (DEPTH, N_EXPERTS, D_MODEL, 2 * D_FF), f32) * D_MODEL ** -0.5,
        'b_gu': 0.02 * nrm(ks[19], (DEPTH, N_EXPERTS, 2 * D_FF), f32),
        'w_down': nrm(ks[20], (DEPTH, N_EXPERTS, D_FF, D_MODEL), f32) * D_FF ** -0.5 * DEEPNORM_BETA,
        'b_down': 0.02 * nrm(ks[21], (DEPTH, N_EXPERTS, D_MODEL), f32),
        'ln2_g': 1.0 + 0.02 * nrm(ks[22], (DEPTH, D_MODEL), f32),
        'ln2_b': 0.02 * nrm(ks[23], (DEPTH, D_MODEL), f32),
    }


def reference(x_prompt, x_sample, state_gla, state_pool, w_in, w_gk2, b_gk, gla_norm_w, w_branch_gla,
              w_pool_grp, pool_scale, w_branch_pool, b_gates, w_out, ln1_g, ln1_b, w_router, b_router,
              w_gu, b_gu, w_down, b_down, ln2_g, ln2_b):
    yp = x_prompt
    ys = x_sample
    Bp = x_prompt.shape[0]
    gla_p, pool_p, gla_s, pool_s = [], [], [], []
    for l in range(DEPTH):
        mix_w = (w_in[l], w_gk2[l], b_gk[l], gla_norm_w[l], w_branch_gla[l], w_pool_grp[l],
                 pool_scale[l], w_branch_pool[l], b_gates[l], w_out[l])
        moe_w = (w_router[l], b_router[l], w_gu[l], b_gu[l], w_down[l], b_down[l])
        s0_p = jnp.zeros((Bp, GLA_HEADS, GLA_DK, GLA_DV), jnp.float32)
        buf0_p = jnp.zeros((Bp, 0, POOL_WIDTH), jnp.float32)
        yp, sp, bp = _layer(yp, s0_p, buf0_p, mix_w, ln1_g[l], ln1_b[l], moe_w, ln2_g[l], ln2_b[l])
        ys, ss, bs = _layer(ys, state_gla[l], state_pool[l], mix_w, ln1_g[l], ln1_b[l], moe_w,
                            ln2_g[l], ln2_b[l])
        gla_p.append(sp.astype(state_gla.dtype))
        pool_p.append(bp.astype(state_pool.dtype))
        gla_s.append(ss.astype(state_gla.dtype))
        pool_s.append(bs.astype(state_pool.dtype))
    new_gla_prompt = jnp.stack(gla_p, 0)
    new_pool_prompt = jnp.stack(pool_p, 0)
    new_gla_sample = jnp.stack(gla_s, 0)
    new_pool_sample = jnp.stack(pool_s, 0)
    return (yp, ys, new_gla_prompt, new_pool_prompt, new_gla_sample, new_pool_sample)
```

```python
import functools
from typing import NamedTuple

import jax
import jax.numpy as jnp
from jax import lax
from jax.experimental import pallas as pl
from jax.experimental.pallas import tpu as pltpu

F32 = jnp.float32
BF16 = jnp.bfloat16

GLA_HEADS = 4
GATE_RANK = 16
GATE_NORMALIZER = 16.0
GLA_CHUNK = 64
RMS_EPS = 1e-6
POOL_WINDOWS = (2, 4, 8, 16)
POOL_WMAX = 16
POOL_BUF = POOL_WMAX - 1
N_EXPERTS = 32
TOP_K = 4
SWIGLU_LIMIT = 7.0
GLU_ALPHA = 1.702
LN_EPS = 1e-5

LANES = 128
SUBLANES = 8
VMEM_LIMIT_BYTES = 56 * 1024 * 1024

MIX_TILE = 256
GLA_SEQS = 8
EXPERT_ROWS = 256
COMBINE_TILE = 256
CUMSUM_BLOCK = 256


class Dims(NamedTuple):
    d: int
    dk: int
    dv: int
    key: int
    val: int
    pw: int
    pgc: int
    alpha: float


def _dims(d, depth):
    return Dims(d=d, dk=d // 8, dv=d // 4, key=d // 2, val=d, pw=d // 2, pgc=d // 8, alpha=(2.0 * depth) ** 0.25)


class MixW(NamedTuple):
    w_qkv: object
    w_lr: object
    w_gk2: object
    b_gk: object
    w_tail: object
    gla_norm: object
    w_ba: object
    w_pg: object
    pool_scale: object
    w_bp: object
    b_gates: object
    w_out: object
    ln_g: object
    ln_b: object
    w_rt: object
    b_rt: object


def _mm(a, b):
    return jnp.dot(a, b, preferred_element_type=F32)


def _layer_norm(z, g, b):
    mu = jnp.mean(z, axis=-1, keepdims=True)
    zc = z - mu
    var = jnp.mean(zc * zc, axis=-1, keepdims=True)
    return zc * lax.rsqrt(var + LN_EPS) * g + b


def _to_row_tiles(ref, val, rows):
    for c in range(SUBLANES):
        ref[pl.ds(c, rows, stride=SUBLANES), :] = val[:, c * LANES:(c + 1) * LANES]


def _from_row_tiles(ref, rows):
    return jnp.concatenate([ref[pl.ds(c, rows, stride=SUBLANES), :] for c in range(SUBLANES)], axis=1)


def _project_qkv(xb, w, dm, q_s, k_s, v_s, gl_s):
    q_s[...] = _mm(xb, w.w_qkv[:, 0:dm.key]) * (dm.dk ** -0.5)
    k_s[...] = _mm(xb, w.w_qkv[:, dm.key:2 * dm.key])
    v_s[...] = _mm(xb, w.w_qkv[:, 2 * dm.key:2 * dm.key + dm.val])
    lr = _mm(xb, w.w_lr[...])
    gk = _mm(lr.astype(BF16), w.w_gk2[...]) + w.b_gk[...]
    gl_s[...] = (jnp.minimum(gk, 0.0) - jnp.log1p(jnp.exp(-jnp.abs(gk)))) / GATE_NORMALIZER


def _gla(q_s, k_s, v_s, gl_s, b_s, o_s, st_ref, *, dm, n_chunks, chunk, chunks_per_seq):
    rows = n_chunks * chunk
    blk = min(rows, CUMSUM_BLOCK)
    r = lax.broadcasted_iota(jnp.int32, (blk, blk), 0)
    c = lax.broadcasted_iota(jnp.int32, (blk, blk), 1)
    tri = ((r // chunk == c // chunk) & (c <= r)).astype(BF16)
    for b0 in range(0, rows, blk):
        gl = gl_s[b0:b0 + blk, :]
        hi = gl.astype(BF16)
        lo = (gl - hi.astype(F32)).astype(BF16)
        b_s[b0:b0 + blk, :] = _mm(tri, hi) + _mm(tri, lo)

    ri = lax.broadcasted_iota(jnp.int32, (chunk, chunk), 0)
    ci = lax.broadcasted_iota(jnp.int32, (chunk, chunk), 1)
    causal = ci <= ri
    nt = (((1,), (1,)), ((), ()))
    tn = (((0,), (0,)), ((), ()))

    def step(idx, carry):
        r0 = pl.multiple_of(idx * chunk, chunk)
        seq = idx // chunks_per_seq
        for h in range(GLA_HEADS):
            ks = slice(h * dm.dk, (h + 1) * dm.dk)
            vs = slice(h * dm.dv, (h + 1) * dm.dv)
            bc = b_s[pl.ds(r0, chunk), ks]
            b_ref = bc[chunk // 2:chunk // 2 + 1, :]
            b_last = bc[chunk - 1:chunk, :]
            qc = q_s[pl.ds(r0, chunk), ks]
            kc = k_s[pl.ds(r0, chunk), ks]
            vc = v_s[pl.ds(r0, chunk), vs].astype(BF16)
            s_old = st_ref[seq, h]
            a = lax.dot_general((qc * jnp.exp(bc - b_ref)).astype(BF16), (kc * jnp.exp(b_ref - bc)).astype(BF16),
                                nt, preferred_element_type=F32)
            a = jnp.where(causal, a, 0.0)
            o = _mm(a.astype(BF16), vc) + _mm((qc * jnp.exp(bc)).astype(BF16), s_old.astype(BF16))
            o_s[pl.ds(r0, chunk), vs] = o
            e_col = jnp.broadcast_to(jnp.exp(b_last), (dm.dk, dm.dk)).T
            decay = jnp.concatenate([e_col] * (dm.dv // dm.dk), axis=1)
            kv = lax.dot_general((kc * jnp.exp(b_last - bc)).astype(BF16), vc, tn, preferred_element_type=F32)
            st_ref[seq, h] = decay * s_old + kv
        return carry

    lax.fori_loop(0, n_chunks, step, 0)


def _mixer_tail(x, xb, o_s, ext_s, pos0, w, dm, *, nseq, seq_len):
    t = nseq * seq_len
    off_u = dm.val
    off_ga = off_u + dm.pw
    off_gb = off_ga + dm.d

    g = _mm(xb, w.w_tail[:, 0:dm.val])
    parts = []
    for h in range(GLA_HEADS):
        vs = slice(h * dm.dv, (h + 1) * dm.dv)
        oh = o_s[:, vs]
        ms = jnp.mean(oh * oh, axis=-1, keepdims=True)
        on = oh * lax.rsqrt(ms + RMS_EPS) * w.gla_norm[...]
        gh = g[:, vs]
        parts.append((on * (gh * jax.nn.sigmoid(gh))).astype(BF16))
    branch_a = _mm(jnp.concatenate(parts, axis=1), w.w_ba[...])

    u = _mm(xb, w.w_tail[:, off_u:off_u + dm.pw])
    ext_s[:, POOL_WMAX:POOL_WMAX + seq_len, :] = u.reshape(nseq, seq_len, dm.pw)
    p = lax.broadcasted_iota(jnp.int32, (nseq, seq_len, dm.pgc), 1)
    pooled = []
    for gi, win in enumerate(POOL_WINDOWS):
        cs = slice(gi * dm.pgc, (gi + 1) * dm.pgc)
        cur = ext_s[:, POOL_WMAX:POOL_WMAX + seq_len, cs]
        acc = cur
        for j in range(1, win):
            acc = acc + ext_s[:, POOL_WMAX - j:POOL_WMAX - j + seq_len, cs]
        cnt = jnp.minimum(win, p + (pos0 + 1)).astype(F32)
        pg = (acc / cnt - cur).reshape(t, dm.pgc)
        pg = _mm(pg.astype(BF16), w.w_pg[gi]) * w.pool_scale[:, cs]
        pooled.append(pg.astype(BF16))
    branch_b = _mm(jnp.concatenate(pooled, axis=1), w.w_bp[...])

    gate_a = jax.nn.sigmoid(_mm(xb, w.w_tail[:, off_ga:off_ga + dm.d]) + w.b_gates[:, 0:dm.d])
    merged = gate_a * branch_a
    gate_b = jax.nn.sigmoid(_mm(xb, w.w_tail[:, off_gb:off_gb + dm.d]) + w.b_gates[:, dm.d:2 * dm.d])
    merged = merged + gate_b * branch_b
    mix = _mm(merged.astype(BF16), w.w_out[...])
    h1 = _layer_norm(dm.alpha * x + mix, w.ln_g[...], w.ln_b[...])
    logits = _mm(h1.astype(BF16), w.w_rt[...]) + w.b_rt[...]
    return h1, logits


N_MIXW = len(MixW._fields)


def _prompt_mixer_kernel(*refs, dm, tile, n_tiles):
    x_ref = refs[0]
    w = MixW(*refs[1:1 + N_MIXW])
    h1_ref, lg_ref, st_ref, buf_ref, q_s, k_s, v_s, gl_s, b_s, o_s, ext_s = refs[1 + N_MIXW:]
    lt = pl.program_id(1)

    @pl.when(lt == 0)
    def _():
        st_ref[...] = jnp.zeros(st_ref.shape, F32)
        ext_s[:, 0:POOL_WMAX, :] = jnp.zeros((1, POOL_WMAX, dm.pw), F32)

    x = x_ref[...]
    xb = x.astype(BF16)
    _project_qkv(xb, w, dm, q_s, k_s, v_s, gl_s)
    chunk = GLA_CHUNK
    _gla(q_s, k_s, v_s, gl_s, b_s, o_s, st_ref, dm=dm, n_chunks=tile // chunk, chunk=chunk,
         chunks_per_seq=tile // chunk)
    h1, logits = _mixer_tail(x, xb, o_s, ext_s, lt * tile, w, dm, nseq=1, seq_len=tile)
    _to_row_tiles(h1_ref, h1, tile)
    lg_ref[...] = logits
    ext_s[:, 0:POOL_WMAX, :] = ext_s[:, tile:tile + POOL_WMAX, :]

    @pl.when(lt == n_tiles - 1)
    def _():
        buf_ref[...] = ext_s[:, 1:POOL_WMAX, :]


def _sample_proj_kernel(*refs, dm):
    x_ref = refs[0]
    w = MixW(*refs[1:1 + N_MIXW])
    q_ref, k_ref, v_ref, gl_ref = refs[1 + N_MIXW:]
    _project_qkv(x_ref[...].astype(BF16), w, dm, q_ref, k_ref, v_ref, gl_ref)


def _sample_gla_kernel(q_ref, k_ref, v_ref, gl_ref, s0_ref, o_ref, st_ref, b_s, *, dm, nseq, seq_len):
    st_ref[...] = s0_ref[...]
    _gla(q_ref, k_ref, v_ref, gl_ref, b_s, o_ref, st_ref, dm=dm, n_chunks=nseq, chunk=seq_len, chunks_per_seq=1)


def _sample_tail_kernel(*refs, dm, nseq, seq_len):
    x_ref, o_ref, hist_ref = refs[0:3]
    w = MixW(*refs[3:3 + N_MIXW])
    h1_ref, lg_ref, buf_ref, ext_s = refs[3 + N_MIXW:]
    ext_s[:, 0:1, :] = jnp.zeros((nseq, 1, dm.pw), F32)
    ext_s[:, 1:POOL_WMAX, :] = hist_ref[...]
    x = x_ref[...]
    h1, logits = _mixer_tail(x, x.astype(BF16), o_ref, ext_s, POOL_BUF, w, dm, nseq=nseq, seq_len=seq_len)
    _to_row_tiles(h1_ref, h1, nseq * seq_len)
    lg_ref[...] = logits
    buf_ref[...] = ext_s[:, seq_len + 1:seq_len + POOL_WMAX, :]


def _const_spec(arr):
    nd = arr.ndim
    return pl.BlockSpec(arr.shape, lambda *_: (0,) * nd, pipeline_mode=pl.Buffered(1))


def _prompt_mixer(x, w, dm):
    bsz, seq, d = x.shape
    tile = MIX_TILE
    n_tiles = seq // tile
    m = bsz * seq
    scratch = [pltpu.VMEM((tile, dm.key), F32), pltpu.VMEM((tile, dm.key), F32), pltpu.VMEM((tile, dm.val), F32),
               pltpu.VMEM((tile, dm.key), F32), pltpu.VMEM((tile, dm.key), F32), pltpu.VMEM((tile, dm.val), F32),
               pltpu.VMEM((1, POOL_WMAX + tile, dm.pw), F32)]
    return pl.pallas_call(
        functools.partial(_prompt_mixer_kernel, dm=dm, tile=tile, n_tiles=n_tiles),
        grid=(bsz, n_tiles),
        in_specs=[pl.BlockSpec((tile, d), lambda b, t: (b * n_tiles + t, 0))] + [_const_spec(a) for a in w],
        out_specs=[pl.BlockSpec((tile * SUBLANES, LANES), lambda b, t: (b * n_tiles + t, 0)),
                   pl.BlockSpec((tile, LANES), lambda b, t: (b * n_tiles + t, 0)),
                   pl.BlockSpec((1, GLA_HEADS, dm.dk, dm.dv), lambda b, t: (b, 0, 0, 0)),
                   pl.BlockSpec((1, POOL_BUF, dm.pw), lambda b, t: (b, 0, 0))],
        out_shape=[jax.ShapeDtypeStruct((m * SUBLANES, LANES), F32),
                   jax.ShapeDtypeStruct((m, LANES), F32),
                   jax.ShapeDtypeStruct((bsz, GLA_HEADS, dm.dk, dm.dv), F32),
                   jax.ShapeDtypeStruct((bsz, POOL_BUF, dm.pw), F32)],
        scratch_shapes=scratch,
        compiler_params=pltpu.CompilerParams(dimension_semantics=("arbitrary", "arbitrary"),
                                             vmem_limit_bytes=VMEM_LIMIT_BYTES),
        name="prompt_mixer",
    )(x.reshape(m, d), *w)


def _sample_mixer(x, s0, hist, w, dm):
    bsz, seq, d = x.shape
    m = bsz * seq
    x2 = x.reshape(m, d)
    tile = MIX_TILE
    cparams = pltpu.CompilerParams(dimension_semantics=("arbitrary",), vmem_limit_bytes=VMEM_LIMIT_BYTES)
    row = lambda n: pl.BlockSpec((tile, n), lambda i: (i, 0))
    q, k, v, gl = pl.pallas_call(
        functools.partial(_sample_proj_kernel, dm=dm),
        grid=(m // tile,),
        in_specs=[row(d)] + [_const_spec(a) for a in w],
        out_specs=[row(dm.key), row(dm.key), row(dm.val), row(dm.key)],
        out_shape=[jax.ShapeDtypeStruct((m, n), F32) for n in (dm.key, dm.key, dm.val, dm.key)],
        compiler_params=cparams,
        name="sample_proj",
    )(x2, *w)

    nseq = GLA_SEQS
    rows = nseq * seq
    grow = lambda n: pl.BlockSpec((rows, n), lambda i: (i, 0))
    st_spec = pl.BlockSpec((nseq, GLA_HEADS, dm.dk, dm.dv), lambda i: (i, 0, 0, 0))
    o, st = pl.pallas_call(
        functools.partial(_sample_gla_kernel, dm=dm, nseq=nseq, seq_len=seq),
        grid=(bsz // nseq,),
        in_specs=[grow(dm.key), grow(dm.key), grow(dm.val), grow(dm.key), st_spec],
        out_specs=[grow(dm.val), st_spec],
        out_shape=[jax.ShapeDtypeStruct((m, dm.val), F32), jax.ShapeDtypeStruct(s0.shape, F32)],
        scratch_shapes=[pltpu.VMEM((rows, dm.key), F32)],
        compiler_params=cparams,
        name="sample_gla",
    )(q, k, v, gl, s0)

    tseq = tile // seq
    hist_spec = pl.BlockSpec((tseq, POOL_BUF, dm.pw), lambda i: (i, 0, 0))
    h1, lg, buf = pl.pallas_call(
        functools.partial(_sample_tail_kernel, dm=dm, nseq=tseq, seq_len=seq),
        grid=(m // tile,),
        in_specs=[row(d), row(dm.val), hist_spec] + [_const_spec(a) for a in w],
        out_specs=[pl.BlockSpec((tile * SUBLANES, LANES), lambda i: (i, 0)), row(LANES), hist_spec],
        out_shape=[jax.ShapeDtypeStruct((m * SUBLANES, LANES), F32),
                   jax.ShapeDtypeStruct((m, LANES), F32),
                   jax.ShapeDtypeStruct((bsz, POOL_BUF, dm.pw), F32)],
        scratch_shapes=[pltpu.VMEM((tseq, POOL_WMAX + seq, dm.pw), F32)],
        compiler_params=cparams,
        name="sample_tail",
    )(x2, o, hist, *w)
    return h1, lg, st, buf


def _expert_kernel(be_ref, nused_ref, rowtok_hbm, h1_hbm, roww_ref, wgu_ref, bgu_ref, wdn_ref, bdn_ref, out_ref,
                   idx_s, xbuf, sem_i, sem_g, *, rows, d_ff):
    i = pl.program_id(0)
    nused = nused_ref[0]
    slot = i % 2

    def idx_copy(blk, sl):
        return pltpu.make_async_copy(rowtok_hbm.at[blk], idx_s.at[sl], sem_i.at[sl])

    def start_rows(sl):
        def body(r, carry):
            dst = xbuf.at[sl, pl.ds(pl.multiple_of(r * SUBLANES, SUBLANES), SUBLANES), :]
            pltpu.make_async_copy(h1_hbm.at[idx_s[sl, r]], dst, sem_g.at[sl]).start()
            return carry
        lax.fori_loop(0, rows, body, 0)

    def wait_rows(sl):
        pltpu.make_async_copy(xbuf.at[sl], xbuf.at[sl], sem_g.at[sl]).wait()

    @pl.when(i == 0)
    def _():
        first = idx_copy(0, 0)
        first.start()
        first.wait()
        start_rows(0)

        @pl.when(nused > 1)
        def _():
            idx_copy(1, 1).start()

    @pl.when(i < nused)
    def _():
        @pl.when(i + 1 < nused)
        def _():
            idx_copy(i + 1, 1 - slot).wait()
            start_rows(1 - slot)

        @pl.when(i + 2 < nused)
        def _():
            idx_copy(i + 2, slot).start()

        wait_rows(slot)
        xb = _from_row_tiles(xbuf.at[slot], rows).astype(BF16)
        hh = _mm(xb, wgu_ref[0]) + bgu_ref[0]
        gate = jnp.minimum(hh[:, 0:d_ff], SWIGLU_LIMIT)
        up = jnp.clip(hh[:, d_ff:2 * d_ff], -SWIGLU_LIMIT, SWIGLU_LIMIT)
        act = (up + 1.0) * (gate * jax.nn.sigmoid(GLU_ALPHA * gate))
        res = (_mm(act.astype(BF16), wdn_ref[0]) + bdn_ref[0]) * roww_ref[...]
        _to_row_tiles(out_ref, res, rows)

    @pl.when(i >= nused)
    def _():
        out_ref[...] = jnp.zeros(out_ref.shape, F32)


def _combine_kernel(pos_hbm, res_hbm, h1_ref, g_ref, b_ref, y_ref, idx_s, gbuf, sem_i, sem_g, *, tile, alpha, n_tiles):
    i = pl.program_id(0)
    slot = i % 2

    def idx_copy(blk, sl):
        return pltpu.make_async_copy(pos_hbm.at[blk], idx_s.at[sl], sem_i.at[sl])

    def start_rows(sl):
        for k in range(TOP_K):
            def body(t, carry, k=k):
                dst = gbuf.at[sl, k, pl.ds(pl.multiple_of(t * SUBLANES, SUBLANES), SUBLANES), :]
                pltpu.make_async_copy(res_hbm.at[idx_s[sl, k * tile + t]], dst, sem_g.at[sl]).start()
                return carry
            lax.fori_loop(0, tile, body, 0)

    def wait_rows(sl):
        pltpu.make_async_copy(gbuf.at[sl], gbuf.at[sl], sem_g.at[sl]).wait()

    @pl.when(i == 0)
    def _():
        first = idx_copy(0, 0)
        first.start()
        first.wait()
        start_rows(0)

        @pl.when(n_tiles > 1)
        def _():
            idx_copy(1, 1).start()

    @pl.when(i + 1 < n_tiles)
    def _():
        idx_copy(i + 1, 1 - slot).wait()
        start_rows(1 - slot)

    @pl.when(i + 2 < n_tiles)
    def _():
        idx_copy(i + 2, slot).start()

    wait_rows(slot)
    f = _from_row_tiles(gbuf.at[slot, 0], tile)
    for k in range(1, TOP_K):
        f = f + _from_row_tiles(gbuf.at[slot, k], tile)
    z = alpha * _from_row_tiles(h1_ref, tile) + f
    y_ref[...] = _layer_norm(z, g_ref[...], b_ref[...])


def _route(logits, rows_per_block):
    m = logits.shape[0]
    mk = m * TOP_K
    top_v, top_i = lax.top_k(logits, TOP_K)
    gate_w = jax.nn.softmax(top_v, axis=-1)
    flat_e = top_i.reshape(-1).astype(jnp.int32)
    flat_tok = jnp.arange(mk, dtype=jnp.int32) // TOP_K
    order = jnp.argsort(flat_e)
    se = flat_e[order]
    stok = flat_tok[order]
    sw = gate_w.reshape(-1)[order]
    counts = jnp.bincount(flat_e, length=N_EXPERTS).astype(jnp.int32)
    padded = (counts + rows_per_block - 1) // rows_per_block * rows_per_block
    pad_end = jnp.cumsum(padded)
    pad_start = pad_end - padded
    start = jnp.cumsum(counts) - counts
    dest = pad_start[se] + jnp.arange(mk, dtype=jnp.int32) - start[se]
    n_blocks = -(-(mk + N_EXPERTS * (rows_per_block - 1)) // rows_per_block)
    n_rows = n_blocks * rows_per_block
    row_tok = jnp.zeros((n_rows,), jnp.int32).at[dest].set(stok)
    row_w = jnp.zeros((n_rows,), F32).at[dest].set(sw)
    pos = jnp.zeros((mk,), jnp.int32).at[order].set(dest)
    block_e = jnp.minimum(jnp.searchsorted(pad_end, jnp.arange(n_blocks, dtype=jnp.int32) * rows_per_block,
                                           side='right'), N_EXPERTS - 1).astype(jnp.int32)
    n_used = (pad_end[-1] // rows_per_block).astype(jnp.int32).reshape(1)
    return block_e, n_used, row_tok.reshape(n_blocks, rows_per_block), row_w.reshape(n_rows, 1), pos.reshape(m, TOP_K)


def _moe(h1_tiles, logits, w_gu, b_gu, w_down, b_down, ln_g, ln_b, dm):
    m = logits.shape[0]
    rows = EXPERT_ROWS
    d_ff = w_down.shape[1]
    block_e, n_used, row_tok, row_w, pos = _route(logits, rows)
    n_blocks = row_tok.shape[0]
    h1_3d = h1_tiles.reshape(m, SUBLANES, LANES)

    res = pl.pallas_call(
        functools.partial(_expert_kernel, rows=rows, d_ff=d_ff),
        grid_spec=pltpu.PrefetchScalarGridSpec(
            num_scalar_prefetch=2,
            grid=(n_blocks,),
            in_specs=[pl.BlockSpec(memory_space=pl.ANY),
                      pl.BlockSpec(memory_space=pl.ANY),
                      pl.BlockSpec((rows, 1), lambda i, be, nu: (i, 0)),
                      pl.BlockSpec((1, dm.d, 2 * d_ff), lambda i, be, nu: (be[i], 0, 0)),
                      pl.BlockSpec((1, 1, 2 * d_ff), lambda i, be, nu: (be[i], 0, 0)),
                      pl.BlockSpec((1, d_ff, dm.d), lambda i, be, nu: (be[i], 0, 0)),
                      pl.BlockSpec((1, 1, dm.d), lambda i, be, nu: (be[i], 0, 0))],
            out_specs=pl.BlockSpec((rows * SUBLANES, LANES), lambda i, be, nu: (i, 0)),
            scratch_shapes=[pltpu.SMEM((2, rows), jnp.int32),
                            pltpu.VMEM((2, rows * SUBLANES, LANES), F32),
                            pltpu.SemaphoreType.DMA((2,)),
                            pltpu.SemaphoreType.DMA((2,))]),
        out_shape=jax.ShapeDtypeStruct((n_blocks * rows * SUBLANES, LANES), F32),
        compiler_params=pltpu.CompilerParams(dimension_semantics=("arbitrary",), vmem_limit_bytes=VMEM_LIMIT_BYTES),
        name="moe_experts",
    )(block_e, n_used, row_tok, h1_3d, row_w, w_gu.astype(BF16), b_gu[:, None, :], w_down.astype(BF16),
      b_down[:, None, :])

    tile = COMBINE_TILE
    n_tiles = m // tile
    pos_t = pos.reshape(n_tiles, tile, TOP_K).transpose(0, 2, 1).reshape(n_tiles, TOP_K * tile)
    y = pl.pallas_call(
        functools.partial(_combine_kernel, tile=tile, alpha=dm.alpha, n_tiles=n_tiles),
        grid=(n_tiles,),
        in_specs=[pl.BlockSpec(memory_space=pl.ANY),
                  pl.BlockSpec(memory_space=pl.ANY),
                  pl.BlockSpec((tile * SUBLANES, LANES), lambda i: (i, 0)),
                  pl.BlockSpec((1, dm.d), lambda i: (0, 0)),
                  pl.BlockSpec((1, dm.d), lambda i: (0, 0))],
        out_specs=pl.BlockSpec((tile, dm.d), lambda i: (i, 0)),
        out_shape=jax.ShapeDtypeStruct((m, dm.d), F32),
        scratch_shapes=[pltpu.SMEM((2, TOP_K * tile), jnp.int32),
                        pltpu.VMEM((2, TOP_K, tile * SUBLANES, LANES), F32),
                        pltpu.SemaphoreType.DMA((2,)),
                        pltpu.SemaphoreType.DMA((2,))],
        compiler_params=pltpu.CompilerParams(dimension_semantics=("arbitrary",), vmem_limit_bytes=VMEM_LIMIT_BYTES),
        name="moe_combine",
    )(pos_t, res.reshape(n_blocks * rows, SUBLANES, LANES), h1_tiles, ln_g[None, :], ln_b[None, :])
    return y


def _pad_cols(a, n):
    return jnp.pad(a, ((0, 0), (0, n - a.shape[1])))


def _mixer_weights(w_in, w_gk2, b_gk, gla_norm_w, w_branch_gla, w_pool_grp, pool_scale, w_branch_pool, b_gates,
                   w_out, ln_g, ln_b, w_router, b_router, dm):
    o_lr = 2 * dm.key + dm.val
    o_tail = o_lr + GATE_RANK
    return MixW(
        w_qkv=w_in[:, 0:o_lr].astype(BF16),
        w_lr=_pad_cols(w_in[:, o_lr:o_tail], LANES).astype(BF16),
        w_gk2=jnp.pad(w_gk2, ((0, LANES - GATE_RANK), (0, 0))).astype(BF16),
        b_gk=b_gk[None, :],
        w_tail=w_in[:, o_tail:].astype(BF16),
        gla_norm=gla_norm_w[None, :],
        w_ba=w_branch_gla.astype(BF16),
        w_pg=w_pool_grp.astype(BF16),
        pool_scale=pool_scale[None, :],
        w_bp=w_branch_pool.astype(BF16),
        b_gates=b_gates[None, :],
        w_out=w_out.astype(BF16),
        ln_g=ln_g[None, :],
        ln_b=ln_b[None, :],
        w_rt=_pad_cols(w_router, LANES).astype(BF16),
        b_rt=_pad_cols(b_router[None, :], LANES),
    )


def kernel(x_prompt, x_sample, state_gla, state_pool, w_in, w_gk2, b_gk, gla_norm_w, w_branch_gla, w_pool_grp,
           pool_scale, w_branch_pool, b_gates, w_out, ln1_g, ln1_b, w_router, b_router, w_gu, b_gu, w_down, b_down,
           ln2_g, ln2_b):
    depth = w_in.shape[0]
    bp, lp, d = x_prompt.shape
    bs, ls, _ = x_sample.shape
    assert d == SUBLANES * LANES and lp % MIX_TILE == 0 and (bs * ls) % MIX_TILE == 0 and MIX_TILE % ls == 0
    assert bs % GLA_SEQS == 0 and ls % SUBLANES == 0 and (bp * lp + bs * ls) % COMBINE_TILE == 0
    dm = _dims(d, depth)
    mp = bp * lp
    yp, ys = x_prompt, x_sample
    gla_p, pool_p, gla_s, pool_s = [], [], [], []
    for l in range(depth):
        w = _mixer_weights(w_in[l], w_gk2[l], b_gk[l], gla_norm_w[l], w_branch_gla[l], w_pool_grp[l], pool_scale[l],
                           w_branch_pool[l], b_gates[l], w_out[l], ln1_g[l], ln1_b[l], w_router[l], b_router[l], dm)
        h1p, lgp, sp, bufp = _prompt_mixer(yp, w, dm)
        h1s, lgs, ss, bufs = _sample_mixer(ys, state_gla[l], state_pool[l], w, dm)
        h1 = jnp.concatenate([h1p, h1s], axis=0)
        logits = jnp.concatenate([lgp, lgs], axis=0)[:, 0:N_EXPERTS]
        y = _moe(h1, logits, w_gu[l], b_gu[l], w_down[l], b_down[l], ln2_g[l], ln2_b[l], dm)
        yp = y[0:mp].reshape(bp, lp, d)
        ys = y[mp:].reshape(bs, ls, d)
        gla_p.append(sp.astype(state_gla.dtype))
        pool_p.append(bufp.astype(state_pool.dtype))
        gla_s.append(ss.astype(state_gla.dtype))
        pool_s.append(bufs.astype(state_pool.dtype))
    return (yp, ys, jnp.stack(gla_p, 0), jnp.stack(pool_p, 0), jnp.stack(gla_s, 0), jnp.stack(pool_s, 0))
```

```python
import functools
from typing import NamedTuple

import jax
import jax.numpy as jnp
from jax import lax
from jax.experimental import pallas as pl
from jax.experimental.pallas import tpu as pltpu

F32 = jnp.float32
BF16 = jnp.bfloat16

GLA_HEADS = 4
GATE_RANK = 16
GATE_NORMALIZER = 16.0
GLA_CHUNK = 64
RMS_EPS = 1e-6
POOL_WINDOWS = (2, 4, 8, 16)
POOL_WMAX = 16
POOL_BUF = POOL_WMAX - 1
N_EXPERTS = 32
TOP_K = 4
SWIGLU_LIMIT = 7.0
GLU_ALPHA = 1.702
LN_EPS = 1e-5

LANES = 128
SUBLANES = 8
VMEM_LIMIT_BYTES = 56 * 1024 * 1024

MIX_TILE = 256
GLA_SEQS = 8
ROUTER_TILE = 512
EXPERT_ROWS = 256
COMBINE_TILE = 256
CUMSUM_BLOCK = 256
DMA_UNROLL = 8
ZERO_FILL_BITS = (EXPERT_ROWS - 1).bit_length()


class Dims(NamedTuple):
    d: int
    dk: int
    dv: int
    key: int
    val: int
    pw: int
    pgc: int
    alpha: float


def _dims(d, depth):
    return Dims(d=d, dk=d // 8, dv=d // 4, key=d // 2, val=d, pw=d // 2, pgc=d // 8, alpha=(2.0 * depth) ** 0.25)


class MixW(NamedTuple):
    w_qkv: object
    w_lr: object
    w_gk2: object
    b_gk: object
    w_tail: object
    gla_norm: object
    w_ba: object
    w_pg: object
    pool_scale: object
    w_bp: object
    b_gates: object
    w_out: object
    ln_g: object
    ln_b: object
    w_rt: object
    b_rt: object


def _mm(a, b):
    return jnp.dot(a, b, preferred_element_type=F32)


def _layer_norm(z, g, b):
    mu = jnp.mean(z, axis=-1, keepdims=True)
    zc = z - mu
    var = jnp.mean(zc * zc, axis=-1, keepdims=True)
    return zc * lax.rsqrt(var + LN_EPS) * g + b


def _to_row_tiles(ref, val, rows):
    for c in range(SUBLANES):
        ref[pl.ds(c, rows, stride=SUBLANES), :] = val[:, c * LANES:(c + 1) * LANES]


def _from_row_tiles(ref, rows):
    return jnp.concatenate([ref[pl.ds(c, rows, stride=SUBLANES), :] for c in range(SUBLANES)], axis=1)


def _project_qkv(xb, w, dm, q_s, k_s, v_s, gl_s):
    q_s[...] = _mm(xb, w.w_qkv[:, 0:dm.key]) * (dm.dk ** -0.5)
    k_s[...] = _mm(xb, w.w_qkv[:, dm.key:2 * dm.key])
    v_s[...] = _mm(xb, w.w_qkv[:, 2 * dm.key:2 * dm.key + dm.val])
    lr = _mm(xb, w.w_lr[...])
    gk = _mm(lr.astype(BF16), w.w_gk2[...]) + w.b_gk[...]
    gl_s[...] = (jnp.minimum(gk, 0.0) - jnp.log1p(jnp.exp(-jnp.abs(gk)))) / GATE_NORMALIZER


def _gla(q_s, k_s, v_s, gl_s, b_s, o_s, st_ref, *, dm, n_chunks, chunk, chunks_per_seq):
    rows = n_chunks * chunk
    blk = min(rows, CUMSUM_BLOCK)
    r = lax.broadcasted_iota(jnp.int32, (blk, blk), 0)
    c = lax.broadcasted_iota(jnp.int32, (blk, blk), 1)
    tri = ((r // chunk == c // chunk) & (c <= r)).astype(BF16)
    for b0 in range(0, rows, blk):
        gl = gl_s[b0:b0 + blk, :]
        hi = gl.astype(BF16)
        lo = (gl - hi.astype(F32)).astype(BF16)
        b_s[b0:b0 + blk, :] = _mm(tri, hi) + _mm(tri, lo)

    ri = lax.broadcasted_iota(jnp.int32, (chunk, chunk), 0)
    ci = lax.broadcasted_iota(jnp.int32, (chunk, chunk), 1)
    causal = ci <= ri
    nt = (((1,), (1,)), ((), ()))
    tn = (((0,), (0,)), ((), ()))

    def step(idx, carry):
        r0 = pl.multiple_of(idx * chunk, chunk)
        seq = idx // chunks_per_seq
        for h in range(GLA_HEADS):
            ks = slice(h * dm.dk, (h + 1) * dm.dk)
            vs = slice(h * dm.dv, (h + 1) * dm.dv)
            bc = b_s[pl.ds(r0, chunk), ks]
            b_ref = bc[chunk // 2:chunk // 2 + 1, :]
            b_last = bc[chunk - 1:chunk, :]
            qc = q_s[pl.ds(r0, chunk), ks]
            kc = k_s[pl.ds(r0, chunk), ks]
            vc = v_s[pl.ds(r0, chunk), vs].astype(BF16)
            s_old = st_ref[seq, h]
            a = lax.dot_general((qc * jnp.exp(bc - b_ref)).astype(BF16), (kc * jnp.exp(b_ref - bc)).astype(BF16),
                                nt, preferred_element_type=F32)
            a = jnp.where(causal, a, 0.0)
            o = _mm(a.astype(BF16), vc) + _mm((qc * jnp.exp(bc)).astype(BF16), s_old.astype(BF16))
            o_s[pl.ds(r0, chunk), vs] = o
            e_col = jnp.broadcast_to(jnp.exp(b_last), (dm.dk, dm.dk)).T
            decay = jnp.concatenate([e_col] * (dm.dv // dm.dk), axis=1)
            kv = lax.dot_general((kc * jnp.exp(b_last - bc)).astype(BF16), vc, tn, preferred_element_type=F32)
            st_ref[seq, h] = decay * s_old + kv
        return carry

    lax.fori_loop(0, n_chunks, step, 0)


def _mixer_tail(x, xb, o_s, ext_s, pos0, w, dm, *, nseq, seq_len):
    t = nseq * seq_len
    off_u = dm.val
    off_ga = off_u + dm.pw
    off_gb = off_ga + dm.d

    g = _mm(xb, w.w_tail[:, 0:dm.val])
    parts = []
    for h in range(GLA_HEADS):
        vs = slice(h * dm.dv, (h + 1) * dm.dv)
        oh = o_s[:, vs]
        ms = jnp.mean(oh * oh, axis=-1, keepdims=True)
        on = oh * lax.rsqrt(ms + RMS_EPS) * w.gla_norm[...]
        gh = g[:, vs]
        parts.append((on * (gh * jax.nn.sigmoid(gh))).astype(BF16))
    branch_a = _mm(jnp.concatenate(parts, axis=1), w.w_ba[...])

    u = _mm(xb, w.w_tail[:, off_u:off_u + dm.pw])
    ext_s[:, POOL_WMAX:POOL_WMAX + seq_len, :] = u.reshape(nseq, seq_len, dm.pw)
    p = lax.broadcasted_iota(jnp.int32, (nseq, seq_len, dm.pgc), 1)
    pooled = []
    for gi, win in enumerate(POOL_WINDOWS):
        cs = slice(gi * dm.pgc, (gi + 1) * dm.pgc)
        cur = ext_s[:, POOL_WMAX:POOL_WMAX + seq_len, cs]
        acc = cur
        for j in range(1, win):
            acc = acc + ext_s[:, POOL_WMAX - j:POOL_WMAX - j + seq_len, cs]
        cnt = jnp.minimum(win, p + (pos0 + 1)).astype(F32)
        pg = (acc / cnt - cur).reshape(t, dm.pgc)
        pg = _mm(pg.astype(BF16), w.w_pg[gi]) * w.pool_scale[:, cs]
        pooled.append(pg.astype(BF16))
    branch_b = _mm(jnp.concatenate(pooled, axis=1), w.w_bp[...])

    gate_a = jax.nn.sigmoid(_mm(xb, w.w_tail[:, off_ga:off_ga + dm.d]) + w.b_gates[:, 0:dm.d])
    merged = gate_a * branch_a
    gate_b = jax.nn.sigmoid(_mm(xb, w.w_tail[:, off_gb:off_gb + dm.d]) + w.b_gates[:, dm.d:2 * dm.d])
    merged = merged + gate_b * branch_b
    mix = _mm(merged.astype(BF16), w.w_out[...])
    h1 = _layer_norm(dm.alpha * x + mix, w.ln_g[...], w.ln_b[...])
    logits = _mm(h1.astype(BF16), w.w_rt[...]) + w.b_rt[...]
    return h1, logits.T[0:N_EXPERTS, :]


N_MIXW = len(MixW._fields)


def _prompt_mixer_kernel(*refs, dm, tile, n_tiles):
    x_ref = refs[0]
    w = MixW(*refs[1:1 + N_MIXW])
    h1_ref, lg_ref, st_ref, buf_ref, q_s, k_s, v_s, gl_s, b_s, o_s, ext_s = refs[1 + N_MIXW:]
    lt = pl.program_id(1)

    @pl.when(lt == 0)
    def _():
        st_ref[...] = jnp.zeros(st_ref.shape, F32)
        ext_s[:, 0:POOL_WMAX, :] = jnp.zeros((1, POOL_WMAX, dm.pw), F32)

    x = x_ref[...]
    xb = x.astype(BF16)
    _project_qkv(xb, w, dm, q_s, k_s, v_s, gl_s)
    chunk = GLA_CHUNK
    _gla(q_s, k_s, v_s, gl_s, b_s, o_s, st_ref, dm=dm, n_chunks=tile // chunk, chunk=chunk,
         chunks_per_seq=tile // chunk)
    h1, logits_t = _mixer_tail(x, xb, o_s, ext_s, lt * tile, w, dm, nseq=1, seq_len=tile)
    _to_row_tiles(h1_ref, h1, tile)
    lg_ref[...] = logits_t
    ext_s[:, 0:POOL_WMAX, :] = ext_s[:, tile:tile + POOL_WMAX, :]

    @pl.when(lt == n_tiles - 1)
    def _():
        buf_ref[...] = ext_s[:, 1:POOL_WMAX, :]


def _sample_proj_kernel(*refs, dm):
    x_ref = refs[0]
    w = MixW(*refs[1:1 + N_MIXW])
    q_ref, k_ref, v_ref, gl_ref = refs[1 + N_MIXW:]
    _project_qkv(x_ref[...].astype(BF16), w, dm, q_ref, k_ref, v_ref, gl_ref)


def _sample_gla_kernel(q_ref, k_ref, v_ref, gl_ref, s0_ref, o_ref, st_ref, b_s, *, dm, nseq, seq_len):
    st_ref[...] = s0_ref[...]
    _gla(q_ref, k_ref, v_ref, gl_ref, b_s, o_ref, st_ref, dm=dm, n_chunks=nseq, chunk=seq_len, chunks_per_seq=1)


def _sample_tail_kernel(*refs, dm, nseq, seq_len):
    x_ref, o_ref, hist_ref = refs[0:3]
    w = MixW(*refs[3:3 + N_MIXW])
    h1_ref, lg_ref, buf_ref, ext_s = refs[3 + N_MIXW:]
    ext_s[:, 0:1, :] = jnp.zeros((nseq, 1, dm.pw), F32)
    ext_s[:, 1:POOL_WMAX, :] = hist_ref[...]
    x = x_ref[...]
    h1, logits_t = _mixer_tail(x, x.astype(BF16), o_ref, ext_s, POOL_BUF, w, dm, nseq=nseq, seq_len=seq_len)
    _to_row_tiles(h1_ref, h1, nseq * seq_len)
    lg_ref[...] = logits_t
    buf_ref[...] = ext_s[:, seq_len + 1:seq_len + POOL_WMAX, :]


def _const_spec(arr):
    nd = arr.ndim
    return pl.BlockSpec(arr.shape, lambda *_: (0,) * nd, pipeline_mode=pl.Buffered(1))


def _prompt_mixer(x, w, dm):
    bsz, seq, d = x.shape
    tile = MIX_TILE
    n_tiles = seq // tile
    m = bsz * seq
    scratch = [pltpu.VMEM((tile, dm.key), F32), pltpu.VMEM((tile, dm.key), F32), pltpu.VMEM((tile, dm.val), F32),
               pltpu.VMEM((tile, dm.key), F32), pltpu.VMEM((tile, dm.key), F32), pltpu.VMEM((tile, dm.val), F32),
               pltpu.VMEM((1, POOL_WMAX + tile, dm.pw), F32)]
    return pl.pallas_call(
        functools.partial(_prompt_mixer_kernel, dm=dm, tile=tile, n_tiles=n_tiles),
        grid=(bsz, n_tiles),
        in_specs=[pl.BlockSpec((tile, d), lambda b, t: (b * n_tiles + t, 0))] + [_const_spec(a) for a in w],
        out_specs=[pl.BlockSpec((tile * SUBLANES, LANES), lambda b, t: (b * n_tiles + t, 0)),
                   pl.BlockSpec((N_EXPERTS, tile), lambda b, t: (0, b * n_tiles + t)),
                   pl.BlockSpec((1, GLA_HEADS, dm.dk, dm.dv), lambda b, t: (b, 0, 0, 0)),
                   pl.BlockSpec((1, POOL_BUF, dm.pw), lambda b, t: (b, 0, 0))],
        out_shape=[jax.ShapeDtypeStruct((m * SUBLANES, LANES), F32),
                   jax.ShapeDtypeStruct((N_EXPERTS, m), F32),
                   jax.ShapeDtypeStruct((bsz, GLA_HEADS, dm.dk, dm.dv), F32),
                   jax.ShapeDtypeStruct((bsz, POOL_BUF, dm.pw), F32)],
        scratch_shapes=scratch,
        compiler_params=pltpu.CompilerParams(dimension_semantics=("arbitrary", "arbitrary"),
                                             vmem_limit_bytes=VMEM_LIMIT_BYTES),
        name="prompt_mixer",
    )(x.reshape(m, d), *w)


def _sample_mixer(x, s0, hist, w, dm):
    bsz, seq, d = x.shape
    m = bsz * seq
    x2 = x.reshape(m, d)
    tile = MIX_TILE
    cparams = pltpu.CompilerParams(dimension_semantics=("arbitrary",), vmem_limit_bytes=VMEM_LIMIT_BYTES)
    row = lambda n: pl.BlockSpec((tile, n), lambda i: (i, 0))
    q, k, v, gl = pl.pallas_call(
        functools.partial(_sample_proj_kernel, dm=dm),
        grid=(m // tile,),
        in_specs=[row(d)] + [_const_spec(a) for a in w],
        out_specs=[row(dm.key), row(dm.key), row(dm.val), row(dm.key)],
        out_shape=[jax.ShapeDtypeStruct((m, n), F32) for n in (dm.key, dm.key, dm.val, dm.key)],
        compiler_params=cparams,
        name="sample_proj",
    )(x2, *w)

    nseq = GLA_SEQS
    rows = nseq * seq
    grow = lambda n: pl.BlockSpec((rows, n), lambda i: (i, 0))
    st_spec = pl.BlockSpec((nseq, GLA_HEADS, dm.dk, dm.dv), lambda i: (i, 0, 0, 0))
    o, st = pl.pallas_call(
        functools.partial(_sample_gla_kernel, dm=dm, nseq=nseq, seq_len=seq),
        grid=(bsz // nseq,),
        in_specs=[grow(dm.key), grow(dm.key), grow(dm.val), grow(dm.key), st_spec],
        out_specs=[grow(dm.val), st_spec],
        out_shape=[jax.ShapeDtypeStruct((m, dm.val), F32), jax.ShapeDtypeStruct(s0.shape, F32)],
        scratch_shapes=[pltpu.VMEM((rows, dm.key), F32)],
        compiler_params=cparams,
        name="sample_gla",
    )(q, k, v, gl, s0)

    tseq = tile // seq
    hist_spec = pl.BlockSpec((tseq, POOL_BUF, dm.pw), lambda i: (i, 0, 0))
    h1, lg, buf = pl.pallas_call(
        functools.partial(_sample_tail_kernel, dm=dm, nseq=tseq, seq_len=seq),
        grid=(m // tile,),
        in_specs=[row(d), row(dm.val), hist_spec] + [_const_spec(a) for a in w],
        out_specs=[pl.BlockSpec((tile * SUBLANES, LANES), lambda i: (i, 0)),
                   pl.BlockSpec((N_EXPERTS, tile), lambda i: (0, i)), hist_spec],
        out_shape=[jax.ShapeDtypeStruct((m * SUBLANES, LANES), F32),
                   jax.ShapeDtypeStruct((N_EXPERTS, m), F32),
                   jax.ShapeDtypeStruct((bsz, POOL_BUF, dm.pw), F32)],
        scratch_shapes=[pltpu.VMEM((tseq, POOL_WMAX + seq, dm.pw), F32)],
        compiler_params=cparams,
        name="sample_tail",
    )(x2, o, hist, *w)
    return h1, lg, st, buf


def _router_kernel(lg_ref, e_ref, w_ref, r_ref, c_ref, carry, *, tile):
    i = pl.program_id(0)

    @pl.when(i == 0)
    def _():
        carry[...] = jnp.zeros(carry.shape, F32)

    eio = lax.broadcasted_iota(jnp.int32, (N_EXPERTS, tile), 0)
    work = lg_ref[...]
    vals, hots = [], []
    for k in range(TOP_K):
        mx = jnp.max(work, axis=0, keepdims=True)
        idx = jnp.min(jnp.where(work == mx, eio, N_EXPERTS), axis=0, keepdims=True)
        hot = eio == idx
        vals.append(mx)
        hots.append(hot)
        e_ref[k:k + 1, :] = idx
        work = jnp.where(hot, -jnp.inf, work)
    ex = [jnp.exp(v - vals[0]) for v in vals]
    den = ex[0]
    for k in range(1, TOP_K):
        den = den + ex[k]
    for k in range(TOP_K):
        w_ref[k:k + 1, :] = ex[k] / den
    sel = hots[0]
    for k in range(1, TOP_K):
        sel = sel | hots[k]
    r = lax.broadcasted_iota(jnp.int32, (tile, tile), 0)
    c = lax.broadcasted_iota(jnp.int32, (tile, tile), 1)
    before = (r < c).astype(BF16)
    pref = _mm(sel.astype(BF16), before) + carry[:, 0:1]
    for k in range(TOP_K):
        r_ref[k:k + 1, :] = jnp.sum(jnp.where(hots[k], pref, 0.0), axis=0, keepdims=True).astype(jnp.int32)
    carry[...] = carry[...] + jnp.sum(sel.astype(F32), axis=1, keepdims=True)
    c_ref[...] = carry[...]


def _dispatch_kernel(fill_start_ref, fill_cnt_ref, tail_start_ref, dest_hbm, h1_ref, xs_hbm, idx_s, zeros_s,
                     sem_i, sem_r, sem_z, *, tile):
    i = pl.program_id(0)
    idx_cp = pltpu.make_async_copy(dest_hbm.at[i], idx_s, sem_i)
    idx_cp.start()

    @pl.when(i == 0)
    def _():
        zeros_s[...] = jnp.zeros(zeros_s.shape, F32)

        def fill_copy(off, bit):
            n = 1 << bit
            return pltpu.make_async_copy(zeros_s.at[pl.ds(0, n)], xs_hbm.at[pl.ds(off, n)], sem_z)

        def for_each_piece(fn):
            def body(e, carry):
                cnt = fill_cnt_ref[e]
                for bit in range(ZERO_FILL_BITS):
                    @pl.when(((cnt >> bit) & 1) == 1)
                    def _(bit=bit):
                        fn(fill_copy(fill_start_ref[e] + (cnt & ((1 << bit) - 1)), bit))
                return carry
            lax.fori_loop(0, N_EXPERTS, body, 0)

        top = ZERO_FILL_BITS - 1
        n_tail = (xs_hbm.shape[0] - tail_start_ref[0]) >> top

        def for_each_tail_piece(fn):
            def body(j, carry):
                fn(fill_copy(tail_start_ref[0] + (j << top), top))
                return carry
            lax.fori_loop(0, n_tail, body, 0)

        for_each_piece(lambda cp: cp.start())
        for_each_tail_piece(lambda cp: cp.start())
        for_each_piece(lambda cp: cp.wait())
        for_each_tail_piece(lambda cp: cp.wait())

    idx_cp.wait()

    def group(g, carry):
        for u in range(DMA_UNROLL):
            t = g * DMA_UNROLL + u
            src = h1_ref.at[pl.ds(pl.multiple_of(t * SUBLANES, SUBLANES), SUBLANES), :]
            for k in range(TOP_K):
                pltpu.make_async_copy(src, xs_hbm.at[idx_s[k * tile + t]], sem_r).start(priority=(u + k) % 2)
        return carry

    lax.fori_loop(0, tile // DMA_UNROLL, group, 0)
    for k in range(TOP_K):
        pltpu.make_async_copy(h1_ref, h1_ref, sem_r).wait()


def _expert_kernel(be_ref, nused_ref, xs_ref, wgu_ref, bgu_ref, wdn_ref, bdn_ref, out_ref, *, rows, d_ff):
    i = pl.program_id(0)

    @pl.when(i < nused_ref[0])
    def _():
        xb = _from_row_tiles(xs_ref, rows).astype(BF16)
        hh = _mm(xb, wgu_ref[0]) + bgu_ref[0]
        gate = jnp.minimum(hh[:, 0:d_ff], SWIGLU_LIMIT)
        up = jnp.clip(hh[:, d_ff:2 * d_ff], -SWIGLU_LIMIT, SWIGLU_LIMIT)
        act = (up + 1.0) * (gate * jax.nn.sigmoid(GLU_ALPHA * gate))
        res = _mm(act.astype(BF16), wdn_ref[0]) + bdn_ref[0]
        _to_row_tiles(out_ref, res, rows)

    @pl.when(i >= nused_ref[0])
    def _():
        out_ref[...] = jnp.zeros(out_ref.shape, F32)


def _combine_kernel(dest_hbm, res_hbm, h1_ref, gw_ref, g_ref, b_ref, y_ref, idx_s, gbuf, sem_i, sem_g,
                    *, tile, alpha, n_tiles):
    i = pl.program_id(0)
    slot = i % 2

    def idx_copy(blk, sl):
        return pltpu.make_async_copy(dest_hbm.at[blk], idx_s.at[sl], sem_i.at[sl])

    def start_rows(sl):
        def group(g, carry):
            for u in range(DMA_UNROLL):
                t = g * DMA_UNROLL + u
                for k in range(TOP_K):
                    dst = gbuf.at[sl, k, pl.ds(pl.multiple_of(t * SUBLANES, SUBLANES), SUBLANES), :]
                    pltpu.make_async_copy(res_hbm.at[idx_s[sl, k * tile + t]], dst,
                                          sem_g.at[sl]).start(priority=(u + k) % 2)
            return carry
        lax.fori_loop(0, tile // DMA_UNROLL, group, 0)

    def wait_rows(sl):
        pltpu.make_async_copy(gbuf.at[sl], gbuf.at[sl], sem_g.at[sl]).wait()

    @pl.when(i == 0)
    def _():
        first = idx_copy(0, 0)
        first.start()
        first.wait()
        start_rows(0)

        @pl.when(n_tiles > 1)
        def _():
            idx_copy(1, 1).start()

    @pl.when(i + 1 < n_tiles)
    def _():
        idx_copy(i + 1, 1 - slot).wait()
        start_rows(1 - slot)

    @pl.when(i + 2 < n_tiles)
    def _():
        idx_copy(i + 2, slot).start()

    wait_rows(slot)
    gw = gw_ref[...]
    z = alpha * _from_row_tiles(h1_ref, tile)
    for k in range(TOP_K):
        z = z + gw[:, k:k + 1] * _from_row_tiles(gbuf.at[slot, k], tile)
    y_ref[...] = _layer_norm(z, g_ref[...], b_ref[...])


def _tile_major(a, tile):
    k, m = a.shape
    return a.reshape(k, m // tile, tile).transpose(1, 0, 2).reshape(m // tile, k * tile)


def _moe(h1_tiles, logits_t, w_gu, b_gu, w_down, b_down, ln_g, ln_b, dm):
    m = logits_t.shape[1]
    rows = EXPERT_ROWS
    d_ff = w_down.shape[1]
    cparams = pltpu.CompilerParams(dimension_semantics=("arbitrary",), vmem_limit_bytes=VMEM_LIMIT_BYTES)

    rt = ROUTER_TILE
    kblk = lambda: pl.BlockSpec((TOP_K, rt), lambda i: (0, i))
    e_idx, gate_w, rank, counts = pl.pallas_call(
        functools.partial(_router_kernel, tile=rt),
        grid=(m // rt,),
        in_specs=[pl.BlockSpec((N_EXPERTS, rt), lambda i: (0, i))],
        out_specs=[kblk(), kblk(), kblk(), pl.BlockSpec((N_EXPERTS, LANES), lambda i: (0, 0))],
        out_shape=[jax.ShapeDtypeStruct((TOP_K, m), jnp.int32), jax.ShapeDtypeStruct((TOP_K, m), F32),
                   jax.ShapeDtypeStruct((TOP_K, m), jnp.int32), jax.ShapeDtypeStruct((N_EXPERTS, LANES), F32)],
        scratch_shapes=[pltpu.VMEM((N_EXPERTS, LANES), F32)],
        compiler_params=cparams,
        name="moe_router",
    )(logits_t)

    counts = counts[:, 0].astype(jnp.int32)
    padded = (counts + rows - 1) // rows * rows
    pad_end = jnp.cumsum(padded)
    pad_start = pad_end - padded
    n_blocks = -(-(m * TOP_K + N_EXPERTS * (rows - 1)) // rows)
    n_rows = n_blocks * rows
    block_e = jnp.minimum(jnp.searchsorted(pad_end, jnp.arange(n_blocks, dtype=jnp.int32) * rows, side='right'),
                          N_EXPERTS - 1).astype(jnp.int32)
    n_used = (pad_end[-1] // rows).astype(jnp.int32).reshape(1)
    experts = jnp.arange(N_EXPERTS, dtype=jnp.int32)[:, None, None]
    dest = rank + jnp.sum(jnp.where(e_idx[None] == experts, pad_start[:, None, None], 0), axis=0)

    tile = COMBINE_TILE
    n_tiles = m // tile
    dest_t = _tile_major(dest, tile)
    xs = pl.pallas_call(
        functools.partial(_dispatch_kernel, tile=tile),
        grid_spec=pltpu.PrefetchScalarGridSpec(
            num_scalar_prefetch=3,
            grid=(n_tiles,),
            in_specs=[pl.BlockSpec(memory_space=pl.ANY),
                      pl.BlockSpec((tile * SUBLANES, LANES), lambda i, fs, fc, ts: (i, 0))],
            out_specs=pl.BlockSpec(memory_space=pl.ANY),
            scratch_shapes=[pltpu.SMEM((TOP_K * tile,), jnp.int32),
                            pltpu.VMEM((1 << (ZERO_FILL_BITS - 1), SUBLANES, LANES), F32),
                            pltpu.SemaphoreType.DMA, pltpu.SemaphoreType.DMA, pltpu.SemaphoreType.DMA]),
        out_shape=jax.ShapeDtypeStruct((n_rows, SUBLANES, LANES), F32),
        compiler_params=cparams,
        name="moe_dispatch",
    )(pad_start + counts, padded - counts, pad_end[-1:], dest_t, h1_tiles)

    last = lambda i, nu: jnp.minimum(i, nu[0] - 1)
    res = pl.pallas_call(
        functools.partial(_expert_kernel, rows=rows, d_ff=d_ff),
        grid_spec=pltpu.PrefetchScalarGridSpec(
            num_scalar_prefetch=2,
            grid=(n_blocks,),
            in_specs=[pl.BlockSpec((rows * SUBLANES, LANES), lambda i, be, nu: (last(i, nu), 0)),
                      pl.BlockSpec((1, dm.d, 2 * d_ff), lambda i, be, nu: (be[i], 0, 0)),
                      pl.BlockSpec((1, 1, 2 * d_ff), lambda i, be, nu: (be[i], 0, 0)),
                      pl.BlockSpec((1, d_ff, dm.d), lambda i, be, nu: (be[i], 0, 0)),
                      pl.BlockSpec((1, 1, dm.d), lambda i, be, nu: (be[i], 0, 0))],
            out_specs=pl.BlockSpec((rows * SUBLANES, LANES), lambda i, be, nu: (i, 0))),
        out_shape=jax.ShapeDtypeStruct((n_rows * SUBLANES, LANES), F32),
        compiler_params=cparams,
        name="moe_experts",
    )(block_e, n_used, xs.reshape(n_rows * SUBLANES, LANES), w_gu.astype(BF16), b_gu[:, None, :],
      w_down.astype(BF16), b_down[:, None, :])

    y = pl.pallas_call(
        functools.partial(_combine_kernel, tile=tile, alpha=dm.alpha, n_tiles=n_tiles),
        grid=(n_tiles,),
        in_specs=[pl.BlockSpec(memory_space=pl.ANY),
                  pl.BlockSpec(memory_space=pl.ANY),
                  pl.BlockSpec((tile * SUBLANES, LANES), lambda i: (i, 0)),
                  pl.BlockSpec((tile, TOP_K), lambda i: (i, 0)),
                  pl.BlockSpec((1, dm.d), lambda i: (0, 0)),
                  pl.BlockSpec((1, dm.d), lambda i: (0, 0))],
        out_specs=pl.BlockSpec((tile, dm.d), lambda i: (i, 0)),
        out_shape=jax.ShapeDtypeStruct((m, dm.d), F32),
        scratch_shapes=[pltpu.SMEM((2, TOP_K * tile), jnp.int32),
                        pltpu.VMEM((2, TOP_K, tile * SUBLANES, LANES), F32),
                        pltpu.SemaphoreType.DMA((2,)),
                        pltpu.SemaphoreType.DMA((2,))],
        compiler_params=cparams,
        name="moe_combine",
    )(dest_t, res.reshape(n_rows, SUBLANES, LANES), h1_tiles, gate_w.T, ln_g[None, :], ln_b[None, :])
    return y


def _pad_cols(a, n):
    return jnp.pad(a, ((0, 0), (0, n - a.shape[1])))


def _mixer_weights(w_in, w_gk2, b_gk, gla_norm_w, w_branch_gla, w_pool_grp, pool_scale, w_branch_pool, b_gates,
                   w_out, ln_g, ln_b, w_router, b_router, dm):
    o_lr = 2 * dm.key + dm.val
    o_tail = o_lr + GATE_RANK
    return MixW(
        w_qkv=w_in[:, 0:o_lr].astype(BF16),
        w_lr=_pad_cols(w_in[:, o_lr:o_tail], LANES).astype(BF16),
        w_gk2=jnp.pad(w_gk2, ((0, LANES - GATE_RANK), (0, 0))).astype(BF16),
        b_gk=b_gk[None, :],
        w_tail=w_in[:, o_tail:].astype(BF16),
        gla_norm=gla_norm_w[None, :],
        w_ba=w_branch_gla.astype(BF16),
        w_pg=w_pool_grp.astype(BF16),
        pool_scale=pool_scale[None, :],
        w_bp=w_branch_pool.astype(BF16),
        b_gates=b_gates[None, :],
        w_out=w_out.astype(BF16),
        ln_g=ln_g[None, :],
        ln_b=ln_b[None, :],
        w_rt=_pad_cols(w_router, LANES).astype(BF16),
        b_rt=_pad_cols(b_router[None, :], LANES),
    )


def kernel(x_prompt, x_sample, state_gla, state_pool, w_in, w_gk2, b_gk, gla_norm_w, w_branch_gla, w_pool_grp,
           pool_scale, w_branch_pool, b_gates, w_out, ln1_g, ln1_b, w_router, b_router, w_gu, b_gu, w_down, b_down,
           ln2_g, ln2_b):
    depth = w_in.shape[0]
    bp, lp, d = x_prompt.shape
    bs, ls, _ = x_sample.shape
    m = bp * lp + bs * ls
    assert d == SUBLANES * LANES and lp % MIX_TILE == 0 and (bs * ls) % MIX_TILE == 0 and MIX_TILE % ls == 0
    assert bs % GLA_SEQS == 0 and ls % SUBLANES == 0 and m % COMBINE_TILE == 0 and m % ROUTER_TILE == 0
    dm = _dims(d, depth)
    mp = bp * lp
    yp, ys = x_prompt, x_sample
    gla_p, pool_p, gla_s, pool_s = [], [], [], []
    for l in range(depth):
        w = _mixer_weights(w_in[l], w_gk2[l], b_gk[l], gla_norm_w[l], w_branch_gla[l], w_pool_grp[l], pool_scale[l],
                           w_branch_pool[l], b_gates[l], w_out[l], ln1_g[l], ln1_b[l], w_router[l], b_router[l], dm)
        h1p, lgp, sp, bufp = _prompt_mixer(yp, w, dm)
        h1s, lgs, ss, bufs = _sample_mixer(ys, state_gla[l], state_pool[l], w, dm)
        h1 = jnp.concatenate([h1p, h1s], axis=0)
        logits_t = jnp.concatenate([lgp, lgs], axis=1)
        y = _moe(h1, logits_t, w_gu[l], b_gu[l], w_down[l], b_down[l], ln2_g[l], ln2_b[l], dm)
        yp = y[0:mp].reshape(bp, lp, d)
        ys = y[mp:].reshape(bs, ls, d)
        gla_p.append(sp.astype(state_gla.dtype))
        pool_p.append(bufp.astype(state_pool.dtype))
        gla_s.append(ss.astype(state_gla.dtype))
        pool_s.append(bufs.astype(state_pool.dtype))
    return (yp, ys, jnp.stack(gla_p, 0), jnp.stack(pool_p, 0), jnp.stack(gla_s, 0), jnp.stack(pool_s, 0))
```

```python
import functools
from typing import NamedTuple

import jax
import jax.numpy as jnp
from jax import lax
from jax.experimental import pallas as pl
from jax.experimental.pallas import tpu as pltpu

F32 = jnp.float32
BF16 = jnp.bfloat16

GLA_HEADS = 4
GATE_RANK = 16
GATE_NORMALIZER = 16.0
GLA_CHUNK = 64
RMS_EPS = 1e-6
POOL_WINDOWS = (2, 4, 8, 16)
POOL_WMAX = 16
POOL_BUF = POOL_WMAX - 1
N_EXPERTS = 32
TOP_K = 4
SWIGLU_LIMIT = 7.0
GLU_ALPHA = 1.702
LN_EPS = 1e-5

LANES = 128
SUBLANES = 8
VMEM_LIMIT_BYTES = 56 * 1024 * 1024

MIX_TILE = 256
GLA_SEQS = 8
ROUTER_TILE = 512
EXPERT_ROWS = 256
COMBINE_TILE = 256
CUMSUM_BLOCK = 256
DMA_UNROLL = 8
ZERO_FILL_BITS = (EXPERT_ROWS - 1).bit_length()


class Dims(NamedTuple):
    d: int
    dk: int
    dv: int
    key: int
    val: int
    pw: int
    pgc: int
    alpha: float


def _dims(d, depth):
    return Dims(d=d, dk=d // 8, dv=d // 4, key=d // 2, val=d, pw=d // 2, pgc=d // 8, alpha=(2.0 * depth) ** 0.25)


class MixW(NamedTuple):
    w_qkv: object
    w_lr: object
    w_gk2: object
    b_gk: object
    w_tail: object
    gla_norm: object
    w_ba: object
    w_pg: object
    pool_scale: object
    w_bp: object
    b_gates: object
    w_out: object
    ln_g: object
    ln_b: object
    w_rt: object
    b_rt: object


def _mm(a, b):
    return jnp.dot(a, b, preferred_element_type=F32)


def _layer_norm(z, g, b):
    mu = jnp.mean(z, axis=-1, keepdims=True)
    zc = z - mu
    var = jnp.mean(zc * zc, axis=-1, keepdims=True)
    return zc * lax.rsqrt(var + LN_EPS) * g + b


def _to_row_tiles(ref, val, rows):
    for c in range(SUBLANES):
        ref[pl.ds(c, rows, stride=SUBLANES), :] = val[:, c * LANES:(c + 1) * LANES]


def _from_row_tiles(ref, rows):
    return jnp.concatenate([ref[pl.ds(c, rows, stride=SUBLANES), :] for c in range(SUBLANES)], axis=1)


def _project_qkv(xb, w, dm, q_s, k_s, v_s, gl_s):
    q_s[...] = _mm(xb, w.w_qkv[:, 0:dm.key]) * (dm.dk ** -0.5)
    k_s[...] = _mm(xb, w.w_qkv[:, dm.key:2 * dm.key])
    v_s[...] = _mm(xb, w.w_qkv[:, 2 * dm.key:2 * dm.key + dm.val])
    lr = _mm(xb, w.w_lr[...])
    gk = _mm(lr.astype(BF16), w.w_gk2[...]) + w.b_gk[...]
    gl_s[...] = (jnp.minimum(gk, 0.0) - jnp.log1p(jnp.exp(-jnp.abs(gk)))) / GATE_NORMALIZER


def _gla(q_s, k_s, v_s, gl_s, b_s, o_s, st_ref, *, dm, n_chunks, chunk, chunks_per_seq):
    rows = n_chunks * chunk
    blk = min(rows, CUMSUM_BLOCK)
    r = lax.broadcasted_iota(jnp.int32, (blk, blk), 0)
    c = lax.broadcasted_iota(jnp.int32, (blk, blk), 1)
    tri = ((r // chunk == c // chunk) & (c <= r)).astype(BF16)
    for b0 in range(0, rows, blk):
        gl = gl_s[b0:b0 + blk, :]
        hi = gl.astype(BF16)
        lo = (gl - hi.astype(F32)).astype(BF16)
        b_s[b0:b0 + blk, :] = _mm(tri, hi) + _mm(tri, lo)

    ri = lax.broadcasted_iota(jnp.int32, (chunk, chunk), 0)
    ci = lax.broadcasted_iota(jnp.int32, (chunk, chunk), 1)
    causal = ci <= ri
    nt = (((1,), (1,)), ((), ()))
    tn = (((0,), (0,)), ((), ()))

    def step(idx, carry):
        r0 = pl.multiple_of(idx * chunk, chunk)
        seq = idx // chunks_per_seq
        for h in range(GLA_HEADS):
            ks = slice(h * dm.dk, (h + 1) * dm.dk)
            vs = slice(h * dm.dv, (h + 1) * dm.dv)
            bc = b_s[pl.ds(r0, chunk), ks]
            b_ref = bc[chunk // 2:chunk // 2 + 1, :]
            b_last = bc[chunk - 1:chunk, :]
            qc = q_s[pl.ds(r0, chunk), ks]
            kc = k_s[pl.ds(r0, chunk), ks]
            vc = v_s[pl.ds(r0, chunk), vs].astype(BF16)
            s_old = st_ref[seq, h]
            a = lax.dot_general((qc * jnp.exp(bc - b_ref)).astype(BF16), (kc * jnp.exp(b_ref - bc)).astype(BF16),
                                nt, preferred_element_type=F32)
            a = jnp.where(causal, a, 0.0)
            o = _mm(a.astype(BF16), vc) + _mm((qc * jnp.exp(bc)).astype(BF16), s_old.astype(BF16))
            o_s[pl.ds(r0, chunk), vs] = o
            e_col = jnp.broadcast_to(jnp.exp(b_last), (dm.dk, dm.dk)).T
            decay = jnp.concatenate([e_col] * (dm.dv // dm.dk), axis=1)
            kv = lax.dot_general((kc * jnp.exp(b_last - bc)).astype(BF16), vc, tn, preferred_element_type=F32)
            st_ref[seq, h] = decay * s_old + kv
        return carry

    lax.fori_loop(0, n_chunks, step, 0)


def _mixer_tail(x, xb, o_s, ext_s, pos0, w, dm, *, nseq, seq_len):
    t = nseq * seq_len
    off_u = dm.val
    off_ga = off_u + dm.pw
    off_gb = off_ga + dm.d

    g = _mm(xb, w.w_tail[:, 0:dm.val])
    parts = []
    for h in range(GLA_HEADS):
        vs = slice(h * dm.dv, (h + 1) * dm.dv)
        oh = o_s[:, vs]
        ms = jnp.mean(oh * oh, axis=-1, keepdims=True)
        on = oh * lax.rsqrt(ms + RMS_EPS) * w.gla_norm[...]
        gh = g[:, vs]
        parts.append((on * (gh * jax.nn.sigmoid(gh))).astype(BF16))
    branch_a = _mm(jnp.concatenate(parts, axis=1), w.w_ba[...])

    u = _mm(xb, w.w_tail[:, off_u:off_u + dm.pw])
    ext_s[:, POOL_WMAX:POOL_WMAX + seq_len, :] = u.reshape(nseq, seq_len, dm.pw)
    p = lax.broadcasted_iota(jnp.int32, (nseq, seq_len, dm.pgc), 1)
    pooled = []
    for gi, win in enumerate(POOL_WINDOWS):
        cs = slice(gi * dm.pgc, (gi + 1) * dm.pgc)
        cur = ext_s[:, POOL_WMAX:POOL_WMAX + seq_len, cs]
        acc = cur
        for j in range(1, win):
            acc = acc + ext_s[:, POOL_WMAX - j:POOL_WMAX - j + seq_len, cs]
        cnt = jnp.minimum(win, p + (pos0 + 1)).astype(F32)
        pg = (acc / cnt - cur).reshape(t, dm.pgc)
        pg = _mm(pg.astype(BF16), w.w_pg[gi]) * w.pool_scale[:, cs]
        pooled.append(pg.astype(BF16))
    branch_b = _mm(jnp.concatenate(pooled, axis=1), w.w_bp[...])

    gate_a = jax.nn.sigmoid(_mm(xb, w.w_tail[:, off_ga:off_ga + dm.d]) + w.b_gates[:, 0:dm.d])
    merged = gate_a * branch_a
    gate_b = jax.nn.sigmoid(_mm(xb, w.w_tail[:, off_gb:off_gb + dm.d]) + w.b_gates[:, dm.d:2 * dm.d])
    merged = merged + gate_b * branch_b
    mix = _mm(merged.astype(BF16), w.w_out[...])
    h1 = _layer_norm(dm.alpha * x + mix, w.ln_g[...], w.ln_b[...])
    logits = _mm(h1.astype(BF16), w.w_rt[...]) + w.b_rt[...]
    return h1, logits.T[0:N_EXPERTS, :]


N_MIXW = len(MixW._fields)


def _prompt_mixer_kernel(*refs, dm, tile, n_tiles):
    x_ref = refs[0]
    w = MixW(*refs[1:1 + N_MIXW])
    h1_ref, lg_ref, st_ref, buf_ref, q_s, k_s, v_s, gl_s, b_s, o_s, ext_s = refs[1 + N_MIXW:]
    lt = pl.program_id(1)

    @pl.when(lt == 0)
    def _():
        st_ref[...] = jnp.zeros(st_ref.shape, F32)
        ext_s[:, 0:POOL_WMAX, :] = jnp.zeros((1, POOL_WMAX, dm.pw), F32)

    x = x_ref[...]
    xb = x.astype(BF16)
    _project_qkv(xb, w, dm, q_s, k_s, v_s, gl_s)
    chunk = GLA_CHUNK
    _gla(q_s, k_s, v_s, gl_s, b_s, o_s, st_ref, dm=dm, n_chunks=tile // chunk, chunk=chunk,
         chunks_per_seq=tile // chunk)
    h1, logits_t = _mixer_tail(x, xb, o_s, ext_s, lt * tile, w, dm, nseq=1, seq_len=tile)
    _to_row_tiles(h1_ref, h1, tile)
    lg_ref[...] = logits_t
    ext_s[:, 0:POOL_WMAX, :] = ext_s[:, tile:tile + POOL_WMAX, :]

    @pl.when(lt == n_tiles - 1)
    def _():
        buf_ref[...] = ext_s[:, 1:POOL_WMAX, :]


def _sample_proj_kernel(*refs, dm):
    x_ref = refs[0]
    w = MixW(*refs[1:1 + N_MIXW])
    q_ref, k_ref, v_ref, gl_ref = refs[1 + N_MIXW:]
    _project_qkv(x_ref[...].astype(BF16), w, dm, q_ref, k_ref, v_ref, gl_ref)


def _sample_gla_kernel(q_ref, k_ref, v_ref, gl_ref, s0_ref, o_ref, st_ref, b_s, *, dm, nseq, seq_len):
    st_ref[...] = s0_ref[...]
    _gla(q_ref, k_ref, v_ref, gl_ref, b_s, o_ref, st_ref, dm=dm, n_chunks=nseq, chunk=seq_len, chunks_per_seq=1)


def _sample_tail_kernel(*refs, dm, nseq, seq_len):
    x_ref, o_ref, hist_ref = refs[0:3]
    w = MixW(*refs[3:3 + N_MIXW])
    h1_ref, lg_ref, buf_ref, ext_s = refs[3 + N_MIXW:]
    ext_s[:, 0:1, :] = jnp.zeros((nseq, 1, dm.pw), F32)
    ext_s[:, 1:POOL_WMAX, :] = hist_ref[...]
    x = x_ref[...]
    h1, logits_t = _mixer_tail(x, x.astype(BF16), o_ref, ext_s, POOL_BUF, w, dm, nseq=nseq, seq_len=seq_len)
    _to_row_tiles(h1_ref, h1, nseq * seq_len)
    lg_ref[...] = logits_t
    buf_ref[...] = ext_s[:, seq_len + 1:seq_len + POOL_WMAX, :]


def _const_spec(arr):
    nd = arr.ndim
    return pl.BlockSpec(arr.shape, lambda *_: (0,) * nd, pipeline_mode=pl.Buffered(1))


def _prompt_mixer(x, w, dm):
    bsz, seq, d = x.shape
    tile = MIX_TILE
    n_tiles = seq // tile
    m = bsz * seq
    scratch = [pltpu.VMEM((tile, dm.key), F32), pltpu.VMEM((tile, dm.key), F32), pltpu.VMEM((tile, dm.val), F32),
               pltpu.VMEM((tile, dm.key), F32), pltpu.VMEM((tile, dm.key), F32), pltpu.VMEM((tile, dm.val), F32),
               pltpu.VMEM((1, POOL_WMAX + tile, dm.pw), F32)]
    return pl.pallas_call(
        functools.partial(_prompt_mixer_kernel, dm=dm, tile=tile, n_tiles=n_tiles),
        grid=(bsz, n_tiles),
        in_specs=[pl.BlockSpec((tile, d), lambda b, t: (b * n_tiles + t, 0))] + [_const_spec(a) for a in w],
        out_specs=[pl.BlockSpec((tile * SUBLANES, LANES), lambda b, t: (b * n_tiles + t, 0)),
                   pl.BlockSpec((N_EXPERTS, tile), lambda b, t: (0, b * n_tiles + t)),
                   pl.BlockSpec((1, GLA_HEADS, dm.dk, dm.dv), lambda b, t: (b, 0, 0, 0)),
                   pl.BlockSpec((1, POOL_BUF, dm.pw), lambda b, t: (b, 0, 0))],
        out_shape=[jax.ShapeDtypeStruct((m * SUBLANES, LANES), F32),
                   jax.ShapeDtypeStruct((N_EXPERTS, m), F32),
                   jax.ShapeDtypeStruct((bsz, GLA_HEADS, dm.dk, dm.dv), F32),
                   jax.ShapeDtypeStruct((bsz, POOL_BUF, dm.pw), F32)],
        scratch_shapes=scratch,
        compiler_params=pltpu.CompilerParams(dimension_semantics=("arbitrary", "arbitrary"),
                                             vmem_limit_bytes=VMEM_LIMIT_BYTES),
        name="prompt_mixer",
    )(x.reshape(m, d), *w)


def _sample_mixer(x, s0, hist, w, dm):
    bsz, seq, d = x.shape
    m = bsz * seq
    x2 = x.reshape(m, d)
    tile = MIX_TILE
    cparams = pltpu.CompilerParams(dimension_semantics=("arbitrary",), vmem_limit_bytes=VMEM_LIMIT_BYTES)
    row = lambda n: pl.BlockSpec((tile, n), lambda i: (i, 0))
    q, k, v, gl = pl.pallas_call(
        functools.partial(_sample_proj_kernel, dm=dm),
        grid=(m // tile,),
        in_specs=[row(d)] + [_const_spec(a) for a in w],
        out_specs=[row(dm.key), row(dm.key), row(dm.val), row(dm.key)],
        out_shape=[jax.ShapeDtypeStruct((m, n), F32) for n in (dm.key, dm.key, dm.val, dm.key)],
        compiler_params=cparams,
        name="sample_proj",
    )(x2, *w)

    nseq = GLA_SEQS
    rows = nseq * seq
    grow = lambda n: pl.BlockSpec((rows, n), lambda i: (i, 0))
    st_spec = pl.BlockSpec((nseq, GLA_HEADS, dm.dk, dm.dv), lambda i: (i, 0, 0, 0))
    o, st = pl.pallas_call(
        functools.partial(_sample_gla_kernel, dm=dm, nseq=nseq, seq_len=seq),
        grid=(bsz // nseq,),
        in_specs=[grow(dm.key), grow(dm.key), grow(dm.val), grow(dm.key), st_spec],
        out_specs=[grow(dm.val), st_spec],
        out_shape=[jax.ShapeDtypeStruct((m, dm.val), F32), jax.ShapeDtypeStruct(s0.shape, F32)],
        scratch_shapes=[pltpu.VMEM((rows, dm.key), F32)],
        compiler_params=cparams,
        name="sample_gla",
    )(q, k, v, gl, s0)

    tseq = tile // seq
    hist_spec = pl.BlockSpec((tseq, POOL_BUF, dm.pw), lambda i: (i, 0, 0))
    h1, lg, buf = pl.pallas_call(
        functools.partial(_sample_tail_kernel, dm=dm, nseq=tseq, seq_len=seq),
        grid=(m // tile,),
        in_specs=[row(d), row(dm.val), hist_spec] + [_const_spec(a) for a in w],
        out_specs=[pl.BlockSpec((tile * SUBLANES, LANES), lambda i: (i, 0)),
                   pl.BlockSpec((N_EXPERTS, tile), lambda i: (0, i)), hist_spec],
        out_shape=[jax.ShapeDtypeStruct((m * SUBLANES, LANES), F32),
                   jax.ShapeDtypeStruct((N_EXPERTS, m), F32),
                   jax.ShapeDtypeStruct((bsz, POOL_BUF, dm.pw), F32)],
        scratch_shapes=[pltpu.VMEM((tseq, POOL_WMAX + seq, dm.pw), F32)],
        compiler_params=cparams,
        name="sample_tail",
    )(x2, o, hist, *w)
    return h1, lg, st, buf


def _router_kernel(lga_ref, lgb_ref, e_ref, w_ref, r_ref, c_ref, carry, *, tile, n_tiles_a):
    i = pl.program_id(0)

    @pl.when(i == 0)
    def _():
        carry[...] = jnp.zeros(carry.shape, F32)

    eio = lax.broadcasted_iota(jnp.int32, (N_EXPERTS, tile), 0)
    work = jnp.where(i < n_tiles_a, lga_ref[...], lgb_ref[...])
    vals, hots = [], []
    for k in range(TOP_K):
        mx = jnp.max(work, axis=0, keepdims=True)
        idx = jnp.min(jnp.where(work == mx, eio, N_EXPERTS), axis=0, keepdims=True)
        hot = eio == idx
        vals.append(mx)
        hots.append(hot)
        e_ref[k:k + 1, :] = idx
        work = jnp.where(hot, -jnp.inf, work)
    ex = [jnp.exp(v - vals[0]) for v in vals]
    den = ex[0]
    for k in range(1, TOP_K):
        den = den + ex[k]
    for k in range(TOP_K):
        w_ref[k:k + 1, :] = ex[k] / den
    sel = hots[0]
    for k in range(1, TOP_K):
        sel = sel | hots[k]
    r = lax.broadcasted_iota(jnp.int32, (tile, tile), 0)
    c = lax.broadcasted_iota(jnp.int32, (tile, tile), 1)
    before = (r < c).astype(BF16)
    pref = _mm(sel.astype(BF16), before) + carry[:, 0:1]
    for k in range(TOP_K):
        r_ref[k:k + 1, :] = jnp.sum(jnp.where(hots[k], pref, 0.0), axis=0, keepdims=True).astype(jnp.int32)
    carry[...] = carry[...] + jnp.sum(sel.astype(F32), axis=1, keepdims=True)
    c_ref[...] = carry[...]


def _dispatch_kernel(fill_start_ref, fill_cnt_ref, tail_start_ref, dest_hbm, h1a_ref, h1b_ref, xs_hbm, idx_s, zeros_s,
                     sem_i, sem_r, sem_z, *, tile, n_tiles_a):
    i = pl.program_id(0)
    idx_cp = pltpu.make_async_copy(dest_hbm.at[i], idx_s, sem_i)
    idx_cp.start()

    @pl.when(i == 0)
    def _():
        zeros_s[...] = jnp.zeros(zeros_s.shape, F32)

        def fill_copy(off, bit):
            n = 1 << bit
            return pltpu.make_async_copy(zeros_s.at[pl.ds(0, n)], xs_hbm.at[pl.ds(off, n)], sem_z)

        def for_each_piece(fn):
            def body(e, carry):
                cnt = fill_cnt_ref[e]
                for bit in range(ZERO_FILL_BITS):
                    @pl.when(((cnt >> bit) & 1) == 1)
                    def _(bit=bit):
                        fn(fill_copy(fill_start_ref[e] + (cnt & ((1 << bit) - 1)), bit))
                return carry
            lax.fori_loop(0, N_EXPERTS, body, 0)

        top = ZERO_FILL_BITS - 1
        n_tail = (xs_hbm.shape[0] - tail_start_ref[0]) >> top

        def for_each_tail_piece(fn):
            def body(j, carry):
                fn(fill_copy(tail_start_ref[0] + (j << top), top))
                return carry
            lax.fori_loop(0, n_tail, body, 0)

        for_each_piece(lambda cp: cp.start())
        for_each_tail_piece(lambda cp: cp.start())
        for_each_piece(lambda cp: cp.wait())
        for_each_tail_piece(lambda cp: cp.wait())

    idx_cp.wait()

    def start_rows(h1_ref):
        def group(g, carry):
            for u in range(DMA_UNROLL):
                t = g * DMA_UNROLL + u
                src = h1_ref.at[pl.ds(pl.multiple_of(t * SUBLANES, SUBLANES), SUBLANES), :]
                for k in range(TOP_K):
                    pltpu.make_async_copy(src, xs_hbm.at[idx_s[k * tile + t]], sem_r).start(priority=(u + k) % 2)
            return carry
        lax.fori_loop(0, tile // DMA_UNROLL, group, 0)

    @pl.when(i < n_tiles_a)
    def _():
        start_rows(h1a_ref)

    @pl.when(i >= n_tiles_a)
    def _():
        start_rows(h1b_ref)

    for k in range(TOP_K):
        pltpu.make_async_copy(h1a_ref, h1a_ref, sem_r).wait()


def _expert_kernel(be_ref, nused_ref, xs_ref, wgu_ref, bgu_ref, wdn_ref, bdn_ref, out_ref, wgu_b, wdn_b,
                   *, rows, d_ff):
    i = pl.program_id(0)

    @pl.when((i == 0) | (be_ref[i] != be_ref[jnp.maximum(i - 1, 0)]))
    def _():
        wgu_b[...] = wgu_ref[0].astype(BF16)
        wdn_b[...] = wdn_ref[0].astype(BF16)

    @pl.when(i < nused_ref[0])
    def _():
        xb = _from_row_tiles(xs_ref, rows).astype(BF16)
        hh = _mm(xb, wgu_b[...]) + bgu_ref[0]
        gate = jnp.minimum(hh[:, 0:d_ff], SWIGLU_LIMIT)
        up = jnp.clip(hh[:, d_ff:2 * d_ff], -SWIGLU_LIMIT, SWIGLU_LIMIT)
        act = (up + 1.0) * (gate * jax.nn.sigmoid(GLU_ALPHA * gate))
        res = _mm(act.astype(BF16), wdn_b[...]) + bdn_ref[0]
        _to_row_tiles(out_ref, res, rows)

    @pl.when(i >= nused_ref[0])
    def _():
        out_ref[...] = jnp.zeros(out_ref.shape, F32)


def _combine_kernel(dest_hbm, res_hbm, h1a_ref, h1b_ref, gw_ref, g_ref, b_ref, ya_ref, yb_ref, idx_s, gbuf,
                    sem_i, sem_g, *, tile, alpha, n_tiles, n_tiles_a):
    i = pl.program_id(0)
    slot = i % 2

    def idx_copy(blk, sl):
        return pltpu.make_async_copy(dest_hbm.at[blk], idx_s.at[sl], sem_i.at[sl])

    def start_rows(sl):
        def group(g, carry):
            for u in range(DMA_UNROLL):
                t = g * DMA_UNROLL + u
                for k in range(TOP_K):
                    dst = gbuf.at[sl, k, pl.ds(pl.multiple_of(t * SUBLANES, SUBLANES), SUBLANES), :]
                    pltpu.make_async_copy(res_hbm.at[idx_s[sl, k * tile + t]], dst,
                                          sem_g.at[sl]).start(priority=(u + k) % 2)
            return carry
        lax.fori_loop(0, tile // DMA_UNROLL, group, 0)

    def wait_rows(sl):
        pltpu.make_async_copy(gbuf.at[sl], gbuf.at[sl], sem_g.at[sl]).wait()

    @pl.when(i == 0)
    def _():
        first = idx_copy(0, 0)
        first.start()
        first.wait()
        start_rows(0)

        @pl.when(n_tiles > 1)
        def _():
            idx_copy(1, 1).start()

    @pl.when(i + 1 < n_tiles)
    def _():
        idx_copy(i + 1, 1 - slot).wait()
        start_rows(1 - slot)

    @pl.when(i + 2 < n_tiles)
    def _():
        idx_copy(i + 2, slot).start()

    wait_rows(slot)
    gw = gw_ref[...]
    z = alpha * jnp.where(i < n_tiles_a, _from_row_tiles(h1a_ref, tile), _from_row_tiles(h1b_ref, tile))
    for k in range(TOP_K):
        z = z + gw[:, k:k + 1] * _from_row_tiles(gbuf.at[slot, k], tile)
    y = _layer_norm(z, g_ref[...], b_ref[...])

    @pl.when(i < n_tiles_a)
    def _():
        ya_ref[...] = y

    @pl.when(i >= n_tiles_a)
    def _():
        yb_ref[...] = y


def _tile_major(a, tile):
    k, m = a.shape
    return a.reshape(k, m // tile, tile).transpose(1, 0, 2).reshape(m // tile, k * tile)


def _moe(h1_a, h1_b, logits_a, logits_b, w_gu, b_gu, w_down, b_down, ln_g, ln_b, dm):
    m_a = logits_a.shape[1]
    m = m_a + logits_b.shape[1]
    rows = EXPERT_ROWS
    d_ff = w_down.shape[1]
    cparams = pltpu.CompilerParams(dimension_semantics=("arbitrary",), vmem_limit_bytes=VMEM_LIMIT_BYTES)

    def two_group_specs(block, n_a, lane_axis):
        pick = (lambda j: (0, j)) if lane_axis else (lambda j: (j, 0))
        return [pl.BlockSpec(block, lambda i, *_: pick(jnp.minimum(i, n_a - 1))),
                pl.BlockSpec(block, lambda i, *_: pick(jnp.maximum(i - n_a, 0)))]

    rt = ROUTER_TILE
    kblk = lambda: pl.BlockSpec((TOP_K, rt), lambda i: (0, i))
    e_idx, gate_w, rank, counts = pl.pallas_call(
        functools.partial(_router_kernel, tile=rt, n_tiles_a=m_a // rt),
        grid=(m // rt,),
        in_specs=two_group_specs((N_EXPERTS, rt), m_a // rt, True),
        out_specs=[kblk(), kblk(), kblk(), pl.BlockSpec((N_EXPERTS, LANES), lambda i: (0, 0))],
        out_shape=[jax.ShapeDtypeStruct((TOP_K, m), jnp.int32), jax.ShapeDtypeStruct((TOP_K, m), F32),
                   jax.ShapeDtypeStruct((TOP_K, m), jnp.int32), jax.ShapeDtypeStruct((N_EXPERTS, LANES), F32)],
        scratch_shapes=[pltpu.VMEM((N_EXPERTS, LANES), F32)],
        compiler_params=cparams,
        name="moe_router",
    )(logits_a, logits_b)

    counts = counts[:, 0].astype(jnp.int32)
    padded = (counts + rows - 1) // rows * rows
    pad_end = jnp.cumsum(padded)
    pad_start = pad_end - padded
    n_blocks = -(-(m * TOP_K + N_EXPERTS * (rows - 1)) // rows)
    n_rows = n_blocks * rows
    block_row0 = jnp.arange(n_blocks, dtype=jnp.int32) * rows
    block_e = jnp.minimum(jnp.sum((pad_end[None, :] <= block_row0[:, None]).astype(jnp.int32), axis=1), N_EXPERTS - 1)
    n_used = (pad_end[-1] // rows).astype(jnp.int32).reshape(1)
    experts = jnp.arange(N_EXPERTS, dtype=jnp.int32)[:, None, None]
    dest = rank + jnp.sum(jnp.where(e_idx[None] == experts, pad_start[:, None, None], 0), axis=0)

    tile = COMBINE_TILE
    n_tiles = m // tile
    n_tiles_a = m_a // tile
    h1_specs = two_group_specs((tile * SUBLANES, LANES), n_tiles_a, False)
    dest_t = _tile_major(dest, tile)
    xs = pl.pallas_call(
        functools.partial(_dispatch_kernel, tile=tile, n_tiles_a=n_tiles_a),
        grid_spec=pltpu.PrefetchScalarGridSpec(
            num_scalar_prefetch=3,
            grid=(n_tiles,),
            in_specs=[pl.BlockSpec(memory_space=pl.ANY)] + h1_specs,
            out_specs=pl.BlockSpec(memory_space=pl.ANY),
            scratch_shapes=[pltpu.SMEM((TOP_K * tile,), jnp.int32),
                            pltpu.VMEM((1 << (ZERO_FILL_BITS - 1), SUBLANES, LANES), F32),
                            pltpu.SemaphoreType.DMA, pltpu.SemaphoreType.DMA, pltpu.SemaphoreType.DMA]),
        out_shape=jax.ShapeDtypeStruct((n_rows, SUBLANES, LANES), F32),
        compiler_params=cparams,
        name="moe_dispatch",
    )(pad_start + counts, padded - counts, pad_end[-1:], dest_t, h1_a, h1_b)

    last = lambda i, nu: jnp.minimum(i, nu[0] - 1)
    res = pl.pallas_call(
        functools.partial(_expert_kernel, rows=rows, d_ff=d_ff),
        grid_spec=pltpu.PrefetchScalarGridSpec(
            num_scalar_prefetch=2,
            grid=(n_blocks,),
            in_specs=[pl.BlockSpec((rows * SUBLANES, LANES), lambda i, be, nu: (last(i, nu), 0)),
                      pl.BlockSpec((1, dm.d, 2 * d_ff), lambda i, be, nu: (be[i], 0, 0)),
                      pl.BlockSpec((1, 1, 2 * d_ff), lambda i, be, nu: (be[i], 0, 0)),
                      pl.BlockSpec((1, d_ff, dm.d), lambda i, be, nu: (be[i], 0, 0)),
                      pl.BlockSpec((1, 1, dm.d), lambda i, be, nu: (be[i], 0, 0))],
            out_specs=pl.BlockSpec((rows * SUBLANES, LANES), lambda i, be, nu: (i, 0)),
            scratch_shapes=[pltpu.VMEM((dm.d, 2 * d_ff), BF16), pltpu.VMEM((d_ff, dm.d), BF16)]),
        out_shape=jax.ShapeDtypeStruct((n_rows * SUBLANES, LANES), F32),
        compiler_params=cparams,
        name="moe_experts",
    )(block_e, n_used, xs.reshape(n_rows * SUBLANES, LANES), w_gu, b_gu[:, None, :], w_down, b_down[:, None, :])

    ya, yb = pl.pallas_call(
        functools.partial(_combine_kernel, tile=tile, alpha=dm.alpha, n_tiles=n_tiles, n_tiles_a=n_tiles_a),
        grid=(n_tiles,),
        in_specs=[pl.BlockSpec(memory_space=pl.ANY),
                  pl.BlockSpec(memory_space=pl.ANY)] + h1_specs + [
                  pl.BlockSpec((tile, TOP_K), lambda i: (i, 0)),
                  pl.BlockSpec((1, dm.d), lambda i: (0, 0)),
                  pl.BlockSpec((1, dm.d), lambda i: (0, 0))],
        out_specs=[pl.BlockSpec((tile, dm.d), lambda i: (jnp.minimum(i, n_tiles_a - 1), 0)),
                   pl.BlockSpec((tile, dm.d), lambda i: (jnp.maximum(i - n_tiles_a, 0), 0))],
        out_shape=[jax.ShapeDtypeStruct((m_a, dm.d), F32), jax.ShapeDtypeStruct((m - m_a, dm.d), F32)],
        scratch_shapes=[pltpu.SMEM((2, TOP_K * tile), jnp.int32),
                        pltpu.VMEM((2, TOP_K, tile * SUBLANES, LANES), F32),
                        pltpu.SemaphoreType.DMA((2,)),
                        pltpu.SemaphoreType.DMA((2,))],
        compiler_params=cparams,
        name="moe_combine",
    )(dest_t, res.reshape(n_rows, SUBLANES, LANES), h1_a, h1_b, gate_w.T, ln_g[None, :], ln_b[None, :])
    return ya, yb


def _pad_cols(a, n):
    return jnp.pad(a, ((0, 0), (0, n - a.shape[1])))


def _mixer_weights(w_in, w_gk2, b_gk, gla_norm_w, w_branch_gla, w_pool_grp, pool_scale, w_branch_pool, b_gates,
                   w_out, ln_g, ln_b, w_router, b_router, dm):
    o_lr = 2 * dm.key + dm.val
    o_tail = o_lr + GATE_RANK
    return MixW(
        w_qkv=w_in[:, 0:o_lr].astype(BF16),
        w_lr=_pad_cols(w_in[:, o_lr:o_tail], LANES).astype(BF16),
        w_gk2=jnp.pad(w_gk2, ((0, LANES - GATE_RANK), (0, 0))).astype(BF16),
        b_gk=b_gk[None, :],
        w_tail=w_in[:, o_tail:].astype(BF16),
        gla_norm=gla_norm_w[None, :],
        w_ba=w_branch_gla.astype(BF16),
        w_pg=w_pool_grp.astype(BF16),
        pool_scale=pool_scale[None, :],
        w_bp=w_branch_pool.astype(BF16),
        b_gates=b_gates[None, :],
        w_out=w_out.astype(BF16),
        ln_g=ln_g[None, :],
        ln_b=ln_b[None, :],
        w_rt=_pad_cols(w_router, LANES).astype(BF16),
        b_rt=_pad_cols(b_router[None, :], LANES),
    )


def kernel(x_prompt, x_sample, state_gla, state_pool, w_in, w_gk2, b_gk, gla_norm_w, w_branch_gla, w_pool_grp,
           pool_scale, w_branch_pool, b_gates, w_out, ln1_g, ln1_b, w_router, b_router, w_gu, b_gu, w_down, b_down,
           ln2_g, ln2_b):
    depth = w_in.shape[0]
    bp, lp, d = x_prompt.shape
    bs, ls, _ = x_sample.shape
    m = bp * lp + bs * ls
    assert d == SUBLANES * LANES and lp % MIX_TILE == 0 and (bs * ls) % MIX_TILE == 0 and MIX_TILE % ls == 0
    assert bs % GLA_SEQS == 0 and ls % SUBLANES == 0
    assert all(n % t == 0 for n in (bp * lp, bs * ls) for t in (ROUTER_TILE, COMBINE_TILE))
    dm = _dims(d, depth)
    mp = bp * lp
    yp, ys = x_prompt, x_sample
    gla_p, pool_p, gla_s, pool_s = [], [], [], []
    for l in range(depth):
        w = _mixer_weights(w_in[l], w_gk2[l], b_gk[l], gla_norm_w[l], w_branch_gla[l], w_pool_grp[l], pool_scale[l],
                           w_branch_pool[l], b_gates[l], w_out[l], ln1_g[l], ln1_b[l], w_router[l], b_router[l], dm)
        h1p, lgp, sp, bufp = _prompt_mixer(yp, w, dm)
        h1s, lgs, ss, bufs = _sample_mixer(ys, state_gla[l], state_pool[l], w, dm)
        yp, ys = _moe(h1p, h1s, lgp, lgs, w_gu[l], b_gu[l], w_down[l], b_down[l], ln2_g[l], ln2_b[l], dm)
        yp = yp.reshape(bp, lp, d)
        ys = ys.reshape(bs, ls, d)
        gla_p.append(sp.astype(state_gla.dtype))
        pool_p.append(bufp.astype(state_pool.dtype))
        gla_s.append(ss.astype(state_gla.dtype))
        pool_s.append(bufs.astype(state_pool.dtype))
    return (yp, ys, jnp.stack(gla_p, 0), jnp.stack(pool_p, 0), jnp.stack(gla_s, 0), jnp.stack(pool_s, 0))
```

```python
import functools
from typing import NamedTuple

import jax
import jax.numpy as jnp
from jax import lax
from jax.experimental import pallas as pl
from jax.experimental.pallas import tpu as pltpu

F32 = jnp.float32
BF16 = jnp.bfloat16

GLA_HEADS = 4
GATE_RANK = 16
GATE_NORMALIZER = 16.0
GLA_CHUNK = 64
RMS_EPS = 1e-6
POOL_WINDOWS = (2, 4, 8, 16)
POOL_WMAX = 16
POOL_BUF = POOL_WMAX - 1
N_EXPERTS = 32
TOP_K = 4
SWIGLU_LIMIT = 7.0
GLU_ALPHA = 1.702
LN_EPS = 1e-5

LANES = 128
SUBLANES = 8
VMEM_LIMIT_BYTES = 56 * 1024 * 1024

PROMPT_TILE = 512
MIX_TILE = 256
GLA_GROUP = 128
GLA_SEQS = 8
ROUTER_TILE = 512
EXPERT_ROWS = 256
COMBINE_TILE = 256
CUMSUM_BLOCK = 256
DMA_UNROLL = 8
ZERO_FILL_BITS = (EXPERT_ROWS - 1).bit_length()


class Dims(NamedTuple):
    d: int
    dk: int
    dv: int
    key: int
    val: int
    pw: int
    pgc: int
    alpha: float


def _dims(d, depth):
    return Dims(d=d, dk=d // 8, dv=d // 4, key=d // 2, val=d, pw=d // 2, pgc=d // 8, alpha=(2.0 * depth) ** 0.25)


class MixW(NamedTuple):
    w_qkv: object
    w_lr: object
    w_gk2: object
    b_gk: object
    w_tail: object
    gla_norm: object
    w_ba: object
    w_pg: object
    pool_scale: object
    w_bp: object
    b_gates: object
    w_out: object
    ln_g: object
    ln_b: object
    w_rt: object
    b_rt: object


def _mm(a, b):
    return jnp.dot(a, b, preferred_element_type=F32)


def _layer_norm(z, g, b):
    mu = jnp.mean(z, axis=-1, keepdims=True)
    zc = z - mu
    var = jnp.mean(zc * zc, axis=-1, keepdims=True)
    return zc * lax.rsqrt(var + LN_EPS) * g + b


def _to_row_tiles(ref, val, rows):
    for c in range(SUBLANES):
        ref[pl.ds(c, rows, stride=SUBLANES), :] = val[:, c * LANES:(c + 1) * LANES]


def _from_row_tiles(ref, rows):
    return jnp.concatenate([ref[pl.ds(c, rows, stride=SUBLANES), :] for c in range(SUBLANES)], axis=1)


def _project_qkv(xb, w, dm, q_s, k_s, v_s, gl_s):
    q_s[...] = _mm(xb, w.w_qkv[:, 0:dm.key]) * (dm.dk ** -0.5)
    k_s[...] = _mm(xb, w.w_qkv[:, dm.key:2 * dm.key])
    v_s[...] = _mm(xb, w.w_qkv[:, 2 * dm.key:2 * dm.key + dm.val])
    lr = _mm(xb, w.w_lr[...])
    gk = _mm(lr.astype(BF16), w.w_gk2[...]) + w.b_gk[...]
    gl_s[...] = (jnp.minimum(gk, 0.0) - jnp.log1p(jnp.exp(-jnp.abs(gk)))) / GATE_NORMALIZER


def _chunk_cumsum(gl_s, b_s, rows, chunk):
    blk = min(rows, CUMSUM_BLOCK)
    r = lax.broadcasted_iota(jnp.int32, (blk, blk), 0)
    c = lax.broadcasted_iota(jnp.int32, (blk, blk), 1)
    tri = ((r // chunk == c // chunk) & (c <= r)).astype(BF16)
    for b0 in range(0, rows, blk):
        gl = gl_s[b0:b0 + blk, :]
        hi = gl.astype(BF16)
        lo = (gl - hi.astype(F32)).astype(BF16)
        b_s[b0:b0 + blk, :] = _mm(tri, hi) + _mm(tri, lo)


NT_DIMS = (((1,), (1,)), ((), ()))
TN_DIMS = (((0,), (0,)), ((), ()))


def _gla_tile(q_s, k_s, v_s, b_s, o_s, st_t, *, dm, rows, chunk, group):
    n_chunks = rows // chunk
    gr = lax.broadcasted_iota(jnp.int32, (group, group), 0)
    gc = lax.broadcasted_iota(jnp.int32, (group, group), 1)
    mask = (gr // chunk == gc // chunk) & (gc <= gr)

    def per_chunk_row(b, row):
        return jnp.concatenate([jnp.broadcast_to(b[c * chunk + row:c * chunk + row + 1, :], (chunk, b.shape[1]))
                                for c in range(n_chunks)], axis=0)

    for h in range(GLA_HEADS):
        ks = slice(h * dm.dk, (h + 1) * dm.dk)
        vs = slice(h * dm.dv, (h + 1) * dm.dv)
        b = b_s[:, ks]
        b_ref = per_chunk_row(b, chunk // 2)
        b_last = per_chunk_row(b, chunk - 1)
        q = q_s[:, ks]
        k = k_s[:, ks]
        vb = v_s[:, vs].astype(BF16)
        qa = (q * jnp.exp(b - b_ref)).astype(BF16)
        ka = (k * jnp.exp(b_ref - b)).astype(BF16)
        qe = (q * jnp.exp(b)).astype(BF16)
        kd = (k * jnp.exp(b_last - b)).astype(BF16)
        for g0 in range(0, rows, group):
            a = lax.dot_general(qa[g0:g0 + group], ka[g0:g0 + group], NT_DIMS, preferred_element_type=F32)
            a = jnp.where(mask, a, 0.0)
            o_s[g0:g0 + group, vs] = _mm(a.astype(BF16), vb[g0:g0 + group])
        s_t = st_t[h]
        for c in range(n_chunks):
            r0 = c * chunk
            o_s[r0:r0 + chunk, vs] += lax.dot_general(qe[r0:r0 + chunk], s_t.astype(BF16), NT_DIMS,
                                                      preferred_element_type=F32)
            decay = jnp.exp(b[r0 + chunk - 1:r0 + chunk, :])
            s_t = s_t * decay + lax.dot_general(vb[r0:r0 + chunk], kd[r0:r0 + chunk], TN_DIMS,
                                                preferred_element_type=F32)
        st_t[h] = s_t


def _gla(q_s, k_s, v_s, gl_s, b_s, o_s, st_ref, *, dm, n_chunks, chunk, chunks_per_seq):
    _chunk_cumsum(gl_s, b_s, n_chunks * chunk, chunk)
    ri = lax.broadcasted_iota(jnp.int32, (chunk, chunk), 0)
    ci = lax.broadcasted_iota(jnp.int32, (chunk, chunk), 1)
    causal = ci <= ri
    nt, tn = NT_DIMS, TN_DIMS

    def step(idx, carry):
        r0 = pl.multiple_of(idx * chunk, chunk)
        seq = idx // chunks_per_seq
        for h in range(GLA_HEADS):
            ks = slice(h * dm.dk, (h + 1) * dm.dk)
            vs = slice(h * dm.dv, (h + 1) * dm.dv)
            bc = b_s[pl.ds(r0, chunk), ks]
            b_ref = bc[chunk // 2:chunk // 2 + 1, :]
            b_last = bc[chunk - 1:chunk, :]
            qc = q_s[pl.ds(r0, chunk), ks]
            kc = k_s[pl.ds(r0, chunk), ks]
            vc = v_s[pl.ds(r0, chunk), vs].astype(BF16)
            s_old = st_ref[seq, h]
            a = lax.dot_general((qc * jnp.exp(bc - b_ref)).astype(BF16), (kc * jnp.exp(b_ref - bc)).astype(BF16),
                                nt, preferred_element_type=F32)
            a = jnp.where(causal, a, 0.0)
            o = _mm(a.astype(BF16), vc) + _mm((qc * jnp.exp(bc)).astype(BF16), s_old.astype(BF16))
            o_s[pl.ds(r0, chunk), vs] = o
            e_col = jnp.broadcast_to(jnp.exp(b_last), (dm.dk, dm.dk)).T
            decay = jnp.concatenate([e_col] * (dm.dv // dm.dk), axis=1)
            kv = lax.dot_general((kc * jnp.exp(b_last - bc)).astype(BF16), vc, tn, preferred_element_type=F32)
            st_ref[seq, h] = decay * s_old + kv
        return carry

    lax.fori_loop(0, n_chunks, step, 0)


def _mixer_tail(x, xb, o_s, ext_s, pos0, w, dm, *, nseq, seq_len):
    t = nseq * seq_len
    off_u = dm.val
    off_ga = off_u + dm.pw
    off_gb = off_ga + dm.d

    g = _mm(xb, w.w_tail[:, 0:dm.val])
    parts = []
    for h in range(GLA_HEADS):
        vs = slice(h * dm.dv, (h + 1) * dm.dv)
        oh = o_s[:, vs]
        ms = jnp.mean(oh * oh, axis=-1, keepdims=True)
        on = oh * lax.rsqrt(ms + RMS_EPS) * w.gla_norm[...]
        gh = g[:, vs]
        parts.append((on * (gh * jax.nn.sigmoid(gh))).astype(BF16))
    branch_a = _mm(jnp.concatenate(parts, axis=1), w.w_ba[...])

    u = _mm(xb, w.w_tail[:, off_u:off_u + dm.pw])
    ext_s[:, POOL_WMAX:POOL_WMAX + seq_len, :] = u.reshape(nseq, seq_len, dm.pw)
    p = lax.broadcasted_iota(jnp.int32, (nseq, seq_len, dm.pgc), 1)
    pooled = []
    for gi, win in enumerate(POOL_WINDOWS):
        cs = slice(gi * dm.pgc, (gi + 1) * dm.pgc)
        cur = ext_s[:, POOL_WMAX:POOL_WMAX + seq_len, cs]
        acc = cur
        for j in range(1, win):
            acc = acc + ext_s[:, POOL_WMAX - j:POOL_WMAX - j + seq_len, cs]
        cnt = jnp.minimum(win, p + (pos0 + 1)).astype(F32)
        pg = (acc / cnt - cur).reshape(t, dm.pgc)
        pg = _mm(pg.astype(BF16), w.w_pg[gi]) * w.pool_scale[:, cs]
        pooled.append(pg.astype(BF16))
    branch_b = _mm(jnp.concatenate(pooled, axis=1), w.w_bp[...])

    gate_a = jax.nn.sigmoid(_mm(xb, w.w_tail[:, off_ga:off_ga + dm.d]) + w.b_gates[:, 0:dm.d])
    merged = gate_a * branch_a
    gate_b = jax.nn.sigmoid(_mm(xb, w.w_tail[:, off_gb:off_gb + dm.d]) + w.b_gates[:, dm.d:2 * dm.d])
    merged = merged + gate_b * branch_b
    mix = _mm(merged.astype(BF16), w.w_out[...])
    h1 = _layer_norm(dm.alpha * x + mix, w.ln_g[...], w.ln_b[...])
    logits = _mm(h1.astype(BF16), w.w_rt[...]) + w.b_rt[...]
    return h1, logits.T[0:N_EXPERTS, :]


N_MIXW = len(MixW._fields)


def _prompt_mixer_kernel(*refs, dm, tile, n_tiles):
    x_ref = refs[0]
    w = MixW(*refs[1:1 + N_MIXW])
    h1_ref, lg_ref, st_ref, buf_ref, q_s, k_s, v_s, gl_s, b_s, o_s, ext_s, st_t = refs[1 + N_MIXW:]
    lt = pl.program_id(1)

    @pl.when(lt == 0)
    def _():
        st_t[...] = jnp.zeros(st_t.shape, F32)
        ext_s[:, 0:POOL_WMAX, :] = jnp.zeros((1, POOL_WMAX, dm.pw), F32)

    x = x_ref[...]
    xb = x.astype(BF16)
    _project_qkv(xb, w, dm, q_s, k_s, v_s, gl_s)
    _chunk_cumsum(gl_s, b_s, tile, GLA_CHUNK)
    _gla_tile(q_s, k_s, v_s, b_s, o_s, st_t, dm=dm, rows=tile, chunk=GLA_CHUNK, group=GLA_GROUP)

    @pl.when(lt == n_tiles - 1)
    def _():
        for h in range(GLA_HEADS):
            st_ref[0, h] = st_t[h].T

    h1, logits_t = _mixer_tail(x, xb, o_s, ext_s, lt * tile, w, dm, nseq=1, seq_len=tile)
    _to_row_tiles(h1_ref, h1, tile)
    lg_ref[...] = logits_t
    ext_s[:, 0:POOL_WMAX, :] = ext_s[:, tile:tile + POOL_WMAX, :]

    @pl.when(lt == n_tiles - 1)
    def _():
        buf_ref[...] = ext_s[:, 1:POOL_WMAX, :]


def _sample_proj_kernel(*refs, dm):
    x_ref = refs[0]
    w = MixW(*refs[1:1 + N_MIXW])
    q_ref, k_ref, v_ref, gl_ref = refs[1 + N_MIXW:]
    _project_qkv(x_ref[...].astype(BF16), w, dm, q_ref, k_ref, v_ref, gl_ref)


def _sample_gla_kernel(q_ref, k_ref, v_ref, gl_ref, s0_ref, o_ref, st_ref, b_s, *, dm, nseq, seq_len):
    st_ref[...] = s0_ref[...]
    _gla(q_ref, k_ref, v_ref, gl_ref, b_s, o_ref, st_ref, dm=dm, n_chunks=nseq, chunk=seq_len, chunks_per_seq=1)


def _sample_tail_kernel(*refs, dm, nseq, seq_len):
    x_ref, o_ref, hist_ref = refs[0:3]
    w = MixW(*refs[3:3 + N_MIXW])
    h1_ref, lg_ref, buf_ref, ext_s = refs[3 + N_MIXW:]
    ext_s[:, 0:1, :] = jnp.zeros((nseq, 1, dm.pw), F32)
    ext_s[:, 1:POOL_WMAX, :] = hist_ref[...]
    x = x_ref[...]
    h1, logits_t = _mixer_tail(x, x.astype(BF16), o_ref, ext_s, POOL_BUF, w, dm, nseq=nseq, seq_len=seq_len)
    _to_row_tiles(h1_ref, h1, nseq * seq_len)
    lg_ref[...] = logits_t
    buf_ref[...] = ext_s[:, seq_len + 1:seq_len + POOL_WMAX, :]


def _const_spec(arr):
    nd = arr.ndim
    return pl.BlockSpec(arr.shape, lambda *_: (0,) * nd, pipeline_mode=pl.Buffered(1))


def _prompt_mixer(x, w, dm):
    bsz, seq, d = x.shape
    tile = PROMPT_TILE
    n_tiles = seq // tile
    m = bsz * seq
    scratch = [pltpu.VMEM((tile, dm.key), F32), pltpu.VMEM((tile, dm.key), F32), pltpu.VMEM((tile, dm.val), F32),
               pltpu.VMEM((tile, dm.key), F32), pltpu.VMEM((tile, dm.key), F32), pltpu.VMEM((tile, dm.val), F32),
               pltpu.VMEM((1, POOL_WMAX + tile, dm.pw), F32),
               pltpu.VMEM((GLA_HEADS, dm.dv, dm.dk), F32)]
    return pl.pallas_call(
        functools.partial(_prompt_mixer_kernel, dm=dm, tile=tile, n_tiles=n_tiles),
        grid=(bsz, n_tiles),
        in_specs=[pl.BlockSpec((tile, d), lambda b, t: (b * n_tiles + t, 0))] + [_const_spec(a) for a in w],
        out_specs=[pl.BlockSpec((tile * SUBLANES, LANES), lambda b, t: (b * n_tiles + t, 0)),
                   pl.BlockSpec((N_EXPERTS, tile), lambda b, t: (0, b * n_tiles + t)),
                   pl.BlockSpec((1, GLA_HEADS, dm.dk, dm.dv), lambda b, t: (b, 0, 0, 0)),
                   pl.BlockSpec((1, POOL_BUF, dm.pw), lambda b, t: (b, 0, 0))],
        out_shape=[jax.ShapeDtypeStruct((m * SUBLANES, LANES), F32),
                   jax.ShapeDtypeStruct((N_EXPERTS, m), F32),
                   jax.ShapeDtypeStruct((bsz, GLA_HEADS, dm.dk, dm.dv), F32),
                   jax.ShapeDtypeStruct((bsz, POOL_BUF, dm.pw), F32)],
        scratch_shapes=scratch,
        compiler_params=pltpu.CompilerParams(dimension_semantics=("arbitrary", "arbitrary"),
                                             vmem_limit_bytes=VMEM_LIMIT_BYTES),
        name="prompt_mixer",
    )(x.reshape(m, d), *w)


def _sample_mixer(x, s0, hist, w, dm):
    bsz, seq, d = x.shape
    m = bsz * seq
    x2 = x.reshape(m, d)
    tile = MIX_TILE
    cparams = pltpu.CompilerParams(dimension_semantics=("arbitrary",), vmem_limit_bytes=VMEM_LIMIT_BYTES)
    row = lambda n: pl.BlockSpec((tile, n), lambda i: (i, 0))
    q, k, v, gl = pl.pallas_call(
        functools.partial(_sample_proj_kernel, dm=dm),
        grid=(m // tile,),
        in_specs=[row(d)] + [_const_spec(a) for a in w],
        out_specs=[row(dm.key), row(dm.key), row(dm.val), row(dm.key)],
        out_shape=[jax.ShapeDtypeStruct((m, n), F32) for n in (dm.key, dm.key, dm.val, dm.key)],
        compiler_params=cparams,
        name="sample_proj",
    )(x2, *w)

    nseq = GLA_SEQS
    rows = nseq * seq
    grow = lambda n: pl.BlockSpec((rows, n), lambda i: (i, 0))
    st_spec = pl.BlockSpec((nseq, GLA_HEADS, dm.dk, dm.dv), lambda i: (i, 0, 0, 0))
    o, st = pl.pallas_call(
        functools.partial(_sample_gla_kernel, dm=dm, nseq=nseq, seq_len=seq),
        grid=(bsz // nseq,),
        in_specs=[grow(dm.key), grow(dm.key), grow(dm.val), grow(dm.key), st_spec],
        out_specs=[grow(dm.val), st_spec],
        out_shape=[jax.ShapeDtypeStruct((m, dm.val), F32), jax.ShapeDtypeStruct(s0.shape, F32)],
        scratch_shapes=[pltpu.VMEM((rows, dm.key), F32)],
        compiler_params=cparams,
        name="sample_gla",
    )(q, k, v, gl, s0)

    tseq = tile // seq
    hist_spec = pl.BlockSpec((tseq, POOL_BUF, dm.pw), lambda i: (i, 0, 0))
    h1, lg, buf = pl.pallas_call(
        functools.partial(_sample_tail_kernel, dm=dm, nseq=tseq, seq_len=seq),
        grid=(m // tile,),
        in_specs=[row(d), row(dm.val), hist_spec] + [_const_spec(a) for a in w],
        out_specs=[pl.BlockSpec((tile * SUBLANES, LANES), lambda i: (i, 0)),
                   pl.BlockSpec((N_EXPERTS, tile), lambda i: (0, i)), hist_spec],
        out_shape=[jax.ShapeDtypeStruct((m * SUBLANES, LANES), F32),
                   jax.ShapeDtypeStruct((N_EXPERTS, m), F32),
                   jax.ShapeDtypeStruct((bsz, POOL_BUF, dm.pw), F32)],
        scratch_shapes=[pltpu.VMEM((tseq, POOL_WMAX + seq, dm.pw), F32)],
        compiler_params=cparams,
        name="sample_tail",
    )(x2, o, hist, *w)
    return h1, lg, st, buf


def _router_kernel(lga_ref, lgb_ref, e_ref, w_ref, r_ref, c_ref, carry, *, tile, n_tiles_a):
    i = pl.program_id(0)

    @pl.when(i == 0)
    def _():
        carry[...] = jnp.zeros(carry.shape, F32)

    eio = lax.broadcasted_iota(jnp.int32, (N_EXPERTS, tile), 0)
    work = jnp.where(i < n_tiles_a, lga_ref[...], lgb_ref[...])
    vals, hots = [], []
    for k in range(TOP_K):
        mx = jnp.max(work, axis=0, keepdims=True)
        idx = jnp.min(jnp.where(work == mx, eio, N_EXPERTS), axis=0, keepdims=True)
        hot = eio == idx
        vals.append(mx)
        hots.append(hot)
        e_ref[k:k + 1, :] = idx
        work = jnp.where(hot, -jnp.inf, work)
    ex = [jnp.exp(v - vals[0]) for v in vals]
    den = ex[0]
    for k in range(1, TOP_K):
        den = den + ex[k]
    for k in range(TOP_K):
        w_ref[k:k + 1, :] = ex[k] / den
    sel = hots[0]
    for k in range(1, TOP_K):
        sel = sel | hots[k]
    r = lax.broadcasted_iota(jnp.int32, (tile, tile), 0)
    c = lax.broadcasted_iota(jnp.int32, (tile, tile), 1)
    before = (r < c).astype(BF16)
    pref = _mm(sel.astype(BF16), before) + carry[:, 0:1]
    for k in range(TOP_K):
        r_ref[k:k + 1, :] = jnp.sum(jnp.where(hots[k], pref, 0.0), axis=0, keepdims=True).astype(jnp.int32)
    carry[...] = carry[...] + jnp.sum(sel.astype(F32), axis=1, keepdims=True)
    c_ref[...] = carry[...]


def _dispatch_kernel(fill_start_ref, fill_cnt_ref, tail_start_ref, dest_hbm, h1a_ref, h1b_ref, xs_hbm, idx_s, zeros_s,
                     sem_i, sem_r, sem_z, *, tile, n_tiles_a):
    i = pl.program_id(0)
    idx_cp = pltpu.make_async_copy(dest_hbm.at[i], idx_s, sem_i)
    idx_cp.start()

    @pl.when(i == 0)
    def _():
        zeros_s[...] = jnp.zeros(zeros_s.shape, F32)

        def fill_copy(off, bit):
            n = 1 << bit
            return pltpu.make_async_copy(zeros_s.at[pl.ds(0, n)], xs_hbm.at[pl.ds(off, n)], sem_z)

        def for_each_piece(fn):
            def body(e, carry):
                cnt = fill_cnt_ref[e]
                for bit in range(ZERO_FILL_BITS):
                    @pl.when(((cnt >> bit) & 1) == 1)
                    def _(bit=bit):
                        fn(fill_copy(fill_start_ref[e] + (cnt & ((1 << bit) - 1)), bit))
                return carry
            lax.fori_loop(0, N_EXPERTS, body, 0)

        top = ZERO_FILL_BITS - 1
        n_tail = (xs_hbm.shape[0] - tail_start_ref[0]) >> top

        def for_each_tail_piece(fn):
            def body(j, carry):
                fn(fill_copy(tail_start_ref[0] + (j << top), top))
                return carry
            lax.fori_loop(0, n_tail, body, 0)

        for_each_piece(lambda cp: cp.start())
        for_each_tail_piece(lambda cp: cp.start())
        for_each_piece(lambda cp: cp.wait())
        for_each_tail_piece(lambda cp: cp.wait())

    idx_cp.wait()

    def start_rows(h1_ref):
        def group(g, carry):
            for u in range(DMA_UNROLL):
                t = g * DMA_UNROLL + u
                src = h1_ref.at[pl.ds(pl.multiple_of(t * SUBLANES, SUBLANES), SUBLANES), :]
                for k in range(TOP_K):
                    pltpu.make_async_copy(src, xs_hbm.at[idx_s[k * tile + t]], sem_r).start(priority=(u + k) % 2)
            return carry
        lax.fori_loop(0, tile // DMA_UNROLL, group, 0)

    @pl.when(i < n_tiles_a)
    def _():
        start_rows(h1a_ref)

    @pl.when(i >= n_tiles_a)
    def _():
        start_rows(h1b_ref)

    for k in range(TOP_K):
        pltpu.make_async_copy(h1a_ref, h1a_ref, sem_r).wait()


def _expert_kernel(be_ref, nused_ref, xs_ref, wgu_ref, bgu_ref, wdn_ref, bdn_ref, out_ref, wgu_b, wdn_b,
                   *, rows, d_ff):
    i = pl.program_id(0)

    @pl.when((i == 0) | (be_ref[i] != be_ref[jnp.maximum(i - 1, 0)]))
    def _():
        wgu_b[...] = wgu_ref[0].astype(BF16)
        wdn_b[...] = wdn_ref[0].astype(BF16)

    @pl.when(i < nused_ref[0])
    def _():
        xb = _from_row_tiles(xs_ref, rows).astype(BF16)
        hh = _mm(xb, wgu_b[...]) + bgu_ref[0]
        gate = jnp.minimum(hh[:, 0:d_ff], SWIGLU_LIMIT)
        up = jnp.clip(hh[:, d_ff:2 * d_ff], -SWIGLU_LIMIT, SWIGLU_LIMIT)
        act = (up + 1.0) * (gate * jax.nn.sigmoid(GLU_ALPHA * gate))
        res = _mm(act.astype(BF16), wdn_b[...]) + bdn_ref[0]
        _to_row_tiles(out_ref, res, rows)

    @pl.when(i >= nused_ref[0])
    def _():
        out_ref[...] = jnp.zeros(out_ref.shape, F32)


def _combine_kernel(dest_hbm, res_hbm, h1a_ref, h1b_ref, gw_ref, g_ref, b_ref, ya_ref, yb_ref, idx_s, gbuf,
                    sem_i, sem_g, *, tile, alpha, n_tiles, n_tiles_a):
    i = pl.program_id(0)
    slot = i % 2

    def idx_copy(blk, sl):
        return pltpu.make_async_copy(dest_hbm.at[blk], idx_s.at[sl], sem_i.at[sl])

    def start_rows(sl):
        def group(g, carry):
            for u in range(DMA_UNROLL):
                t = g * DMA_UNROLL + u
                for k in range(TOP_K):
                    dst = gbuf.at[sl, k, pl.ds(pl.multiple_of(t * SUBLANES, SUBLANES), SUBLANES), :]
                    pltpu.make_async_copy(res_hbm.at[idx_s[sl, k * tile + t]], dst,
                                          sem_g.at[sl]).start(priority=(u + k) % 2)
            return carry
        lax.fori_loop(0, tile // DMA_UNROLL, group, 0)

    def wait_rows(sl):
        pltpu.make_async_copy(gbuf.at[sl], gbuf.at[sl], sem_g.at[sl]).wait()

    @pl.when(i == 0)
    def _():
        first = idx_copy(0, 0)
        first.start()
        first.wait()
        start_rows(0)

        @pl.when(n_tiles > 1)
        def _():
            idx_copy(1, 1).start()

    @pl.when(i + 1 < n_tiles)
    def _():
        idx_copy(i + 1, 1 - slot).wait()
        start_rows(1 - slot)

    @pl.when(i + 2 < n_tiles)
    def _():
        idx_copy(i + 2, slot).start()

    wait_rows(slot)
    gw = gw_ref[...]
    z = alpha * jnp.where(i < n_tiles_a, _from_row_tiles(h1a_ref, tile), _from_row_tiles(h1b_ref, tile))
    for k in range(TOP_K):
        z = z + gw[:, k:k + 1] * _from_row_tiles(gbuf.at[slot, k], tile)
    y = _layer_norm(z, g_ref[...], b_ref[...])

    @pl.when(i < n_tiles_a)
    def _():
        ya_ref[...] = y

    @pl.when(i >= n_tiles_a)
    def _():
        yb_ref[...] = y


def _tile_major(a, tile):
    k, m = a.shape
    return a.reshape(k, m // tile, tile).transpose(1, 0, 2).reshape(m // tile, k * tile)


def _moe(h1_a, h1_b, logits_a, logits_b, w_gu, b_gu, w_down, b_down, ln_g, ln_b, dm):
    m_a = logits_a.shape[1]
    m = m_a + logits_b.shape[1]
    rows = EXPERT_ROWS
    d_ff = w_down.shape[1]
    cparams = pltpu.CompilerParams(dimension_semantics=("arbitrary",), vmem_limit_bytes=VMEM_LIMIT_BYTES)

    def two_group_specs(block, n_a, lane_axis):
        pick = (lambda j: (0, j)) if lane_axis else (lambda j: (j, 0))
        return [pl.BlockSpec(block, lambda i, *_: pick(jnp.minimum(i, n_a - 1))),
                pl.BlockSpec(block, lambda i, *_: pick(jnp.maximum(i - n_a, 0)))]

    rt = ROUTER_TILE
    kblk = lambda: pl.BlockSpec((TOP_K, rt), lambda i: (0, i))
    e_idx, gate_w, rank, counts = pl.pallas_call(
        functools.partial(_router_kernel, tile=rt, n_tiles_a=m_a // rt),
        grid=(m // rt,),
        in_specs=two_group_specs((N_EXPERTS, rt), m_a // rt, True),
        out_specs=[kblk(), kblk(), kblk(), pl.BlockSpec((N_EXPERTS, LANES), lambda i: (0, 0))],
        out_shape=[jax.ShapeDtypeStruct((TOP_K, m), jnp.int32), jax.ShapeDtypeStruct((TOP_K, m), F32),
                   jax.ShapeDtypeStruct((TOP_K, m), jnp.int32), jax.ShapeDtypeStruct((N_EXPERTS, LANES), F32)],
        scratch_shapes=[pltpu.VMEM((N_EXPERTS, LANES), F32)],
        compiler_params=cparams,
        name="moe_router",
    )(logits_a, logits_b)

    counts = counts[:, 0].astype(jnp.int32)
    padded = (counts + rows - 1) // rows * rows
    pad_end = jnp.cumsum(padded)
    pad_start = pad_end - padded
    n_blocks = -(-(m * TOP_K + N_EXPERTS * (rows - 1)) // rows)
    n_rows = n_blocks * rows
    block_row0 = jnp.arange(n_blocks, dtype=jnp.int32) * rows
    block_e = jnp.minimum(jnp.sum((pad_end[None, :] <= block_row0[:, None]).astype(jnp.int32), axis=1), N_EXPERTS - 1)
    n_used = (pad_end[-1] // rows).astype(jnp.int32).reshape(1)
    experts = jnp.arange(N_EXPERTS, dtype=jnp.int32)[:, None, None]
    dest = rank + jnp.sum(jnp.where(e_idx[None] == experts, pad_start[:, None, None], 0), axis=0)

    tile = COMBINE_TILE
    n_tiles = m // tile
    n_tiles_a = m_a // tile
    h1_specs = two_group_specs((tile * SUBLANES, LANES), n_tiles_a, False)
    dest_t = _tile_major(dest, tile)
    xs = pl.pallas_call(
        functools.partial(_dispatch_kernel, tile=tile, n_tiles_a=n_tiles_a),
        grid_spec=pltpu.PrefetchScalarGridSpec(
            num_scalar_prefetch=3,
            grid=(n_tiles,),
            in_specs=[pl.BlockSpec(memory_space=pl.ANY)] + h1_specs,
            out_specs=pl.BlockSpec(memory_space=pl.ANY),
            scratch_shapes=[pltpu.SMEM((TOP_K * tile,), jnp.int32),
                            pltpu.VMEM((1 << (ZERO_FILL_BITS - 1), SUBLANES, LANES), F32),
                            pltpu.SemaphoreType.DMA, pltpu.SemaphoreType.DMA, pltpu.SemaphoreType.DMA]),
        out_shape=jax.ShapeDtypeStruct((n_rows, SUBLANES, LANES), F32),
        compiler_params=cparams,
        name="moe_dispatch",
    )(pad_start + counts, padded - counts, pad_end[-1:], dest_t, h1_a, h1_b)

    last = lambda i, nu: jnp.minimum(i, nu[0] - 1)
    res = pl.pallas_call(
        functools.partial(_expert_kernel, rows=rows, d_ff=d_ff),
        grid_spec=pltpu.PrefetchScalarGridSpec(
            num_scalar_prefetch=2,
            grid=(n_blocks,),
            in_specs=[pl.BlockSpec((rows * SUBLANES, LANES), lambda i, be, nu: (last(i, nu), 0)),
                      pl.BlockSpec((1, dm.d, 2 * d_ff), lambda i, be, nu: (be[i], 0, 0)),
                      pl.BlockSpec((1, 1, 2 * d_ff), lambda i, be, nu: (be[i], 0, 0)),
                      pl.BlockSpec((1, d_ff, dm.d), lambda i, be, nu: (be[i], 0, 0)),
                      pl.BlockSpec((1, 1, dm.d), lambda i, be, nu: (be[i], 0, 0))],
            out_specs=pl.BlockSpec((rows * SUBLANES, LANES), lambda i, be, nu: (i, 0)),
            scratch_shapes=[pltpu.VMEM((dm.d, 2 * d_ff), BF16), pltpu.VMEM((d_ff, dm.d), BF16)]),
        out_shape=jax.ShapeDtypeStruct((n_rows * SUBLANES, LANES), F32),
        compiler_params=cparams,
        name="moe_experts",
    )(block_e, n_used, xs.reshape(n_rows * SUBLANES, LANES), w_gu, b_gu[:, None, :], w_down, b_down[:, None, :])

    ya, yb = pl.pallas_call(
        functools.partial(_combine_kernel, tile=tile, alpha=dm.alpha, n_tiles=n_tiles, n_tiles_a=n_tiles_a),
        grid=(n_tiles,),
        in_specs=[pl.BlockSpec(memory_space=pl.ANY),
                  pl.BlockSpec(memory_space=pl.ANY)] + h1_specs + [
                  pl.BlockSpec((tile, TOP_K), lambda i: (i, 0)),
                  pl.BlockSpec((1, dm.d), lambda i: (0, 0)),
                  pl.BlockSpec((1, dm.d), lambda i: (0, 0))],
        out_specs=[pl.BlockSpec((tile, dm.d), lambda i: (jnp.minimum(i, n_tiles_a - 1), 0)),
                   pl.BlockSpec((tile, dm.d), lambda i: (jnp.maximum(i - n_tiles_a, 0), 0))],
        out_shape=[jax.ShapeDtypeStruct((m_a, dm.d), F32), jax.ShapeDtypeStruct((m - m_a, dm.d), F32)],
        scratch_shapes=[pltpu.SMEM((2, TOP_K * tile), jnp.int32),
                        pltpu.VMEM((2, TOP_K, tile * SUBLANES, LANES), F32),
                        pltpu.SemaphoreType.DMA((2,)),
                        pltpu.SemaphoreType.DMA((2,))],
        compiler_params=cparams,
        name="moe_combine",
    )(dest_t, res.reshape(n_rows, SUBLANES, LANES), h1_a, h1_b, gate_w.T, ln_g[None, :], ln_b[None, :])
    return ya, yb


def _pad_cols(a, n):
    return jnp.pad(a, ((0, 0), (0, n - a.shape[1])))


def _mixer_weights(w_in, w_gk2, b_gk, gla_norm_w, w_branch_gla, w_pool_grp, pool_scale, w_branch_pool, b_gates,
                   w_out, ln_g, ln_b, w_router, b_router, dm):
    o_lr = 2 * dm.key + dm.val
    o_tail = o_lr + GATE_RANK
    return MixW(
        w_qkv=w_in[:, 0:o_lr].astype(BF16),
        w_lr=_pad_cols(w_in[:, o_lr:o_tail], LANES).astype(BF16),
        w_gk2=jnp.pad(w_gk2, ((0, LANES - GATE_RANK), (0, 0))).astype(BF16),
        b_gk=b_gk[None, :],
        w_tail=w_in[:, o_tail:].astype(BF16),
        gla_norm=gla_norm_w[None, :],
        w_ba=w_branch_gla.astype(BF16),
        w_pg=w_pool_grp.astype(BF16),
        pool_scale=pool_scale[None, :],
        w_bp=w_branch_pool.astype(BF16),
        b_gates=b_gates[None, :],
        w_out=w_out.astype(BF16),
        ln_g=ln_g[None, :],
        ln_b=ln_b[None, :],
        w_rt=_pad_cols(w_router, LANES).astype(BF16),
        b_rt=_pad_cols(b_router[None, :], LANES),
    )


def kernel(x_prompt, x_sample, state_gla, state_pool, w_in, w_gk2, b_gk, gla_norm_w, w_branch_gla, w_pool_grp,
           pool_scale, w_branch_pool, b_gates, w_out, ln1_g, ln1_b, w_router, b_router, w_gu, b_gu, w_down, b_down,
           ln2_g, ln2_b):
    depth = w_in.shape[0]
    bp, lp, d = x_prompt.shape
    bs, ls, _ = x_sample.shape
    m = bp * lp + bs * ls
    assert d == SUBLANES * LANES and lp % PROMPT_TILE == 0 and (bs * ls) % MIX_TILE == 0 and MIX_TILE % ls == 0
    assert bs % GLA_SEQS == 0 and ls % SUBLANES == 0
    assert all(n % t == 0 for n in (bp * lp, bs * ls) for t in (ROUTER_TILE, COMBINE_TILE))
    dm = _dims(d, depth)
    mp = bp * lp
    yp, ys = x_prompt, x_sample
    gla_p, pool_p, gla_s, pool_s = [], [], [], []
    for l in range(depth):
        w = _mixer_weights(w_in[l], w_gk2[l], b_gk[l], gla_norm_w[l], w_branch_gla[l], w_pool_grp[l], pool_scale[l],
                           w_branch_pool[l], b_gates[l], w_out[l], ln1_g[l], ln1_b[l], w_router[l], b_router[l], dm)
        h1p, lgp, sp, bufp = _prompt_mixer(yp, w, dm)
        h1s, lgs, ss, bufs = _sample_mixer(ys, state_gla[l], state_pool[l], w, dm)
        yp, ys = _moe(h1p, h1s, lgp, lgs, w_gu[l], b_gu[l], w_down[l], b_down[l], ln2_g[l], ln2_b[l], dm)
        yp = yp.reshape(bp, lp, d)
        ys = ys.reshape(bs, ls, d)
        gla_p.append(sp.astype(state_gla.dtype))
        pool_p.append(bufp.astype(state_pool.dtype))
        gla_s.append(ss.astype(state_gla.dtype))
        pool_s.append(bufs.astype(state_pool.dtype))
    return (yp, ys, jnp.stack(gla_p, 0), jnp.stack(pool_p, 0), jnp.stack(gla_s, 0), jnp.stack(pool_s, 0))
```

```python
import functools
from typing import NamedTuple

import jax
import jax.numpy as jnp
from jax import lax
from jax.experimental import pallas as pl
from jax.experimental.pallas import tpu as pltpu

F32 = jnp.float32
BF16 = jnp.bfloat16

GLA_HEADS = 4
GATE_RANK = 16
GATE_NORMALIZER = 16.0
GLA_CHUNK = 64
RMS_EPS = 1e-6
POOL_WINDOWS = (2, 4, 8, 16)
POOL_WMAX = 16
POOL_BUF = POOL_WMAX - 1
N_EXPERTS = 32
TOP_K = 4
SWIGLU_LIMIT = 7.0
GLU_ALPHA = 1.702
LN_EPS = 1e-5

LANES = 128
SUBLANES = 8
VMEM_LIMIT_BYTES = 56 * 1024 * 1024

PROMPT_TILE = 512
MIX_TILE = 256
GLA_GROUP = 128
GLA_SEQS = 8
ROUTER_TILE = 512
EXPERT_ROWS = 512
COMBINE_TILE = 256
CUMSUM_BLOCK = 256
DMA_UNROLL = 8
ZERO_FILL_BITS = (EXPERT_ROWS - 1).bit_length()


class Dims(NamedTuple):
    d: int
    dk: int
    dv: int
    key: int
    val: int
    pw: int
    pgc: int
    alpha: float


def _dims(d, depth):
    return Dims(d=d, dk=d // 8, dv=d // 4, key=d // 2, val=d, pw=d // 2, pgc=d // 8, alpha=(2.0 * depth) ** 0.25)


class MixW(NamedTuple):
    w_qkv: object
    w_lr: object
    w_gk2: object
    b_gk: object
    w_tail: object
    gla_norm: object
    w_ba: object
    w_pg: object
    pool_scale: object
    w_bp: object
    b_gates: object
    w_out: object
    ln_g: object
    ln_b: object
    w_rt: object
    b_rt: object


def _mm(a, b):
    return jnp.dot(a, b, preferred_element_type=F32)


def _layer_norm(z, g, b):
    mu = jnp.mean(z, axis=-1, keepdims=True)
    zc = z - mu
    var = jnp.mean(zc * zc, axis=-1, keepdims=True)
    return zc * lax.rsqrt(var + LN_EPS) * g + b


def _to_row_tiles(ref, val, rows):
    for c in range(SUBLANES):
        ref[pl.ds(c, rows, stride=SUBLANES), :] = val[:, c * LANES:(c + 1) * LANES]


def _from_row_tiles(ref, rows):
    return jnp.concatenate([ref[pl.ds(c, rows, stride=SUBLANES), :] for c in range(SUBLANES)], axis=1)


def _project_qkv(xb, w, dm, q_s, k_s, v_s, gl_s):
    q_s[...] = _mm(xb, w.w_qkv[:, 0:dm.key]) * (dm.dk ** -0.5)
    k_s[...] = _mm(xb, w.w_qkv[:, dm.key:2 * dm.key])
    v_s[...] = _mm(xb, w.w_qkv[:, 2 * dm.key:2 * dm.key + dm.val])
    lr = _mm(xb, w.w_lr[...])
    gk = _mm(lr.astype(BF16), w.w_gk2[...]) + w.b_gk[...]
    gl_s[...] = (jnp.minimum(gk, 0.0) - jnp.log1p(jnp.exp(-jnp.abs(gk)))) / GATE_NORMALIZER


def _chunk_cumsum(gl_s, b_s, rows, chunk):
    blk = min(rows, CUMSUM_BLOCK)
    r = lax.broadcasted_iota(jnp.int32, (blk, blk), 0)
    c = lax.broadcasted_iota(jnp.int32, (blk, blk), 1)
    tri = ((r // chunk == c // chunk) & (c <= r)).astype(BF16)
    for b0 in range(0, rows, blk):
        gl = gl_s[b0:b0 + blk, :]
        hi = gl.astype(BF16)
        lo = (gl - hi.astype(F32)).astype(BF16)
        b_s[b0:b0 + blk, :] = _mm(tri, hi) + _mm(tri, lo)


NT_DIMS = (((1,), (1,)), ((), ()))
TN_DIMS = (((0,), (0,)), ((), ()))


def _gla_tile(q_s, k_s, v_s, b_s, o_s, st_t, *, dm, rows, chunk, group):
    n_chunks = rows // chunk
    gr = lax.broadcasted_iota(jnp.int32, (group, group), 0)
    gc = lax.broadcasted_iota(jnp.int32, (group, group), 1)
    mask = (gr // chunk == gc // chunk) & (gc <= gr)

    def per_chunk_row(b, row):
        return jnp.concatenate([jnp.broadcast_to(b[c * chunk + row:c * chunk + row + 1, :], (chunk, b.shape[1]))
                                for c in range(n_chunks)], axis=0)

    for h in range(GLA_HEADS):
        ks = slice(h * dm.dk, (h + 1) * dm.dk)
        vs = slice(h * dm.dv, (h + 1) * dm.dv)
        b = b_s[:, ks]
        b_ref = per_chunk_row(b, chunk // 2)
        b_last = per_chunk_row(b, chunk - 1)
        q = q_s[:, ks]
        k = k_s[:, ks]
        vb = v_s[:, vs].astype(BF16)
        qa = (q * jnp.exp(b - b_ref)).astype(BF16)
        ka = (k * jnp.exp(b_ref - b)).astype(BF16)
        qe = (q * jnp.exp(b)).astype(BF16)
        kd = (k * jnp.exp(b_last - b)).astype(BF16)
        for g0 in range(0, rows, group):
            a = lax.dot_general(qa[g0:g0 + group], ka[g0:g0 + group], NT_DIMS, preferred_element_type=F32)
            a = jnp.where(mask, a, 0.0)
            o_s[g0:g0 + group, vs] = _mm(a.astype(BF16), vb[g0:g0 + group])
        s_t = st_t[h]
        for c in range(n_chunks):
            r0 = c * chunk
            o_s[r0:r0 + chunk, vs] += lax.dot_general(qe[r0:r0 + chunk], s_t.astype(BF16), NT_DIMS,
                                                      preferred_element_type=F32)
            decay = jnp.exp(b[r0 + chunk - 1:r0 + chunk, :])
            s_t = s_t * decay + lax.dot_general(vb[r0:r0 + chunk], kd[r0:r0 + chunk], TN_DIMS,
                                                preferred_element_type=F32)
        st_t[h] = s_t


def _gla(q_s, k_s, v_s, gl_s, b_s, o_s, st_ref, *, dm, n_chunks, chunk, chunks_per_seq):
    _chunk_cumsum(gl_s, b_s, n_chunks * chunk, chunk)
    ri = lax.broadcasted_iota(jnp.int32, (chunk, chunk), 0)
    ci = lax.broadcasted_iota(jnp.int32, (chunk, chunk), 1)
    causal = ci <= ri
    nt, tn = NT_DIMS, TN_DIMS

    def step(idx, carry):
        r0 = pl.multiple_of(idx * chunk, chunk)
        seq = idx // chunks_per_seq
        for h in range(GLA_HEADS):
            ks = slice(h * dm.dk, (h + 1) * dm.dk)
            vs = slice(h * dm.dv, (h + 1) * dm.dv)
            bc = b_s[pl.ds(r0, chunk), ks]
            b_ref = bc[chunk // 2:chunk // 2 + 1, :]
            b_last = bc[chunk - 1:chunk, :]
            qc = q_s[pl.ds(r0, chunk), ks]
            kc = k_s[pl.ds(r0, chunk), ks]
            vc = v_s[pl.ds(r0, chunk), vs].astype(BF16)
            s_old = st_ref[seq, h]
            a = lax.dot_general((qc * jnp.exp(bc - b_ref)).astype(BF16), (kc * jnp.exp(b_ref - bc)).astype(BF16),
                                nt, preferred_element_type=F32)
            a = jnp.where(causal, a, 0.0)
            o = _mm(a.astype(BF16), vc) + _mm((qc * jnp.exp(bc)).astype(BF16), s_old.astype(BF16))
            o_s[pl.ds(r0, chunk), vs] = o
            e_col = jnp.broadcast_to(jnp.exp(b_last), (dm.dk, dm.dk)).T
            decay = jnp.concatenate([e_col] * (dm.dv // dm.dk), axis=1)
            kv = lax.dot_general((kc * jnp.exp(b_last - bc)).astype(BF16), vc, tn, preferred_element_type=F32)
            st_ref[seq, h] = decay * s_old + kv
        return carry

    lax.fori_loop(0, n_chunks, step, 0)


def _mixer_tail(x, xb, o_s, ext_s, pos0, w, dm, *, nseq, seq_len):
    t = nseq * seq_len
    off_u = dm.val
    off_ga = off_u + dm.pw
    off_gb = off_ga + dm.d

    g = _mm(xb, w.w_tail[:, 0:dm.val])
    parts = []
    for h in range(GLA_HEADS):
        vs = slice(h * dm.dv, (h + 1) * dm.dv)
        oh = o_s[:, vs]
        ms = jnp.mean(oh * oh, axis=-1, keepdims=True)
        on = oh * lax.rsqrt(ms + RMS_EPS) * w.gla_norm[...]
        gh = g[:, vs]
        parts.append((on * (gh * jax.nn.sigmoid(gh))).astype(BF16))
    branch_a = _mm(jnp.concatenate(parts, axis=1), w.w_ba[...])

    u = _mm(xb, w.w_tail[:, off_u:off_u + dm.pw])
    ext_s[:, POOL_WMAX:POOL_WMAX + seq_len, :] = u.reshape(nseq, seq_len, dm.pw)
    p = lax.broadcasted_iota(jnp.int32, (nseq, seq_len, dm.pgc), 1)
    pooled = []
    for gi, win in enumerate(POOL_WINDOWS):
        cs = slice(gi * dm.pgc, (gi + 1) * dm.pgc)
        cur = ext_s[:, POOL_WMAX:POOL_WMAX + seq_len, cs]
        acc = cur
        for j in range(1, win):
            acc = acc + ext_s[:, POOL_WMAX - j:POOL_WMAX - j + seq_len, cs]
        cnt = jnp.minimum(win, p + (pos0 + 1)).astype(F32)
        pg = (acc / cnt - cur).reshape(t, dm.pgc)
        pg = _mm(pg.astype(BF16), w.w_pg[gi]) * w.pool_scale[:, cs]
        pooled.append(pg.astype(BF16))
    branch_b = _mm(jnp.concatenate(pooled, axis=1), w.w_bp[...])

    gate_a = jax.nn.sigmoid(_mm(xb, w.w_tail[:, off_ga:off_ga + dm.d]) + w.b_gates[:, 0:dm.d])
    merged = gate_a * branch_a
    gate_b = jax.nn.sigmoid(_mm(xb, w.w_tail[:, off_gb:off_gb + dm.d]) + w.b_gates[:, dm.d:2 * dm.d])
    merged = merged + gate_b * branch_b
    mix = _mm(merged.astype(BF16), w.w_out[...])
    h1 = _layer_norm(dm.alpha * x + mix, w.ln_g[...], w.ln_b[...])
    logits = _mm(h1.astype(BF16), w.w_rt[...]) + w.b_rt[...]
    return h1, logits.T[0:N_EXPERTS, :]


N_MIXW = len(MixW._fields)


def _prompt_mixer_kernel(*refs, dm, tile, n_tiles):
    x_ref = refs[0]
    w = MixW(*refs[1:1 + N_MIXW])
    h1_ref, lg_ref, st_ref, buf_ref, q_s, k_s, v_s, gl_s, b_s, o_s, ext_s, st_t = refs[1 + N_MIXW:]
    lt = pl.program_id(1)

    @pl.when(lt == 0)
    def _():
        st_t[...] = jnp.zeros(st_t.shape, F32)
        ext_s[:, 0:POOL_WMAX, :] = jnp.zeros((1, POOL_WMAX, dm.pw), F32)

    x = x_ref[...]
    xb = x.astype(BF16)
    _project_qkv(xb, w, dm, q_s, k_s, v_s, gl_s)
    _chunk_cumsum(gl_s, b_s, tile, GLA_CHUNK)
    _gla_tile(q_s, k_s, v_s, b_s, o_s, st_t, dm=dm, rows=tile, chunk=GLA_CHUNK, group=GLA_GROUP)

    @pl.when(lt == n_tiles - 1)
    def _():
        for h in range(GLA_HEADS):
            st_ref[0, h] = st_t[h].T

    h1, logits_t = _mixer_tail(x, xb, o_s, ext_s, lt * tile, w, dm, nseq=1, seq_len=tile)
    _to_row_tiles(h1_ref, h1, tile)
    lg_ref[...] = logits_t
    ext_s[:, 0:POOL_WMAX, :] = ext_s[:, tile:tile + POOL_WMAX, :]

    @pl.when(lt == n_tiles - 1)
    def _():
        buf_ref[...] = ext_s[:, 1:POOL_WMAX, :]


def _sample_proj_kernel(*refs, dm):
    x_ref = refs[0]
    w = MixW(*refs[1:1 + N_MIXW])
    q_ref, k_ref, v_ref, gl_ref = refs[1 + N_MIXW:]
    _project_qkv(x_ref[...].astype(BF16), w, dm, q_ref, k_ref, v_ref, gl_ref)


def _sample_gla_kernel(q_ref, k_ref, v_ref, gl_ref, s0_ref, o_ref, st_ref, b_s, *, dm, nseq, seq_len):
    st_ref[...] = s0_ref[...]
    _gla(q_ref, k_ref, v_ref, gl_ref, b_s, o_ref, st_ref, dm=dm, n_chunks=nseq, chunk=seq_len, chunks_per_seq=1)


def _sample_tail_kernel(*refs, dm, nseq, seq_len):
    x_ref, o_ref, hist_ref = refs[0:3]
    w = MixW(*refs[3:3 + N_MIXW])
    h1_ref, lg_ref, buf_ref, ext_s = refs[3 + N_MIXW:]
    ext_s[:, 0:1, :] = jnp.zeros((nseq, 1, dm.pw), F32)
    ext_s[:, 1:POOL_WMAX, :] = hist_ref[...]
    x = x_ref[...]
    h1, logits_t = _mixer_tail(x, x.astype(BF16), o_ref, ext_s, POOL_BUF, w, dm, nseq=nseq, seq_len=seq_len)
    _to_row_tiles(h1_ref, h1, nseq * seq_len)
    lg_ref[...] = logits_t
    buf_ref[...] = ext_s[:, seq_len + 1:seq_len + POOL_WMAX, :]


def _const_spec(arr):
    nd = arr.ndim
    return pl.BlockSpec(arr.shape, lambda *_: (0,) * nd, pipeline_mode=pl.Buffered(1))


def _prompt_mixer(x, w, dm):
    bsz, seq, d = x.shape
    tile = PROMPT_TILE
    n_tiles = seq // tile
    m = bsz * seq
    scratch = [pltpu.VMEM((tile, dm.key), F32), pltpu.VMEM((tile, dm.key), F32), pltpu.VMEM((tile, dm.val), F32),
               pltpu.VMEM((tile, dm.key), F32), pltpu.VMEM((tile, dm.key), F32), pltpu.VMEM((tile, dm.val), F32),
               pltpu.VMEM((1, POOL_WMAX + tile, dm.pw), F32),
               pltpu.VMEM((GLA_HEADS, dm.dv, dm.dk), F32)]
    return pl.pallas_call(
        functools.partial(_prompt_mixer_kernel, dm=dm, tile=tile, n_tiles=n_tiles),
        grid=(bsz, n_tiles),
        in_specs=[pl.BlockSpec((tile, d), lambda b, t: (b * n_tiles + t, 0))] + [_const_spec(a) for a in w],
        out_specs=[pl.BlockSpec((tile * SUBLANES, LANES), lambda b, t: (b * n_tiles + t, 0)),
                   pl.BlockSpec((N_EXPERTS, tile), lambda b, t: (0, b * n_tiles + t)),
                   pl.BlockSpec((1, GLA_HEADS, dm.dk, dm.dv), lambda b, t: (b, 0, 0, 0)),
                   pl.BlockSpec((1, POOL_BUF, dm.pw), lambda b, t: (b, 0, 0))],
        out_shape=[jax.ShapeDtypeStruct((m * SUBLANES, LANES), F32),
                   jax.ShapeDtypeStruct((N_EXPERTS, m), F32),
                   jax.ShapeDtypeStruct((bsz, GLA_HEADS, dm.dk, dm.dv), F32),
                   jax.ShapeDtypeStruct((bsz, POOL_BUF, dm.pw), F32)],
        scratch_shapes=scratch,
        compiler_params=pltpu.CompilerParams(dimension_semantics=("arbitrary", "arbitrary"),
                                             vmem_limit_bytes=VMEM_LIMIT_BYTES),
        name="prompt_mixer",
    )(x.reshape(m, d), *w)


def _sample_mixer(x, s0, hist, w, dm):
    bsz, seq, d = x.shape
    m = bsz * seq
    x2 = x.reshape(m, d)
    tile = MIX_TILE
    cparams = pltpu.CompilerParams(dimension_semantics=("arbitrary",), vmem_limit_bytes=VMEM_LIMIT_BYTES)
    row = lambda n: pl.BlockSpec((tile, n), lambda i: (i, 0))
    q, k, v, gl = pl.pallas_call(
        functools.partial(_sample_proj_kernel, dm=dm),
        grid=(m // tile,),
        in_specs=[row(d)] + [_const_spec(a) for a in w],
        out_specs=[row(dm.key), row(dm.key), row(dm.val), row(dm.key)],
        out_shape=[jax.ShapeDtypeStruct((m, n), F32) for n in (dm.key, dm.key, dm.val, dm.key)],
        compiler_params=cparams,
        name="sample_proj",
    )(x2, *w)

    nseq = GLA_SEQS
    rows = nseq * seq
    grow = lambda n: pl.BlockSpec((rows, n), lambda i: (i, 0))
    st_spec = pl.BlockSpec((nseq, GLA_HEADS, dm.dk, dm.dv), lambda i: (i, 0, 0, 0))
    o, st = pl.pallas_call(
        functools.partial(_sample_gla_kernel, dm=dm, nseq=nseq, seq_len=seq),
        grid=(bsz // nseq,),
        in_specs=[grow(dm.key), grow(dm.key), grow(dm.val), grow(dm.key), st_spec],
        out_specs=[grow(dm.val), st_spec],
        out_shape=[jax.ShapeDtypeStruct((m, dm.val), F32), jax.ShapeDtypeStruct(s0.shape, F32)],
        scratch_shapes=[pltpu.VMEM((rows, dm.key), F32)],
        compiler_params=cparams,
        name="sample_gla",
    )(q, k, v, gl, s0)

    tseq = tile // seq
    hist_spec = pl.BlockSpec((tseq, POOL_BUF, dm.pw), lambda i: (i, 0, 0))
    h1, lg, buf = pl.pallas_call(
        functools.partial(_sample_tail_kernel, dm=dm, nseq=tseq, seq_len=seq),
        grid=(m // tile,),
        in_specs=[row(d), row(dm.val), hist_spec] + [_const_spec(a) for a in w],
        out_specs=[pl.BlockSpec((tile * SUBLANES, LANES), lambda i: (i, 0)),
                   pl.BlockSpec((N_EXPERTS, tile), lambda i: (0, i)), hist_spec],
        out_shape=[jax.ShapeDtypeStruct((m * SUBLANES, LANES), F32),
                   jax.ShapeDtypeStruct((N_EXPERTS, m), F32),
                   jax.ShapeDtypeStruct((bsz, POOL_BUF, dm.pw), F32)],
        scratch_shapes=[pltpu.VMEM((tseq, POOL_WMAX + seq, dm.pw), F32)],
        compiler_params=cparams,
        name="sample_tail",
    )(x2, o, hist, *w)
    return h1, lg, st, buf


def _router_kernel(lga_ref, lgb_ref, e_ref, w_ref, r_ref, c_ref, carry, *, tile, n_tiles_a):
    i = pl.program_id(0)

    @pl.when(i == 0)
    def _():
        carry[...] = jnp.zeros(carry.shape, F32)

    eio = lax.broadcasted_iota(jnp.int32, (N_EXPERTS, tile), 0)
    work = jnp.where(i < n_tiles_a, lga_ref[...], lgb_ref[...])
    vals, hots = [], []
    for k in range(TOP_K):
        mx = jnp.max(work, axis=0, keepdims=True)
        idx = jnp.min(jnp.where(work == mx, eio, N_EXPERTS), axis=0, keepdims=True)
        hot = eio == idx
        vals.append(mx)
        hots.append(hot)
        e_ref[k:k + 1, :] = idx
        work = jnp.where(hot, -jnp.inf, work)
    ex = [jnp.exp(v - vals[0]) for v in vals]
    den = ex[0]
    for k in range(1, TOP_K):
        den = den + ex[k]
    for k in range(TOP_K):
        w_ref[k:k + 1, :] = ex[k] / den
    sel = hots[0]
    for k in range(1, TOP_K):
        sel = sel | hots[k]
    r = lax.broadcasted_iota(jnp.int32, (tile, tile), 0)
    c = lax.broadcasted_iota(jnp.int32, (tile, tile), 1)
    before = (r < c).astype(BF16)
    pref = _mm(sel.astype(BF16), before) + carry[:, 0:1]
    for k in range(TOP_K):
        r_ref[k:k + 1, :] = jnp.sum(jnp.where(hots[k], pref, 0.0), axis=0, keepdims=True).astype(jnp.int32)
    carry[...] = carry[...] + jnp.sum(sel.astype(F32), axis=1, keepdims=True)
    c_ref[...] = carry[...]


def _dispatch_kernel(fill_start_ref, fill_cnt_ref, tail_start_ref, dest_hbm, h1a_ref, h1b_ref, xs_hbm, idx_s, zeros_s,
                     sem_i, sem_r, sem_z, *, tile, n_tiles_a):
    i = pl.program_id(0)
    idx_cp = pltpu.make_async_copy(dest_hbm.at[i], idx_s, sem_i)
    idx_cp.start()

    @pl.when(i == 0)
    def _():
        zeros_s[...] = jnp.zeros(zeros_s.shape, F32)

        def fill_copy(off, bit):
            n = 1 << bit
            return pltpu.make_async_copy(zeros_s.at[pl.ds(0, n)], xs_hbm.at[pl.ds(off, n)], sem_z)

        def for_each_piece(fn):
            def body(e, carry):
                cnt = fill_cnt_ref[e]
                for bit in range(ZERO_FILL_BITS):
                    @pl.when(((cnt >> bit) & 1) == 1)
                    def _(bit=bit):
                        fn(fill_copy(fill_start_ref[e] + (cnt & ((1 << bit) - 1)), bit))
                return carry
            lax.fori_loop(0, N_EXPERTS, body, 0)

        top = ZERO_FILL_BITS - 1
        n_tail = (xs_hbm.shape[0] - tail_start_ref[0]) >> top

        def for_each_tail_piece(fn):
            def body(j, carry):
                fn(fill_copy(tail_start_ref[0] + (j << top), top))
                return carry
            lax.fori_loop(0, n_tail, body, 0)

        for_each_piece(lambda cp: cp.start())
        for_each_tail_piece(lambda cp: cp.start())
        for_each_piece(lambda cp: cp.wait())
        for_each_tail_piece(lambda cp: cp.wait())

    idx_cp.wait()

    def start_rows(h1_ref):
        def group(g, carry):
            for u in range(DMA_UNROLL):
                t = g * DMA_UNROLL + u
                src = h1_ref.at[pl.ds(pl.multiple_of(t * SUBLANES, SUBLANES), SUBLANES), :]
                for k in range(TOP_K):
                    pltpu.make_async_copy(src, xs_hbm.at[idx_s[k * tile + t]], sem_r).start(priority=(u + k) % 2)
            return carry
        lax.fori_loop(0, tile // DMA_UNROLL, group, 0)

    @pl.when(i < n_tiles_a)
    def _():
        start_rows(h1a_ref)

    @pl.when(i >= n_tiles_a)
    def _():
        start_rows(h1b_ref)

    for k in range(TOP_K):
        pltpu.make_async_copy(h1a_ref, h1a_ref, sem_r).wait()


def _expert_kernel(be_ref, nused_ref, xs_ref, wgu_ref, bgu_ref, wdn_ref, bdn_ref, out_ref, wgu_b, wdn_b,
                   *, rows, d_ff):
    i = pl.program_id(0)

    @pl.when((i == 0) | (be_ref[i] != be_ref[jnp.maximum(i - 1, 0)]))
    def _():
        wgu_b[...] = wgu_ref[0].astype(BF16)
        wdn_b[...] = wdn_ref[0].astype(BF16)

    @pl.when(i < nused_ref[0])
    def _():
        xb = _from_row_tiles(xs_ref, rows).astype(BF16)
        hh = _mm(xb, wgu_b[...]) + bgu_ref[0]
        gate = jnp.minimum(hh[:, 0:d_ff], SWIGLU_LIMIT)
        up = jnp.clip(hh[:, d_ff:2 * d_ff], -SWIGLU_LIMIT, SWIGLU_LIMIT)
        act = (up + 1.0) * (gate * jax.nn.sigmoid(GLU_ALPHA * gate))
        res = _mm(act.astype(BF16), wdn_b[...]) + bdn_ref[0]
        _to_row_tiles(out_ref, res, rows)

    @pl.when(i >= nused_ref[0])
    def _():
        out_ref[...] = jnp.zeros(out_ref.shape, F32)


def _combine_kernel(dest_hbm, res_hbm, h1a_ref, h1b_ref, gw_ref, g_ref, b_ref, ya_ref, yb_ref, idx_s, gbuf,
                    sem_i, sem_g, *, tile, alpha, n_tiles, n_tiles_a):
    i = pl.program_id(0)
    slot = i % 2

    def idx_copy(blk, sl):
        return pltpu.make_async_copy(dest_hbm.at[blk], idx_s.at[sl], sem_i.at[sl])

    def start_rows(sl):
        def group(g, carry):
            for u in range(DMA_UNROLL):
                t = g * DMA_UNROLL + u
                for k in range(TOP_K):
                    dst = gbuf.at[sl, k, pl.ds(pl.multiple_of(t * SUBLANES, SUBLANES), SUBLANES), :]
                    pltpu.make_async_copy(res_hbm.at[idx_s[sl, k * tile + t]], dst,
                                          sem_g.at[sl]).start(priority=(u + k) % 2)
            return carry
        lax.fori_loop(0, tile // DMA_UNROLL, group, 0)

    def wait_rows(sl):
        pltpu.make_async_copy(gbuf.at[sl], gbuf.at[sl], sem_g.at[sl]).wait()

    @pl.when(i == 0)
    def _():
        first = idx_copy(0, 0)
        first.start()
        first.wait()
        start_rows(0)

        @pl.when(n_tiles > 1)
        def _():
            idx_copy(1, 1).start()

    @pl.when(i + 1 < n_tiles)
    def _():
        idx_copy(i + 1, 1 - slot).wait()
        start_rows(1 - slot)

    @pl.when(i + 2 < n_tiles)
    def _():
        idx_copy(i + 2, slot).start()

    wait_rows(slot)
    gw = gw_ref[...]
    z = alpha * jnp.where(i < n_tiles_a, _from_row_tiles(h1a_ref, tile), _from_row_tiles(h1b_ref, tile))
    for k in range(TOP_K):
        z = z + gw[:, k:k + 1] * _from_row_tiles(gbuf.at[slot, k], tile)
    y = _layer_norm(z, g_ref[...], b_ref[...])

    @pl.when(i < n_tiles_a)
    def _():
        ya_ref[...] = y

    @pl.when(i >= n_tiles_a)
    def _():
        yb_ref[...] = y


def _tile_major(a, tile):
    k, m = a.shape
    return a.reshape(k, m // tile, tile).transpose(1, 0, 2).reshape(m // tile, k * tile)


def _moe(h1_a, h1_b, logits_a, logits_b, w_gu, b_gu, w_down, b_down, ln_g, ln_b, dm):
    m_a = logits_a.shape[1]
    m = m_a + logits_b.shape[1]
    rows = EXPERT_ROWS
    d_ff = w_down.shape[1]
    cparams = pltpu.CompilerParams(dimension_semantics=("arbitrary",), vmem_limit_bytes=VMEM_LIMIT_BYTES)

    def two_group_specs(block, n_a, lane_axis):
        pick = (lambda j: (0, j)) if lane_axis else (lambda j: (j, 0))
        return [pl.BlockSpec(block, lambda i, *_: pick(jnp.minimum(i, n_a - 1))),
                pl.BlockSpec(block, lambda i, *_: pick(jnp.maximum(i - n_a, 0)))]

    rt = ROUTER_TILE
    kblk = lambda: pl.BlockSpec((TOP_K, rt), lambda i: (0, i))
    e_idx, gate_w, rank, counts = pl.pallas_call(
        functools.partial(_router_kernel, tile=rt, n_tiles_a=m_a // rt),
        grid=(m // rt,),
        in_specs=two_group_specs((N_EXPERTS, rt), m_a // rt, True),
        out_specs=[kblk(), kblk(), kblk(), pl.BlockSpec((N_EXPERTS, LANES), lambda i: (0, 0))],
        out_shape=[jax.ShapeDtypeStruct((TOP_K, m), jnp.int32), jax.ShapeDtypeStruct((TOP_K, m), F32),
                   jax.ShapeDtypeStruct((TOP_K, m), jnp.int32), jax.ShapeDtypeStruct((N_EXPERTS, LANES), F32)],
        scratch_shapes=[pltpu.VMEM((N_EXPERTS, LANES), F32)],
        compiler_params=cparams,
        name="moe_router",
    )(logits_a, logits_b)

    counts = counts[:, 0].astype(jnp.int32)
    padded = (counts + rows - 1) // rows * rows
    pad_end = jnp.cumsum(padded)
    pad_start = pad_end - padded
    n_blocks = -(-(m * TOP_K + N_EXPERTS * (rows - 1)) // rows)
    n_rows = n_blocks * rows
    block_row0 = jnp.arange(n_blocks, dtype=jnp.int32) * rows
    block_e = jnp.minimum(jnp.sum((pad_end[None, :] <= block_row0[:, None]).astype(jnp.int32), axis=1), N_EXPERTS - 1)
    n_used = (pad_end[-1] // rows).astype(jnp.int32).reshape(1)
    experts = jnp.arange(N_EXPERTS, dtype=jnp.int32)[:, None, None]
    dest = rank + jnp.sum(jnp.where(e_idx[None] == experts, pad_start[:, None, None], 0), axis=0)

    tile = COMBINE_TILE
    n_tiles = m // tile
    n_tiles_a = m_a // tile
    h1_specs = two_group_specs((tile * SUBLANES, LANES), n_tiles_a, False)
    dest_t = _tile_major(dest, tile)
    xs = pl.pallas_call(
        functools.partial(_dispatch_kernel, tile=tile, n_tiles_a=n_tiles_a),
        grid_spec=pltpu.PrefetchScalarGridSpec(
            num_scalar_prefetch=3,
            grid=(n_tiles,),
            in_specs=[pl.BlockSpec(memory_space=pl.ANY)] + h1_specs,
            out_specs=pl.BlockSpec(memory_space=pl.ANY),
            scratch_shapes=[pltpu.SMEM((TOP_K * tile,), jnp.int32),
                            pltpu.VMEM((1 << (ZERO_FILL_BITS - 1), SUBLANES, LANES), F32),
                            pltpu.SemaphoreType.DMA, pltpu.SemaphoreType.DMA, pltpu.SemaphoreType.DMA]),
        out_shape=jax.ShapeDtypeStruct((n_rows, SUBLANES, LANES), F32),
        compiler_params=cparams,
        name="moe_dispatch",
    )(pad_start + counts, padded - counts, pad_end[-1:], dest_t, h1_a, h1_b)

    last = lambda i, nu: jnp.minimum(i, nu[0] - 1)
    res = pl.pallas_call(
        functools.partial(_expert_kernel, rows=rows, d_ff=d_ff),
        grid_spec=pltpu.PrefetchScalarGridSpec(
            num_scalar_prefetch=2,
            grid=(n_blocks,),
            in_specs=[pl.BlockSpec((rows * SUBLANES, LANES), lambda i, be, nu: (last(i, nu), 0)),
                      pl.BlockSpec((1, dm.d, 2 * d_ff), lambda i, be, nu: (be[i], 0, 0)),
                      pl.BlockSpec((1, 1, 2 * d_ff), lambda i, be, nu: (be[i], 0, 0)),
                      pl.BlockSpec((1, d_ff, dm.d), lambda i, be, nu: (be[i], 0, 0)),
                      pl.BlockSpec((1, 1, dm.d), lambda i, be, nu: (be[i], 0, 0))],
            out_specs=pl.BlockSpec((rows * SUBLANES, LANES), lambda i, be, nu: (i, 0)),
            scratch_shapes=[pltpu.VMEM((dm.d, 2 * d_ff), BF16), pltpu.VMEM((d_ff, dm.d), BF16)]),
        out_shape=jax.ShapeDtypeStruct((n_rows * SUBLANES, LANES), F32),
        compiler_params=cparams,
        name="moe_experts",
    )(block_e, n_used, xs.reshape(n_rows * SUBLANES, LANES), w_gu, b_gu[:, None, :], w_down, b_down[:, None, :])

    ya, yb = pl.pallas_call(
        functools.partial(_combine_kernel, tile=tile, alpha=dm.alpha, n_tiles=n_tiles, n_tiles_a=n_tiles_a),
        grid=(n_tiles,),
        in_specs=[pl.BlockSpec(memory_space=pl.ANY),
                  pl.BlockSpec(memory_space=pl.ANY)] + h1_specs + [
                  pl.BlockSpec((tile, TOP_K), lambda i: (i, 0)),
                  pl.BlockSpec((1, dm.d), lambda i: (0, 0)),
                  pl.BlockSpec((1, dm.d), lambda i: (0, 0))],
        out_specs=[pl.BlockSpec((tile, dm.d), lambda i: (jnp.minimum(i, n_tiles_a - 1), 0)),
                   pl.BlockSpec((tile, dm.d), lambda i: (jnp.maximum(i - n_tiles_a, 0), 0))],
        out_shape=[jax.ShapeDtypeStruct((m_a, dm.d), F32), jax.ShapeDtypeStruct((m - m_a, dm.d), F32)],
        scratch_shapes=[pltpu.SMEM((2, TOP_K * tile), jnp.int32),
                        pltpu.VMEM((2, TOP_K, tile * SUBLANES, LANES), F32),
                        pltpu.SemaphoreType.DMA((2,)),
                        pltpu.SemaphoreType.DMA((2,))],
        compiler_params=cparams,
        name="moe_combine",
    )(dest_t, res.reshape(n_rows, SUBLANES, LANES), h1_a, h1_b, gate_w.T, ln_g[None, :], ln_b[None, :])
    return ya, yb


def _pad_cols(a, n):
    return jnp.pad(a, ((0, 0), (0, n - a.shape[1])))


def _mixer_weights(w_in, w_gk2, b_gk, gla_norm_w, w_branch_gla, w_pool_grp, pool_scale, w_branch_pool, b_gates,
                   w_out, ln_g, ln_b, w_router, b_router, dm):
    o_lr = 2 * dm.key + dm.val
    o_tail = o_lr + GATE_RANK
    return MixW(
        w_qkv=w_in[:, 0:o_lr].astype(BF16),
        w_lr=_pad_cols(w_in[:, o_lr:o_tail], LANES).astype(BF16),
        w_gk2=jnp.pad(w_gk2, ((0, LANES - GATE_RANK), (0, 0))).astype(BF16),
        b_gk=b_gk[None, :],
        w_tail=w_in[:, o_tail:].astype(BF16),
        gla_norm=gla_norm_w[None, :],
        w_ba=w_branch_gla.astype(BF16),
        w_pg=w_pool_grp.astype(BF16),
        pool_scale=pool_scale[None, :],
        w_bp=w_branch_pool.astype(BF16),
        b_gates=b_gates[None, :],
        w_out=w_out.astype(BF16),
        ln_g=ln_g[None, :],
        ln_b=ln_b[None, :],
        w_rt=_pad_cols(w_router, LANES).astype(BF16),
        b_rt=_pad_cols(b_router[None, :], LANES),
    )


def kernel(x_prompt, x_sample, state_gla, state_pool, w_in, w_gk2, b_gk, gla_norm_w, w_branch_gla, w_pool_grp,
           pool_scale, w_branch_pool, b_gates, w_out, ln1_g, ln1_b, w_router, b_router, w_gu, b_gu, w_down, b_down,
           ln2_g, ln2_b):
    depth = w_in.shape[0]
    bp, lp, d = x_prompt.shape
    bs, ls, _ = x_sample.shape
    m = bp * lp + bs * ls
    assert d == SUBLANES * LANES and lp % PROMPT_TILE == 0 and (bs * ls) % MIX_TILE == 0 and MIX_TILE % ls == 0
    assert bs % GLA_SEQS == 0 and ls % SUBLANES == 0
    assert all(n % t == 0 for n in (bp * lp, bs * ls) for t in (ROUTER_TILE, COMBINE_TILE))
    dm = _dims(d, depth)
    mp = bp * lp
    yp, ys = x_prompt, x_sample
    gla_p, pool_p, gla_s, pool_s = [], [], [], []
    for l in range(depth):
        w = _mixer_weights(w_in[l], w_gk2[l], b_gk[l], gla_norm_w[l], w_branch_gla[l], w_pool_grp[l], pool_scale[l],
                           w_branch_pool[l], b_gates[l], w_out[l], ln1_g[l], ln1_b[l], w_router[l], b_router[l], dm)
        h1p, lgp, sp, bufp = _prompt_mixer(yp, w, dm)
        h1s, lgs, ss, bufs = _sample_mixer(ys, state_gla[l], state_pool[l], w, dm)
        yp, ys = _moe(h1p, h1s, lgp, lgs, w_gu[l], b_gu[l], w_down[l], b_down[l], ln2_g[l], ln2_b[l], dm)
        yp = yp.reshape(bp, lp, d)
        ys = ys.reshape(bs, ls, d)
        gla_p.append(sp.astype(state_gla.dtype))
        pool_p.append(bufp.astype(state_pool.dtype))
        gla_s.append(ss.astype(state_gla.dtype))
        pool_s.append(bufs.astype(state_pool.dtype))
    return (yp, ys, jnp.stack(gla_p, 0), jnp.stack(pool_p, 0), jnp.stack(gla_s, 0), jnp.stack(pool_s, 0))
```

```python
import functools
from typing import NamedTuple

import jax
import jax.numpy as jnp
from jax import lax
from jax.experimental import pallas as pl
from jax.experimental.pallas import tpu as pltpu

F32 = jnp.float32
BF16 = jnp.bfloat16

GLA_HEADS = 4
GATE_RANK = 16
GATE_NORMALIZER = 16.0
GLA_CHUNK = 64
RMS_EPS = 1e-6
POOL_WINDOWS = (2, 4, 8, 16)
POOL_WMAX = 16
POOL_BUF = POOL_WMAX - 1
N_EXPERTS = 32
TOP_K = 4
SWIGLU_LIMIT = 7.0
GLU_ALPHA = 1.702
LN_EPS = 1e-5

LANES = 128
SUBLANES = 8
VMEM_LIMIT_BYTES = 56 * 1024 * 1024

PROMPT_TILE = 512
MIX_TILE = 256
GLA_GROUP = 128
GLA_SEQS = 8
EXPERT_ROWS = 512
MOE_TILE = 256
CUMSUM_BLOCK = 256
DMA_UNROLL = 8
ZERO_FILL_BITS = (EXPERT_ROWS - 1).bit_length()
RUN_BITS = MOE_TILE.bit_length()


class Dims(NamedTuple):
    d: int
    dk: int
    dv: int
    key: int
    val: int
    pw: int
    pgc: int
    alpha: float


def _dims(d, depth):
    return Dims(d=d, dk=d // 8, dv=d // 4, key=d // 2, val=d, pw=d // 2, pgc=d // 8, alpha=(2.0 * depth) ** 0.25)


class MixW(NamedTuple):
    w_qkv: object
    w_lr: object
    w_gk2: object
    b_gk: object
    w_tail: object
    gla_norm: object
    w_ba: object
    w_pg: object
    pool_scale: object
    w_bp: object
    b_gates: object
    w_out: object
    ln_g: object
    ln_b: object
    w_rt: object
    b_rt: object


def _mm(a, b):
    return jnp.dot(a, b, preferred_element_type=F32)


def _layer_norm(z, g, b):
    mu = jnp.mean(z, axis=-1, keepdims=True)
    zc = z - mu
    var = jnp.mean(zc * zc, axis=-1, keepdims=True)
    return zc * lax.rsqrt(var + LN_EPS) * g + b


def _to_row_tiles(ref, val, rows):
    for c in range(SUBLANES):
        ref[pl.ds(c, rows, stride=SUBLANES), :] = val[:, c * LANES:(c + 1) * LANES]


def _from_row_tiles(ref, rows):
    return jnp.concatenate([ref[pl.ds(c, rows, stride=SUBLANES), :] for c in range(SUBLANES)], axis=1)


def _project_qkv(xb, w, dm, q_s, k_s, v_s, gl_s):
    q_s[...] = _mm(xb, w.w_qkv[:, 0:dm.key]) * (dm.dk ** -0.5)
    k_s[...] = _mm(xb, w.w_qkv[:, dm.key:2 * dm.key])
    v_s[...] = _mm(xb, w.w_qkv[:, 2 * dm.key:2 * dm.key + dm.val])
    lr = _mm(xb, w.w_lr[...])
    gk = _mm(lr.astype(BF16), w.w_gk2[...]) + w.b_gk[...]
    gl_s[...] = (jnp.minimum(gk, 0.0) - jnp.log1p(jnp.exp(-jnp.abs(gk)))) / GATE_NORMALIZER


def _chunk_cumsum(gl_s, b_s, rows, chunk):
    blk = min(rows, CUMSUM_BLOCK)
    r = lax.broadcasted_iota(jnp.int32, (blk, blk), 0)
    c = lax.broadcasted_iota(jnp.int32, (blk, blk), 1)
    tri = ((r // chunk == c // chunk) & (c <= r)).astype(BF16)
    for b0 in range(0, rows, blk):
        gl = gl_s[b0:b0 + blk, :]
        hi = gl.astype(BF16)
        lo = (gl - hi.astype(F32)).astype(BF16)
        b_s[b0:b0 + blk, :] = _mm(tri, hi) + _mm(tri, lo)


NT_DIMS = (((1,), (1,)), ((), ()))
TN_DIMS = (((0,), (0,)), ((), ()))


def _gla_tile(q_s, k_s, v_s, b_s, o_s, st_t, *, dm, rows, chunk, group):
    n_chunks = rows // chunk
    gr = lax.broadcasted_iota(jnp.int32, (group, group), 0)
    gc = lax.broadcasted_iota(jnp.int32, (group, group), 1)
    mask = (gr // chunk == gc // chunk) & (gc <= gr)

    def per_chunk_row(b, row):
        return jnp.concatenate([jnp.broadcast_to(b[c * chunk + row:c * chunk + row + 1, :], (chunk, b.shape[1]))
                                for c in range(n_chunks)], axis=0)

    for h in range(GLA_HEADS):
        ks = slice(h * dm.dk, (h + 1) * dm.dk)
        vs = slice(h * dm.dv, (h + 1) * dm.dv)
        b = b_s[:, ks]
        b_ref = per_chunk_row(b, chunk // 2)
        b_last = per_chunk_row(b, chunk - 1)
        q = q_s[:, ks]
        k = k_s[:, ks]
        vb = v_s[:, vs].astype(BF16)
        qa = (q * jnp.exp(b - b_ref)).astype(BF16)
        ka = (k * jnp.exp(b_ref - b)).astype(BF16)
        qe = (q * jnp.exp(b)).astype(BF16)
        kd = (k * jnp.exp(b_last - b)).astype(BF16)
        for g0 in range(0, rows, group):
            a = lax.dot_general(qa[g0:g0 + group], ka[g0:g0 + group], NT_DIMS, preferred_element_type=F32)
            a = jnp.where(mask, a, 0.0)
            o_s[g0:g0 + group, vs] = _mm(a.astype(BF16), vb[g0:g0 + group])
        s_t = st_t[h]
        for c in range(n_chunks):
            r0 = c * chunk
            o_s[r0:r0 + chunk, vs] += lax.dot_general(qe[r0:r0 + chunk], s_t.astype(BF16), NT_DIMS,
                                                      preferred_element_type=F32)
            decay = jnp.exp(b[r0 + chunk - 1:r0 + chunk, :])
            s_t = s_t * decay + lax.dot_general(vb[r0:r0 + chunk], kd[r0:r0 + chunk], TN_DIMS,
                                                preferred_element_type=F32)
        st_t[h] = s_t


def _gla(q_s, k_s, v_s, gl_s, b_s, o_s, st_ref, *, dm, n_chunks, chunk, chunks_per_seq):
    _chunk_cumsum(gl_s, b_s, n_chunks * chunk, chunk)
    ri = lax.broadcasted_iota(jnp.int32, (chunk, chunk), 0)
    ci = lax.broadcasted_iota(jnp.int32, (chunk, chunk), 1)
    causal = ci <= ri
    nt, tn = NT_DIMS, TN_DIMS

    def step(idx, carry):
        r0 = pl.multiple_of(idx * chunk, chunk)
        seq = idx // chunks_per_seq
        for h in range(GLA_HEADS):
            ks = slice(h * dm.dk, (h + 1) * dm.dk)
            vs = slice(h * dm.dv, (h + 1) * dm.dv)
            bc = b_s[pl.ds(r0, chunk), ks]
            b_ref = bc[chunk // 2:chunk // 2 + 1, :]
            b_last = bc[chunk - 1:chunk, :]
            qc = q_s[pl.ds(r0, chunk), ks]
            kc = k_s[pl.ds(r0, chunk), ks]
            vc = v_s[pl.ds(r0, chunk), vs].astype(BF16)
            s_old = st_ref[seq, h]
            a = lax.dot_general((qc * jnp.exp(bc - b_ref)).astype(BF16), (kc * jnp.exp(b_ref - bc)).astype(BF16),
                                nt, preferred_element_type=F32)
            a = jnp.where(causal, a, 0.0)
            o = _mm(a.astype(BF16), vc) + _mm((qc * jnp.exp(bc)).astype(BF16), s_old.astype(BF16))
            o_s[pl.ds(r0, chunk), vs] = o
            e_col = jnp.broadcast_to(jnp.exp(b_last), (dm.dk, dm.dk)).T
            decay = jnp.concatenate([e_col] * (dm.dv // dm.dk), axis=1)
            kv = lax.dot_general((kc * jnp.exp(b_last - bc)).astype(BF16), vc, tn, preferred_element_type=F32)
            st_ref[seq, h] = decay * s_old + kv
        return carry

    lax.fori_loop(0, n_chunks, step, 0)


def _mixer_tail(x, xb, o_s, ext_s, pos0, w, dm, *, nseq, seq_len):
    t = nseq * seq_len
    off_u = dm.val
    off_ga = off_u + dm.pw
    off_gb = off_ga + dm.d

    g = _mm(xb, w.w_tail[:, 0:dm.val])
    parts = []
    for h in range(GLA_HEADS):
        vs = slice(h * dm.dv, (h + 1) * dm.dv)
        oh = o_s[:, vs]
        ms = jnp.mean(oh * oh, axis=-1, keepdims=True)
        on = oh * lax.rsqrt(ms + RMS_EPS) * w.gla_norm[...]
        gh = g[:, vs]
        parts.append((on * (gh * jax.nn.sigmoid(gh))).astype(BF16))
    branch_a = _mm(jnp.concatenate(parts, axis=1), w.w_ba[...])

    u = _mm(xb, w.w_tail[:, off_u:off_u + dm.pw])
    ext_s[:, POOL_WMAX:POOL_WMAX + seq_len, :] = u.reshape(nseq, seq_len, dm.pw)
    p = lax.broadcasted_iota(jnp.int32, (nseq, seq_len, dm.pgc), 1)
    pooled = []
    for gi, win in enumerate(POOL_WINDOWS):
        cs = slice(gi * dm.pgc, (gi + 1) * dm.pgc)
        cur = ext_s[:, POOL_WMAX:POOL_WMAX + seq_len, cs]
        acc = cur
        for j in range(1, win):
            acc = acc + ext_s[:, POOL_WMAX - j:POOL_WMAX - j + seq_len, cs]
        cnt = jnp.minimum(win, p + (pos0 + 1)).astype(F32)
        pg = (acc / cnt - cur).reshape(t, dm.pgc)
        pg = _mm(pg.astype(BF16), w.w_pg[gi]) * w.pool_scale[:, cs]
        pooled.append(pg.astype(BF16))
    branch_b = _mm(jnp.concatenate(pooled, axis=1), w.w_bp[...])

    gate_a = jax.nn.sigmoid(_mm(xb, w.w_tail[:, off_ga:off_ga + dm.d]) + w.b_gates[:, 0:dm.d])
    merged = gate_a * branch_a
    gate_b = jax.nn.sigmoid(_mm(xb, w.w_tail[:, off_gb:off_gb + dm.d]) + w.b_gates[:, dm.d:2 * dm.d])
    merged = merged + gate_b * branch_b
    mix = _mm(merged.astype(BF16), w.w_out[...])
    h1 = _layer_norm(dm.alpha * x + mix, w.ln_g[...], w.ln_b[...])
    logits = _mm(h1.astype(BF16), w.w_rt[...]) + w.b_rt[...]
    return h1, logits.T[0:N_EXPERTS, :]


N_MIXW = len(MixW._fields)


def _prompt_mixer_kernel(*refs, dm, tile, n_tiles):
    x_ref = refs[0]
    w = MixW(*refs[1:1 + N_MIXW])
    h1_ref, lg_ref, st_ref, buf_ref, q_s, k_s, v_s, gl_s, b_s, o_s, ext_s, st_t = refs[1 + N_MIXW:]
    lt = pl.program_id(1)

    @pl.when(lt == 0)
    def _():
        st_t[...] = jnp.zeros(st_t.shape, F32)
        ext_s[:, 0:POOL_WMAX, :] = jnp.zeros((1, POOL_WMAX, dm.pw), F32)

    x = x_ref[...]
    xb = x.astype(BF16)
    _project_qkv(xb, w, dm, q_s, k_s, v_s, gl_s)
    _chunk_cumsum(gl_s, b_s, tile, GLA_CHUNK)
    _gla_tile(q_s, k_s, v_s, b_s, o_s, st_t, dm=dm, rows=tile, chunk=GLA_CHUNK, group=GLA_GROUP)

    @pl.when(lt == n_tiles - 1)
    def _():
        for h in range(GLA_HEADS):
            st_ref[0, h] = st_t[h].T

    h1, logits_t = _mixer_tail(x, xb, o_s, ext_s, lt * tile, w, dm, nseq=1, seq_len=tile)
    _to_row_tiles(h1_ref, h1, tile)
    lg_ref[...] = logits_t
    ext_s[:, 0:POOL_WMAX, :] = ext_s[:, tile:tile + POOL_WMAX, :]

    @pl.when(lt == n_tiles - 1)
    def _():
        buf_ref[...] = ext_s[:, 1:POOL_WMAX, :]


def _sample_proj_kernel(*refs, dm):
    x_ref = refs[0]
    w = MixW(*refs[1:1 + N_MIXW])
    q_ref, k_ref, v_ref, gl_ref = refs[1 + N_MIXW:]
    _project_qkv(x_ref[...].astype(BF16), w, dm, q_ref, k_ref, v_ref, gl_ref)


def _sample_gla_kernel(q_ref, k_ref, v_ref, gl_ref, s0_ref, o_ref, st_ref, b_s, *, dm, nseq, seq_len):
    st_ref[...] = s0_ref[...]
    _gla(q_ref, k_ref, v_ref, gl_ref, b_s, o_ref, st_ref, dm=dm, n_chunks=nseq, chunk=seq_len, chunks_per_seq=1)


def _sample_tail_kernel(*refs, dm, nseq, seq_len):
    x_ref, o_ref, hist_ref = refs[0:3]
    w = MixW(*refs[3:3 + N_MIXW])
    h1_ref, lg_ref, buf_ref, ext_s = refs[3 + N_MIXW:]
    ext_s[:, 0:1, :] = jnp.zeros((nseq, 1, dm.pw), F32)
    ext_s[:, 1:POOL_WMAX, :] = hist_ref[...]
    x = x_ref[...]
    h1, logits_t = _mixer_tail(x, x.astype(BF16), o_ref, ext_s, POOL_BUF, w, dm, nseq=nseq, seq_len=seq_len)
    _to_row_tiles(h1_ref, h1, nseq * seq_len)
    lg_ref[...] = logits_t
    buf_ref[...] = ext_s[:, seq_len + 1:seq_len + POOL_WMAX, :]


def _const_spec(arr):
    nd = arr.ndim
    return pl.BlockSpec(arr.shape, lambda *_: (0,) * nd, pipeline_mode=pl.Buffered(1))


def _prompt_mixer(x, w, dm):
    bsz, seq, d = x.shape
    tile = PROMPT_TILE
    n_tiles = seq // tile
    m = bsz * seq
    scratch = [pltpu.VMEM((tile, dm.key), F32), pltpu.VMEM((tile, dm.key), F32), pltpu.VMEM((tile, dm.val), F32),
               pltpu.VMEM((tile, dm.key), F32), pltpu.VMEM((tile, dm.key), F32), pltpu.VMEM((tile, dm.val), F32),
               pltpu.VMEM((1, POOL_WMAX + tile, dm.pw), F32),
               pltpu.VMEM((GLA_HEADS, dm.dv, dm.dk), F32)]
    return pl.pallas_call(
        functools.partial(_prompt_mixer_kernel, dm=dm, tile=tile, n_tiles=n_tiles),
        grid=(bsz, n_tiles),
        in_specs=[pl.BlockSpec((tile, d), lambda b, t: (b * n_tiles + t, 0))] + [_const_spec(a) for a in w],
        out_specs=[pl.BlockSpec((tile * SUBLANES, LANES), lambda b, t: (b * n_tiles + t, 0)),
                   pl.BlockSpec((N_EXPERTS, tile), lambda b, t: (0, b * n_tiles + t)),
                   pl.BlockSpec((1, GLA_HEADS, dm.dk, dm.dv), lambda b, t: (b, 0, 0, 0)),
                   pl.BlockSpec((1, POOL_BUF, dm.pw), lambda b, t: (b, 0, 0))],
        out_shape=[jax.ShapeDtypeStruct((m * SUBLANES, LANES), F32),
                   jax.ShapeDtypeStruct((N_EXPERTS, m), F32),
                   jax.ShapeDtypeStruct((bsz, GLA_HEADS, dm.dk, dm.dv), F32),
                   jax.ShapeDtypeStruct((bsz, POOL_BUF, dm.pw), F32)],
        scratch_shapes=scratch,
        compiler_params=pltpu.CompilerParams(dimension_semantics=("arbitrary", "arbitrary"),
                                             vmem_limit_bytes=VMEM_LIMIT_BYTES),
        name="prompt_mixer",
    )(x.reshape(m, d), *w)


def _sample_mixer(x, s0, hist, w, dm):
    bsz, seq, d = x.shape
    m = bsz * seq
    x2 = x.reshape(m, d)
    tile = MIX_TILE
    cparams = pltpu.CompilerParams(dimension_semantics=("arbitrary",), vmem_limit_bytes=VMEM_LIMIT_BYTES)
    row = lambda n: pl.BlockSpec((tile, n), lambda i: (i, 0))
    q, k, v, gl = pl.pallas_call(
        functools.partial(_sample_proj_kernel, dm=dm),
        grid=(m // tile,),
        in_specs=[row(d)] + [_const_spec(a) for a in w],
        out_specs=[row(dm.key), row(dm.key), row(dm.val), row(dm.key)],
        out_shape=[jax.ShapeDtypeStruct((m, n), F32) for n in (dm.key, dm.key, dm.val, dm.key)],
        compiler_params=cparams,
        name="sample_proj",
    )(x2, *w)

    nseq = GLA_SEQS
    rows = nseq * seq
    grow = lambda n: pl.BlockSpec((rows, n), lambda i: (i, 0))
    st_spec = pl.BlockSpec((nseq, GLA_HEADS, dm.dk, dm.dv), lambda i: (i, 0, 0, 0))
    o, st = pl.pallas_call(
        functools.partial(_sample_gla_kernel, dm=dm, nseq=nseq, seq_len=seq),
        grid=(bsz // nseq,),
        in_specs=[grow(dm.key), grow(dm.key), grow(dm.val), grow(dm.key), st_spec],
        out_specs=[grow(dm.val), st_spec],
        out_shape=[jax.ShapeDtypeStruct((m, dm.val), F32), jax.ShapeDtypeStruct(s0.shape, F32)],
        scratch_shapes=[pltpu.VMEM((rows, dm.key), F32)],
        compiler_params=cparams,
        name="sample_gla",
    )(q, k, v, gl, s0)

    tseq = tile // seq
    hist_spec = pl.BlockSpec((tseq, POOL_BUF, dm.pw), lambda i: (i, 0, 0))
    h1, lg, buf = pl.pallas_call(
        functools.partial(_sample_tail_kernel, dm=dm, nseq=tseq, seq_len=seq),
        grid=(m // tile,),
        in_specs=[row(d), row(dm.val), hist_spec] + [_const_spec(a) for a in w],
        out_specs=[pl.BlockSpec((tile * SUBLANES, LANES), lambda i: (i, 0)),
                   pl.BlockSpec((N_EXPERTS, tile), lambda i: (0, i)), hist_spec],
        out_shape=[jax.ShapeDtypeStruct((m * SUBLANES, LANES), F32),
                   jax.ShapeDtypeStruct((N_EXPERTS, m), F32),
                   jax.ShapeDtypeStruct((bsz, POOL_BUF, dm.pw), F32)],
        scratch_shapes=[pltpu.VMEM((tseq, POOL_WMAX + seq, dm.pw), F32)],
        compiler_params=cparams,
        name="sample_tail",
    )(x2, o, hist, *w)
    return h1, lg, st, buf


def _router_kernel(lga_ref, lgb_ref, w_ref, s_ref, meta_ref, c_ref, carry, *, tile, n_tiles_a):
    i = pl.program_id(0)

    @pl.when(i == 0)
    def _():
        carry[...] = jnp.zeros(carry.shape, F32)

    eio = lax.broadcasted_iota(jnp.int32, (N_EXPERTS, tile), 0)
    work = jnp.where(i < n_tiles_a, lga_ref[...], lgb_ref[...])
    vals, hots = [], []
    for k in range(TOP_K):
        mx = jnp.max(work, axis=0, keepdims=True)
        idx = jnp.min(jnp.where(work == mx, eio, N_EXPERTS), axis=0, keepdims=True)
        hot = eio == idx
        vals.append(mx)
        hots.append(hot)
        work = jnp.where(hot, -jnp.inf, work)
    ex = [jnp.exp(v - vals[0]) for v in vals]
    den = ex[0]
    for k in range(1, TOP_K):
        den = den + ex[k]
    for k in range(TOP_K):
        w_ref[k:k + 1, :] = ex[k] / den
    sel = hots[0]
    for k in range(1, TOP_K):
        sel = sel | hots[k]
    r = lax.broadcasted_iota(jnp.int32, (tile, tile), 0)
    c = lax.broadcasted_iota(jnp.int32, (tile, tile), 1)
    earlier_token = (r < c).astype(BF16)
    rank_in_tile = _mm(sel.astype(BF16), earlier_token)
    cnt = jnp.broadcast_to(jnp.sum(sel.astype(F32), axis=1, keepdims=True), (N_EXPERTS, LANES))
    er = lax.broadcasted_iota(jnp.int32, (N_EXPERTS, N_EXPERTS), 0)
    ec = lax.broadcasted_iota(jnp.int32, (N_EXPERTS, N_EXPERTS), 1)
    first_slot = _mm((ec < er).astype(BF16), cnt.astype(BF16))
    slot = rank_in_tile + first_slot[:, 0:1]
    for k in range(TOP_K):
        s_ref[k:k + 1, :] = jnp.sum(jnp.where(hots[k], slot, 0.0), axis=0, keepdims=True).astype(jnp.int32)
    meta_ref[0, 0:N_EXPERTS, :] = cnt
    meta_ref[0, N_EXPERTS:2 * N_EXPERTS, :] = first_slot
    meta_ref[0, 2 * N_EXPERTS:3 * N_EXPERTS, :] = carry[...]
    carry[...] = carry[...] + cnt
    c_ref[...] = carry[...]


def _for_each_run(meta_s, sl, fn):
    def body(e, carry):
        cnt = meta_s[sl, e]
        for bit in range(RUN_BITS):
            @pl.when(((cnt >> bit) & 1) == 1)
            def _(bit=bit):
                done = cnt & ((1 << bit) - 1)
                fn(meta_s[sl, N_EXPERTS + e] + done, meta_s[sl, 2 * N_EXPERTS + e] + done, 1 << bit)
        return carry
    lax.fori_loop(0, N_EXPERTS, body, 0)


def _rows(ref, first, n):
    return ref.at[pl.ds(pl.multiple_of(first * SUBLANES, SUBLANES), n * SUBLANES), :]


def _dispatch_kernel(fill_start_ref, fill_cnt_ref, tail_start_ref, meta_hbm, slot_hbm, h1a_ref, h1b_ref, xs_hbm,
                     meta_s, slot_s, stage, zeros_s, sem_m, sem_s, sem_r, sem_z, *, tile, n_tiles, n_tiles_a):
    i = pl.program_id(0)
    sl = i % 2

    def index_copies(blk, s):
        return (pltpu.make_async_copy(meta_hbm.at[blk], meta_s.at[s], sem_m.at[s]),
                pltpu.make_async_copy(slot_hbm.at[blk], slot_s.at[s], sem_s.at[s]))

    def wait_stage(s):
        pltpu.make_async_copy(stage.at[s], stage.at[s], sem_r.at[s]).wait()

    @pl.when(i == 0)
    def _():
        for cp in index_copies(0, 0):
            cp.start()
        zeros_s[...] = jnp.zeros(zeros_s.shape, F32)

        def fill_copy(off, bit):
            n = 1 << bit
            return pltpu.make_async_copy(zeros_s.at[pl.ds(0, n * SUBLANES), :], _rows(xs_hbm, off, n), sem_z)

        def for_each_piece(fn):
            def body(e, carry):
                cnt = fill_cnt_ref[e]
                for bit in range(ZERO_FILL_BITS):
                    @pl.when(((cnt >> bit) & 1) == 1)
                    def _(bit=bit):
                        fn(fill_copy(fill_start_ref[e] + (cnt & ((1 << bit) - 1)), bit))
                return carry
            lax.fori_loop(0, N_EXPERTS, body, 0)

        top = ZERO_FILL_BITS - 1
        n_tail = (xs_hbm.shape[0] // SUBLANES - tail_start_ref[0]) >> top

        def for_each_tail_piece(fn):
            def body(j, carry):
                fn(fill_copy(tail_start_ref[0] + (j << top), top))
                return carry
            lax.fori_loop(0, n_tail, body, 0)

        for_each_piece(lambda cp: cp.start())
        for_each_tail_piece(lambda cp: cp.start())
        for_each_piece(lambda cp: cp.wait())
        for_each_tail_piece(lambda cp: cp.wait())

    @pl.when(i + 1 < n_tiles)
    def _():
        for cp in index_copies(i + 1, 1 - sl):
            cp.start()

    for cp in index_copies(i, sl):
        cp.wait()

    @pl.when(i >= 2)
    def _():
        wait_stage(sl)

    def group_rows(h1_ref):
        def group(g, carry):
            for u in range(DMA_UNROLL):
                t = g * DMA_UNROLL + u
                row = h1_ref[pl.ds(pl.multiple_of(t * SUBLANES, SUBLANES), SUBLANES), :]
                for k in range(TOP_K):
                    s = slot_s[sl, k * tile + t]
                    stage[sl, pl.ds(pl.multiple_of(s * SUBLANES, SUBLANES), SUBLANES), :] = row
            return carry
        lax.fori_loop(0, tile // DMA_UNROLL, group, 0)

    @pl.when(i < n_tiles_a)
    def _():
        group_rows(h1a_ref)

    @pl.when(i >= n_tiles_a)
    def _():
        group_rows(h1b_ref)

    _for_each_run(meta_s, sl, lambda s0, d0, n: pltpu.make_async_copy(
        _rows(stage.at[sl], s0, n), _rows(xs_hbm, d0, n), sem_r.at[sl]).start())

    @pl.when(i == n_tiles - 1)
    def _():
        @pl.when(n_tiles > 1)
        def _():
            wait_stage(1 - sl)
        wait_stage(sl)


def _expert_kernel(be_ref, nused_ref, xs_ref, wgu_ref, bgu_ref, wdn_ref, bdn_ref, out_ref, wgu_b, wdn_b,
                   *, rows, d_ff):
    i = pl.program_id(0)

    @pl.when((i == 0) | (be_ref[i] != be_ref[jnp.maximum(i - 1, 0)]))
    def _():
        wgu_b[...] = wgu_ref[0].astype(BF16)
        wdn_b[...] = wdn_ref[0].astype(BF16)

    @pl.when(i < nused_ref[0])
    def _():
        xb = _from_row_tiles(xs_ref, rows).astype(BF16)
        hh = _mm(xb, wgu_b[...]) + bgu_ref[0]
        gate = jnp.minimum(hh[:, 0:d_ff], SWIGLU_LIMIT)
        up = jnp.clip(hh[:, d_ff:2 * d_ff], -SWIGLU_LIMIT, SWIGLU_LIMIT)
        act = (up + 1.0) * (gate * jax.nn.sigmoid(GLU_ALPHA * gate))
        res = _mm(act.astype(BF16), wdn_b[...]) + bdn_ref[0]
        _to_row_tiles(out_ref, res, rows)

    @pl.when(i >= nused_ref[0])
    def _():
        out_ref[...] = jnp.zeros(out_ref.shape, F32)


def _combine_kernel(meta_hbm, slot_hbm, res_hbm, h1a_ref, h1b_ref, gw_ref, g_ref, b_ref, ya_ref, yb_ref,
                    meta_s, slot_s, stage, gbuf, sem_m, sem_s, sem_g, *, tile, alpha, n_tiles, n_tiles_a):
    i = pl.program_id(0)
    sl = i % 2

    def index_copies(blk, s):
        return (pltpu.make_async_copy(meta_hbm.at[blk], meta_s.at[s], sem_m.at[s]),
                pltpu.make_async_copy(slot_hbm.at[blk], slot_s.at[s], sem_s.at[s]))

    def fetch_runs(s):
        _for_each_run(meta_s, s, lambda s0, d0, n: pltpu.make_async_copy(
            _rows(res_hbm, d0, n), _rows(stage.at[s], s0, n), sem_g.at[s]).start())

    @pl.when(i == 0)
    def _():
        for cp in index_copies(0, 0):
            cp.start()
        for cp in index_copies(0, 0):
            cp.wait()
        fetch_runs(0)

        @pl.when(n_tiles > 1)
        def _():
            for cp in index_copies(1, 1):
                cp.start()

    @pl.when(i + 1 < n_tiles)
    def _():
        for cp in index_copies(i + 1, 1 - sl):
            cp.wait()
        fetch_runs(1 - sl)

    pltpu.make_async_copy(stage.at[sl], stage.at[sl], sem_g.at[sl]).wait()

    def group(g, carry):
        for u in range(DMA_UNROLL):
            t = g * DMA_UNROLL + u
            for k in range(TOP_K):
                s = slot_s[sl, k * tile + t]
                gbuf[k, pl.ds(pl.multiple_of(t * SUBLANES, SUBLANES), SUBLANES), :] = (
                    stage[sl, pl.ds(pl.multiple_of(s * SUBLANES, SUBLANES), SUBLANES), :])
        return carry
    lax.fori_loop(0, tile // DMA_UNROLL, group, 0)

    @pl.when(i + 2 < n_tiles)
    def _():
        for cp in index_copies(i + 2, sl):
            cp.start()

    gw = gw_ref[...]
    z = alpha * jnp.where(i < n_tiles_a, _from_row_tiles(h1a_ref, tile), _from_row_tiles(h1b_ref, tile))
    for k in range(TOP_K):
        z = z + gw[:, k:k + 1] * _from_row_tiles(gbuf.at[k], tile)
    y = _layer_norm(z, g_ref[...], b_ref[...])

    @pl.when(i < n_tiles_a)
    def _():
        ya_ref[...] = y

    @pl.when(i >= n_tiles_a)
    def _():
        yb_ref[...] = y


def _tile_major(a, tile):
    k, m = a.shape
    return a.reshape(k, m // tile, tile).transpose(1, 0, 2).reshape(m // tile, k * tile)


def _moe(h1_a, h1_b, logits_a, logits_b, w_gu, b_gu, w_down, b_down, ln_g, ln_b, dm):
    m_a = logits_a.shape[1]
    m = m_a + logits_b.shape[1]
    rows = EXPERT_ROWS
    d_ff = w_down.shape[1]
    tile = MOE_TILE
    n_tiles = m // tile
    n_tiles_a = m_a // tile
    cparams = pltpu.CompilerParams(dimension_semantics=("arbitrary",), vmem_limit_bytes=VMEM_LIMIT_BYTES)

    def two_group_specs(block, lane_axis):
        pick = (lambda j: (0, j)) if lane_axis else (lambda j: (j, 0))
        return [pl.BlockSpec(block, lambda i, *_: pick(jnp.minimum(i, n_tiles_a - 1))),
                pl.BlockSpec(block, lambda i, *_: pick(jnp.maximum(i - n_tiles_a, 0)))]

    kblk = lambda: pl.BlockSpec((TOP_K, tile), lambda i: (0, i))
    gate_w, slot, meta, counts = pl.pallas_call(
        functools.partial(_router_kernel, tile=tile, n_tiles_a=n_tiles_a),
        grid=(n_tiles,),
        in_specs=two_group_specs((N_EXPERTS, tile), True),
        out_specs=[kblk(), kblk(), pl.BlockSpec((1, 3 * N_EXPERTS, LANES), lambda i: (i, 0, 0)),
                   pl.BlockSpec((N_EXPERTS, LANES), lambda i: (0, 0))],
        out_shape=[jax.ShapeDtypeStruct((TOP_K, m), F32), jax.ShapeDtypeStruct((TOP_K, m), jnp.int32),
                   jax.ShapeDtypeStruct((n_tiles, 3 * N_EXPERTS, LANES), F32),
                   jax.ShapeDtypeStruct((N_EXPERTS, LANES), F32)],
        scratch_shapes=[pltpu.VMEM((N_EXPERTS, LANES), F32)],
        compiler_params=cparams,
        name="moe_router",
    )(logits_a, logits_b)

    counts = counts[:, 0].astype(jnp.int32)
    padded = (counts + rows - 1) // rows * rows
    pad_end = jnp.cumsum(padded)
    pad_start = pad_end - padded
    n_blocks = -(-(m * TOP_K + N_EXPERTS * (rows - 1)) // rows)
    n_rows = n_blocks * rows
    block_row0 = jnp.arange(n_blocks, dtype=jnp.int32) * rows
    block_e = jnp.minimum(jnp.sum((pad_end[None, :] <= block_row0[:, None]).astype(jnp.int32), axis=1), N_EXPERTS - 1)
    n_used = (pad_end[-1] // rows).astype(jnp.int32).reshape(1)
    meta = meta[:, :, 0].astype(jnp.int32)
    tile_meta = jnp.concatenate([meta[:, 0:2 * N_EXPERTS], meta[:, 2 * N_EXPERTS:] + pad_start[None, :],
                                 jnp.zeros((n_tiles, LANES - 3 * N_EXPERTS), jnp.int32)], axis=1)
    slot_t = _tile_major(slot, tile)

    h1_specs = two_group_specs((tile * SUBLANES, LANES), False)
    any_spec = pl.BlockSpec(memory_space=pl.ANY)
    index_scratch = [pltpu.SMEM((2, LANES), jnp.int32), pltpu.SMEM((2, TOP_K * tile), jnp.int32),
                     pltpu.VMEM((2, TOP_K * tile * SUBLANES, LANES), F32)]
    xs = pl.pallas_call(
        functools.partial(_dispatch_kernel, tile=tile, n_tiles=n_tiles, n_tiles_a=n_tiles_a),
        grid_spec=pltpu.PrefetchScalarGridSpec(
            num_scalar_prefetch=3,
            grid=(n_tiles,),
            in_specs=[any_spec, any_spec] + h1_specs,
            out_specs=any_spec,
            scratch_shapes=index_scratch + [
                pltpu.VMEM(((1 << (ZERO_FILL_BITS - 1)) * SUBLANES, LANES), F32),
                pltpu.SemaphoreType.DMA((2,)), pltpu.SemaphoreType.DMA((2,)), pltpu.SemaphoreType.DMA((2,)),
                pltpu.SemaphoreType.DMA]),
        out_shape=jax.ShapeDtypeStruct((n_rows * SUBLANES, LANES), F32),
        compiler_params=cparams,
        name="moe_dispatch",
    )(pad_start + counts, padded - counts, pad_end[-1:], tile_meta, slot_t, h1_a, h1_b)

    last = lambda i, nu: jnp.minimum(i, nu[0] - 1)
    res = pl.pallas_call(
        functools.partial(_expert_kernel, rows=rows, d_ff=d_ff),
        grid_spec=pltpu.PrefetchScalarGridSpec(
            num_scalar_prefetch=2,
            grid=(n_blocks,),
            in_specs=[pl.BlockSpec((rows * SUBLANES, LANES), lambda i, be, nu: (last(i, nu), 0)),
                      pl.BlockSpec((1, dm.d, 2 * d_ff), lambda i, be, nu: (be[i], 0, 0)),
                      pl.BlockSpec((1, 1, 2 * d_ff), lambda i, be, nu: (be[i], 0, 0)),
                      pl.BlockSpec((1, d_ff, dm.d), lambda i, be, nu: (be[i], 0, 0)),
                      pl.BlockSpec((1, 1, dm.d), lambda i, be, nu: (be[i], 0, 0))],
            out_specs=pl.BlockSpec((rows * SUBLANES, LANES), lambda i, be, nu: (i, 0)),
            scratch_shapes=[pltpu.VMEM((dm.d, 2 * d_ff), BF16), pltpu.VMEM((d_ff, dm.d), BF16)]),
        out_shape=jax.ShapeDtypeStruct((n_rows * SUBLANES, LANES), F32),
        compiler_params=cparams,
        name="moe_experts",
    )(block_e, n_used, xs, w_gu, b_gu[:, None, :], w_down, b_down[:, None, :])

    ya, yb = pl.pallas_call(
        functools.partial(_combine_kernel, tile=tile, alpha=dm.alpha, n_tiles=n_tiles, n_tiles_a=n_tiles_a),
        grid=(n_tiles,),
        in_specs=[any_spec, any_spec, any_spec] + h1_specs + [
                  pl.BlockSpec((tile, TOP_K), lambda i: (i, 0)),
                  pl.BlockSpec((1, dm.d), lambda i: (0, 0)),
                  pl.BlockSpec((1, dm.d), lambda i: (0, 0))],
        out_specs=[pl.BlockSpec((tile, dm.d), lambda i: (jnp.minimum(i, n_tiles_a - 1), 0)),
                   pl.BlockSpec((tile, dm.d), lambda i: (jnp.maximum(i - n_tiles_a, 0), 0))],
        out_shape=[jax.ShapeDtypeStruct((m_a, dm.d), F32), jax.ShapeDtypeStruct((m - m_a, dm.d), F32)],
        scratch_shapes=index_scratch + [
            pltpu.VMEM((TOP_K, tile * SUBLANES, LANES), F32),
            pltpu.SemaphoreType.DMA((2,)), pltpu.SemaphoreType.DMA((2,)), pltpu.SemaphoreType.DMA((2,))],
        compiler_params=cparams,
        name="moe_combine",
    )(tile_meta, slot_t, res, h1_a, h1_b, gate_w.T, ln_g[None, :], ln_b[None, :])
    return ya, yb


def _pad_cols(a, n):
    return jnp.pad(a, ((0, 0), (0, n - a.shape[1])))


def _mixer_weights(w_in, w_gk2, b_gk, gla_norm_w, w_branch_gla, w_pool_grp, pool_scale, w_branch_pool, b_gates,
                   w_out, ln_g, ln_b, w_router, b_router, dm):
    o_lr = 2 * dm.key + dm.val
    o_tail = o_lr + GATE_RANK
    return MixW(
        w_qkv=w_in[:, 0:o_lr].astype(BF16),
        w_lr=_pad_cols(w_in[:, o_lr:o_tail], LANES).astype(BF16),
        w_gk2=jnp.pad(w_gk2, ((0, LANES - GATE_RANK), (0, 0))).astype(BF16),
        b_gk=b_gk[None, :],
        w_tail=w_in[:, o_tail:].astype(BF16),
        gla_norm=gla_norm_w[None, :],
        w_ba=w_branch_gla.astype(BF16),
        w_pg=w_pool_grp.astype(BF16),
        pool_scale=pool_scale[None, :],
        w_bp=w_branch_pool.astype(BF16),
        b_gates=b_gates[None, :],
        w_out=w_out.astype(BF16),
        ln_g=ln_g[None, :],
        ln_b=ln_b[None, :],
        w_rt=_pad_cols(w_router, LANES).astype(BF16),
        b_rt=_pad_cols(b_router[None, :], LANES),
    )


def kernel(x_prompt, x_sample, state_gla, state_pool, w_in, w_gk2, b_gk, gla_norm_w, w_branch_gla, w_pool_grp,
           pool_scale, w_branch_pool, b_gates, w_out, ln1_g, ln1_b, w_router, b_router, w_gu, b_gu, w_down, b_down,
           ln2_g, ln2_b):
    depth = w_in.shape[0]
    bp, lp, d = x_prompt.shape
    bs, ls, _ = x_sample.shape
    assert d == SUBLANES * LANES and lp % PROMPT_TILE == 0 and (bs * ls) % MIX_TILE == 0 and MIX_TILE % ls == 0
    assert bs % GLA_SEQS == 0 and ls % SUBLANES == 0
    assert (bp * lp) % MOE_TILE == 0 and (bs * ls) % MOE_TILE == 0
    dm = _dims(d, depth)
    yp, ys = x_prompt, x_sample
    gla_p, pool_p, gla_s, pool_s = [], [], [], []
    for l in range(depth):
        w = _mixer_weights(w_in[l], w_gk2[l], b_gk[l], gla_norm_w[l], w_branch_gla[l], w_pool_grp[l], pool_scale[l],
                           w_branch_pool[l], b_gates[l], w_out[l], ln1_g[l], ln1_b[l], w_router[l], b_router[l], dm)
        h1p, lgp, sp, bufp = _prompt_mixer(yp, w, dm)
        h1s, lgs, ss, bufs = _sample_mixer(ys, state_gla[l], state_pool[l], w, dm)
        yp, ys = _moe(h1p, h1s, lgp, lgs, w_gu[l], b_gu[l], w_down[l], b_down[l], ln2_g[l], ln2_b[l], dm)
        yp = yp.reshape(bp, lp, d)
        ys = ys.reshape(bs, ls, d)
        gla_p.append(sp.astype(state_gla.dtype))
        pool_p.append(bufp.astype(state_pool.dtype))
        gla_s.append(ss.astype(state_gla.dtype))
        pool_s.append(bufs.astype(state_pool.dtype))
    return (yp, ys, jnp.stack(gla_p, 0), jnp.stack(pool_p, 0), jnp.stack(gla_s, 0), jnp.stack(pool_s, 0))
```

```python
import functools
from typing import NamedTuple

import jax
import jax.numpy as jnp
from jax import lax
from jax.experimental import pallas as pl
from jax.experimental.pallas import tpu as pltpu

F32 = jnp.float32
BF16 = jnp.bfloat16

GLA_HEADS = 4
GATE_RANK = 16
GATE_NORMALIZER = 16.0
GLA_CHUNK = 64
RMS_EPS = 1e-6
POOL_WINDOWS = (2, 4, 8, 16)
POOL_WMAX = 16
POOL_BUF = POOL_WMAX - 1
N_EXPERTS = 32
TOP_K = 4
SWIGLU_LIMIT = 7.0
GLU_ALPHA = 1.702
LN_EPS = 1e-5

LANES = 128
SUBLANES = 8
VMEM_LIMIT_BYTES = 56 * 1024 * 1024

PROMPT_TILE = 512
MIX_TILE = 256
GLA_GROUP = 128
GLA_SEQS = 8
EXPERT_ROWS = 512
MOE_TILE = 256
CUMSUM_BLOCK = 256
DMA_UNROLL = 8
ZERO_FILL_BITS = (EXPERT_ROWS - 1).bit_length()
RUN_BITS = MOE_TILE.bit_length()


class Dims(NamedTuple):
    d: int
    dk: int
    dv: int
    key: int
    val: int
    pw: int
    pgc: int
    alpha: float


def _dims(d, depth):
    return Dims(d=d, dk=d // 8, dv=d // 4, key=d // 2, val=d, pw=d // 2, pgc=d // 8, alpha=(2.0 * depth) ** 0.25)


class MixW(NamedTuple):
    w_qkv: object
    w_lr: object
    w_gk2: object
    b_gk: object
    w_tail: object
    gla_norm: object
    w_ba: object
    w_pg: object
    pool_scale: object
    w_bp: object
    b_gates: object
    w_out: object
    ln_g: object
    ln_b: object
    w_rt: object
    b_rt: object


def _mm(a, b):
    return jnp.dot(a, b, preferred_element_type=F32)


def _layer_norm(z, g, b):
    mu = jnp.mean(z, axis=-1, keepdims=True)
    zc = z - mu
    var = jnp.mean(zc * zc, axis=-1, keepdims=True)
    return zc * lax.rsqrt(var + LN_EPS) * g + b


def _to_row_tiles(ref, val, rows):
    for c in range(SUBLANES):
        ref[pl.ds(c, rows, stride=SUBLANES), :] = val[:, c * LANES:(c + 1) * LANES]


def _from_row_tiles(ref, rows):
    return jnp.concatenate([ref[pl.ds(c, rows, stride=SUBLANES), :] for c in range(SUBLANES)], axis=1)


def _project_qkv(xb, w, dm, q_s, k_s, v_s, gl_s):
    q_s[...] = _mm(xb, w.w_qkv[:, 0:dm.key]) * (dm.dk ** -0.5)
    k_s[...] = _mm(xb, w.w_qkv[:, dm.key:2 * dm.key])
    v_s[...] = _mm(xb, w.w_qkv[:, 2 * dm.key:2 * dm.key + dm.val])
    lr = _mm(xb, w.w_lr[...])
    gk = _mm(lr.astype(BF16), w.w_gk2[...]) + w.b_gk[...]
    gl_s[...] = (jnp.minimum(gk, 0.0) - jnp.log1p(jnp.exp(-jnp.abs(gk)))) / GATE_NORMALIZER


def _chunk_cumsum(gl_s, b_s, rows, chunk):
    blk = min(rows, CUMSUM_BLOCK)
    r = lax.broadcasted_iota(jnp.int32, (blk, blk), 0)
    c = lax.broadcasted_iota(jnp.int32, (blk, blk), 1)
    tri = ((r // chunk == c // chunk) & (c <= r)).astype(BF16)
    for b0 in range(0, rows, blk):
        gl = gl_s[b0:b0 + blk, :]
        hi = gl.astype(BF16)
        lo = (gl - hi.astype(F32)).astype(BF16)
        b_s[b0:b0 + blk, :] = _mm(tri, hi) + _mm(tri, lo)


NT_DIMS = (((1,), (1,)), ((), ()))
TN_DIMS = (((0,), (0,)), ((), ()))


def _gla_tile(q_s, k_s, v_s, b_s, o_s, st_t, *, dm, rows, chunk, group):
    n_chunks = rows // chunk
    gr = lax.broadcasted_iota(jnp.int32, (group, group), 0)
    gc = lax.broadcasted_iota(jnp.int32, (group, group), 1)
    mask = (gr // chunk == gc // chunk) & (gc <= gr)

    def per_chunk_row(b, row):
        return jnp.concatenate([jnp.broadcast_to(b[c * chunk + row:c * chunk + row + 1, :], (chunk, b.shape[1]))
                                for c in range(n_chunks)], axis=0)

    for h in range(GLA_HEADS):
        ks = slice(h * dm.dk, (h + 1) * dm.dk)
        vs = slice(h * dm.dv, (h + 1) * dm.dv)
        b = b_s[:, ks]
        b_ref = per_chunk_row(b, chunk // 2)
        b_last = per_chunk_row(b, chunk - 1)
        q = q_s[:, ks]
        k = k_s[:, ks]
        vb = v_s[:, vs].astype(BF16)
        qa = (q * jnp.exp(b - b_ref)).astype(BF16)
        ka = (k * jnp.exp(b_ref - b)).astype(BF16)
        qe = (q * jnp.exp(b)).astype(BF16)
        kd = (k * jnp.exp(b_last - b)).astype(BF16)
        for g0 in range(0, rows, group):
            a = lax.dot_general(qa[g0:g0 + group], ka[g0:g0 + group], NT_DIMS, preferred_element_type=F32)
            a = jnp.where(mask, a, 0.0)
            o_s[g0:g0 + group, vs] = _mm(a.astype(BF16), vb[g0:g0 + group])
        s_t = st_t[h]
        for c in range(n_chunks):
            r0 = c * chunk
            o_s[r0:r0 + chunk, vs] += lax.dot_general(qe[r0:r0 + chunk], s_t.astype(BF16), NT_DIMS,
                                                      preferred_element_type=F32)
            decay = jnp.exp(b[r0 + chunk - 1:r0 + chunk, :])
            s_t = s_t * decay + lax.dot_general(vb[r0:r0 + chunk], kd[r0:r0 + chunk], TN_DIMS,
                                                preferred_element_type=F32)
        st_t[h] = s_t


def _gla(q_s, k_s, v_s, gl_s, b_s, o_s, st_ref, *, dm, n_chunks, chunk, chunks_per_seq):
    _chunk_cumsum(gl_s, b_s, n_chunks * chunk, chunk)
    ri = lax.broadcasted_iota(jnp.int32, (chunk, chunk), 0)
    ci = lax.broadcasted_iota(jnp.int32, (chunk, chunk), 1)
    causal = ci <= ri
    nt, tn = NT_DIMS, TN_DIMS

    def step(idx, carry):
        r0 = pl.multiple_of(idx * chunk, chunk)
        seq = idx // chunks_per_seq
        for h in range(GLA_HEADS):
            ks = slice(h * dm.dk, (h + 1) * dm.dk)
            vs = slice(h * dm.dv, (h + 1) * dm.dv)
            bc = b_s[pl.ds(r0, chunk), ks]
            b_ref = bc[chunk // 2:chunk // 2 + 1, :]
            b_last = bc[chunk - 1:chunk, :]
            qc = q_s[pl.ds(r0, chunk), ks]
            kc = k_s[pl.ds(r0, chunk), ks]
            vc = v_s[pl.ds(r0, chunk), vs].astype(BF16)
            s_old = st_ref[seq, h]
            a = lax.dot_general((qc * jnp.exp(bc - b_ref)).astype(BF16), (kc * jnp.exp(b_ref - bc)).astype(BF16),
                                nt, preferred_element_type=F32)
            a = jnp.where(causal, a, 0.0)
            o = _mm(a.astype(BF16), vc) + _mm((qc * jnp.exp(bc)).astype(BF16), s_old.astype(BF16))
            o_s[pl.ds(r0, chunk), vs] = o
            e_col = jnp.broadcast_to(jnp.exp(b_last), (dm.dk, dm.dk)).T
            decay = jnp.concatenate([e_col] * (dm.dv // dm.dk), axis=1)
            kv = lax.dot_general((kc * jnp.exp(b_last - bc)).astype(BF16), vc, tn, preferred_element_type=F32)
            st_ref[seq, h] = decay * s_old + kv
        return carry

    lax.fori_loop(0, n_chunks, step, 0)


def _mixer_tail(x, xb, o_s, ext_s, pos0, w, dm, *, nseq, seq_len):
    t = nseq * seq_len
    off_u = dm.val
    off_ga = off_u + dm.pw
    off_gb = off_ga + dm.d

    g = _mm(xb, w.w_tail[:, 0:dm.val])
    parts = []
    for h in range(GLA_HEADS):
        vs = slice(h * dm.dv, (h + 1) * dm.dv)
        oh = o_s[:, vs]
        ms = jnp.mean(oh * oh, axis=-1, keepdims=True)
        on = oh * lax.rsqrt(ms + RMS_EPS) * w.gla_norm[...]
        gh = g[:, vs]
        parts.append((on * (gh * jax.nn.sigmoid(gh))).astype(BF16))
    branch_a = _mm(jnp.concatenate(parts, axis=1), w.w_ba[...])

    u = _mm(xb, w.w_tail[:, off_u:off_u + dm.pw])
    ext_s[:, POOL_WMAX:POOL_WMAX + seq_len, :] = u.reshape(nseq, seq_len, dm.pw)
    p = lax.broadcasted_iota(jnp.int32, (nseq, seq_len, dm.pgc), 1)
    pooled = []
    for gi, win in enumerate(POOL_WINDOWS):
        cs = slice(gi * dm.pgc, (gi + 1) * dm.pgc)
        cur = ext_s[:, POOL_WMAX:POOL_WMAX + seq_len, cs]
        acc = cur
        for j in range(1, win):
            acc = acc + ext_s[:, POOL_WMAX - j:POOL_WMAX - j + seq_len, cs]
        cnt = jnp.minimum(win, p + (pos0 + 1)).astype(F32)
        pg = (acc / cnt - cur).reshape(t, dm.pgc)
        pg = _mm(pg.astype(BF16), w.w_pg[gi]) * w.pool_scale[:, cs]
        pooled.append(pg.astype(BF16))
    branch_b = _mm(jnp.concatenate(pooled, axis=1), w.w_bp[...])

    gate_a = jax.nn.sigmoid(_mm(xb, w.w_tail[:, off_ga:off_ga + dm.d]) + w.b_gates[:, 0:dm.d])
    merged = gate_a * branch_a
    gate_b = jax.nn.sigmoid(_mm(xb, w.w_tail[:, off_gb:off_gb + dm.d]) + w.b_gates[:, dm.d:2 * dm.d])
    merged = merged + gate_b * branch_b
    mix = _mm(merged.astype(BF16), w.w_out[...])
    h1 = _layer_norm(dm.alpha * x + mix, w.ln_g[...], w.ln_b[...])
    logits = _mm(h1.astype(BF16), w.w_rt[...]) + w.b_rt[...]
    return h1, logits.T[0:N_EXPERTS, :]


N_MIXW = len(MixW._fields)


def _prompt_mixer_kernel(*refs, dm, tile, n_tiles):
    x_ref = refs[0]
    w = MixW(*refs[1:1 + N_MIXW])
    h1_ref, lg_ref, st_ref, buf_ref, q_s, k_s, v_s, gl_s, b_s, o_s, ext_s, st_t = refs[1 + N_MIXW:]
    lt = pl.program_id(1)

    @pl.when(lt == 0)
    def _():
        st_t[...] = jnp.zeros(st_t.shape, F32)
        ext_s[:, 0:POOL_WMAX, :] = jnp.zeros((1, POOL_WMAX, dm.pw), F32)

    x = x_ref[...]
    xb = x.astype(BF16)
    _project_qkv(xb, w, dm, q_s, k_s, v_s, gl_s)
    _chunk_cumsum(gl_s, b_s, tile, GLA_CHUNK)
    _gla_tile(q_s, k_s, v_s, b_s, o_s, st_t, dm=dm, rows=tile, chunk=GLA_CHUNK, group=GLA_GROUP)

    @pl.when(lt == n_tiles - 1)
    def _():
        for h in range(GLA_HEADS):
            st_ref[0, h] = st_t[h].T

    h1, logits_t = _mixer_tail(x, xb, o_s, ext_s, lt * tile, w, dm, nseq=1, seq_len=tile)
    _to_row_tiles(h1_ref, h1, tile)
    lg_ref[...] = logits_t
    ext_s[:, 0:POOL_WMAX, :] = ext_s[:, tile:tile + POOL_WMAX, :]

    @pl.when(lt == n_tiles - 1)
    def _():
        buf_ref[...] = ext_s[:, 1:POOL_WMAX, :]


def _sample_proj_kernel(*refs, dm):
    x_ref = refs[0]
    w = MixW(*refs[1:1 + N_MIXW])
    q_ref, k_ref, v_ref, gl_ref = refs[1 + N_MIXW:]
    _project_qkv(x_ref[...].astype(BF16), w, dm, q_ref, k_ref, v_ref, gl_ref)


def _sample_gla_kernel(q_ref, k_ref, v_ref, gl_ref, s0_ref, o_ref, st_ref, b_s, *, dm, nseq, seq_len):
    st_ref[...] = s0_ref[...]
    _gla(q_ref, k_ref, v_ref, gl_ref, b_s, o_ref, st_ref, dm=dm, n_chunks=nseq, chunk=seq_len, chunks_per_seq=1)


def _sample_tail_kernel(*refs, dm, nseq, seq_len):
    x_ref, o_ref, hist_ref = refs[0:3]
    w = MixW(*refs[3:3 + N_MIXW])
    h1_ref, lg_ref, buf_ref, ext_s = refs[3 + N_MIXW:]
    ext_s[:, 0:1, :] = jnp.zeros((nseq, 1, dm.pw), F32)
    ext_s[:, 1:POOL_WMAX, :] = hist_ref[...]
    x = x_ref[...]
    h1, logits_t = _mixer_tail(x, x.astype(BF16), o_ref, ext_s, POOL_BUF, w, dm, nseq=nseq, seq_len=seq_len)
    _to_row_tiles(h1_ref, h1, nseq * seq_len)
    lg_ref[...] = logits_t
    buf_ref[...] = ext_s[:, seq_len + 1:seq_len + POOL_WMAX, :]


def _const_spec(arr):
    nd = arr.ndim
    return pl.BlockSpec(arr.shape, lambda *_: (0,) * nd, pipeline_mode=pl.Buffered(1))


def _prompt_mixer(x, w, dm):
    bsz, seq, d = x.shape
    tile = PROMPT_TILE
    n_tiles = seq // tile
    m = bsz * seq
    scratch = [pltpu.VMEM((tile, dm.key), F32), pltpu.VMEM((tile, dm.key), F32), pltpu.VMEM((tile, dm.val), F32),
               pltpu.VMEM((tile, dm.key), F32), pltpu.VMEM((tile, dm.key), F32), pltpu.VMEM((tile, dm.val), F32),
               pltpu.VMEM((1, POOL_WMAX + tile, dm.pw), F32),
               pltpu.VMEM((GLA_HEADS, dm.dv, dm.dk), F32)]
    return pl.pallas_call(
        functools.partial(_prompt_mixer_kernel, dm=dm, tile=tile, n_tiles=n_tiles),
        grid=(bsz, n_tiles),
        in_specs=[pl.BlockSpec((tile, d), lambda b, t: (b * n_tiles + t, 0))] + [_const_spec(a) for a in w],
        out_specs=[pl.BlockSpec((tile * SUBLANES, LANES), lambda b, t: (b * n_tiles + t, 0)),
                   pl.BlockSpec((N_EXPERTS, tile), lambda b, t: (0, b * n_tiles + t)),
                   pl.BlockSpec((1, GLA_HEADS, dm.dk, dm.dv), lambda b, t: (b, 0, 0, 0)),
                   pl.BlockSpec((1, POOL_BUF, dm.pw), lambda b, t: (b, 0, 0))],
        out_shape=[jax.ShapeDtypeStruct((m * SUBLANES, LANES), F32),
                   jax.ShapeDtypeStruct((N_EXPERTS, m), F32),
                   jax.ShapeDtypeStruct((bsz, GLA_HEADS, dm.dk, dm.dv), F32),
                   jax.ShapeDtypeStruct((bsz, POOL_BUF, dm.pw), F32)],
        scratch_shapes=scratch,
        compiler_params=pltpu.CompilerParams(dimension_semantics=("arbitrary", "arbitrary"),
                                             vmem_limit_bytes=VMEM_LIMIT_BYTES),
        name="prompt_mixer",
    )(x.reshape(m, d), *w)


def _sample_mixer(x, s0, hist, w, dm):
    bsz, seq, d = x.shape
    m = bsz * seq
    x2 = x.reshape(m, d)
    tile = MIX_TILE
    cparams = pltpu.CompilerParams(dimension_semantics=("arbitrary",), vmem_limit_bytes=VMEM_LIMIT_BYTES)
    row = lambda n: pl.BlockSpec((tile, n), lambda i: (i, 0))
    q, k, v, gl = pl.pallas_call(
        functools.partial(_sample_proj_kernel, dm=dm),
        grid=(m // tile,),
        in_specs=[row(d)] + [_const_spec(a) for a in w],
        out_specs=[row(dm.key), row(dm.key), row(dm.val), row(dm.key)],
        out_shape=[jax.ShapeDtypeStruct((m, n), F32) for n in (dm.key, dm.key, dm.val, dm.key)],
        compiler_params=cparams,
        name="sample_proj",
    )(x2, *w)

    nseq = GLA_SEQS
    rows = nseq * seq
    grow = lambda n: pl.BlockSpec((rows, n), lambda i: (i, 0))
    st_spec = pl.BlockSpec((nseq, GLA_HEADS, dm.dk, dm.dv), lambda i: (i, 0, 0, 0))
    o, st = pl.pallas_call(
        functools.partial(_sample_gla_kernel, dm=dm, nseq=nseq, seq_len=seq),
        grid=(bsz // nseq,),
        in_specs=[grow(dm.key), grow(dm.key), grow(dm.val), grow(dm.key), st_spec],
        out_specs=[grow(dm.val), st_spec],
        out_shape=[jax.ShapeDtypeStruct((m, dm.val), F32), jax.ShapeDtypeStruct(s0.shape, F32)],
        scratch_shapes=[pltpu.VMEM((rows, dm.key), F32)],
        compiler_params=cparams,
        name="sample_gla",
    )(q, k, v, gl, s0)

    tseq = tile // seq
    hist_spec = pl.BlockSpec((tseq, POOL_BUF, dm.pw), lambda i: (i, 0, 0))
    h1, lg, buf = pl.pallas_call(
        functools.partial(_sample_tail_kernel, dm=dm, nseq=tseq, seq_len=seq),
        grid=(m // tile,),
        in_specs=[row(d), row(dm.val), hist_spec] + [_const_spec(a) for a in w],
        out_specs=[pl.BlockSpec((tile * SUBLANES, LANES), lambda i: (i, 0)),
                   pl.BlockSpec((N_EXPERTS, tile), lambda i: (0, i)), hist_spec],
        out_shape=[jax.ShapeDtypeStruct((m * SUBLANES, LANES), F32),
                   jax.ShapeDtypeStruct((N_EXPERTS, m), F32),
                   jax.ShapeDtypeStruct((bsz, POOL_BUF, dm.pw), F32)],
        scratch_shapes=[pltpu.VMEM((tseq, POOL_WMAX + seq, dm.pw), F32)],
        compiler_params=cparams,
        name="sample_tail",
    )(x2, o, hist, *w)
    return h1, lg, st, buf


def _router_kernel(lga_ref, lgb_ref, w_ref, s_ref, meta_ref, c_ref, carry, *, tile, n_tiles_a):
    i = pl.program_id(0)

    @pl.when(i == 0)
    def _():
        carry[...] = jnp.zeros(carry.shape, F32)

    eio = lax.broadcasted_iota(jnp.int32, (N_EXPERTS, tile), 0)
    work = jnp.where(i < n_tiles_a, lga_ref[...], lgb_ref[...])
    vals, hots = [], []
    for k in range(TOP_K):
        mx = jnp.max(work, axis=0, keepdims=True)
        idx = jnp.min(jnp.where(work == mx, eio, N_EXPERTS), axis=0, keepdims=True)
        hot = eio == idx
        vals.append(mx)
        hots.append(hot)
        work = jnp.where(hot, -jnp.inf, work)
    ex = [jnp.exp(v - vals[0]) for v in vals]
    den = ex[0]
    for k in range(1, TOP_K):
        den = den + ex[k]
    for k in range(TOP_K):
        w_ref[k:k + 1, :] = ex[k] / den
    sel = hots[0]
    for k in range(1, TOP_K):
        sel = sel | hots[k]
    r = lax.broadcasted_iota(jnp.int32, (tile, tile), 0)
    c = lax.broadcasted_iota(jnp.int32, (tile, tile), 1)
    earlier_token = (r < c).astype(BF16)
    rank_in_tile = _mm(sel.astype(BF16), earlier_token)
    cnt = jnp.broadcast_to(jnp.sum(sel.astype(F32), axis=1, keepdims=True), (N_EXPERTS, LANES))
    er = lax.broadcasted_iota(jnp.int32, (N_EXPERTS, N_EXPERTS), 0)
    ec = lax.broadcasted_iota(jnp.int32, (N_EXPERTS, N_EXPERTS), 1)
    first_slot = _mm((ec < er).astype(BF16), cnt.astype(BF16))
    slot = rank_in_tile + first_slot[:, 0:1]
    for k in range(TOP_K):
        s_ref[k:k + 1, :] = jnp.sum(jnp.where(hots[k], slot, 0.0), axis=0, keepdims=True).astype(jnp.int32)
    meta_ref[0, 0:N_EXPERTS, :] = cnt
    meta_ref[0, N_EXPERTS:2 * N_EXPERTS, :] = first_slot
    meta_ref[0, 2 * N_EXPERTS:3 * N_EXPERTS, :] = carry[...]
    carry[...] = carry[...] + cnt
    c_ref[...] = carry[...]


def _for_each_run(meta, fn):
    def body(e, carry):
        cnt = meta[e]
        first = meta[N_EXPERTS + e]
        dst = meta[2 * N_EXPERTS + e]
        for bit in range(RUN_BITS):
            @pl.when(((cnt >> bit) & 1) == 1)
            def _(bit=bit):
                done = cnt & ((1 << bit) - 1)
                fn(first + done, dst + done, 1 << bit)
        return carry
    lax.fori_loop(0, N_EXPERTS, body, 0)


def _rows(ref, first, n):
    return ref.at[pl.ds(pl.multiple_of(first * SUBLANES, SUBLANES), n * SUBLANES), :]


def _dispatch_kernel(fill_start_ref, fill_cnt_ref, tail_start_ref, meta_hbm, slot_hbm, h1a_ref, h1b_ref, xs_hbm,
                     meta_s0, meta_s1, slot_s0, slot_s1, stage, zeros_s, sem_m, sem_s, sem_r, sem_z,
                     *, tile, n_tiles, n_tiles_a):
    i = pl.program_id(0)
    sl = i % 2
    meta_s = (meta_s0, meta_s1)
    slot_s = (slot_s0, slot_s1)

    def index_copies(blk, s):
        return (pltpu.make_async_copy(meta_hbm.at[blk], meta_s[s], sem_m.at[s]),
                pltpu.make_async_copy(slot_hbm.at[blk], slot_s[s], sem_s.at[s]))

    def wait_stage(s):
        pltpu.make_async_copy(stage.at[s], stage.at[s], sem_r.at[s]).wait()

    @pl.when(i == 0)
    def _():
        for cp in index_copies(0, 0):
            cp.start()
        zeros_s[...] = jnp.zeros(zeros_s.shape, F32)

        def fill_copy(off, bit):
            n = 1 << bit
            return pltpu.make_async_copy(zeros_s.at[pl.ds(0, n * SUBLANES), :], _rows(xs_hbm, off, n), sem_z)

        def for_each_piece(fn):
            def body(e, carry):
                cnt = fill_cnt_ref[e]
                for bit in range(ZERO_FILL_BITS):
                    @pl.when(((cnt >> bit) & 1) == 1)
                    def _(bit=bit):
                        fn(fill_copy(fill_start_ref[e] + (cnt & ((1 << bit) - 1)), bit))
                return carry
            lax.fori_loop(0, N_EXPERTS, body, 0)

        top = ZERO_FILL_BITS - 1
        n_tail = (xs_hbm.shape[0] // SUBLANES - tail_start_ref[0]) >> top

        def for_each_tail_piece(fn):
            def body(j, carry):
                fn(fill_copy(tail_start_ref[0] + (j << top), top))
                return carry
            lax.fori_loop(0, n_tail, body, 0)

        for_each_piece(lambda cp: cp.start())
        for_each_tail_piece(lambda cp: cp.start())
        for_each_piece(lambda cp: cp.wait())
        for_each_tail_piece(lambda cp: cp.wait())

    def group_rows(h1_ref, s_):
        def group(g, carry):
            for u in range(DMA_UNROLL):
                t = g * DMA_UNROLL + u
                row = h1_ref[pl.ds(pl.multiple_of(t * SUBLANES, SUBLANES), SUBLANES), :]
                for k in range(TOP_K):
                    s = slot_s[s_][k * tile + t]
                    stage[s_, pl.ds(pl.multiple_of(s * SUBLANES, SUBLANES), SUBLANES), :] = row
            return carry
        lax.fori_loop(0, tile // DMA_UNROLL, group, 0)

    for s_ in range(2):
        @pl.when((sl == 1 - s_) & (i + 1 < n_tiles))
        def _(s_=s_):
            for cp in index_copies(i + 1, s_):
                cp.start()

    for s_ in range(2):
        @pl.when(sl == s_)
        def _(s_=s_):
            for cp in index_copies(i, s_):
                cp.wait()

            @pl.when(i >= 2)
            def _():
                wait_stage(s_)

            @pl.when(i < n_tiles_a)
            def _():
                group_rows(h1a_ref, s_)

            @pl.when(i >= n_tiles_a)
            def _():
                group_rows(h1b_ref, s_)

            _for_each_run(meta_s[s_], lambda s0, d0, n: pltpu.make_async_copy(
                _rows(stage.at[s_], s0, n), _rows(xs_hbm, d0, n), sem_r.at[s_]).start())

    @pl.when(i == n_tiles - 1)
    def _():
        @pl.when(n_tiles > 1)
        def _():
            wait_stage(1 - sl)
        wait_stage(sl)


def _expert_kernel(be_ref, nused_ref, xs_ref, wgu_ref, bgu_ref, wdn_ref, bdn_ref, out_ref, wgu_b, wdn_b,
                   *, rows, d_ff):
    i = pl.program_id(0)

    @pl.when((i == 0) | (be_ref[i] != be_ref[jnp.maximum(i - 1, 0)]))
    def _():
        wgu_b[...] = wgu_ref[0].astype(BF16)
        wdn_b[...] = wdn_ref[0].astype(BF16)

    @pl.when(i < nused_ref[0])
    def _():
        xb = _from_row_tiles(xs_ref, rows).astype(BF16)
        hh = _mm(xb, wgu_b[...]) + bgu_ref[0]
        gate = jnp.minimum(hh[:, 0:d_ff], SWIGLU_LIMIT)
        up = jnp.clip(hh[:, d_ff:2 * d_ff], -SWIGLU_LIMIT, SWIGLU_LIMIT)
        act = (up + 1.0) * (gate * jax.nn.sigmoid(GLU_ALPHA * gate))
        res = _mm(act.astype(BF16), wdn_b[...]) + bdn_ref[0]
        _to_row_tiles(out_ref, res, rows)

    @pl.when(i >= nused_ref[0])
    def _():
        out_ref[...] = jnp.zeros(out_ref.shape, F32)


def _combine_kernel(meta_hbm, slot_hbm, res_hbm, h1a_ref, h1b_ref, gw_ref, g_ref, b_ref, ya_ref, yb_ref,
                    meta_s0, meta_s1, slot_s0, slot_s1, stage, gbuf, sem_m, sem_s, sem_g,
                    *, tile, alpha, n_tiles, n_tiles_a):
    i = pl.program_id(0)
    sl = i % 2
    meta_s = (meta_s0, meta_s1)
    slot_s = (slot_s0, slot_s1)

    def index_copies(blk, s):
        return (pltpu.make_async_copy(meta_hbm.at[blk], meta_s[s], sem_m.at[s]),
                pltpu.make_async_copy(slot_hbm.at[blk], slot_s[s], sem_s.at[s]))

    def fetch_runs(s):
        _for_each_run(meta_s[s], lambda s0, d0, n: pltpu.make_async_copy(
            _rows(res_hbm, d0, n), _rows(stage.at[s], s0, n), sem_g.at[s]).start())

    @pl.when(i == 0)
    def _():
        for cp in index_copies(0, 0):
            cp.start()
        for cp in index_copies(0, 0):
            cp.wait()
        fetch_runs(0)

        @pl.when(n_tiles > 1)
        def _():
            for cp in index_copies(1, 1):
                cp.start()

    def regroup(s_):
        def group(g, carry):
            for u in range(DMA_UNROLL):
                t = g * DMA_UNROLL + u
                for k in range(TOP_K):
                    s = slot_s[s_][k * tile + t]
                    gbuf[k, pl.ds(pl.multiple_of(t * SUBLANES, SUBLANES), SUBLANES), :] = (
                        stage[s_, pl.ds(pl.multiple_of(s * SUBLANES, SUBLANES), SUBLANES), :])
            return carry
        lax.fori_loop(0, tile // DMA_UNROLL, group, 0)

    for s_ in range(2):
        @pl.when((sl == 1 - s_) & (i + 1 < n_tiles))
        def _(s_=s_):
            for cp in index_copies(i + 1, s_):
                cp.wait()
            fetch_runs(s_)

    for s_ in range(2):
        @pl.when(sl == s_)
        def _(s_=s_):
            pltpu.make_async_copy(stage.at[s_], stage.at[s_], sem_g.at[s_]).wait()
            regroup(s_)

    for s_ in range(2):
        @pl.when((sl == s_) & (i + 2 < n_tiles))
        def _(s_=s_):
            for cp in index_copies(i + 2, s_):
                cp.start()

    gw = gw_ref[...]
    z = alpha * jnp.where(i < n_tiles_a, _from_row_tiles(h1a_ref, tile), _from_row_tiles(h1b_ref, tile))
    for k in range(TOP_K):
        z = z + gw[:, k:k + 1] * _from_row_tiles(gbuf.at[k], tile)
    y = _layer_norm(z, g_ref[...], b_ref[...])

    @pl.when(i < n_tiles_a)
    def _():
        ya_ref[...] = y

    @pl.when(i >= n_tiles_a)
    def _():
        yb_ref[...] = y


def _tile_major(a, tile):
    k, m = a.shape
    return a.reshape(k, m // tile, tile).transpose(1, 0, 2).reshape(m // tile, k * tile)


def _moe(h1_a, h1_b, logits_a, logits_b, w_gu, b_gu, w_down, b_down, ln_g, ln_b, dm):
    m_a = logits_a.shape[1]
    m = m_a + logits_b.shape[1]
    rows = EXPERT_ROWS
    d_ff = w_down.shape[1]
    tile = MOE_TILE
    n_tiles = m // tile
    n_tiles_a = m_a // tile
    cparams = pltpu.CompilerParams(dimension_semantics=("arbitrary",), vmem_limit_bytes=VMEM_LIMIT_BYTES)

    def two_group_specs(block, lane_axis):
        pick = (lambda j: (0, j)) if lane_axis else (lambda j: (j, 0))
        return [pl.BlockSpec(block, lambda i, *_: pick(jnp.minimum(i, n_tiles_a - 1))),
                pl.BlockSpec(block, lambda i, *_: pick(jnp.maximum(i - n_tiles_a, 0)))]

    kblk = lambda: pl.BlockSpec((TOP_K, tile), lambda i: (0, i))
    gate_w, slot, meta, counts = pl.pallas_call(
        functools.partial(_router_kernel, tile=tile, n_tiles_a=n_tiles_a),
        grid=(n_tiles,),
        in_specs=two_group_specs((N_EXPERTS, tile), True),
        out_specs=[kblk(), kblk(), pl.BlockSpec((1, 3 * N_EXPERTS, LANES), lambda i: (i, 0, 0)),
                   pl.BlockSpec((N_EXPERTS, LANES), lambda i: (0, 0))],
        out_shape=[jax.ShapeDtypeStruct((TOP_K, m), F32), jax.ShapeDtypeStruct((TOP_K, m), jnp.int32),
                   jax.ShapeDtypeStruct((n_tiles, 3 * N_EXPERTS, LANES), F32),
                   jax.ShapeDtypeStruct((N_EXPERTS, LANES), F32)],
        scratch_shapes=[pltpu.VMEM((N_EXPERTS, LANES), F32)],
        compiler_params=cparams,
        name="moe_router",
    )(logits_a, logits_b)

    counts = counts[:, 0].astype(jnp.int32)
    padded = (counts + rows - 1) // rows * rows
    pad_end = jnp.cumsum(padded)
    pad_start = pad_end - padded
    n_blocks = -(-(m * TOP_K + N_EXPERTS * (rows - 1)) // rows)
    n_rows = n_blocks * rows
    block_row0 = jnp.arange(n_blocks, dtype=jnp.int32) * rows
    block_e = jnp.minimum(jnp.sum((pad_end[None, :] <= block_row0[:, None]).astype(jnp.int32), axis=1), N_EXPERTS - 1)
    n_used = (pad_end[-1] // rows).astype(jnp.int32).reshape(1)
    meta = meta[:, :, 0].astype(jnp.int32)
    tile_meta = jnp.concatenate([meta[:, 0:2 * N_EXPERTS], meta[:, 2 * N_EXPERTS:] + pad_start[None, :],
                                 jnp.zeros((n_tiles, LANES - 3 * N_EXPERTS), jnp.int32)], axis=1)
    slot_t = _tile_major(slot, tile)

    h1_specs = two_group_specs((tile * SUBLANES, LANES), False)
    any_spec = pl.BlockSpec(memory_space=pl.ANY)
    index_scratch = [pltpu.SMEM((LANES,), jnp.int32), pltpu.SMEM((LANES,), jnp.int32),
                     pltpu.SMEM((TOP_K * tile,), jnp.int32), pltpu.SMEM((TOP_K * tile,), jnp.int32),
                     pltpu.VMEM((2, TOP_K * tile * SUBLANES, LANES), F32)]
    xs = pl.pallas_call(
        functools.partial(_dispatch_kernel, tile=tile, n_tiles=n_tiles, n_tiles_a=n_tiles_a),
        grid_spec=pltpu.PrefetchScalarGridSpec(
            num_scalar_prefetch=3,
            grid=(n_tiles,),
            in_specs=[any_spec, any_spec] + h1_specs,
            out_specs=any_spec,
            scratch_shapes=index_scratch + [
                pltpu.VMEM(((1 << (ZERO_FILL_BITS - 1)) * SUBLANES, LANES), F32),
                pltpu.SemaphoreType.DMA((2,)), pltpu.SemaphoreType.DMA((2,)), pltpu.SemaphoreType.DMA((2,)),
                pltpu.SemaphoreType.DMA]),
        out_shape=jax.ShapeDtypeStruct((n_rows * SUBLANES, LANES), F32),
        compiler_params=cparams,
        name="moe_dispatch",
    )(pad_start + counts, padded - counts, pad_end[-1:], tile_meta, slot_t, h1_a, h1_b)

    last = lambda i, nu: jnp.minimum(i, nu[0] - 1)
    res = pl.pallas_call(
        functools.partial(_expert_kernel, rows=rows, d_ff=d_ff),
        grid_spec=pltpu.PrefetchScalarGridSpec(
            num_scalar_prefetch=2,
            grid=(n_blocks,),
            in_specs=[pl.BlockSpec((rows * SUBLANES, LANES), lambda i, be, nu: (last(i, nu), 0)),
                      pl.BlockSpec((1, dm.d, 2 * d_ff), lambda i, be, nu: (be[i], 0, 0)),
                      pl.BlockSpec((1, 1, 2 * d_ff), lambda i, be, nu: (be[i], 0, 0)),
                      pl.BlockSpec((1, d_ff, dm.d), lambda i, be, nu: (be[i], 0, 0)),
                      pl.BlockSpec((1, 1, dm.d), lambda i, be, nu: (be[i], 0, 0))],
            out_specs=pl.BlockSpec((rows * SUBLANES, LANES), lambda i, be, nu: (i, 0)),
            scratch_shapes=[pltpu.VMEM((dm.d, 2 * d_ff), BF16), pltpu.VMEM((d_ff, dm.d), BF16)]),
        out_shape=jax.ShapeDtypeStruct((n_rows * SUBLANES, LANES), F32),
        compiler_params=cparams,
        name="moe_experts",
    )(block_e, n_used, xs, w_gu, b_gu[:, None, :], w_down, b_down[:, None, :])

    ya, yb = pl.pallas_call(
        functools.partial(_combine_kernel, tile=tile, alpha=dm.alpha, n_tiles=n_tiles, n_tiles_a=n_tiles_a),
        grid=(n_tiles,),
        in_specs=[any_spec, any_spec, any_spec] + h1_specs + [
                  pl.BlockSpec((tile, TOP_K), lambda i: (i, 0)),
                  pl.BlockSpec((1, dm.d), lambda i: (0, 0)),
                  pl.BlockSpec((1, dm.d), lambda i: (0, 0))],
        out_specs=[pl.BlockSpec((tile, dm.d), lambda i: (jnp.minimum(i, n_tiles_a - 1), 0)),
                   pl.BlockSpec((tile, dm.d), lambda i: (jnp.maximum(i - n_tiles_a, 0), 0))],
        out_shape=[jax.ShapeDtypeStruct((m_a, dm.d), F32), jax.ShapeDtypeStruct((m - m_a, dm.d), F32)],
        scratch_shapes=index_scratch + [
            pltpu.VMEM((TOP_K, tile * SUBLANES, LANES), F32),
            pltpu.SemaphoreType.DMA((2,)), pltpu.SemaphoreType.DMA((2,)), pltpu.SemaphoreType.DMA((2,))],
        compiler_params=cparams,
        name="moe_combine",
    )(tile_meta, slot_t, res, h1_a, h1_b, gate_w.T, ln_g[None, :], ln_b[None, :])
    return ya, yb


def _pad_cols(a, n):
    return jnp.pad(a, ((0, 0), (0, n - a.shape[1])))


def _mixer_weights(w_in, w_gk2, b_gk, gla_norm_w, w_branch_gla, w_pool_grp, pool_scale, w_branch_pool, b_gates,
                   w_out, ln_g, ln_b, w_router, b_router, dm):
    o_lr = 2 * dm.key + dm.val
    o_tail = o_lr + GATE_RANK
    return MixW(
        w_qkv=w_in[:, 0:o_lr].astype(BF16),
        w_lr=_pad_cols(w_in[:, o_lr:o_tail], LANES).astype(BF16),
        w_gk2=jnp.pad(w_gk2, ((0, LANES - GATE_RANK), (0, 0))).astype(BF16),
        b_gk=b_gk[None, :],
        w_tail=w_in[:, o_tail:].astype(BF16),
        gla_norm=gla_norm_w[None, :],
        w_ba=w_branch_gla.astype(BF16),
        w_pg=w_pool_grp.astype(BF16),
        pool_scale=pool_scale[None, :],
        w_bp=w_branch_pool.astype(BF16),
        b_gates=b_gates[None, :],
        w_out=w_out.astype(BF16),
        ln_g=ln_g[None, :],
        ln_b=ln_b[None, :],
        w_rt=_pad_cols(w_router, LANES).astype(BF16),
        b_rt=_pad_cols(b_router[None, :], LANES),
    )


def kernel(x_prompt, x_sample, state_gla, state_pool, w_in, w_gk2, b_gk, gla_norm_w, w_branch_gla, w_pool_grp,
           pool_scale, w_branch_pool, b_gates, w_out, ln1_g, ln1_b, w_router, b_router, w_gu, b_gu, w_down, b_down,
           ln2_g, ln2_b):
    depth = w_in.shape[0]
    bp, lp, d = x_prompt.shape
    bs, ls, _ = x_sample.shape
    assert d == SUBLANES * LANES and lp % PROMPT_TILE == 0 and (bs * ls) % MIX_TILE == 0 and MIX_TILE % ls == 0
    assert bs % GLA_SEQS == 0 and ls % SUBLANES == 0
    assert (bp * lp) % MOE_TILE == 0 and (bs * ls) % MOE_TILE == 0
    dm = _dims(d, depth)
    yp, ys = x_prompt, x_sample
    gla_p, pool_p, gla_s, pool_s = [], [], [], []
    for l in range(depth):
        w = _mixer_weights(w_in[l], w_gk2[l], b_gk[l], gla_norm_w[l], w_branch_gla[l], w_pool_grp[l], pool_scale[l],
                           w_branch_pool[l], b_gates[l], w_out[l], ln1_g[l], ln1_b[l], w_router[l], b_router[l], dm)
        h1p, lgp, sp, bufp = _prompt_mixer(yp, w, dm)
        h1s, lgs, ss, bufs = _sample_mixer(ys, state_gla[l], state_pool[l], w, dm)
        yp, ys = _moe(h1p, h1s, lgp, lgs, w_gu[l], b_gu[l], w_down[l], b_down[l], ln2_g[l], ln2_b[l], dm)
        yp = yp.reshape(bp, lp, d)
        ys = ys.reshape(bs, ls, d)
        gla_p.append(sp.astype(state_gla.dtype))
        pool_p.append(bufp.astype(state_pool.dtype))
        gla_s.append(ss.astype(state_gla.dtype))
        pool_s.append(bufs.astype(state_pool.dtype))
    return (yp, ys, jnp.stack(gla_p, 0), jnp.stack(pool_p, 0), jnp.stack(gla_s, 0), jnp.stack(pool_s, 0))
```

```python
import functools
from typing import NamedTuple

import jax
import jax.numpy as jnp
from jax import lax
from jax.experimental import pallas as pl
from jax.experimental.pallas import tpu as pltpu

F32 = jnp.float32
BF16 = jnp.bfloat16

GLA_HEADS = 4
GATE_RANK = 16
GATE_NORMALIZER = 16.0
GLA_CHUNK = 64
RMS_EPS = 1e-6
POOL_WINDOWS = (2, 4, 8, 16)
POOL_WMAX = 16
POOL_BUF = POOL_WMAX - 1
N_EXPERTS = 32
TOP_K = 4
SWIGLU_LIMIT = 7.0
GLU_ALPHA = 1.702
LN_EPS = 1e-5

LANES = 128
SUBLANES = 8
VMEM_LIMIT_BYTES = 56 * 1024 * 1024

PROMPT_TILE = 512
MIX_TILE = 256
GLA_GROUP = 128
GLA_SEQS = 8
EXPERT_ROWS = 512
MOE_TILE = 512
CUMSUM_BLOCK = 256
DMA_UNROLL = 8
ZERO_FILL_BITS = (EXPERT_ROWS - 1).bit_length()
RUN_BITS = MOE_TILE.bit_length()


class Dims(NamedTuple):
    d: int
    dk: int
    dv: int
    key: int
    val: int
    pw: int
    pgc: int
    alpha: float


def _dims(d, depth):
    return Dims(d=d, dk=d // 8, dv=d // 4, key=d // 2, val=d, pw=d // 2, pgc=d // 8, alpha=(2.0 * depth) ** 0.25)


class MixW(NamedTuple):
    w_qkv: object
    w_lr: object
    w_gk2: object
    b_gk: object
    w_tail: object
    gla_norm: object
    w_ba: object
    w_pg: object
    pool_scale: object
    w_bp: object
    b_gates: object
    w_out: object
    ln_g: object
    ln_b: object
    w_rt: object
    b_rt: object


def _mm(a, b):
    return jnp.dot(a, b, preferred_element_type=F32)


def _layer_norm(z, g, b):
    mu = jnp.mean(z, axis=-1, keepdims=True)
    zc = z - mu
    var = jnp.mean(zc * zc, axis=-1, keepdims=True)
    return zc * lax.rsqrt(var + LN_EPS) * g + b


def _to_row_tiles(ref, val, rows):
    for c in range(SUBLANES):
        ref[pl.ds(c, rows, stride=SUBLANES), :] = val[:, c * LANES:(c + 1) * LANES]


def _from_row_tiles(ref, rows):
    return jnp.concatenate([ref[pl.ds(c, rows, stride=SUBLANES), :] for c in range(SUBLANES)], axis=1)


def _project_qkv(xb, w, dm, q_s, k_s, v_s, gl_s):
    q_s[...] = _mm(xb, w.w_qkv[:, 0:dm.key]) * (dm.dk ** -0.5)
    k_s[...] = _mm(xb, w.w_qkv[:, dm.key:2 * dm.key])
    v_s[...] = _mm(xb, w.w_qkv[:, 2 * dm.key:2 * dm.key + dm.val])
    lr = _mm(xb, w.w_lr[...])
    gk = _mm(lr.astype(BF16), w.w_gk2[...]) + w.b_gk[...]
    gl_s[...] = (jnp.minimum(gk, 0.0) - jnp.log1p(jnp.exp(-jnp.abs(gk)))) / GATE_NORMALIZER


def _chunk_cumsum(gl_s, b_s, rows, chunk):
    blk = min(rows, CUMSUM_BLOCK)
    r = lax.broadcasted_iota(jnp.int32, (blk, blk), 0)
    c = lax.broadcasted_iota(jnp.int32, (blk, blk), 1)
    tri = ((r // chunk == c // chunk) & (c <= r)).astype(BF16)
    for b0 in range(0, rows, blk):
        gl = gl_s[b0:b0 + blk, :]
        hi = gl.astype(BF16)
        lo = (gl - hi.astype(F32)).astype(BF16)
        b_s[b0:b0 + blk, :] = _mm(tri, hi) + _mm(tri, lo)


NT_DIMS = (((1,), (1,)), ((), ()))
TN_DIMS = (((0,), (0,)), ((), ()))


def _gla_tile(q_s, k_s, v_s, b_s, o_s, st_t, *, dm, rows, chunk, group):
    n_chunks = rows // chunk
    gr = lax.broadcasted_iota(jnp.int32, (group, group), 0)
    gc = lax.broadcasted_iota(jnp.int32, (group, group), 1)
    mask = (gr // chunk == gc // chunk) & (gc <= gr)

    def per_chunk_row(b, row):
        return jnp.concatenate([jnp.broadcast_to(b[c * chunk + row:c * chunk + row + 1, :], (chunk, b.shape[1]))
                                for c in range(n_chunks)], axis=0)

    for h in range(GLA_HEADS):
        ks = slice(h * dm.dk, (h + 1) * dm.dk)
        vs = slice(h * dm.dv, (h + 1) * dm.dv)
        b = b_s[:, ks]
        b_ref = per_chunk_row(b, chunk // 2)
        b_last = per_chunk_row(b, chunk - 1)
        q = q_s[:, ks]
        k = k_s[:, ks]
        vb = v_s[:, vs].astype(BF16)
        qa = (q * jnp.exp(b - b_ref)).astype(BF16)
        ka = (k * jnp.exp(b_ref - b)).astype(BF16)
        qe = (q * jnp.exp(b)).astype(BF16)
        kd = (k * jnp.exp(b_last - b)).astype(BF16)
        for g0 in range(0, rows, group):
            a = lax.dot_general(qa[g0:g0 + group], ka[g0:g0 + group], NT_DIMS, preferred_element_type=F32)
            a = jnp.where(mask, a, 0.0)
            o_s[g0:g0 + group, vs] = _mm(a.astype(BF16), vb[g0:g0 + group])
        s_t = st_t[h]
        for c in range(n_chunks):
            r0 = c * chunk
            o_s[r0:r0 + chunk, vs] += lax.dot_general(qe[r0:r0 + chunk], s_t.astype(BF16), NT_DIMS,
                                                      preferred_element_type=F32)
            decay = jnp.exp(b[r0 + chunk - 1:r0 + chunk, :])
            s_t = s_t * decay + lax.dot_general(vb[r0:r0 + chunk], kd[r0:r0 + chunk], TN_DIMS,
                                                preferred_element_type=F32)
        st_t[h] = s_t


def _gla(q_s, k_s, v_s, gl_s, b_s, o_s, st_ref, *, dm, n_chunks, chunk, chunks_per_seq):
    _chunk_cumsum(gl_s, b_s, n_chunks * chunk, chunk)
    ri = lax.broadcasted_iota(jnp.int32, (chunk, chunk), 0)
    ci = lax.broadcasted_iota(jnp.int32, (chunk, chunk), 1)
    causal = ci <= ri
    nt, tn = NT_DIMS, TN_DIMS

    def step(idx, carry):
        r0 = pl.multiple_of(idx * chunk, chunk)
        seq = idx // chunks_per_seq
        for h in range(GLA_HEADS):
            ks = slice(h * dm.dk, (h + 1) * dm.dk)
            vs = slice(h * dm.dv, (h + 1) * dm.dv)
            bc = b_s[pl.ds(r0, chunk), ks]
            b_ref = bc[chunk // 2:chunk // 2 + 1, :]
            b_last = bc[chunk - 1:chunk, :]
            qc = q_s[pl.ds(r0, chunk), ks]
            kc = k_s[pl.ds(r0, chunk), ks]
            vc = v_s[pl.ds(r0, chunk), vs].astype(BF16)
            s_old = st_ref[seq, h]
            a = lax.dot_general((qc * jnp.exp(bc - b_ref)).astype(BF16), (kc * jnp.exp(b_ref - bc)).astype(BF16),
                                nt, preferred_element_type=F32)
            a = jnp.where(causal, a, 0.0)
            o = _mm(a.astype(BF16), vc) + _mm((qc * jnp.exp(bc)).astype(BF16), s_old.astype(BF16))
            o_s[pl.ds(r0, chunk), vs] = o
            e_col = jnp.broadcast_to(jnp.exp(b_last), (dm.dk, dm.dk)).T
            decay = jnp.concatenate([e_col] * (dm.dv // dm.dk), axis=1)
            kv = lax.dot_general((kc * jnp.exp(b_last - bc)).astype(BF16), vc, tn, preferred_element_type=F32)
            st_ref[seq, h] = decay * s_old + kv
        return carry

    lax.fori_loop(0, n_chunks, step, 0)


def _mixer_tail(x, xb, o_s, ext_s, pos0, w, dm, *, nseq, seq_len):
    t = nseq * seq_len
    off_u = dm.val
    off_ga = off_u + dm.pw
    off_gb = off_ga + dm.d

    g = _mm(xb, w.w_tail[:, 0:dm.val])
    parts = []
    for h in range(GLA_HEADS):
        vs = slice(h * dm.dv, (h + 1) * dm.dv)
        oh = o_s[:, vs]
        ms = jnp.mean(oh * oh, axis=-1, keepdims=True)
        on = oh * lax.rsqrt(ms + RMS_EPS) * w.gla_norm[...]
        gh = g[:, vs]
        parts.append((on * (gh * jax.nn.sigmoid(gh))).astype(BF16))
    branch_a = _mm(jnp.concatenate(parts, axis=1), w.w_ba[...])

    u = _mm(xb, w.w_tail[:, off_u:off_u + dm.pw])
    ext_s[:, POOL_WMAX:POOL_WMAX + seq_len, :] = u.reshape(nseq, seq_len, dm.pw)
    p = lax.broadcasted_iota(jnp.int32, (nseq, seq_len, dm.pgc), 1)
    pooled = []
    for gi, win in enumerate(POOL_WINDOWS):
        cs = slice(gi * dm.pgc, (gi + 1) * dm.pgc)
        cur = ext_s[:, POOL_WMAX:POOL_WMAX + seq_len, cs]
        acc = cur
        for j in range(1, win):
            acc = acc + ext_s[:, POOL_WMAX - j:POOL_WMAX - j + seq_len, cs]
        cnt = jnp.minimum(win, p + (pos0 + 1)).astype(F32)
        pg = (acc / cnt - cur).reshape(t, dm.pgc)
        pg = _mm(pg.astype(BF16), w.w_pg[gi]) * w.pool_scale[:, cs]
        pooled.append(pg.astype(BF16))
    branch_b = _mm(jnp.concatenate(pooled, axis=1), w.w_bp[...])

    gate_a = jax.nn.sigmoid(_mm(xb, w.w_tail[:, off_ga:off_ga + dm.d]) + w.b_gates[:, 0:dm.d])
    merged = gate_a * branch_a
    gate_b = jax.nn.sigmoid(_mm(xb, w.w_tail[:, off_gb:off_gb + dm.d]) + w.b_gates[:, dm.d:2 * dm.d])
    merged = merged + gate_b * branch_b
    mix = _mm(merged.astype(BF16), w.w_out[...])
    h1 = _layer_norm(dm.alpha * x + mix, w.ln_g[...], w.ln_b[...])
    logits = _mm(h1.astype(BF16), w.w_rt[...]) + w.b_rt[...]
    return h1, logits.T[0:N_EXPERTS, :]


N_MIXW = len(MixW._fields)


def _prompt_mixer_kernel(*refs, dm, tile, n_tiles):
    x_ref = refs[0]
    w = MixW(*refs[1:1 + N_MIXW])
    h1_ref, lg_ref, st_ref, buf_ref, q_s, k_s, v_s, gl_s, b_s, o_s, ext_s, st_t = refs[1 + N_MIXW:]
    lt = pl.program_id(1)

    @pl.when(lt == 0)
    def _():
        st_t[...] = jnp.zeros(st_t.shape, F32)
        ext_s[:, 0:POOL_WMAX, :] = jnp.zeros((1, POOL_WMAX, dm.pw), F32)

    x = x_ref[...]
    xb = x.astype(BF16)
    _project_qkv(xb, w, dm, q_s, k_s, v_s, gl_s)
    _chunk_cumsum(gl_s, b_s, tile, GLA_CHUNK)
    _gla_tile(q_s, k_s, v_s, b_s, o_s, st_t, dm=dm, rows=tile, chunk=GLA_CHUNK, group=GLA_GROUP)

    @pl.when(lt == n_tiles - 1)
    def _():
        for h in range(GLA_HEADS):
            st_ref[0, h] = st_t[h].T

    h1, logits_t = _mixer_tail(x, xb, o_s, ext_s, lt * tile, w, dm, nseq=1, seq_len=tile)
    _to_row_tiles(h1_ref, h1, tile)
    lg_ref[...] = logits_t
    ext_s[:, 0:POOL_WMAX, :] = ext_s[:, tile:tile + POOL_WMAX, :]

    @pl.when(lt == n_tiles - 1)
    def _():
        buf_ref[...] = ext_s[:, 1:POOL_WMAX, :]


def _sample_proj_kernel(*refs, dm):
    x_ref = refs[0]
    w = MixW(*refs[1:1 + N_MIXW])
    q_ref, k_ref, v_ref, gl_ref = refs[1 + N_MIXW:]
    _project_qkv(x_ref[...].astype(BF16), w, dm, q_ref, k_ref, v_ref, gl_ref)


def _sample_gla_kernel(q_ref, k_ref, v_ref, gl_ref, s0_ref, o_ref, st_ref, b_s, *, dm, nseq, seq_len):
    st_ref[...] = s0_ref[...]
    _gla(q_ref, k_ref, v_ref, gl_ref, b_s, o_ref, st_ref, dm=dm, n_chunks=nseq, chunk=seq_len, chunks_per_seq=1)


def _sample_tail_kernel(*refs, dm, nseq, seq_len):
    x_ref, o_ref, hist_ref = refs[0:3]
    w = MixW(*refs[3:3 + N_MIXW])
    h1_ref, lg_ref, buf_ref, ext_s = refs[3 + N_MIXW:]
    ext_s[:, 0:1, :] = jnp.zeros((nseq, 1, dm.pw), F32)
    ext_s[:, 1:POOL_WMAX, :] = hist_ref[...]
    x = x_ref[...]
    h1, logits_t = _mixer_tail(x, x.astype(BF16), o_ref, ext_s, POOL_BUF, w, dm, nseq=nseq, seq_len=seq_len)
    _to_row_tiles(h1_ref, h1, nseq * seq_len)
    lg_ref[...] = logits_t
    buf_ref[...] = ext_s[:, seq_len + 1:seq_len + POOL_WMAX, :]


def _const_spec(arr):
    nd = arr.ndim
    return pl.BlockSpec(arr.shape, lambda *_: (0,) * nd, pipeline_mode=pl.Buffered(1))


def _prompt_mixer(x, w, dm):
    bsz, seq, d = x.shape
    tile = PROMPT_TILE
    n_tiles = seq // tile
    m = bsz * seq
    scratch = [pltpu.VMEM((tile, dm.key), F32), pltpu.VMEM((tile, dm.key), F32), pltpu.VMEM((tile, dm.val), F32),
               pltpu.VMEM((tile, dm.key), F32), pltpu.VMEM((tile, dm.key), F32), pltpu.VMEM((tile, dm.val), F32),
               pltpu.VMEM((1, POOL_WMAX + tile, dm.pw), F32),
               pltpu.VMEM((GLA_HEADS, dm.dv, dm.dk), F32)]
    return pl.pallas_call(
        functools.partial(_prompt_mixer_kernel, dm=dm, tile=tile, n_tiles=n_tiles),
        grid=(bsz, n_tiles),
        in_specs=[pl.BlockSpec((tile, d), lambda b, t: (b * n_tiles + t, 0))] + [_const_spec(a) for a in w],
        out_specs=[pl.BlockSpec((tile * SUBLANES, LANES), lambda b, t: (b * n_tiles + t, 0)),
                   pl.BlockSpec((N_EXPERTS, tile), lambda b, t: (0, b * n_tiles + t)),
                   pl.BlockSpec((1, GLA_HEADS, dm.dk, dm.dv), lambda b, t: (b, 0, 0, 0)),
                   pl.BlockSpec((1, POOL_BUF, dm.pw), lambda b, t: (b, 0, 0))],
        out_shape=[jax.ShapeDtypeStruct((m * SUBLANES, LANES), F32),
                   jax.ShapeDtypeStruct((N_EXPERTS, m), F32),
                   jax.ShapeDtypeStruct((bsz, GLA_HEADS, dm.dk, dm.dv), F32),
                   jax.ShapeDtypeStruct((bsz, POOL_BUF, dm.pw), F32)],
        scratch_shapes=scratch,
        compiler_params=pltpu.CompilerParams(dimension_semantics=("arbitrary", "arbitrary"),
                                             vmem_limit_bytes=VMEM_LIMIT_BYTES),
        name="prompt_mixer",
    )(x.reshape(m, d), *w)


def _sample_mixer(x, s0, hist, w, dm):
    bsz, seq, d = x.shape
    m = bsz * seq
    x2 = x.reshape(m, d)
    tile = MIX_TILE
    cparams = pltpu.CompilerParams(dimension_semantics=("arbitrary",), vmem_limit_bytes=VMEM_LIMIT_BYTES)
    row = lambda n: pl.BlockSpec((tile, n), lambda i: (i, 0))
    q, k, v, gl = pl.pallas_call(
        functools.partial(_sample_proj_kernel, dm=dm),
        grid=(m // tile,),
        in_specs=[row(d)] + [_const_spec(a) for a in w],
        out_specs=[row(dm.key), row(dm.key), row(dm.val), row(dm.key)],
        out_shape=[jax.ShapeDtypeStruct((m, n), F32) for n in (dm.key, dm.key, dm.val, dm.key)],
        compiler_params=cparams,
        name="sample_proj",
    )(x2, *w)

    nseq = GLA_SEQS
    rows = nseq * seq
    grow = lambda n: pl.BlockSpec((rows, n), lambda i: (i, 0))
    st_spec = pl.BlockSpec((nseq, GLA_HEADS, dm.dk, dm.dv), lambda i: (i, 0, 0, 0))
    o, st = pl.pallas_call(
        functools.partial(_sample_gla_kernel, dm=dm, nseq=nseq, seq_len=seq),
        grid=(bsz // nseq,),
        in_specs=[grow(dm.key), grow(dm.key), grow(dm.val), grow(dm.key), st_spec],
        out_specs=[grow(dm.val), st_spec],
        out_shape=[jax.ShapeDtypeStruct((m, dm.val), F32), jax.ShapeDtypeStruct(s0.shape, F32)],
        scratch_shapes=[pltpu.VMEM((rows, dm.key), F32)],
        compiler_params=cparams,
        name="sample_gla",
    )(q, k, v, gl, s0)

    tseq = tile // seq
    hist_spec = pl.BlockSpec((tseq, POOL_BUF, dm.pw), lambda i: (i, 0, 0))
    h1, lg, buf = pl.pallas_call(
        functools.partial(_sample_tail_kernel, dm=dm, nseq=tseq, seq_len=seq),
        grid=(m // tile,),
        in_specs=[row(d), row(dm.val), hist_spec] + [_const_spec(a) for a in w],
        out_specs=[pl.BlockSpec((tile * SUBLANES, LANES), lambda i: (i, 0)),
                   pl.BlockSpec((N_EXPERTS, tile), lambda i: (0, i)), hist_spec],
        out_shape=[jax.ShapeDtypeStruct((m * SUBLANES, LANES), F32),
                   jax.ShapeDtypeStruct((N_EXPERTS, m), F32),
                   jax.ShapeDtypeStruct((bsz, POOL_BUF, dm.pw), F32)],
        scratch_shapes=[pltpu.VMEM((tseq, POOL_WMAX + seq, dm.pw), F32)],
        compiler_params=cparams,
        name="sample_tail",
    )(x2, o, hist, *w)
    return h1, lg, st, buf


def _router_kernel(lga_ref, lgb_ref, w_ref, s_ref, meta_ref, c_ref, carry, *, tile, n_tiles_a):
    i = pl.program_id(0)

    @pl.when(i == 0)
    def _():
        carry[...] = jnp.zeros(carry.shape, F32)

    eio = lax.broadcasted_iota(jnp.int32, (N_EXPERTS, tile), 0)
    work = jnp.where(i < n_tiles_a, lga_ref[...], lgb_ref[...])
    vals, hots = [], []
    for k in range(TOP_K):
        mx = jnp.max(work, axis=0, keepdims=True)
        idx = jnp.min(jnp.where(work == mx, eio, N_EXPERTS), axis=0, keepdims=True)
        hot = eio == idx
        vals.append(mx)
        hots.append(hot)
        work = jnp.where(hot, -jnp.inf, work)
    ex = [jnp.exp(v - vals[0]) for v in vals]
    den = ex[0]
    for k in range(1, TOP_K):
        den = den + ex[k]
    for k in range(TOP_K):
        w_ref[k:k + 1, :] = ex[k] / den
    sel = hots[0]
    for k in range(1, TOP_K):
        sel = sel | hots[k]
    r = lax.broadcasted_iota(jnp.int32, (tile, tile), 0)
    c = lax.broadcasted_iota(jnp.int32, (tile, tile), 1)
    earlier_token = (r < c).astype(BF16)
    rank_in_tile = _mm(sel.astype(BF16), earlier_token)
    cnt = jnp.broadcast_to(jnp.sum(sel.astype(F32), axis=1, keepdims=True), (N_EXPERTS, LANES))
    er = lax.broadcasted_iota(jnp.int32, (N_EXPERTS, N_EXPERTS), 0)
    ec = lax.broadcasted_iota(jnp.int32, (N_EXPERTS, N_EXPERTS), 1)
    first_slot = _mm((ec < er).astype(BF16), cnt.astype(BF16))
    slot = rank_in_tile + first_slot[:, 0:1]
    for k in range(TOP_K):
        s_ref[k:k + 1, :] = jnp.sum(jnp.where(hots[k], slot, 0.0), axis=0, keepdims=True).astype(jnp.int32)
    meta_ref[0, 0:N_EXPERTS, :] = cnt
    meta_ref[0, N_EXPERTS:2 * N_EXPERTS, :] = first_slot
    meta_ref[0, 2 * N_EXPERTS:3 * N_EXPERTS, :] = carry[...]
    carry[...] = carry[...] + cnt
    c_ref[...] = carry[...]


def _for_each_run(meta, fn):
    def body(e, carry):
        cnt = meta[e]
        first = meta[N_EXPERTS + e]
        dst = meta[2 * N_EXPERTS + e]
        for bit in range(RUN_BITS):
            @pl.when(((cnt >> bit) & 1) == 1)
            def _(bit=bit):
                done = cnt & ((1 << bit) - 1)
                fn(first + done, dst + done, 1 << bit)
        return carry
    lax.fori_loop(0, N_EXPERTS, body, 0)


def _rows(ref, first, n):
    return ref.at[pl.ds(pl.multiple_of(first * SUBLANES, SUBLANES), n * SUBLANES), :]


def _dispatch_kernel(fill_start_ref, fill_cnt_ref, tail_start_ref, meta_hbm, slot_hbm, h1a_ref, h1b_ref, xs_hbm,
                     meta_s0, meta_s1, slot_s0, slot_s1, stage, zeros_s, sem_m, sem_s, sem_r, sem_z,
                     *, tile, n_tiles, n_tiles_a):
    i = pl.program_id(0)
    sl = i % 2
    meta_s = (meta_s0, meta_s1)
    slot_s = (slot_s0, slot_s1)

    def index_copies(blk, s):
        return (pltpu.make_async_copy(meta_hbm.at[blk], meta_s[s], sem_m.at[s]),
                pltpu.make_async_copy(slot_hbm.at[blk], slot_s[s], sem_s.at[s]))

    def wait_stage(s):
        pltpu.make_async_copy(stage.at[s], stage.at[s], sem_r.at[s]).wait()

    @pl.when(i == 0)
    def _():
        for cp in index_copies(0, 0):
            cp.start()
        zeros_s[...] = jnp.zeros(zeros_s.shape, F32)

        def fill_copy(off, bit):
            n = 1 << bit
            return pltpu.make_async_copy(zeros_s.at[pl.ds(0, n * SUBLANES), :], _rows(xs_hbm, off, n), sem_z)

        def for_each_piece(fn):
            def body(e, carry):
                cnt = fill_cnt_ref[e]
                for bit in range(ZERO_FILL_BITS):
                    @pl.when(((cnt >> bit) & 1) == 1)
                    def _(bit=bit):
                        fn(fill_copy(fill_start_ref[e] + (cnt & ((1 << bit) - 1)), bit))
                return carry
            lax.fori_loop(0, N_EXPERTS, body, 0)

        top = ZERO_FILL_BITS - 1
        n_tail = (xs_hbm.shape[0] // SUBLANES - tail_start_ref[0]) >> top

        def for_each_tail_piece(fn):
            def body(j, carry):
                fn(fill_copy(tail_start_ref[0] + (j << top), top))
                return carry
            lax.fori_loop(0, n_tail, body, 0)

        for_each_piece(lambda cp: cp.start())
        for_each_tail_piece(lambda cp: cp.start())
        for_each_piece(lambda cp: cp.wait())
        for_each_tail_piece(lambda cp: cp.wait())

    def group_rows(h1_ref, s_):
        def group(g, carry):
            for u in range(DMA_UNROLL):
                t = g * DMA_UNROLL + u
                row = h1_ref[pl.ds(pl.multiple_of(t * SUBLANES, SUBLANES), SUBLANES), :]
                for k in range(TOP_K):
                    s = slot_s[s_][k * tile + t]
                    stage[s_, pl.ds(pl.multiple_of(s * SUBLANES, SUBLANES), SUBLANES), :] = row
            return carry
        lax.fori_loop(0, tile // DMA_UNROLL, group, 0)

    for s_ in range(2):
        @pl.when((sl == 1 - s_) & (i + 1 < n_tiles))
        def _(s_=s_):
            for cp in index_copies(i + 1, s_):
                cp.start()

    for s_ in range(2):
        @pl.when(sl == s_)
        def _(s_=s_):
            for cp in index_copies(i, s_):
                cp.wait()

            @pl.when(i >= 2)
            def _():
                wait_stage(s_)

            @pl.when(i < n_tiles_a)
            def _():
                group_rows(h1a_ref, s_)

            @pl.when(i >= n_tiles_a)
            def _():
                group_rows(h1b_ref, s_)

            _for_each_run(meta_s[s_], lambda s0, d0, n: pltpu.make_async_copy(
                _rows(stage.at[s_], s0, n), _rows(xs_hbm, d0, n), sem_r.at[s_]).start())

    @pl.when(i == n_tiles - 1)
    def _():
        @pl.when(n_tiles > 1)
        def _():
            wait_stage(1 - sl)
        wait_stage(sl)


def _expert_kernel(be_ref, nused_ref, next_e_ref, xs_ref, wgu_hbm, bgu_ref, wdn_hbm, bdn_ref, out_ref,
                   wgu_f, wdn_f, wgu_b, wdn_b, sem_w, *, rows, d_ff):
    i = pl.program_id(0)

    def weight_copies(e):
        return (pltpu.make_async_copy(wgu_hbm.at[e], wgu_f, sem_w.at[0]),
                pltpu.make_async_copy(wdn_hbm.at[e], wdn_f, sem_w.at[1]))

    @pl.when(i == 0)
    def _():
        for cp in weight_copies(be_ref[0]):
            cp.start()

    @pl.when((i < nused_ref[0]) & ((i == 0) | (be_ref[i] != be_ref[jnp.maximum(i - 1, 0)])))
    def _():
        for cp in weight_copies(be_ref[i]):
            cp.wait()
        wgu_b[...] = wgu_f[...].astype(BF16)
        wdn_b[...] = wdn_f[...].astype(BF16)

        @pl.when(next_e_ref[i] >= 0)
        def _():
            for cp in weight_copies(next_e_ref[i]):
                cp.start()

    @pl.when(i < nused_ref[0])
    def _():
        xb = _from_row_tiles(xs_ref, rows).astype(BF16)
        hh = _mm(xb, wgu_b[...]) + bgu_ref[0]
        gate = jnp.minimum(hh[:, 0:d_ff], SWIGLU_LIMIT)
        up = jnp.clip(hh[:, d_ff:2 * d_ff], -SWIGLU_LIMIT, SWIGLU_LIMIT)
        act = (up + 1.0) * (gate * jax.nn.sigmoid(GLU_ALPHA * gate))
        res = _mm(act.astype(BF16), wdn_b[...]) + bdn_ref[0]
        _to_row_tiles(out_ref, res, rows)

    @pl.when(i >= nused_ref[0])
    def _():
        out_ref[...] = jnp.zeros(out_ref.shape, F32)


def _combine_kernel(meta_hbm, slot_hbm, res_hbm, h1a_ref, h1b_ref, gw_ref, g_ref, b_ref, ya_ref, yb_ref,
                    meta_s0, meta_s1, slot_s0, slot_s1, stage, gbuf, sem_m, sem_s, sem_g,
                    *, tile, alpha, n_tiles, n_tiles_a):
    i = pl.program_id(0)
    sl = i % 2
    meta_s = (meta_s0, meta_s1)
    slot_s = (slot_s0, slot_s1)

    def index_copies(blk, s):
        return (pltpu.make_async_copy(meta_hbm.at[blk], meta_s[s], sem_m.at[s]),
                pltpu.make_async_copy(slot_hbm.at[blk], slot_s[s], sem_s.at[s]))

    def fetch_runs(s):
        _for_each_run(meta_s[s], lambda s0, d0, n: pltpu.make_async_copy(
            _rows(res_hbm, d0, n), _rows(stage.at[s], s0, n), sem_g.at[s]).start())

    @pl.when(i == 0)
    def _():
        for cp in index_copies(0, 0):
            cp.start()
        for cp in index_copies(0, 0):
            cp.wait()
        fetch_runs(0)

        @pl.when(n_tiles > 1)
        def _():
            for cp in index_copies(1, 1):
                cp.start()

    def regroup(s_):
        def group(g, carry):
            for u in range(DMA_UNROLL):
                t = g * DMA_UNROLL + u
                for k in range(TOP_K):
                    s = slot_s[s_][k * tile + t]
                    gbuf[k, pl.ds(pl.multiple_of(t * SUBLANES, SUBLANES), SUBLANES), :] = (
                        stage[s_, pl.ds(pl.multiple_of(s * SUBLANES, SUBLANES), SUBLANES), :])
            return carry
        lax.fori_loop(0, tile // DMA_UNROLL, group, 0)

    for s_ in range(2):
        @pl.when((sl == 1 - s_) & (i + 1 < n_tiles))
        def _(s_=s_):
            for cp in index_copies(i + 1, s_):
                cp.wait()
            fetch_runs(s_)

    for s_ in range(2):
        @pl.when(sl == s_)
        def _(s_=s_):
            pltpu.make_async_copy(stage.at[s_], stage.at[s_], sem_g.at[s_]).wait()
            regroup(s_)

    for s_ in range(2):
        @pl.when((sl == s_) & (i + 2 < n_tiles))
        def _(s_=s_):
            for cp in index_copies(i + 2, s_):
                cp.start()

    gw = gw_ref[...]
    z = alpha * jnp.where(i < n_tiles_a, _from_row_tiles(h1a_ref, tile), _from_row_tiles(h1b_ref, tile))
    for k in range(TOP_K):
        z = z + gw[:, k:k + 1] * _from_row_tiles(gbuf.at[k], tile)
    y = _layer_norm(z, g_ref[...], b_ref[...])

    @pl.when(i < n_tiles_a)
    def _():
        ya_ref[...] = y

    @pl.when(i >= n_tiles_a)
    def _():
        yb_ref[...] = y


def _tile_major(a, tile):
    k, m = a.shape
    return a.reshape(k, m // tile, tile).transpose(1, 0, 2).reshape(m // tile, k * tile)


def _moe(h1_a, h1_b, logits_a, logits_b, w_gu, b_gu, w_down, b_down, ln_g, ln_b, dm):
    m_a = logits_a.shape[1]
    m = m_a + logits_b.shape[1]
    rows = EXPERT_ROWS
    d_ff = w_down.shape[1]
    tile = MOE_TILE
    n_tiles = m // tile
    n_tiles_a = m_a // tile
    cparams = pltpu.CompilerParams(dimension_semantics=("arbitrary",), vmem_limit_bytes=VMEM_LIMIT_BYTES)

    def two_group_specs(block, lane_axis):
        pick = (lambda j: (0, j)) if lane_axis else (lambda j: (j, 0))
        return [pl.BlockSpec(block, lambda i, *_: pick(jnp.minimum(i, n_tiles_a - 1))),
                pl.BlockSpec(block, lambda i, *_: pick(jnp.maximum(i - n_tiles_a, 0)))]

    kblk = lambda: pl.BlockSpec((TOP_K, tile), lambda i: (0, i))
    gate_w, slot, meta, counts = pl.pallas_call(
        functools.partial(_router_kernel, tile=tile, n_tiles_a=n_tiles_a),
        grid=(n_tiles,),
        in_specs=two_group_specs((N_EXPERTS, tile), True),
        out_specs=[kblk(), kblk(), pl.BlockSpec((1, 3 * N_EXPERTS, LANES), lambda i: (i, 0, 0)),
                   pl.BlockSpec((N_EXPERTS, LANES), lambda i: (0, 0))],
        out_shape=[jax.ShapeDtypeStruct((TOP_K, m), F32), jax.ShapeDtypeStruct((TOP_K, m), jnp.int32),
                   jax.ShapeDtypeStruct((n_tiles, 3 * N_EXPERTS, LANES), F32),
                   jax.ShapeDtypeStruct((N_EXPERTS, LANES), F32)],
        scratch_shapes=[pltpu.VMEM((N_EXPERTS, LANES), F32)],
        compiler_params=cparams,
        name="moe_router",
    )(logits_a, logits_b)

    counts = counts[:, 0].astype(jnp.int32)
    padded = (counts + rows - 1) // rows * rows
    pad_end = jnp.cumsum(padded)
    pad_start = pad_end - padded
    n_blocks = -(-(m * TOP_K + N_EXPERTS * (rows - 1)) // rows)
    n_rows = n_blocks * rows
    block_row0 = jnp.arange(n_blocks, dtype=jnp.int32) * rows
    block_e = jnp.minimum(jnp.sum((pad_end[None, :] <= block_row0[:, None]).astype(jnp.int32), axis=1), N_EXPERTS - 1)
    n_used = (pad_end[-1] // rows).astype(jnp.int32).reshape(1)
    meta = meta[:, :, 0].astype(jnp.int32)
    tile_meta = jnp.concatenate([meta[:, 0:2 * N_EXPERTS], meta[:, 2 * N_EXPERTS:] + pad_start[None, :],
                                 jnp.zeros((n_tiles, LANES - 3 * N_EXPERTS), jnp.int32)], axis=1)
    slot_t = _tile_major(slot, tile)

    h1_specs = two_group_specs((tile * SUBLANES, LANES), False)
    any_spec = pl.BlockSpec(memory_space=pl.ANY)
    index_scratch = [pltpu.SMEM((LANES,), jnp.int32), pltpu.SMEM((LANES,), jnp.int32),
                     pltpu.SMEM((TOP_K * tile,), jnp.int32), pltpu.SMEM((TOP_K * tile,), jnp.int32),
                     pltpu.VMEM((2, TOP_K * tile * SUBLANES, LANES), F32)]
    xs = pl.pallas_call(
        functools.partial(_dispatch_kernel, tile=tile, n_tiles=n_tiles, n_tiles_a=n_tiles_a),
        grid_spec=pltpu.PrefetchScalarGridSpec(
            num_scalar_prefetch=3,
            grid=(n_tiles,),
            in_specs=[any_spec, any_spec] + h1_specs,
            out_specs=any_spec,
            scratch_shapes=index_scratch + [
                pltpu.VMEM(((1 << (ZERO_FILL_BITS - 1)) * SUBLANES, LANES), F32),
                pltpu.SemaphoreType.DMA((2,)), pltpu.SemaphoreType.DMA((2,)), pltpu.SemaphoreType.DMA((2,)),
                pltpu.SemaphoreType.DMA]),
        out_shape=jax.ShapeDtypeStruct((n_rows * SUBLANES, LANES), F32),
        compiler_params=cparams,
        name="moe_dispatch",
    )(pad_start + counts, padded - counts, pad_end[-1:], tile_meta, slot_t, h1_a, h1_b)

    blk = jnp.arange(n_blocks, dtype=jnp.int32)
    run_start = (blk < n_used[0]) & ((blk == 0) | (block_e != jnp.roll(block_e, 1)))
    next_start = jnp.min(jnp.where((blk[None, :] > blk[:, None]) & run_start[None, :], blk[None, :], n_blocks), axis=1)
    next_e = jnp.where(next_start < n_blocks, block_e[jnp.minimum(next_start, n_blocks - 1)], -1).astype(jnp.int32)

    last = lambda i, nu: jnp.minimum(i, nu[0] - 1)
    res = pl.pallas_call(
        functools.partial(_expert_kernel, rows=rows, d_ff=d_ff),
        grid_spec=pltpu.PrefetchScalarGridSpec(
            num_scalar_prefetch=3,
            grid=(n_blocks,),
            in_specs=[pl.BlockSpec((rows * SUBLANES, LANES), lambda i, be, nu, ne: (last(i, nu), 0)),
                      any_spec,
                      pl.BlockSpec((1, 1, 2 * d_ff), lambda i, be, nu, ne: (be[i], 0, 0)),
                      any_spec,
                      pl.BlockSpec((1, 1, dm.d), lambda i, be, nu, ne: (be[i], 0, 0))],
            out_specs=pl.BlockSpec((rows * SUBLANES, LANES), lambda i, be, nu, ne: (i, 0)),
            scratch_shapes=[pltpu.VMEM((dm.d, 2 * d_ff), F32), pltpu.VMEM((d_ff, dm.d), F32),
                            pltpu.VMEM((dm.d, 2 * d_ff), BF16), pltpu.VMEM((d_ff, dm.d), BF16),
                            pltpu.SemaphoreType.DMA((2,))]),
        out_shape=jax.ShapeDtypeStruct((n_rows * SUBLANES, LANES), F32),
        compiler_params=cparams,
        name="moe_experts",
    )(block_e, n_used, next_e, xs, w_gu, b_gu[:, None, :], w_down, b_down[:, None, :])

    ya, yb = pl.pallas_call(
        functools.partial(_combine_kernel, tile=tile, alpha=dm.alpha, n_tiles=n_tiles, n_tiles_a=n_tiles_a),
        grid=(n_tiles,),
        in_specs=[any_spec, any_spec, any_spec] + h1_specs + [
                  pl.BlockSpec((tile, TOP_K), lambda i: (i, 0)),
                  pl.BlockSpec((1, dm.d), lambda i: (0, 0)),
                  pl.BlockSpec((1, dm.d), lambda i: (0, 0))],
        out_specs=[pl.BlockSpec((tile, dm.d), lambda i: (jnp.minimum(i, n_tiles_a - 1), 0)),
                   pl.BlockSpec((tile, dm.d), lambda i: (jnp.maximum(i - n_tiles_a, 0), 0))],
        out_shape=[jax.ShapeDtypeStruct((m_a, dm.d), F32), jax.ShapeDtypeStruct((m - m_a, dm.d), F32)],
        scratch_shapes=index_scratch + [
            pltpu.VMEM((TOP_K, tile * SUBLANES, LANES), F32),
            pltpu.SemaphoreType.DMA((2,)), pltpu.SemaphoreType.DMA((2,)), pltpu.SemaphoreType.DMA((2,))],
        compiler_params=cparams,
        name="moe_combine",
    )(tile_meta, slot_t, res, h1_a, h1_b, gate_w.T, ln_g[None, :], ln_b[None, :])
    return ya, yb


def _pad_cols(a, n):
    return jnp.pad(a, ((0, 0), (0, n - a.shape[1])))


def _mixer_weights(w_in, w_gk2, b_gk, gla_norm_w, w_branch_gla, w_pool_grp, pool_scale, w_branch_pool, b_gates,
                   w_out, ln_g, ln_b, w_router, b_router, dm):
    o_lr = 2 * dm.key + dm.val
    o_tail = o_lr + GATE_RANK
    return MixW(
        w_qkv=w_in[:, 0:o_lr].astype(BF16),
        w_lr=_pad_cols(w_in[:, o_lr:o_tail], LANES).astype(BF16),
        w_gk2=jnp.pad(w_gk2, ((0, LANES - GATE_RANK), (0, 0))).astype(BF16),
        b_gk=b_gk[None, :],
        w_tail=w_in[:, o_tail:].astype(BF16),
        gla_norm=gla_norm_w[None, :],
        w_ba=w_branch_gla.astype(BF16),
        w_pg=w_pool_grp.astype(BF16),
        pool_scale=pool_scale[None, :],
        w_bp=w_branch_pool.astype(BF16),
        b_gates=b_gates[None, :],
        w_out=w_out.astype(BF16),
        ln_g=ln_g[None, :],
        ln_b=ln_b[None, :],
        w_rt=_pad_cols(w_router, LANES).astype(BF16),
        b_rt=_pad_cols(b_router[None, :], LANES),
    )


def kernel(x_prompt, x_sample, state_gla, state_pool, w_in, w_gk2, b_gk, gla_norm_w, w_branch_gla, w_pool_grp,
           pool_scale, w_branch_pool, b_gates, w_out, ln1_g, ln1_b, w_router, b_router, w_gu, b_gu, w_down, b_down,
           ln2_g, ln2_b):
    depth = w_in.shape[0]
    bp, lp, d = x_prompt.shape
    bs, ls, _ = x_sample.shape
    assert d == SUBLANES * LANES and lp % PROMPT_TILE == 0 and (bs * ls) % MIX_TILE == 0 and MIX_TILE % ls == 0
    assert bs % GLA_SEQS == 0 and ls % SUBLANES == 0
    assert (bp * lp) % MOE_TILE == 0 and (bs * ls) % MOE_TILE == 0
    dm = _dims(d, depth)
    yp, ys = x_prompt, x_sample
    gla_p, pool_p, gla_s, pool_s = [], [], [], []
    for l in range(depth):
        w = _mixer_weights(w_in[l], w_gk2[l], b_gk[l], gla_norm_w[l], w_branch_gla[l], w_pool_grp[l], pool_scale[l],
                           w_branch_pool[l], b_gates[l], w_out[l], ln1_g[l], ln1_b[l], w_router[l], b_router[l], dm)
        h1p, lgp, sp, bufp = _prompt_mixer(yp, w, dm)
        h1s, lgs, ss, bufs = _sample_mixer(ys, state_gla[l], state_pool[l], w, dm)
        yp, ys = _moe(h1p, h1s, lgp, lgs, w_gu[l], b_gu[l], w_down[l], b_down[l], ln2_g[l], ln2_b[l], dm)
        yp = yp.reshape(bp, lp, d)
        ys = ys.reshape(bs, ls, d)
        gla_p.append(sp.astype(state_gla.dtype))
        pool_p.append(bufp.astype(state_pool.dtype))
        gla_s.append(ss.astype(state_gla.dtype))
        pool_s.append(bufs.astype(state_pool.dtype))
    return (yp, ys, jnp.stack(gla_p, 0), jnp.stack(pool_p, 0), jnp.stack(gla_s, 0), jnp.stack(pool_s, 0))
```

```python
import functools
from typing import NamedTuple

import jax
import jax.numpy as jnp
from jax import lax
from jax.experimental import pallas as pl
from jax.experimental.pallas import tpu as pltpu

F32 = jnp.float32
BF16 = jnp.bfloat16

GLA_HEADS = 4
GATE_RANK = 16
GATE_NORMALIZER = 16.0
GLA_CHUNK = 64
RMS_EPS = 1e-6
POOL_WINDOWS = (2, 4, 8, 16)
POOL_WMAX = 16
POOL_BUF = POOL_WMAX - 1
N_EXPERTS = 32
TOP_K = 4
SWIGLU_LIMIT = 7.0
GLU_ALPHA = 1.702
LN_EPS = 1e-5

LANES = 128
SUBLANES = 8
VMEM_LIMIT_BYTES = 56 * 1024 * 1024

PROMPT_TILE = 512
MIX_TILE = 256
GLA_GROUP = 128
GLA_SEQS = 8
GLA_SEQ_UNROLL = 2
EXPERT_ROWS = 512
MOE_TILE = 512
CUMSUM_BLOCK = 256
DMA_UNROLL = 8
ZERO_FILL_BITS = (EXPERT_ROWS - 1).bit_length()
RUN_BITS = MOE_TILE.bit_length()


class Dims(NamedTuple):
    d: int
    dk: int
    dv: int
    key: int
    val: int
    pw: int
    pgc: int
    alpha: float


def _dims(d, depth):
    return Dims(d=d, dk=d // 8, dv=d // 4, key=d // 2, val=d, pw=d // 2, pgc=d // 8, alpha=(2.0 * depth) ** 0.25)


class MixW(NamedTuple):
    w_in: object
    w_gk2: object
    b_gk: object
    gla_norm: object
    w_ba: object
    w_pg: object
    pool_scale: object
    w_bp: object
    b_gates: object
    w_out: object
    ln_g: object
    ln_b: object
    w_rt: object
    b_rt: object


def _mm(a, b):
    return jnp.dot(a, b, preferred_element_type=F32)


def _layer_norm(z, g, b):
    mu = jnp.mean(z, axis=-1, keepdims=True)
    zc = z - mu
    var = jnp.mean(zc * zc, axis=-1, keepdims=True)
    return zc * lax.rsqrt(var + LN_EPS) * g + b


def _to_row_tiles(ref, val, rows):
    for c in range(SUBLANES):
        ref[pl.ds(c, rows, stride=SUBLANES), :] = val[:, c * LANES:(c + 1) * LANES]


def _from_row_tiles(ref, rows):
    return jnp.concatenate([ref[pl.ds(c, rows, stride=SUBLANES), :] for c in range(SUBLANES)], axis=1)


def _project_qkv(xb, w, dm, q_s, k_s, v_s, gl_s):
    q_s[...] = _mm(xb, w.w_in[:, 0:dm.key]) * (dm.dk ** -0.5)
    k_s[...] = _mm(xb, w.w_in[:, dm.key:2 * dm.key])
    v_s[...] = _mm(xb, w.w_in[:, 2 * dm.key:2 * dm.key + dm.val])
    off_lr = 2 * dm.key + 2 * dm.val + dm.pw + 2 * dm.d
    lr = _mm(xb, w.w_in[:, off_lr:off_lr + LANES])
    gk = _mm(lr.astype(BF16), w.w_gk2[...]) + w.b_gk[...]
    gl_s[...] = (jnp.minimum(gk, 0.0) - jnp.log1p(jnp.exp(-jnp.abs(gk)))) / GATE_NORMALIZER


def _chunk_cumsum(gl_s, b_s, rows, chunk):
    blk = min(rows, CUMSUM_BLOCK)
    r = lax.broadcasted_iota(jnp.int32, (blk, blk), 0)
    c = lax.broadcasted_iota(jnp.int32, (blk, blk), 1)
    tri = ((r // chunk == c // chunk) & (c <= r)).astype(BF16)
    for b0 in range(0, rows, blk):
        gl = gl_s[b0:b0 + blk, :]
        hi = gl.astype(BF16)
        lo = (gl - hi.astype(F32)).astype(BF16)
        b_s[b0:b0 + blk, :] = _mm(tri, hi) + _mm(tri, lo)


NT_DIMS = (((1,), (1,)), ((), ()))
TN_DIMS = (((0,), (0,)), ((), ()))


def _gla_tile(q_s, k_s, v_s, b_s, o_s, st_t, *, dm, rows, chunk, group):
    n_chunks = rows // chunk
    gr = lax.broadcasted_iota(jnp.int32, (group, group), 0)
    gc = lax.broadcasted_iota(jnp.int32, (group, group), 1)
    mask = (gr // chunk == gc // chunk) & (gc <= gr)

    def per_chunk_row(b, row):
        return jnp.concatenate([jnp.broadcast_to(b[c * chunk + row:c * chunk + row + 1, :], (chunk, b.shape[1]))
                                for c in range(n_chunks)], axis=0)

    for h in range(GLA_HEADS):
        ks = slice(h * dm.dk, (h + 1) * dm.dk)
        vs = slice(h * dm.dv, (h + 1) * dm.dv)
        b = b_s[:, ks]
        b_ref = per_chunk_row(b, chunk // 2)
        b_last = per_chunk_row(b, chunk - 1)
        q = q_s[:, ks]
        k = k_s[:, ks]
        vb = v_s[:, vs].astype(BF16)
        qa = (q * jnp.exp(b - b_ref)).astype(BF16)
        ka = (k * jnp.exp(b_ref - b)).astype(BF16)
        qe = (q * jnp.exp(b)).astype(BF16)
        kd = (k * jnp.exp(b_last - b)).astype(BF16)
        for g0 in range(0, rows, group):
            a = lax.dot_general(qa[g0:g0 + group], ka[g0:g0 + group], NT_DIMS, preferred_element_type=F32)
            a = jnp.where(mask, a, 0.0)
            o_s[g0:g0 + group, vs] = _mm(a.astype(BF16), vb[g0:g0 + group])
        s_t = st_t[h]
        for c in range(n_chunks):
            r0 = c * chunk
            o_s[r0:r0 + chunk, vs] += lax.dot_general(qe[r0:r0 + chunk], s_t.astype(BF16), NT_DIMS,
                                                      preferred_element_type=F32)
            decay = jnp.exp(b[r0 + chunk - 1:r0 + chunk, :])
            s_t = s_t * decay + lax.dot_general(vb[r0:r0 + chunk], kd[r0:r0 + chunk], TN_DIMS,
                                                preferred_element_type=F32)
        st_t[h] = s_t


def _gla_seqs(q_s, k_s, v_s, gl_s, b_s, o_s, s0_ref, st_ref, *, dm, nseq, chunk, unroll):
    _chunk_cumsum(gl_s, b_s, nseq * chunk, chunk)
    ri = lax.broadcasted_iota(jnp.int32, (chunk, chunk), 0)
    ci = lax.broadcasted_iota(jnp.int32, (chunk, chunk), 1)
    causal = ci <= ri
    nt, tn = NT_DIMS, TN_DIMS

    def one_seq(seq):
        r0 = pl.multiple_of(seq * chunk, chunk)
        for h in range(GLA_HEADS):
            ks = slice(h * dm.dk, (h + 1) * dm.dk)
            vs = slice(h * dm.dv, (h + 1) * dm.dv)
            bc = b_s[pl.ds(r0, chunk), ks]
            b_ref = bc[chunk // 2:chunk // 2 + 1, :]
            b_last = bc[chunk - 1:chunk, :]
            qc = q_s[pl.ds(r0, chunk), ks]
            kc = k_s[pl.ds(r0, chunk), ks]
            vc = v_s[pl.ds(r0, chunk), vs].astype(BF16)
            s_old = s0_ref[seq, h]
            a = lax.dot_general((qc * jnp.exp(bc - b_ref)).astype(BF16), (kc * jnp.exp(b_ref - bc)).astype(BF16),
                                nt, preferred_element_type=F32)
            a = jnp.where(causal, a, 0.0)
            o = _mm(a.astype(BF16), vc) + _mm((qc * jnp.exp(bc)).astype(BF16), s_old.astype(BF16))
            o_s[pl.ds(r0, chunk), vs] = o
            e_col = jnp.broadcast_to(jnp.exp(b_last), (dm.dk, dm.dk)).T
            decay = jnp.concatenate([e_col] * (dm.dv // dm.dk), axis=1)
            kv = lax.dot_general((kc * jnp.exp(b_last - bc)).astype(BF16), vc, tn, preferred_element_type=F32)
            st_ref[seq, h] = decay * s_old + kv

    def step(g, carry):
        for u in range(unroll):
            one_seq(g * unroll + u)
        return carry

    lax.fori_loop(0, nseq // unroll, step, 0)


def _mixer_tail(x, xb, o_s, ext_s, pos0, w, dm, *, nseq, seq_len):
    t = nseq * seq_len
    off_g = 2 * dm.key + dm.val
    off_u = off_g + dm.val
    off_ga = off_u + dm.pw
    off_gb = off_ga + dm.d

    g = _mm(xb, w.w_in[:, off_g:off_g + dm.val])
    parts = []
    for h in range(GLA_HEADS):
        vs = slice(h * dm.dv, (h + 1) * dm.dv)
        oh = o_s[:, vs]
        ms = jnp.mean(oh * oh, axis=-1, keepdims=True)
        on = oh * lax.rsqrt(ms + RMS_EPS) * w.gla_norm[...]
        gh = g[:, vs]
        parts.append((on * (gh * jax.nn.sigmoid(gh))).astype(BF16))
    branch_a = _mm(jnp.concatenate(parts, axis=1), w.w_ba[...])

    u = _mm(xb, w.w_in[:, off_u:off_u + dm.pw])
    ext_s[:, POOL_WMAX:POOL_WMAX + seq_len, :] = u.reshape(nseq, seq_len, dm.pw)
    p = lax.broadcasted_iota(jnp.int32, (nseq, seq_len, dm.pgc), 1)
    pooled = []
    for gi, win in enumerate(POOL_WINDOWS):
        cs = slice(gi * dm.pgc, (gi + 1) * dm.pgc)
        cur = ext_s[:, POOL_WMAX:POOL_WMAX + seq_len, cs]
        acc = cur
        for j in range(1, win):
            acc = acc + ext_s[:, POOL_WMAX - j:POOL_WMAX - j + seq_len, cs]
        cnt = jnp.minimum(win, p + (pos0 + 1)).astype(F32)
        pg = (acc / cnt - cur).reshape(t, dm.pgc)
        pg = _mm(pg.astype(BF16), w.w_pg[gi]) * w.pool_scale[:, cs]
        pooled.append(pg.astype(BF16))
    branch_b = _mm(jnp.concatenate(pooled, axis=1), w.w_bp[...])

    gate_a = jax.nn.sigmoid(_mm(xb, w.w_in[:, off_ga:off_ga + dm.d]) + w.b_gates[:, 0:dm.d])
    merged = gate_a * branch_a
    gate_b = jax.nn.sigmoid(_mm(xb, w.w_in[:, off_gb:off_gb + dm.d]) + w.b_gates[:, dm.d:2 * dm.d])
    merged = merged + gate_b * branch_b
    mix = _mm(merged.astype(BF16), w.w_out[...])
    h1 = _layer_norm(dm.alpha * x + mix, w.ln_g[...], w.ln_b[...])
    logits = _mm(h1.astype(BF16), w.w_rt[...]) + w.b_rt[...]
    return h1, logits.T[0:N_EXPERTS, :]


N_MIXW = len(MixW._fields)


def _prompt_mixer_kernel(*refs, dm, tile, n_tiles):
    x_ref = refs[0]
    w = MixW(*refs[1:1 + N_MIXW])
    h1_ref, lg_ref, st_ref, buf_ref, q_s, k_s, v_s, gl_s, b_s, o_s, ext_s, st_t = refs[1 + N_MIXW:]
    lt = pl.program_id(1)

    @pl.when(lt == 0)
    def _():
        st_t[...] = jnp.zeros(st_t.shape, F32)
        ext_s[:, 0:POOL_WMAX, :] = jnp.zeros((1, POOL_WMAX, dm.pw), F32)

    x = x_ref[...]
    xb = x.astype(BF16)
    _project_qkv(xb, w, dm, q_s, k_s, v_s, gl_s)
    _chunk_cumsum(gl_s, b_s, tile, GLA_CHUNK)
    _gla_tile(q_s, k_s, v_s, b_s, o_s, st_t, dm=dm, rows=tile, chunk=GLA_CHUNK, group=GLA_GROUP)

    @pl.when(lt == n_tiles - 1)
    def _():
        for h in range(GLA_HEADS):
            st_ref[0, h] = st_t[h].T

    h1, logits_t = _mixer_tail(x, xb, o_s, ext_s, lt * tile, w, dm, nseq=1, seq_len=tile)
    _to_row_tiles(h1_ref, h1, tile)
    lg_ref[...] = logits_t
    ext_s[:, 0:POOL_WMAX, :] = ext_s[:, tile:tile + POOL_WMAX, :]

    @pl.when(lt == n_tiles - 1)
    def _():
        buf_ref[...] = ext_s[:, 1:POOL_WMAX, :]


def _sample_proj_kernel(*refs, dm):
    x_ref = refs[0]
    w = MixW(*refs[1:1 + N_MIXW])
    q_ref, k_ref, v_ref, gl_ref = refs[1 + N_MIXW:]
    _project_qkv(x_ref[...].astype(BF16), w, dm, q_ref, k_ref, v_ref, gl_ref)


def _sample_gla_kernel(q_ref, k_ref, v_ref, gl_ref, s0_ref, o_ref, st_ref, b_s, *, dm, nseq, seq_len):
    _gla_seqs(q_ref, k_ref, v_ref, gl_ref, b_s, o_ref, s0_ref, st_ref, dm=dm, nseq=nseq, chunk=seq_len,
              unroll=GLA_SEQ_UNROLL)


def _sample_tail_kernel(*refs, dm, nseq, seq_len):
    x_ref, o_ref, hist_ref = refs[0:3]
    w = MixW(*refs[3:3 + N_MIXW])
    h1_ref, lg_ref, buf_ref, ext_s = refs[3 + N_MIXW:]
    ext_s[:, 0:1, :] = jnp.zeros((nseq, 1, dm.pw), F32)
    ext_s[:, 1:POOL_WMAX, :] = hist_ref[...]
    x = x_ref[...]
    h1, logits_t = _mixer_tail(x, x.astype(BF16), o_ref, ext_s, POOL_BUF, w, dm, nseq=nseq, seq_len=seq_len)
    _to_row_tiles(h1_ref, h1, nseq * seq_len)
    lg_ref[...] = logits_t
    buf_ref[...] = ext_s[:, seq_len + 1:seq_len + POOL_WMAX, :]


def _const_spec(arr):
    nd = arr.ndim
    return pl.BlockSpec(arr.shape, lambda *_: (0,) * nd, pipeline_mode=pl.Buffered(1))


def _prompt_mixer(x, w, dm):
    bsz, seq, d = x.shape
    tile = PROMPT_TILE
    n_tiles = seq // tile
    m = bsz * seq
    scratch = [pltpu.VMEM((tile, dm.key), F32), pltpu.VMEM((tile, dm.key), F32), pltpu.VMEM((tile, dm.val), F32),
               pltpu.VMEM((tile, dm.key), F32), pltpu.VMEM((tile, dm.key), F32), pltpu.VMEM((tile, dm.val), F32),
               pltpu.VMEM((1, POOL_WMAX + tile, dm.pw), F32),
               pltpu.VMEM((GLA_HEADS, dm.dv, dm.dk), F32)]
    return pl.pallas_call(
        functools.partial(_prompt_mixer_kernel, dm=dm, tile=tile, n_tiles=n_tiles),
        grid=(bsz, n_tiles),
        in_specs=[pl.BlockSpec((tile, d), lambda b, t: (b * n_tiles + t, 0))] + [_const_spec(a) for a in w],
        out_specs=[pl.BlockSpec((tile * SUBLANES, LANES), lambda b, t: (b * n_tiles + t, 0)),
                   pl.BlockSpec((N_EXPERTS, tile), lambda b, t: (0, b * n_tiles + t)),
                   pl.BlockSpec((1, GLA_HEADS, dm.dk, dm.dv), lambda b, t: (b, 0, 0, 0)),
                   pl.BlockSpec((1, POOL_BUF, dm.pw), lambda b, t: (b, 0, 0))],
        out_shape=[jax.ShapeDtypeStruct((m * SUBLANES, LANES), F32),
                   jax.ShapeDtypeStruct((N_EXPERTS, m), F32),
                   jax.ShapeDtypeStruct((bsz, GLA_HEADS, dm.dk, dm.dv), F32),
                   jax.ShapeDtypeStruct((bsz, POOL_BUF, dm.pw), F32)],
        scratch_shapes=scratch,
        compiler_params=pltpu.CompilerParams(dimension_semantics=("arbitrary", "arbitrary"),
                                             vmem_limit_bytes=VMEM_LIMIT_BYTES),
        name="prompt_mixer",
    )(x.reshape(m, d), *w)


def _sample_mixer(x, s0, hist, w, dm):
    bsz, seq, d = x.shape
    m = bsz * seq
    x2 = x.reshape(m, d)
    tile = MIX_TILE
    cparams = pltpu.CompilerParams(dimension_semantics=("arbitrary",), vmem_limit_bytes=VMEM_LIMIT_BYTES)
    row = lambda n: pl.BlockSpec((tile, n), lambda i: (i, 0))
    q, k, v, gl = pl.pallas_call(
        functools.partial(_sample_proj_kernel, dm=dm),
        grid=(m // tile,),
        in_specs=[row(d)] + [_const_spec(a) for a in w],
        out_specs=[row(dm.key), row(dm.key), row(dm.val), row(dm.key)],
        out_shape=[jax.ShapeDtypeStruct((m, n), F32) for n in (dm.key, dm.key, dm.val, dm.key)],
        compiler_params=cparams,
        name="sample_proj",
    )(x2, *w)

    nseq = GLA_SEQS
    rows = nseq * seq
    grow = lambda n: pl.BlockSpec((rows, n), lambda i: (i, 0))
    st_spec = pl.BlockSpec((nseq, GLA_HEADS, dm.dk, dm.dv), lambda i: (i, 0, 0, 0))
    o, st = pl.pallas_call(
        functools.partial(_sample_gla_kernel, dm=dm, nseq=nseq, seq_len=seq),
        grid=(bsz // nseq,),
        in_specs=[grow(dm.key), grow(dm.key), grow(dm.val), grow(dm.key), st_spec],
        out_specs=[grow(dm.val), st_spec],
        out_shape=[jax.ShapeDtypeStruct((m, dm.val), F32), jax.ShapeDtypeStruct(s0.shape, F32)],
        scratch_shapes=[pltpu.VMEM((rows, dm.key), F32)],
        compiler_params=cparams,
        name="sample_gla",
    )(q, k, v, gl, s0)

    tseq = tile // seq
    hist_spec = pl.BlockSpec((tseq, POOL_BUF, dm.pw), lambda i: (i, 0, 0))
    h1, lg, buf = pl.pallas_call(
        functools.partial(_sample_tail_kernel, dm=dm, nseq=tseq, seq_len=seq),
        grid=(m // tile,),
        in_specs=[row(d), row(dm.val), hist_spec] + [_const_spec(a) for a in w],
        out_specs=[pl.BlockSpec((tile * SUBLANES, LANES), lambda i: (i, 0)),
                   pl.BlockSpec((N_EXPERTS, tile), lambda i: (0, i)), hist_spec],
        out_shape=[jax.ShapeDtypeStruct((m * SUBLANES, LANES), F32),
                   jax.ShapeDtypeStruct((N_EXPERTS, m), F32),
                   jax.ShapeDtypeStruct((bsz, POOL_BUF, dm.pw), F32)],
        scratch_shapes=[pltpu.VMEM((tseq, POOL_WMAX + seq, dm.pw), F32)],
        compiler_params=cparams,
        name="sample_tail",
    )(x2, o, hist, *w)
    return h1, lg, st, buf


def _router_kernel(lga_ref, lgb_ref, w_ref, s_ref, meta_ref, c_ref, carry, *, tile, n_tiles_a):
    i = pl.program_id(0)

    @pl.when(i == 0)
    def _():
        carry[...] = jnp.zeros(carry.shape, F32)

    eio = lax.broadcasted_iota(jnp.int32, (N_EXPERTS, tile), 0)
    work = jnp.where(i < n_tiles_a, lga_ref[...], lgb_ref[...])
    vals, hots = [], []
    for k in range(TOP_K):
        mx = jnp.max(work, axis=0, keepdims=True)
        idx = jnp.min(jnp.where(work == mx, eio, N_EXPERTS), axis=0, keepdims=True)
        hot = eio == idx
        vals.append(mx)
        hots.append(hot)
        work = jnp.where(hot, -jnp.inf, work)
    ex = [jnp.exp(v - vals[0]) for v in vals]
    den = ex[0]
    for k in range(1, TOP_K):
        den = den + ex[k]
    for k in range(TOP_K):
        w_ref[k:k + 1, :] = ex[k] / den
    sel = hots[0]
    for k in range(1, TOP_K):
        sel = sel | hots[k]
    r = lax.broadcasted_iota(jnp.int32, (tile, tile), 0)
    c = lax.broadcasted_iota(jnp.int32, (tile, tile), 1)
    earlier_token = (r < c).astype(BF16)
    rank_in_tile = _mm(sel.astype(BF16), earlier_token)
    cnt = jnp.broadcast_to(jnp.sum(sel.astype(F32), axis=1, keepdims=True), (N_EXPERTS, LANES))
    er = lax.broadcasted_iota(jnp.int32, (N_EXPERTS, N_EXPERTS), 0)
    ec = lax.broadcasted_iota(jnp.int32, (N_EXPERTS, N_EXPERTS), 1)
    first_slot = _mm((ec < er).astype(BF16), cnt.astype(BF16))
    slot = rank_in_tile + first_slot[:, 0:1]
    for k in range(TOP_K):
        s_ref[k:k + 1, :] = jnp.sum(jnp.where(hots[k], slot, 0.0), axis=0, keepdims=True).astype(jnp.int32)
    meta_ref[0, 0:N_EXPERTS, :] = cnt
    meta_ref[0, N_EXPERTS:2 * N_EXPERTS, :] = first_slot
    meta_ref[0, 2 * N_EXPERTS:3 * N_EXPERTS, :] = carry[...]
    carry[...] = carry[...] + cnt
    c_ref[...] = carry[...]


def _for_each_run(meta, fn):
    def body(e, carry):
        cnt = meta[e]
        first = meta[N_EXPERTS + e]
        dst = meta[2 * N_EXPERTS + e]
        for bit in range(RUN_BITS):
            @pl.when(((cnt >> bit) & 1) == 1)
            def _(bit=bit):
                done = cnt & ((1 << bit) - 1)
                fn(first + done, dst + done, 1 << bit)
        return carry
    lax.fori_loop(0, N_EXPERTS, body, 0)


def _rows(ref, first, n):
    return ref.at[pl.ds(pl.multiple_of(first * SUBLANES, SUBLANES), n * SUBLANES), :]


def _dispatch_kernel(fill_start_ref, fill_cnt_ref, tail_start_ref, meta_hbm, slot_hbm, h1a_ref, h1b_ref, xs_hbm,
                     meta_s0, meta_s1, slot_s0, slot_s1, stage, zeros_s, sem_m, sem_s, sem_r, sem_z,
                     *, tile, n_tiles, n_tiles_a):
    i = pl.program_id(0)
    sl = i % 2
    meta_s = (meta_s0, meta_s1)
    slot_s = (slot_s0, slot_s1)

    def index_copies(blk, s):
        return (pltpu.make_async_copy(meta_hbm.at[blk], meta_s[s], sem_m.at[s]),
                pltpu.make_async_copy(slot_hbm.at[blk], slot_s[s], sem_s.at[s]))

    def wait_stage(s):
        pltpu.make_async_copy(stage.at[s], stage.at[s], sem_r.at[s]).wait()

    @pl.when(i == 0)
    def _():
        for cp in index_copies(0, 0):
            cp.start()
        zeros_s[...] = jnp.zeros(zeros_s.shape, F32)

        def fill_copy(off, bit):
            n = 1 << bit
            return pltpu.make_async_copy(zeros_s.at[pl.ds(0, n * SUBLANES), :], _rows(xs_hbm, off, n), sem_z)

        def for_each_piece(fn):
            def body(e, carry):
                cnt = fill_cnt_ref[e]
                for bit in range(ZERO_FILL_BITS):
                    @pl.when(((cnt >> bit) & 1) == 1)
                    def _(bit=bit):
                        fn(fill_copy(fill_start_ref[e] + (cnt & ((1 << bit) - 1)), bit))
                return carry
            lax.fori_loop(0, N_EXPERTS, body, 0)

        top = ZERO_FILL_BITS - 1
        n_tail = (xs_hbm.shape[0] // SUBLANES - tail_start_ref[0]) >> top

        def for_each_tail_piece(fn):
            def body(j, carry):
                fn(fill_copy(tail_start_ref[0] + (j << top), top))
                return carry
            lax.fori_loop(0, n_tail, body, 0)

        for_each_piece(lambda cp: cp.start())
        for_each_tail_piece(lambda cp: cp.start())
        for_each_piece(lambda cp: cp.wait())
        for_each_tail_piece(lambda cp: cp.wait())

    def group_rows(h1_ref, s_):
        def group(g, carry):
            for u in range(DMA_UNROLL):
                t = g * DMA_UNROLL + u
                row = h1_ref[pl.ds(pl.multiple_of(t * SUBLANES, SUBLANES), SUBLANES), :]
                for k in range(TOP_K):
                    s = slot_s[s_][k * tile + t]
                    stage[s_, pl.ds(pl.multiple_of(s * SUBLANES, SUBLANES), SUBLANES), :] = row
            return carry
        lax.fori_loop(0, tile // DMA_UNROLL, group, 0)

    for s_ in range(2):
        @pl.when((sl == 1 - s_) & (i + 1 < n_tiles))
        def _(s_=s_):
            for cp in index_copies(i + 1, s_):
                cp.start()

    for s_ in range(2):
        @pl.when(sl == s_)
        def _(s_=s_):
            for cp in index_copies(i, s_):
                cp.wait()

            @pl.when(i >= 2)
            def _():
                wait_stage(s_)

            @pl.when(i < n_tiles_a)
            def _():
                group_rows(h1a_ref, s_)

            @pl.when(i >= n_tiles_a)
            def _():
                group_rows(h1b_ref, s_)

            _for_each_run(meta_s[s_], lambda s0, d0, n: pltpu.make_async_copy(
                _rows(stage.at[s_], s0, n), _rows(xs_hbm, d0, n), sem_r.at[s_]).start())

    @pl.when(i == n_tiles - 1)
    def _():
        @pl.when(n_tiles > 1)
        def _():
            wait_stage(1 - sl)
        wait_stage(sl)


def _expert_kernel(be_ref, nused_ref, next_e_ref, xs_ref, wgu_hbm, bgu_ref, wdn_hbm, bdn_ref, out_ref,
                   wgu_f, wdn_f, wgu_b, wdn_b, sem_w, *, rows, d_ff):
    i = pl.program_id(0)

    def weight_copies(e):
        return (pltpu.make_async_copy(wgu_hbm.at[e], wgu_f, sem_w.at[0]),
                pltpu.make_async_copy(wdn_hbm.at[e], wdn_f, sem_w.at[1]))

    @pl.when(i == 0)
    def _():
        for cp in weight_copies(be_ref[0]):
            cp.start()

    @pl.when((i < nused_ref[0]) & ((i == 0) | (be_ref[i] != be_ref[jnp.maximum(i - 1, 0)])))
    def _():
        for cp in weight_copies(be_ref[i]):
            cp.wait()
        wgu_b[...] = wgu_f[...].astype(BF16)
        wdn_b[...] = wdn_f[...].astype(BF16)

        @pl.when(next_e_ref[i] >= 0)
        def _():
            for cp in weight_copies(next_e_ref[i]):
                cp.start()

    @pl.when(i < nused_ref[0])
    def _():
        xb = _from_row_tiles(xs_ref, rows).astype(BF16)
        hh = _mm(xb, wgu_b[...]) + bgu_ref[0]
        gate = jnp.minimum(hh[:, 0:d_ff], SWIGLU_LIMIT)
        up = jnp.clip(hh[:, d_ff:2 * d_ff], -SWIGLU_LIMIT, SWIGLU_LIMIT)
        act = (up + 1.0) * (gate * jax.nn.sigmoid(GLU_ALPHA * gate))
        res = _mm(act.astype(BF16), wdn_b[...]) + bdn_ref[0]
        _to_row_tiles(out_ref, res, rows)

    @pl.when(i >= nused_ref[0])
    def _():
        out_ref[...] = jnp.zeros(out_ref.shape, F32)


def _combine_kernel(meta_hbm, slot_hbm, res_hbm, h1a_ref, h1b_ref, gw_ref, g_ref, b_ref, ya_ref, yb_ref,
                    meta_s0, meta_s1, slot_s0, slot_s1, stage, gbuf, sem_m, sem_s, sem_g,
                    *, tile, alpha, n_tiles, n_tiles_a):
    i = pl.program_id(0)
    sl = i % 2
    meta_s = (meta_s0, meta_s1)
    slot_s = (slot_s0, slot_s1)

    def index_copies(blk, s):
        return (pltpu.make_async_copy(meta_hbm.at[blk], meta_s[s], sem_m.at[s]),
                pltpu.make_async_copy(slot_hbm.at[blk], slot_s[s], sem_s.at[s]))

    def fetch_runs(s):
        _for_each_run(meta_s[s], lambda s0, d0, n: pltpu.make_async_copy(
            _rows(res_hbm, d0, n), _rows(stage.at[s], s0, n), sem_g.at[s]).start())

    @pl.when(i == 0)
    def _():
        for cp in index_copies(0, 0):
            cp.start()
        for cp in index_copies(0, 0):
            cp.wait()
        fetch_runs(0)

        @pl.when(n_tiles > 1)
        def _():
            for cp in index_copies(1, 1):
                cp.start()

    def regroup(s_):
        def group(g, carry):
            for u in range(DMA_UNROLL):
                t = g * DMA_UNROLL + u
                for k in range(TOP_K):
                    s = slot_s[s_][k * tile + t]
                    gbuf[k, pl.ds(pl.multiple_of(t * SUBLANES, SUBLANES), SUBLANES), :] = (
                        stage[s_, pl.ds(pl.multiple_of(s * SUBLANES, SUBLANES), SUBLANES), :])
            return carry
        lax.fori_loop(0, tile // DMA_UNROLL, group, 0)

    for s_ in range(2):
        @pl.when((sl == 1 - s_) & (i + 1 < n_tiles))
        def _(s_=s_):
            for cp in index_copies(i + 1, s_):
                cp.wait()
            fetch_runs(s_)

    for s_ in range(2):
        @pl.when(sl == s_)
        def _(s_=s_):
            pltpu.make_async_copy(stage.at[s_], stage.at[s_], sem_g.at[s_]).wait()
            regroup(s_)

    for s_ in range(2):
        @pl.when((sl == s_) & (i + 2 < n_tiles))
        def _(s_=s_):
            for cp in index_copies(i + 2, s_):
                cp.start()

    gw = gw_ref[...]
    z = alpha * jnp.where(i < n_tiles_a, _from_row_tiles(h1a_ref, tile), _from_row_tiles(h1b_ref, tile))
    for k in range(TOP_K):
        z = z + gw[:, k:k + 1] * _from_row_tiles(gbuf.at[k], tile)
    y = _layer_norm(z, g_ref[...], b_ref[...])

    @pl.when(i < n_tiles_a)
    def _():
        ya_ref[...] = y

    @pl.when(i >= n_tiles_a)
    def _():
        yb_ref[...] = y


def _tile_major(a, tile):
    k, m = a.shape
    return a.reshape(k, m // tile, tile).transpose(1, 0, 2).reshape(m // tile, k * tile)


def _moe(h1_a, h1_b, logits_a, logits_b, w_gu, b_gu, w_down, b_down, ln_g, ln_b, dm):
    m_a = logits_a.shape[1]
    m = m_a + logits_b.shape[1]
    rows = EXPERT_ROWS
    d_ff = w_down.shape[1]
    tile = MOE_TILE
    n_tiles = m // tile
    n_tiles_a = m_a // tile
    cparams = pltpu.CompilerParams(dimension_semantics=("arbitrary",), vmem_limit_bytes=VMEM_LIMIT_BYTES)

    def two_group_specs(block, lane_axis):
        pick = (lambda j: (0, j)) if lane_axis else (lambda j: (j, 0))
        return [pl.BlockSpec(block, lambda i, *_: pick(jnp.minimum(i, n_tiles_a - 1))),
                pl.BlockSpec(block, lambda i, *_: pick(jnp.maximum(i - n_tiles_a, 0)))]

    kblk = lambda: pl.BlockSpec((TOP_K, tile), lambda i: (0, i))
    gate_w, slot, meta, counts = pl.pallas_call(
        functools.partial(_router_kernel, tile=tile, n_tiles_a=n_tiles_a),
        grid=(n_tiles,),
        in_specs=two_group_specs((N_EXPERTS, tile), True),
        out_specs=[kblk(), kblk(), pl.BlockSpec((1, 3 * N_EXPERTS, LANES), lambda i: (i, 0, 0)),
                   pl.BlockSpec((N_EXPERTS, LANES), lambda i: (0, 0))],
        out_shape=[jax.ShapeDtypeStruct((TOP_K, m), F32), jax.ShapeDtypeStruct((TOP_K, m), jnp.int32),
                   jax.ShapeDtypeStruct((n_tiles, 3 * N_EXPERTS, LANES), F32),
                   jax.ShapeDtypeStruct((N_EXPERTS, LANES), F32)],
        scratch_shapes=[pltpu.VMEM((N_EXPERTS, LANES), F32)],
        compiler_params=cparams,
        name="moe_router",
    )(logits_a, logits_b)

    counts = counts[:, 0].astype(jnp.int32)
    padded = (counts + rows - 1) // rows * rows
    pad_end = jnp.cumsum(padded)
    pad_start = pad_end - padded
    n_blocks = -(-(m * TOP_K + N_EXPERTS * (rows - 1)) // rows)
    n_rows = n_blocks * rows
    block_row0 = jnp.arange(n_blocks, dtype=jnp.int32) * rows
    block_e = jnp.minimum(jnp.sum((pad_end[None, :] <= block_row0[:, None]).astype(jnp.int32), axis=1), N_EXPERTS - 1)
    n_used = (pad_end[-1] // rows).astype(jnp.int32).reshape(1)
    meta = meta[:, :, 0].astype(jnp.int32)
    tile_meta = jnp.concatenate([meta[:, 0:2 * N_EXPERTS], meta[:, 2 * N_EXPERTS:] + pad_start[None, :],
                                 jnp.zeros((n_tiles, LANES - 3 * N_EXPERTS), jnp.int32)], axis=1)
    slot_t = _tile_major(slot, tile)

    h1_specs = two_group_specs((tile * SUBLANES, LANES), False)
    any_spec = pl.BlockSpec(memory_space=pl.ANY)
    index_scratch = [pltpu.SMEM((LANES,), jnp.int32), pltpu.SMEM((LANES,), jnp.int32),
                     pltpu.SMEM((TOP_K * tile,), jnp.int32), pltpu.SMEM((TOP_K * tile,), jnp.int32),
                     pltpu.VMEM((2, TOP_K * tile * SUBLANES, LANES), F32)]
    xs = pl.pallas_call(
        functools.partial(_dispatch_kernel, tile=tile, n_tiles=n_tiles, n_tiles_a=n_tiles_a),
        grid_spec=pltpu.PrefetchScalarGridSpec(
            num_scalar_prefetch=3,
            grid=(n_tiles,),
            in_specs=[any_spec, any_spec] + h1_specs,
            out_specs=any_spec,
            scratch_shapes=index_scratch + [
                pltpu.VMEM(((1 << (ZERO_FILL_BITS - 1)) * SUBLANES, LANES), F32),
                pltpu.SemaphoreType.DMA((2,)), pltpu.SemaphoreType.DMA((2,)), pltpu.SemaphoreType.DMA((2,)),
                pltpu.SemaphoreType.DMA]),
        out_shape=jax.ShapeDtypeStruct((n_rows * SUBLANES, LANES), F32),
        compiler_params=cparams,
        name="moe_dispatch",
    )(pad_start + counts, padded - counts, pad_end[-1:], tile_meta, slot_t, h1_a, h1_b)

    blk = jnp.arange(n_blocks, dtype=jnp.int32)
    run_start = (blk < n_used[0]) & ((blk == 0) | (block_e != jnp.roll(block_e, 1)))
    next_start = jnp.min(jnp.where((blk[None, :] > blk[:, None]) & run_start[None, :], blk[None, :], n_blocks), axis=1)
    next_e = jnp.where(next_start < n_blocks, block_e[jnp.minimum(next_start, n_blocks - 1)], -1).astype(jnp.int32)

    last = lambda i, nu: jnp.minimum(i, nu[0] - 1)
    res = pl.pallas_call(
        functools.partial(_expert_kernel, rows=rows, d_ff=d_ff),
        grid_spec=pltpu.PrefetchScalarGridSpec(
            num_scalar_prefetch=3,
            grid=(n_blocks,),
            in_specs=[pl.BlockSpec((rows * SUBLANES, LANES), lambda i, be, nu, ne: (last(i, nu), 0)),
                      any_spec,
                      pl.BlockSpec((1, 1, 2 * d_ff), lambda i, be, nu, ne: (be[i], 0, 0)),
                      any_spec,
                      pl.BlockSpec((1, 1, dm.d), lambda i, be, nu, ne: (be[i], 0, 0))],
            out_specs=pl.BlockSpec((rows * SUBLANES, LANES), lambda i, be, nu, ne: (i, 0)),
            scratch_shapes=[pltpu.VMEM((dm.d, 2 * d_ff), F32), pltpu.VMEM((d_ff, dm.d), F32),
                            pltpu.VMEM((dm.d, 2 * d_ff), BF16), pltpu.VMEM((d_ff, dm.d), BF16),
                            pltpu.SemaphoreType.DMA((2,))]),
        out_shape=jax.ShapeDtypeStruct((n_rows * SUBLANES, LANES), F32),
        compiler_params=cparams,
        name="moe_experts",
    )(block_e, n_used, next_e, xs, w_gu, b_gu[:, None, :], w_down, b_down[:, None, :])

    ya, yb = pl.pallas_call(
        functools.partial(_combine_kernel, tile=tile, alpha=dm.alpha, n_tiles=n_tiles, n_tiles_a=n_tiles_a),
        grid=(n_tiles,),
        in_specs=[any_spec, any_spec, any_spec] + h1_specs + [
                  pl.BlockSpec((tile, TOP_K), lambda i: (i, 0)),
                  pl.BlockSpec((1, dm.d), lambda i: (0, 0)),
                  pl.BlockSpec((1, dm.d), lambda i: (0, 0))],
        out_specs=[pl.BlockSpec((tile, dm.d), lambda i: (jnp.minimum(i, n_tiles_a - 1), 0)),
                   pl.BlockSpec((tile, dm.d), lambda i: (jnp.maximum(i - n_tiles_a, 0), 0))],
        out_shape=[jax.ShapeDtypeStruct((m_a, dm.d), F32), jax.ShapeDtypeStruct((m - m_a, dm.d), F32)],
        scratch_shapes=index_scratch + [
            pltpu.VMEM((TOP_K, tile * SUBLANES, LANES), F32),
            pltpu.SemaphoreType.DMA((2,)), pltpu.SemaphoreType.DMA((2,)), pltpu.SemaphoreType.DMA((2,))],
        compiler_params=cparams,
        name="moe_combine",
    )(tile_meta, slot_t, res, h1_a, h1_b, gate_w.T, ln_g[None, :], ln_b[None, :])
    return ya, yb


def _pad_cols(a, n):
    return jnp.pad(a, ((0, 0), (0, n - a.shape[1])))


def _mixer_weights(w_in, w_gk2, b_gk, gla_norm_w, w_branch_gla, w_pool_grp, pool_scale, w_branch_pool, b_gates,
                   w_out, ln_g, ln_b, w_router, b_router, dm):
    o_lr = 2 * dm.key + dm.val
    o_tail = o_lr + GATE_RANK
    return MixW(
        w_in=jnp.concatenate([w_in[:, 0:o_lr], w_in[:, o_tail:], _pad_cols(w_in[:, o_lr:o_tail], LANES)],
                             axis=1).astype(BF16),
        w_gk2=jnp.pad(w_gk2, ((0, LANES - GATE_RANK), (0, 0))).astype(BF16),
        b_gk=b_gk[None, :],
        gla_norm=gla_norm_w[None, :],
        w_ba=w_branch_gla.astype(BF16),
        w_pg=w_pool_grp.astype(BF16),
        pool_scale=pool_scale[None, :],
        w_bp=w_branch_pool.astype(BF16),
        b_gates=b_gates[None, :],
        w_out=w_out.astype(BF16),
        ln_g=ln_g[None, :],
        ln_b=ln_b[None, :],
        w_rt=_pad_cols(w_router, LANES).astype(BF16),
        b_rt=_pad_cols(b_router[None, :], LANES),
    )


def kernel(x_prompt, x_sample, state_gla, state_pool, w_in, w_gk2, b_gk, gla_norm_w, w_branch_gla, w_pool_grp,
           pool_scale, w_branch_pool, b_gates, w_out, ln1_g, ln1_b, w_router, b_router, w_gu, b_gu, w_down, b_down,
           ln2_g, ln2_b):
    depth = w_in.shape[0]
    bp, lp, d = x_prompt.shape
    bs, ls, _ = x_sample.shape
    assert d == SUBLANES * LANES and lp % PROMPT_TILE == 0 and (bs * ls) % MIX_TILE == 0 and MIX_TILE % ls == 0
    assert bs % GLA_SEQS == 0 and ls % SUBLANES == 0
    assert (bp * lp) % MOE_TILE == 0 and (bs * ls) % MOE_TILE == 0
    dm = _dims(d, depth)
    yp, ys = x_prompt, x_sample
    gla_p, pool_p, gla_s, pool_s = [], [], [], []
    for l in range(depth):
        w = _mixer_weights(w_in[l], w_gk2[l], b_gk[l], gla_norm_w[l], w_branch_gla[l], w_pool_grp[l], pool_scale[l],
                           w_branch_pool[l], b_gates[l], w_out[l], ln1_g[l], ln1_b[l], w_router[l], b_router[l], dm)
        h1p, lgp, sp, bufp = _prompt_mixer(yp, w, dm)
        h1s, lgs, ss, bufs = _sample_mixer(ys, state_gla[l], state_pool[l], w, dm)
        yp, ys = _moe(h1p, h1s, lgp, lgs, w_gu[l], b_gu[l], w_down[l], b_down[l], ln2_g[l], ln2_b[l], dm)
        yp = yp.reshape(bp, lp, d)
        ys = ys.reshape(bs, ls, d)
        gla_p.append(sp.astype(state_gla.dtype))
        pool_p.append(bufp.astype(state_pool.dtype))
        gla_s.append(ss.astype(state_gla.dtype))
        pool_s.append(bufs.astype(state_pool.dtype))
    return (yp, ys, jnp.stack(gla_p, 0), jnp.stack(pool_p, 0), jnp.stack(gla_s, 0), jnp.stack(pool_s, 0))
```

```python
import functools
from typing import NamedTuple

import jax
import jax.numpy as jnp
from jax import lax
from jax.experimental import pallas as pl
from jax.experimental.pallas import tpu as pltpu

F32 = jnp.float32
BF16 = jnp.bfloat16

GLA_HEADS = 4
GATE_RANK = 16
GATE_NORMALIZER = 16.0
GLA_CHUNK = 64
RMS_EPS = 1e-6
POOL_WINDOWS = (2, 4, 8, 16)
POOL_WMAX = 16
POOL_BUF = POOL_WMAX - 1
N_EXPERTS = 32
TOP_K = 4
SWIGLU_LIMIT = 7.0
GLU_ALPHA = 1.702
LN_EPS = 1e-5

LANES = 128
SUBLANES = 8
VMEM_LIMIT_BYTES = 56 * 1024 * 1024

PROMPT_TILE = 512
MIX_TILE = 256
GLA_GROUP = 128
GLA_SEQS = 8
GLA_SEQ_UNROLL = 2
EXPERT_ROWS = 512
MOE_TILE = 512
CUMSUM_BLOCK = 256
DMA_UNROLL = 8
ZERO_FILL_BITS = (EXPERT_ROWS - 1).bit_length()
RUN_BITS = MOE_TILE.bit_length()


class Dims(NamedTuple):
    d: int
    dk: int
    dv: int
    key: int
    val: int
    pw: int
    pgc: int
    alpha: float


def _dims(d, depth):
    return Dims(d=d, dk=d // 8, dv=d // 4, key=d // 2, val=d, pw=d // 2, pgc=d // 8, alpha=(2.0 * depth) ** 0.25)


class MixW(NamedTuple):
    w_in: object
    w_gk2: object
    b_gk: object
    gla_norm: object
    w_ba: object
    w_pg: object
    pool_scale: object
    w_bp: object
    b_gates: object
    w_out: object
    ln_g: object
    ln_b: object
    w_rt: object
    b_rt: object


def _mm(a, b):
    return jnp.dot(a, b, preferred_element_type=F32)


def _layer_norm(z, g, b):
    mu = jnp.mean(z, axis=-1, keepdims=True)
    zc = z - mu
    var = jnp.mean(zc * zc, axis=-1, keepdims=True)
    return zc * lax.rsqrt(var + LN_EPS) * g + b


def _to_row_tiles(ref, val, rows):
    for c in range(SUBLANES):
        ref[pl.ds(c, rows, stride=SUBLANES), :] = val[:, c * LANES:(c + 1) * LANES]


def _from_row_tiles(ref, rows):
    return jnp.concatenate([ref[pl.ds(c, rows, stride=SUBLANES), :] for c in range(SUBLANES)], axis=1)


def _project_qkv(xb, w, dm, q_s, k_s, v_s, gl_s):
    q_s[...] = _mm(xb, w.w_in[:, 0:dm.key]) * (dm.dk ** -0.5)
    k_s[...] = _mm(xb, w.w_in[:, dm.key:2 * dm.key])
    v_s[...] = _mm(xb, w.w_in[:, 2 * dm.key:2 * dm.key + dm.val])
    off_lr = 2 * dm.key + 2 * dm.val + dm.pw + 2 * dm.d
    lr = _mm(xb, w.w_in[:, off_lr:off_lr + LANES])
    gk = _mm(lr.astype(BF16), w.w_gk2[...]) + w.b_gk[...]
    gl_s[...] = (jnp.minimum(gk, 0.0) - jnp.log1p(jnp.exp(-jnp.abs(gk)))) / GATE_NORMALIZER


def _chunk_cumsum(gl_s, b_s, rows, chunk):
    blk = min(rows, CUMSUM_BLOCK)
    r = lax.broadcasted_iota(jnp.int32, (blk, blk), 0)
    c = lax.broadcasted_iota(jnp.int32, (blk, blk), 1)
    tri = ((r // chunk == c // chunk) & (c <= r)).astype(BF16)
    for b0 in range(0, rows, blk):
        gl = gl_s[b0:b0 + blk, :]
        hi = gl.astype(BF16)
        lo = (gl - hi.astype(F32)).astype(BF16)
        b_s[b0:b0 + blk, :] = _mm(tri, hi) + _mm(tri, lo)


NT_DIMS = (((1,), (1,)), ((), ()))
TN_DIMS = (((0,), (0,)), ((), ()))


def _gla_tile(q_s, k_s, v_s, b_s, o_s, st_t, *, dm, rows, chunk, group):
    n_chunks = rows // chunk
    gr = lax.broadcasted_iota(jnp.int32, (group, group), 0)
    gc = lax.broadcasted_iota(jnp.int32, (group, group), 1)
    mask = (gr // chunk == gc // chunk) & (gc <= gr)

    def per_chunk_row(b, row):
        return jnp.concatenate([jnp.broadcast_to(b[c * chunk + row:c * chunk + row + 1, :], (chunk, b.shape[1]))
                                for c in range(n_chunks)], axis=0)

    for h in range(GLA_HEADS):
        ks = slice(h * dm.dk, (h + 1) * dm.dk)
        vs = slice(h * dm.dv, (h + 1) * dm.dv)
        b = b_s[:, ks]
        b_ref = per_chunk_row(b, chunk // 2)
        b_last = per_chunk_row(b, chunk - 1)
        q = q_s[:, ks]
        k = k_s[:, ks]
        vb = v_s[:, vs].astype(BF16)
        qa = (q * jnp.exp(b - b_ref)).astype(BF16)
        ka = (k * jnp.exp(b_ref - b)).astype(BF16)
        qe = (q * jnp.exp(b)).astype(BF16)
        kd = (k * jnp.exp(b_last - b)).astype(BF16)
        for g0 in range(0, rows, group):
            a = lax.dot_general(qa[g0:g0 + group], ka[g0:g0 + group], NT_DIMS, preferred_element_type=F32)
            a = jnp.where(mask, a, 0.0)
            o_s[g0:g0 + group, vs] = _mm(a.astype(BF16), vb[g0:g0 + group])
        s_t = st_t[h]
        for c in range(n_chunks):
            r0 = c * chunk
            o_s[r0:r0 + chunk, vs] += lax.dot_general(qe[r0:r0 + chunk], s_t.astype(BF16), NT_DIMS,
                                                      preferred_element_type=F32)
            decay = jnp.exp(b[r0 + chunk - 1:r0 + chunk, :])
            s_t = s_t * decay + lax.dot_general(vb[r0:r0 + chunk], kd[r0:r0 + chunk], TN_DIMS,
                                                preferred_element_type=F32)
        st_t[h] = s_t


def _gla_seqs(q_s, k_s, v_s, gl_s, b_s, o_s, s0_ref, st_ref, *, dm, nseq, chunk, unroll):
    _chunk_cumsum(gl_s, b_s, nseq * chunk, chunk)
    ri = lax.broadcasted_iota(jnp.int32, (chunk, chunk), 0)
    ci = lax.broadcasted_iota(jnp.int32, (chunk, chunk), 1)
    causal = ci <= ri
    nt, tn = NT_DIMS, TN_DIMS

    def one_seq(seq):
        r0 = pl.multiple_of(seq * chunk, chunk)
        for h in range(GLA_HEADS):
            ks = slice(h * dm.dk, (h + 1) * dm.dk)
            vs = slice(h * dm.dv, (h + 1) * dm.dv)
            bc = b_s[pl.ds(r0, chunk), ks]
            b_ref = bc[chunk // 2:chunk // 2 + 1, :]
            b_last = bc[chunk - 1:chunk, :]
            qc = q_s[pl.ds(r0, chunk), ks]
            kc = k_s[pl.ds(r0, chunk), ks]
            vc = v_s[pl.ds(r0, chunk), vs].astype(BF16)
            s_old = s0_ref[seq, h]
            a = lax.dot_general((qc * jnp.exp(bc - b_ref)).astype(BF16), (kc * jnp.exp(b_ref - bc)).astype(BF16),
                                nt, preferred_element_type=F32)
            a = jnp.where(causal, a, 0.0)
            o = _mm(a.astype(BF16), vc) + _mm((qc * jnp.exp(bc)).astype(BF16), s_old.astype(BF16))
            o_s[pl.ds(r0, chunk), vs] = o
            e_col = jnp.broadcast_to(jnp.exp(b_last), (dm.dk, dm.dk)).T
            decay = jnp.concatenate([e_col] * (dm.dv // dm.dk), axis=1)
            kv = lax.dot_general((kc * jnp.exp(b_last - bc)).astype(BF16), vc, tn, preferred_element_type=F32)
            st_ref[seq, h] = decay * s_old + kv

    def step(g, carry):
        for u in range(unroll):
            one_seq(g * unroll + u)
        return carry

    lax.fori_loop(0, nseq // unroll, step, 0)


def _mixer_tail(x, xb, o_s, ext_s, pos0, w, dm, *, nseq, seq_len):
    t = nseq * seq_len
    off_g = 2 * dm.key + dm.val
    off_u = off_g + dm.val
    off_ga = off_u + dm.pw
    off_gb = off_ga + dm.d

    g = _mm(xb, w.w_in[:, off_g:off_g + dm.val])
    parts = []
    for h in range(GLA_HEADS):
        vs = slice(h * dm.dv, (h + 1) * dm.dv)
        oh = o_s[:, vs]
        ms = jnp.mean(oh * oh, axis=-1, keepdims=True)
        on = oh * lax.rsqrt(ms + RMS_EPS) * w.gla_norm[...]
        gh = g[:, vs]
        parts.append((on * (gh * jax.nn.sigmoid(gh))).astype(BF16))
    branch_a = _mm(jnp.concatenate(parts, axis=1), w.w_ba[...])

    u = _mm(xb, w.w_in[:, off_u:off_u + dm.pw])
    ext_s[:, POOL_WMAX:POOL_WMAX + seq_len, :] = u.reshape(nseq, seq_len, dm.pw)
    p = lax.broadcasted_iota(jnp.int32, (nseq, seq_len, dm.pgc), 1)
    pooled = []
    for gi, win in enumerate(POOL_WINDOWS):
        cs = slice(gi * dm.pgc, (gi + 1) * dm.pgc)
        cur = ext_s[:, POOL_WMAX:POOL_WMAX + seq_len, cs]
        acc = cur
        for j in range(1, win):
            acc = acc + ext_s[:, POOL_WMAX - j:POOL_WMAX - j + seq_len, cs]
        cnt = jnp.minimum(win, p + (pos0 + 1)).astype(F32)
        pg = (acc / cnt - cur).reshape(t, dm.pgc)
        pg = _mm(pg.astype(BF16), w.w_pg[gi]) * w.pool_scale[:, cs]
        pooled.append(pg.astype(BF16))
    branch_b = _mm(jnp.concatenate(pooled, axis=1), w.w_bp[...])

    gate_a = jax.nn.sigmoid(_mm(xb, w.w_in[:, off_ga:off_ga + dm.d]) + w.b_gates[:, 0:dm.d])
    merged = gate_a * branch_a
    gate_b = jax.nn.sigmoid(_mm(xb, w.w_in[:, off_gb:off_gb + dm.d]) + w.b_gates[:, dm.d:2 * dm.d])
    merged = merged + gate_b * branch_b
    mix = _mm(merged.astype(BF16), w.w_out[...])
    h1 = _layer_norm(dm.alpha * x + mix, w.ln_g[...], w.ln_b[...])
    logits = _mm(h1.astype(BF16), w.w_rt[...]) + w.b_rt[...]
    return h1, logits.T[0:N_EXPERTS, :]


N_MIXW = len(MixW._fields)


def _prompt_mixer_kernel(*refs, dm, tile, n_tiles):
    x_ref = refs[0]
    w = MixW(*refs[1:1 + N_MIXW])
    h1_ref, lg_ref, st_ref, buf_ref, q_s, k_s, v_s, gl_s, b_s, o_s, ext_s, st_t = refs[1 + N_MIXW:]
    lt = pl.program_id(1)

    @pl.when(lt == 0)
    def _():
        st_t[...] = jnp.zeros(st_t.shape, F32)
        ext_s[:, 0:POOL_WMAX, :] = jnp.zeros((1, POOL_WMAX, dm.pw), F32)

    x = x_ref[...]
    xb = x.astype(BF16)
    _project_qkv(xb, w, dm, q_s, k_s, v_s, gl_s)
    _chunk_cumsum(gl_s, b_s, tile, GLA_CHUNK)
    _gla_tile(q_s, k_s, v_s, b_s, o_s, st_t, dm=dm, rows=tile, chunk=GLA_CHUNK, group=GLA_GROUP)

    @pl.when(lt == n_tiles - 1)
    def _():
        for h in range(GLA_HEADS):
            st_ref[0, h] = st_t[h].T

    h1, logits_t = _mixer_tail(x, xb, o_s, ext_s, lt * tile, w, dm, nseq=1, seq_len=tile)
    _to_row_tiles(h1_ref, h1, tile)
    lg_ref[...] = logits_t
    ext_s[:, 0:POOL_WMAX, :] = ext_s[:, tile:tile + POOL_WMAX, :]

    @pl.when(lt == n_tiles - 1)
    def _():
        buf_ref[...] = ext_s[:, 1:POOL_WMAX, :]


def _sample_proj_kernel(*refs, dm):
    x_ref = refs[0]
    w = MixW(*refs[1:1 + N_MIXW])
    q_ref, k_ref, v_ref, gl_ref = refs[1 + N_MIXW:]
    _project_qkv(x_ref[...].astype(BF16), w, dm, q_ref, k_ref, v_ref, gl_ref)


def _sample_gla_kernel(q_ref, k_ref, v_ref, gl_ref, s0_ref, o_ref, st_ref, b_s, *, dm, nseq, seq_len):
    _gla_seqs(q_ref, k_ref, v_ref, gl_ref, b_s, o_ref, s0_ref, st_ref, dm=dm, nseq=nseq, chunk=seq_len,
              unroll=GLA_SEQ_UNROLL)


def _sample_tail_kernel(*refs, dm, nseq, seq_len):
    x_ref, o_ref, hist_ref = refs[0:3]
    w = MixW(*refs[3:3 + N_MIXW])
    h1_ref, lg_ref, buf_ref, ext_s = refs[3 + N_MIXW:]
    ext_s[:, 0:1, :] = jnp.zeros((nseq, 1, dm.pw), F32)
    ext_s[:, 1:POOL_WMAX, :] = hist_ref[...]
    x = x_ref[...]
    h1, logits_t = _mixer_tail(x, x.astype(BF16), o_ref, ext_s, POOL_BUF, w, dm, nseq=nseq, seq_len=seq_len)
    _to_row_tiles(h1_ref, h1, nseq * seq_len)
    lg_ref[...] = logits_t
    buf_ref[...] = ext_s[:, seq_len + 1:seq_len + POOL_WMAX, :]


def _const_spec(arr):
    nd = arr.ndim
    return pl.BlockSpec(arr.shape, lambda *_: (0,) * nd, pipeline_mode=pl.Buffered(1))


def _prompt_mixer(x, w, dm):
    bsz, seq, d = x.shape
    tile = PROMPT_TILE
    n_tiles = seq // tile
    m = bsz * seq
    scratch = [pltpu.VMEM((tile, dm.key), F32), pltpu.VMEM((tile, dm.key), F32), pltpu.VMEM((tile, dm.val), F32),
               pltpu.VMEM((tile, dm.key), F32), pltpu.VMEM((tile, dm.key), F32), pltpu.VMEM((tile, dm.val), F32),
               pltpu.VMEM((1, POOL_WMAX + tile, dm.pw), F32),
               pltpu.VMEM((GLA_HEADS, dm.dv, dm.dk), F32)]
    return pl.pallas_call(
        functools.partial(_prompt_mixer_kernel, dm=dm, tile=tile, n_tiles=n_tiles),
        grid=(bsz, n_tiles),
        in_specs=[pl.BlockSpec((tile, d), lambda b, t: (b * n_tiles + t, 0))] + [_const_spec(a) for a in w],
        out_specs=[pl.BlockSpec((tile * SUBLANES, LANES), lambda b, t: (b * n_tiles + t, 0)),
                   pl.BlockSpec((N_EXPERTS, tile), lambda b, t: (0, b * n_tiles + t)),
                   pl.BlockSpec((1, GLA_HEADS, dm.dk, dm.dv), lambda b, t: (b, 0, 0, 0)),
                   pl.BlockSpec((1, POOL_BUF, dm.pw), lambda b, t: (b, 0, 0))],
        out_shape=[jax.ShapeDtypeStruct((m * SUBLANES, LANES), F32),
                   jax.ShapeDtypeStruct((N_EXPERTS, m), F32),
                   jax.ShapeDtypeStruct((bsz, GLA_HEADS, dm.dk, dm.dv), F32),
                   jax.ShapeDtypeStruct((bsz, POOL_BUF, dm.pw), F32)],
        scratch_shapes=scratch,
        compiler_params=pltpu.CompilerParams(dimension_semantics=("arbitrary", "arbitrary"),
                                             vmem_limit_bytes=VMEM_LIMIT_BYTES),
        name="prompt_mixer",
    )(x.reshape(m, d), *w)


def _sample_mixer(x, s0, hist, w, dm):
    bsz, seq, d = x.shape
    m = bsz * seq
    x2 = x.reshape(m, d)
    tile = MIX_TILE
    cparams = pltpu.CompilerParams(dimension_semantics=("arbitrary",), vmem_limit_bytes=VMEM_LIMIT_BYTES)
    row = lambda n: pl.BlockSpec((tile, n), lambda i: (i, 0))
    q, k, v, gl = pl.pallas_call(
        functools.partial(_sample_proj_kernel, dm=dm),
        grid=(m // tile,),
        in_specs=[row(d)] + [_const_spec(a) for a in w],
        out_specs=[row(dm.key), row(dm.key), row(dm.val), row(dm.key)],
        out_shape=[jax.ShapeDtypeStruct((m, n), F32) for n in (dm.key, dm.key, dm.val, dm.key)],
        compiler_params=cparams,
        name="sample_proj",
    )(x2, *w)

    nseq = GLA_SEQS
    rows = nseq * seq
    grow = lambda n: pl.BlockSpec((rows, n), lambda i: (i, 0))
    st_spec = pl.BlockSpec((nseq, GLA_HEADS, dm.dk, dm.dv), lambda i: (i, 0, 0, 0))
    o, st = pl.pallas_call(
        functools.partial(_sample_gla_kernel, dm=dm, nseq=nseq, seq_len=seq),
        grid=(bsz // nseq,),
        in_specs=[grow(dm.key), grow(dm.key), grow(dm.val), grow(dm.key), st_spec],
        out_specs=[grow(dm.val), st_spec],
        out_shape=[jax.ShapeDtypeStruct((m, dm.val), F32), jax.ShapeDtypeStruct(s0.shape, F32)],
        scratch_shapes=[pltpu.VMEM((rows, dm.key), F32)],
        compiler_params=cparams,
        name="sample_gla",
    )(q, k, v, gl, s0)

    tseq = tile // seq
    hist_spec = pl.BlockSpec((tseq, POOL_BUF, dm.pw), lambda i: (i, 0, 0))
    h1, lg, buf = pl.pallas_call(
        functools.partial(_sample_tail_kernel, dm=dm, nseq=tseq, seq_len=seq),
        grid=(m // tile,),
        in_specs=[row(d), row(dm.val), hist_spec] + [_const_spec(a) for a in w],
        out_specs=[pl.BlockSpec((tile * SUBLANES, LANES), lambda i: (i, 0)),
                   pl.BlockSpec((N_EXPERTS, tile), lambda i: (0, i)), hist_spec],
        out_shape=[jax.ShapeDtypeStruct((m * SUBLANES, LANES), F32),
                   jax.ShapeDtypeStruct((N_EXPERTS, m), F32),
                   jax.ShapeDtypeStruct((bsz, POOL_BUF, dm.pw), F32)],
        scratch_shapes=[pltpu.VMEM((tseq, POOL_WMAX + seq, dm.pw), F32)],
        compiler_params=cparams,
        name="sample_tail",
    )(x2, o, hist, *w)
    return h1, lg, st, buf


def _router_kernel(lga_ref, lgb_ref, w_ref, s_ref, meta_ref, c_ref, carry, *, tile, n_tiles_a):
    i = pl.program_id(0)

    @pl.when(i == 0)
    def _():
        carry[...] = jnp.zeros(carry.shape, F32)

    eio = lax.broadcasted_iota(jnp.int32, (N_EXPERTS, tile), 0)
    work = jnp.where(i < n_tiles_a, lga_ref[...], lgb_ref[...])
    vals, hots = [], []
    for k in range(TOP_K):
        mx = jnp.max(work, axis=0, keepdims=True)
        idx = jnp.min(jnp.where(work == mx, eio, N_EXPERTS), axis=0, keepdims=True)
        hot = eio == idx
        vals.append(mx)
        hots.append(hot)
        work = jnp.where(hot, -jnp.inf, work)
    ex = [jnp.exp(v - vals[0]) for v in vals]
    den = ex[0]
    for k in range(1, TOP_K):
        den = den + ex[k]
    for k in range(TOP_K):
        w_ref[k:k + 1, :] = ex[k] / den
    sel = hots[0]
    for k in range(1, TOP_K):
        sel = sel | hots[k]
    r = lax.broadcasted_iota(jnp.int32, (tile, tile), 0)
    c = lax.broadcasted_iota(jnp.int32, (tile, tile), 1)
    earlier_token = (r < c).astype(BF16)
    rank_in_tile = _mm(sel.astype(BF16), earlier_token)
    cnt = jnp.broadcast_to(jnp.sum(sel.astype(F32), axis=1, keepdims=True), (N_EXPERTS, LANES))
    er = lax.broadcasted_iota(jnp.int32, (N_EXPERTS, N_EXPERTS), 0)
    ec = lax.broadcasted_iota(jnp.int32, (N_EXPERTS, N_EXPERTS), 1)
    lower = (ec < er).astype(BF16)
    cnt_hi = jnp.floor(cnt * (1.0 / 256.0))
    first_slot = _mm(lower, (cnt - 256.0 * cnt_hi).astype(BF16)) + 256.0 * _mm(lower, cnt_hi.astype(BF16))
    slot = rank_in_tile + first_slot[:, 0:1]
    for k in range(TOP_K):
        s_ref[k:k + 1, :] = jnp.sum(jnp.where(hots[k], slot, 0.0), axis=0, keepdims=True).astype(jnp.int32)
    meta_ref[0, 0:N_EXPERTS, :] = cnt
    meta_ref[0, N_EXPERTS:2 * N_EXPERTS, :] = first_slot
    meta_ref[0, 2 * N_EXPERTS:3 * N_EXPERTS, :] = carry[...]
    carry[...] = carry[...] + cnt
    c_ref[...] = carry[...]


def _for_each_run(meta, fn):
    def body(e, carry):
        cnt = meta[e]
        first = meta[N_EXPERTS + e]
        dst = meta[2 * N_EXPERTS + e]
        for bit in range(RUN_BITS):
            @pl.when(((cnt >> bit) & 1) == 1)
            def _(bit=bit):
                done = cnt & ((1 << bit) - 1)
                fn(first + done, dst + done, 1 << bit)
        return carry
    lax.fori_loop(0, N_EXPERTS, body, 0)


def _rows(ref, first, n):
    return ref.at[pl.ds(pl.multiple_of(first * SUBLANES, SUBLANES), n * SUBLANES), :]


def _dispatch_kernel(fill_start_ref, fill_cnt_ref, tail_start_ref, meta_hbm, slot_hbm, h1a_ref, h1b_ref, xs_hbm,
                     meta_s0, meta_s1, slot_s0, slot_s1, stage, zeros_s, sem_m, sem_s, sem_r, sem_z,
                     *, tile, n_tiles, n_tiles_a):
    i = pl.program_id(0)
    sl = i % 2
    meta_s = (meta_s0, meta_s1)
    slot_s = (slot_s0, slot_s1)

    def index_copies(blk, s):
        return (pltpu.make_async_copy(meta_hbm.at[blk], meta_s[s], sem_m.at[s]),
                pltpu.make_async_copy(slot_hbm.at[blk], slot_s[s], sem_s.at[s]))

    def wait_stage(s):
        pltpu.make_async_copy(stage.at[s], stage.at[s], sem_r.at[s]).wait()

    @pl.when(i == 0)
    def _():
        for cp in index_copies(0, 0):
            cp.start()
        zeros_s[...] = jnp.zeros(zeros_s.shape, F32)

        def fill_copy(off, bit):
            n = 1 << bit
            return pltpu.make_async_copy(zeros_s.at[pl.ds(0, n * SUBLANES), :], _rows(xs_hbm, off, n), sem_z)

        def for_each_piece(fn):
            def body(e, carry):
                cnt = fill_cnt_ref[e]
                for bit in range(ZERO_FILL_BITS):
                    @pl.when(((cnt >> bit) & 1) == 1)
                    def _(bit=bit):
                        fn(fill_copy(fill_start_ref[e] + (cnt & ((1 << bit) - 1)), bit))
                return carry
            lax.fori_loop(0, N_EXPERTS, body, 0)

        top = ZERO_FILL_BITS - 1
        n_tail = (xs_hbm.shape[0] // SUBLANES - tail_start_ref[0]) >> top

        def for_each_tail_piece(fn):
            def body(j, carry):
                fn(fill_copy(tail_start_ref[0] + (j << top), top))
                return carry
            lax.fori_loop(0, n_tail, body, 0)

        for_each_piece(lambda cp: cp.start())
        for_each_tail_piece(lambda cp: cp.start())
        for_each_piece(lambda cp: cp.wait())
        for_each_tail_piece(lambda cp: cp.wait())

    def group_rows(h1_ref, s_):
        def group(g, carry):
            for u in range(DMA_UNROLL):
                t = g * DMA_UNROLL + u
                row = h1_ref[pl.ds(pl.multiple_of(t * SUBLANES, SUBLANES), SUBLANES), :]
                for k in range(TOP_K):
                    s = slot_s[s_][k * tile + t]
                    stage[s_, pl.ds(pl.multiple_of(s * SUBLANES, SUBLANES), SUBLANES), :] = row
            return carry
        lax.fori_loop(0, tile // DMA_UNROLL, group, 0)

    for s_ in range(2):
        @pl.when((sl == 1 - s_) & (i + 1 < n_tiles))
        def _(s_=s_):
            for cp in index_copies(i + 1, s_):
                cp.start()

    for s_ in range(2):
        @pl.when(sl == s_)
        def _(s_=s_):
            for cp in index_copies(i, s_):
                cp.wait()

            @pl.when(i >= 2)
            def _():
                wait_stage(s_)

            @pl.when(i < n_tiles_a)
            def _():
                group_rows(h1a_ref, s_)

            @pl.when(i >= n_tiles_a)
            def _():
                group_rows(h1b_ref, s_)

            _for_each_run(meta_s[s_], lambda s0, d0, n: pltpu.make_async_copy(
                _rows(stage.at[s_], s0, n), _rows(xs_hbm, d0, n), sem_r.at[s_]).start())

    @pl.when(i == n_tiles - 1)
    def _():
        @pl.when(n_tiles > 1)
        def _():
            wait_stage(1 - sl)
        wait_stage(sl)


def _expert_kernel(be_ref, nused_ref, next_e_ref, xs_ref, wgu_hbm, bgu_ref, wdn_hbm, bdn_ref, out_ref,
                   wgu_f, wdn_f, wgu_b, wdn_b, sem_w, *, rows, d_ff):
    i = pl.program_id(0)

    def weight_copies(e):
        return (pltpu.make_async_copy(wgu_hbm.at[e], wgu_f, sem_w.at[0]),
                pltpu.make_async_copy(wdn_hbm.at[e], wdn_f, sem_w.at[1]))

    @pl.when(i == 0)
    def _():
        for cp in weight_copies(be_ref[0]):
            cp.start()

    @pl.when((i < nused_ref[0]) & ((i == 0) | (be_ref[i] != be_ref[jnp.maximum(i - 1, 0)])))
    def _():
        for cp in weight_copies(be_ref[i]):
            cp.wait()
        wgu_b[...] = wgu_f[...].astype(BF16)
        wdn_b[...] = wdn_f[...].astype(BF16)

        @pl.when(next_e_ref[i] >= 0)
        def _():
            for cp in weight_copies(next_e_ref[i]):
                cp.start()

    @pl.when(i < nused_ref[0])
    def _():
        xb = _from_row_tiles(xs_ref, rows).astype(BF16)
        hh = _mm(xb, wgu_b[...]) + bgu_ref[0]
        gate = jnp.minimum(hh[:, 0:d_ff], SWIGLU_LIMIT)
        up = jnp.clip(hh[:, d_ff:2 * d_ff], -SWIGLU_LIMIT, SWIGLU_LIMIT)
        act = (up + 1.0) * (gate * jax.nn.sigmoid(GLU_ALPHA * gate))
        res = _mm(act.astype(BF16), wdn_b[...]) + bdn_ref[0]
        _to_row_tiles(out_ref, res, rows)

    @pl.when(i >= nused_ref[0])
    def _():
        out_ref[...] = jnp.zeros(out_ref.shape, F32)


def _combine_kernel(meta_hbm, slot_hbm, res_hbm, h1a_ref, h1b_ref, gw_ref, g_ref, b_ref, ya_ref, yb_ref,
                    meta_s0, meta_s1, slot_s0, slot_s1, stage, gbuf, sem_m, sem_s, sem_g,
                    *, tile, alpha, n_tiles, n_tiles_a):
    i = pl.program_id(0)
    sl = i % 2
    meta_s = (meta_s0, meta_s1)
    slot_s = (slot_s0, slot_s1)

    def index_copies(blk, s):
        return (pltpu.make_async_copy(meta_hbm.at[blk], meta_s[s], sem_m.at[s]),
                pltpu.make_async_copy(slot_hbm.at[blk], slot_s[s], sem_s.at[s]))

    def fetch_runs(s):
        _for_each_run(meta_s[s], lambda s0, d0, n: pltpu.make_async_copy(
            _rows(res_hbm, d0, n), _rows(stage.at[s], s0, n), sem_g.at[s]).start())

    @pl.when(i == 0)
    def _():
        for cp in index_copies(0, 0):
            cp.start()
        for cp in index_copies(0, 0):
            cp.wait()
        fetch_runs(0)

        @pl.when(n_tiles > 1)
        def _():
            for cp in index_copies(1, 1):
                cp.start()

    def regroup(s_):
        def group(g, carry):
            for u in range(DMA_UNROLL):
                t = g * DMA_UNROLL + u
                for k in range(TOP_K):
                    s = slot_s[s_][k * tile + t]
                    gbuf[k, pl.ds(pl.multiple_of(t * SUBLANES, SUBLANES), SUBLANES), :] = (
                        stage[s_, pl.ds(pl.multiple_of(s * SUBLANES, SUBLANES), SUBLANES), :])
            return carry
        lax.fori_loop(0, tile // DMA_UNROLL, group, 0)

    for s_ in range(2):
        @pl.when((sl == 1 - s_) & (i + 1 < n_tiles))
        def _(s_=s_):
            for cp in index_copies(i + 1, s_):
                cp.wait()
            fetch_runs(s_)

    for s_ in range(2):
        @pl.when(sl == s_)
        def _(s_=s_):
            pltpu.make_async_copy(stage.at[s_], stage.at[s_], sem_g.at[s_]).wait()
            regroup(s_)

    for s_ in range(2):
        @pl.when((sl == s_) & (i + 2 < n_tiles))
        def _(s_=s_):
            for cp in index_copies(i + 2, s_):
                cp.start()

    gw = gw_ref[...]
    z = alpha * jnp.where(i < n_tiles_a, _from_row_tiles(h1a_ref, tile), _from_row_tiles(h1b_ref, tile))
    for k in range(TOP_K):
        z = z + gw[:, k:k + 1] * _from_row_tiles(gbuf.at[k], tile)
    y = _layer_norm(z, g_ref[...], b_ref[...])

    @pl.when(i < n_tiles_a)
    def _():
        ya_ref[...] = y

    @pl.when(i >= n_tiles_a)
    def _():
        yb_ref[...] = y


def _tile_major(a, tile):
    k, m = a.shape
    return a.reshape(k, m // tile, tile).transpose(1, 0, 2).reshape(m // tile, k * tile)


def _moe(h1_a, h1_b, logits_a, logits_b, w_gu, b_gu, w_down, b_down, ln_g, ln_b, dm):
    m_a = logits_a.shape[1]
    m = m_a + logits_b.shape[1]
    rows = EXPERT_ROWS
    d_ff = w_down.shape[1]
    tile = MOE_TILE
    n_tiles = m // tile
    n_tiles_a = m_a // tile
    cparams = pltpu.CompilerParams(dimension_semantics=("arbitrary",), vmem_limit_bytes=VMEM_LIMIT_BYTES)

    def two_group_specs(block, lane_axis):
        pick = (lambda j: (0, j)) if lane_axis else (lambda j: (j, 0))
        return [pl.BlockSpec(block, lambda i, *_: pick(jnp.minimum(i, n_tiles_a - 1))),
                pl.BlockSpec(block, lambda i, *_: pick(jnp.maximum(i - n_tiles_a, 0)))]

    kblk = lambda: pl.BlockSpec((TOP_K, tile), lambda i: (0, i))
    gate_w, slot, meta, counts = pl.pallas_call(
        functools.partial(_router_kernel, tile=tile, n_tiles_a=n_tiles_a),
        grid=(n_tiles,),
        in_specs=two_group_specs((N_EXPERTS, tile), True),
        out_specs=[kblk(), kblk(), pl.BlockSpec((1, 3 * N_EXPERTS, LANES), lambda i: (i, 0, 0)),
                   pl.BlockSpec((N_EXPERTS, LANES), lambda i: (0, 0))],
        out_shape=[jax.ShapeDtypeStruct((TOP_K, m), F32), jax.ShapeDtypeStruct((TOP_K, m), jnp.int32),
                   jax.ShapeDtypeStruct((n_tiles, 3 * N_EXPERTS, LANES), F32),
                   jax.ShapeDtypeStruct((N_EXPERTS, LANES), F32)],
        scratch_shapes=[pltpu.VMEM((N_EXPERTS, LANES), F32)],
        compiler_params=cparams,
        name="moe_router",
    )(logits_a, logits_b)

    counts = counts[:, 0].astype(jnp.int32)
    padded = (counts + rows - 1) // rows * rows
    pad_end = jnp.cumsum(padded)
    pad_start = pad_end - padded
    n_blocks = -(-(m * TOP_K + N_EXPERTS * (rows - 1)) // rows)
    n_rows = n_blocks * rows
    block_row0 = jnp.arange(n_blocks, dtype=jnp.int32) * rows
    block_e = jnp.minimum(jnp.sum((pad_end[None, :] <= block_row0[:, None]).astype(jnp.int32), axis=1), N_EXPERTS - 1)
    n_used = (pad_end[-1] // rows).astype(jnp.int32).reshape(1)
    meta = meta[:, :, 0].astype(jnp.int32)
    tile_meta = jnp.concatenate([meta[:, 0:2 * N_EXPERTS], meta[:, 2 * N_EXPERTS:] + pad_start[None, :],
                                 jnp.zeros((n_tiles, LANES - 3 * N_EXPERTS), jnp.int32)], axis=1)
    slot_t = _tile_major(slot, tile)

    h1_specs = two_group_specs((tile * SUBLANES, LANES), False)
    any_spec = pl.BlockSpec(memory_space=pl.ANY)
    index_scratch = [pltpu.SMEM((LANES,), jnp.int32), pltpu.SMEM((LANES,), jnp.int32),
                     pltpu.SMEM((TOP_K * tile,), jnp.int32), pltpu.SMEM((TOP_K * tile,), jnp.int32),
                     pltpu.VMEM((2, TOP_K * tile * SUBLANES, LANES), F32)]
    xs = pl.pallas_call(
        functools.partial(_dispatch_kernel, tile=tile, n_tiles=n_tiles, n_tiles_a=n_tiles_a),
        grid_spec=pltpu.PrefetchScalarGridSpec(
            num_scalar_prefetch=3,
            grid=(n_tiles,),
            in_specs=[any_spec, any_spec] + h1_specs,
            out_specs=any_spec,
            scratch_shapes=index_scratch + [
                pltpu.VMEM(((1 << (ZERO_FILL_BITS - 1)) * SUBLANES, LANES), F32),
                pltpu.SemaphoreType.DMA((2,)), pltpu.SemaphoreType.DMA((2,)), pltpu.SemaphoreType.DMA((2,)),
                pltpu.SemaphoreType.DMA]),
        out_shape=jax.ShapeDtypeStruct((n_rows * SUBLANES, LANES), F32),
        compiler_params=cparams,
        name="moe_dispatch",
    )(pad_start + counts, padded - counts, pad_end[-1:], tile_meta, slot_t, h1_a, h1_b)

    blk = jnp.arange(n_blocks, dtype=jnp.int32)
    run_start = (blk < n_used[0]) & ((blk == 0) | (block_e != jnp.roll(block_e, 1)))
    next_start = jnp.min(jnp.where((blk[None, :] > blk[:, None]) & run_start[None, :], blk[None, :], n_blocks), axis=1)
    next_e = jnp.where(next_start < n_blocks, block_e[jnp.minimum(next_start, n_blocks - 1)], -1).astype(jnp.int32)

    last = lambda i, nu: jnp.minimum(i, nu[0] - 1)
    res = pl.pallas_call(
        functools.partial(_expert_kernel, rows=rows, d_ff=d_ff),
        grid_spec=pltpu.PrefetchScalarGridSpec(
            num_scalar_prefetch=3,
            grid=(n_blocks,),
            in_specs=[pl.BlockSpec((rows * SUBLANES, LANES), lambda i, be, nu, ne: (last(i, nu), 0)),
                      any_spec,
                      pl.BlockSpec((1, 1, 2 * d_ff), lambda i, be, nu, ne: (be[i], 0, 0)),
                      any_spec,
                      pl.BlockSpec((1, 1, dm.d), lambda i, be, nu, ne: (be[i], 0, 0))],
            out_specs=pl.BlockSpec((rows * SUBLANES, LANES), lambda i, be, nu, ne: (i, 0)),
            scratch_shapes=[pltpu.VMEM((dm.d, 2 * d_ff), F32), pltpu.VMEM((d_ff, dm.d), F32),
                            pltpu.VMEM((dm.d, 2 * d_ff), BF16), pltpu.VMEM((d_ff, dm.d), BF16),
                            pltpu.SemaphoreType.DMA((2,))]),
        out_shape=jax.ShapeDtypeStruct((n_rows * SUBLANES, LANES), F32),
        compiler_params=cparams,
        name="moe_experts",
    )(block_e, n_used, next_e, xs, w_gu, b_gu[:, None, :], w_down, b_down[:, None, :])

    ya, yb = pl.pallas_call(
        functools.partial(_combine_kernel, tile=tile, alpha=dm.alpha, n_tiles=n_tiles, n_tiles_a=n_tiles_a),
        grid=(n_tiles,),
        in_specs=[any_spec, any_spec, any_spec] + h1_specs + [
                  pl.BlockSpec((tile, TOP_K), lambda i: (i, 0)),
                  pl.BlockSpec((1, dm.d), lambda i: (0, 0)),
                  pl.BlockSpec((1, dm.d), lambda i: (0, 0))],
        out_specs=[pl.BlockSpec((tile, dm.d), lambda i: (jnp.minimum(i, n_tiles_a - 1), 0)),
                   pl.BlockSpec((tile, dm.d), lambda i: (jnp.maximum(i - n_tiles_a, 0), 0))],
        out_shape=[jax.ShapeDtypeStruct((m_a, dm.d), F32), jax.ShapeDtypeStruct((m - m_a, dm.d), F32)],
        scratch_shapes=index_scratch + [
            pltpu.VMEM((TOP_K, tile * SUBLANES, LANES), F32),
            pltpu.SemaphoreType.DMA((2,)), pltpu.SemaphoreType.DMA((2,)), pltpu.SemaphoreType.DMA((2,))],
        compiler_params=cparams,
        name="moe_combine",
    )(tile_meta, slot_t, res, h1_a, h1_b, gate_w.T, ln_g[None, :], ln_b[None, :])
    return ya, yb


def _pad_cols(a, n):
    return jnp.pad(a, ((0, 0), (0, n - a.shape[1])))


def _mixer_weights(w_in, w_gk2, b_gk, gla_norm_w, w_branch_gla, w_pool_grp, pool_scale, w_branch_pool, b_gates,
                   w_out, ln_g, ln_b, w_router, b_router, dm):
    o_lr = 2 * dm.key + dm.val
    o_tail = o_lr + GATE_RANK
    return MixW(
        w_in=jnp.concatenate([w_in[:, 0:o_lr], w_in[:, o_tail:], _pad_cols(w_in[:, o_lr:o_tail], LANES)],
                             axis=1).astype(BF16),
        w_gk2=jnp.pad(w_gk2, ((0, LANES - GATE_RANK), (0, 0))).astype(BF16),
        b_gk=b_gk[None, :],
        gla_norm=gla_norm_w[None, :],
        w_ba=w_branch_gla.astype(BF16),
        w_pg=w_pool_grp.astype(BF16),
        pool_scale=pool_scale[None, :],
        w_bp=w_branch_pool.astype(BF16),
        b_gates=b_gates[None, :],
        w_out=w_out.astype(BF16),
        ln_g=ln_g[None, :],
        ln_b=ln_b[None, :],
        w_rt=_pad_cols(w_router, LANES).astype(BF16),
        b_rt=_pad_cols(b_router[None, :], LANES),
    )


def kernel(x_prompt, x_sample, state_gla, state_pool, w_in, w_gk2, b_gk, gla_norm_w, w_branch_gla, w_pool_grp,
           pool_scale, w_branch_pool, b_gates, w_out, ln1_g, ln1_b, w_router, b_router, w_gu, b_gu, w_down, b_down,
           ln2_g, ln2_b):
    depth = w_in.shape[0]
    bp, lp, d = x_prompt.shape
    bs, ls, _ = x_sample.shape
    assert d == SUBLANES * LANES and lp % PROMPT_TILE == 0 and (bs * ls) % MIX_TILE == 0 and MIX_TILE % ls == 0
    assert bs % GLA_SEQS == 0 and ls % SUBLANES == 0
    assert (bp * lp) % MOE_TILE == 0 and (bs * ls) % MOE_TILE == 0
    dm = _dims(d, depth)
    yp, ys = x_prompt, x_sample
    gla_p, pool_p, gla_s, pool_s = [], [], [], []
    for l in range(depth):
        w = _mixer_weights(w_in[l], w_gk2[l], b_gk[l], gla_norm_w[l], w_branch_gla[l], w_pool_grp[l], pool_scale[l],
                           w_branch_pool[l], b_gates[l], w_out[l], ln1_g[l], ln1_b[l], w_router[l], b_router[l], dm)
        h1p, lgp, sp, bufp = _prompt_mixer(yp, w, dm)
        h1s, lgs, ss, bufs = _sample_mixer(ys, state_gla[l], state_pool[l], w, dm)
        yp, ys = _moe(h1p, h1s, lgp, lgs, w_gu[l], b_gu[l], w_down[l], b_down[l], ln2_g[l], ln2_b[l], dm)
        yp = yp.reshape(bp, lp, d)
        ys = ys.reshape(bs, ls, d)
        gla_p.append(sp.astype(state_gla.dtype))
        pool_p.append(bufp.astype(state_pool.dtype))
        gla_s.append(ss.astype(state_gla.dtype))
        pool_s.append(bufs.astype(state_pool.dtype))
    stack = lambda xs: xs[0][None] if len(xs) == 1 else jnp.stack(xs, 0)
    return (yp, ys, stack(gla_p), stack(pool_p), stack(gla_s), stack(pool_s))
```

```python
import functools
from typing import NamedTuple

import jax
import jax.numpy as jnp
from jax import lax
from jax.experimental import pallas as pl
from jax.experimental.pallas import tpu as pltpu

F32 = jnp.float32
BF16 = jnp.bfloat16

GLA_HEADS = 4
GATE_RANK = 16
GATE_NORMALIZER = 16.0
GLA_CHUNK = 64
RMS_EPS = 1e-6
POOL_WINDOWS = (2, 4, 8, 16)
POOL_WMAX = 16
POOL_BUF = POOL_WMAX - 1
N_EXPERTS = 32
TOP_K = 4
SWIGLU_LIMIT = 7.0
GLU_ALPHA = 1.702
LN_EPS = 1e-5

LANES = 128
SUBLANES = 8
VMEM_LIMIT_BYTES = 56 * 1024 * 1024

PROMPT_TILE = 512
MIX_TILE = 256
GLA_GROUP = 128
GLA_SEQS = 8
GLA_SEQ_UNROLL = 2
EXPERT_ROWS = 512
MOE_TILE = 512
CUMSUM_BLOCK = 256
DMA_UNROLL = 8
REGROUP_ROWS = 256
ZERO_FILL_BITS = (EXPERT_ROWS - 1).bit_length()
RUN_BITS = MOE_TILE.bit_length()


class Dims(NamedTuple):
    d: int
    dk: int
    dv: int
    key: int
    val: int
    pw: int
    pgc: int
    alpha: float


def _dims(d, depth):
    return Dims(d=d, dk=d // 8, dv=d // 4, key=d // 2, val=d, pw=d // 2, pgc=d // 8, alpha=(2.0 * depth) ** 0.25)


class MixW(NamedTuple):
    w_in: object
    w_gk2: object
    b_gk: object
    gla_norm: object
    w_ba: object
    w_pg: object
    pool_scale: object
    w_bp: object
    b_gates: object
    w_out: object
    ln_g: object
    ln_b: object
    w_rt: object
    b_rt: object


def _mm(a, b):
    return jnp.dot(a, b, preferred_element_type=F32)


def _layer_norm(z, g, b):
    mu = jnp.mean(z, axis=-1, keepdims=True)
    zc = z - mu
    var = jnp.mean(zc * zc, axis=-1, keepdims=True)
    return zc * lax.rsqrt(var + LN_EPS) * g + b


def _to_row_tiles(ref, val, rows):
    for c in range(SUBLANES):
        ref[pl.ds(c, rows, stride=SUBLANES), :] = val[:, c * LANES:(c + 1) * LANES]


def _from_row_tiles(ref, rows):
    return jnp.concatenate([ref[pl.ds(c, rows, stride=SUBLANES), :] for c in range(SUBLANES)], axis=1)


def _project_qkv(xb, w, dm, q_s, k_s, v_s, gl_s):
    q_s[...] = _mm(xb, w.w_in[:, 0:dm.key]) * (dm.dk ** -0.5)
    k_s[...] = _mm(xb, w.w_in[:, dm.key:2 * dm.key])
    v_s[...] = _mm(xb, w.w_in[:, 2 * dm.key:2 * dm.key + dm.val])
    off_lr = 2 * dm.key + 2 * dm.val + dm.pw + 2 * dm.d
    lr = _mm(xb, w.w_in[:, off_lr:off_lr + LANES])
    gk = _mm(lr.astype(BF16), w.w_gk2[...]) + w.b_gk[...]
    gl_s[...] = (jnp.minimum(gk, 0.0) - jnp.log1p(jnp.exp(-jnp.abs(gk)))) / GATE_NORMALIZER


def _chunk_cumsum(gl_s, b_s, rows, chunk):
    blk = min(rows, CUMSUM_BLOCK)
    r = lax.broadcasted_iota(jnp.int32, (blk, blk), 0)
    c = lax.broadcasted_iota(jnp.int32, (blk, blk), 1)
    tri = ((r // chunk == c // chunk) & (c <= r)).astype(BF16)
    for b0 in range(0, rows, blk):
        gl = gl_s[b0:b0 + blk, :]
        hi = gl.astype(BF16)
        lo = (gl - hi.astype(F32)).astype(BF16)
        b_s[b0:b0 + blk, :] = _mm(tri, hi) + _mm(tri, lo)


NT_DIMS = (((1,), (1,)), ((), ()))
TN_DIMS = (((0,), (0,)), ((), ()))


def _gla_tile(q_s, k_s, v_s, b_s, o_s, st_t, *, dm, rows, chunk, group):
    n_chunks = rows // chunk
    gr = lax.broadcasted_iota(jnp.int32, (group, group), 0)
    gc = lax.broadcasted_iota(jnp.int32, (group, group), 1)
    mask = (gr // chunk == gc // chunk) & (gc <= gr)

    def per_chunk_row(b, row):
        return jnp.concatenate([jnp.broadcast_to(b[c * chunk + row:c * chunk + row + 1, :], (chunk, b.shape[1]))
                                for c in range(n_chunks)], axis=0)

    for h in range(GLA_HEADS):
        ks = slice(h * dm.dk, (h + 1) * dm.dk)
        vs = slice(h * dm.dv, (h + 1) * dm.dv)
        b = b_s[:, ks]
        b_ref = per_chunk_row(b, chunk // 2)
        b_last = per_chunk_row(b, chunk - 1)
        q = q_s[:, ks]
        k = k_s[:, ks]
        vb = v_s[:, vs].astype(BF16)
        qa = (q * jnp.exp(b - b_ref)).astype(BF16)
        ka = (k * jnp.exp(b_ref - b)).astype(BF16)
        qe = (q * jnp.exp(b)).astype(BF16)
        kd = (k * jnp.exp(b_last - b)).astype(BF16)
        for g0 in range(0, rows, group):
            a = lax.dot_general(qa[g0:g0 + group], ka[g0:g0 + group], NT_DIMS, preferred_element_type=F32)
            a = jnp.where(mask, a, 0.0)
            o_s[g0:g0 + group, vs] = _mm(a.astype(BF16), vb[g0:g0 + group])
        s_t = st_t[h]
        for c in range(n_chunks):
            r0 = c * chunk
            o_s[r0:r0 + chunk, vs] += lax.dot_general(qe[r0:r0 + chunk], s_t.astype(BF16), NT_DIMS,
                                                      preferred_element_type=F32)
            decay = jnp.exp(b[r0 + chunk - 1:r0 + chunk, :])
            s_t = s_t * decay + lax.dot_general(vb[r0:r0 + chunk], kd[r0:r0 + chunk], TN_DIMS,
                                                preferred_element_type=F32)
        st_t[h] = s_t


def _gla_seqs(q_s, k_s, v_s, gl_s, b_s, o_s, s0_ref, st_ref, *, dm, nseq, chunk, unroll):
    _chunk_cumsum(gl_s, b_s, nseq * chunk, chunk)
    ri = lax.broadcasted_iota(jnp.int32, (chunk, chunk), 0)
    ci = lax.broadcasted_iota(jnp.int32, (chunk, chunk), 1)
    causal = ci <= ri
    nt, tn = NT_DIMS, TN_DIMS

    def one_seq(seq):
        r0 = pl.multiple_of(seq * chunk, chunk)
        for h in range(GLA_HEADS):
            ks = slice(h * dm.dk, (h + 1) * dm.dk)
            vs = slice(h * dm.dv, (h + 1) * dm.dv)
            bc = b_s[pl.ds(r0, chunk), ks]
            b_ref = bc[chunk // 2:chunk // 2 + 1, :]
            b_last = bc[chunk - 1:chunk, :]
            qc = q_s[pl.ds(r0, chunk), ks]
            kc = k_s[pl.ds(r0, chunk), ks]
            vc = v_s[pl.ds(r0, chunk), vs].astype(BF16)
            s_old = s0_ref[seq, h]
            a = lax.dot_general((qc * jnp.exp(bc - b_ref)).astype(BF16), (kc * jnp.exp(b_ref - bc)).astype(BF16),
                                nt, preferred_element_type=F32)
            a = jnp.where(causal, a, 0.0)
            o = _mm(a.astype(BF16), vc) + _mm((qc * jnp.exp(bc)).astype(BF16), s_old.astype(BF16))
            o_s[pl.ds(r0, chunk), vs] = o
            e_col = jnp.broadcast_to(jnp.exp(b_last), (dm.dk, dm.dk)).T
            decay = jnp.concatenate([e_col] * (dm.dv // dm.dk), axis=1)
            kv = lax.dot_general((kc * jnp.exp(b_last - bc)).astype(BF16), vc, tn, preferred_element_type=F32)
            st_ref[seq, h] = decay * s_old + kv

    def step(g, carry):
        for u in range(unroll):
            one_seq(g * unroll + u)
        return carry

    lax.fori_loop(0, nseq // unroll, step, 0)


def _mixer_tail(x, xb, o_s, ext_s, pos0, w, dm, *, nseq, seq_len):
    t = nseq * seq_len
    off_g = 2 * dm.key + dm.val
    off_u = off_g + dm.val
    off_ga = off_u + dm.pw
    off_gb = off_ga + dm.d

    g = _mm(xb, w.w_in[:, off_g:off_g + dm.val])
    parts = []
    for h in range(GLA_HEADS):
        vs = slice(h * dm.dv, (h + 1) * dm.dv)
        oh = o_s[:, vs]
        ms = jnp.mean(oh * oh, axis=-1, keepdims=True)
        on = oh * lax.rsqrt(ms + RMS_EPS) * w.gla_norm[...]
        gh = g[:, vs]
        parts.append((on * (gh * jax.nn.sigmoid(gh))).astype(BF16))
    branch_a = _mm(jnp.concatenate(parts, axis=1), w.w_ba[...])

    u = _mm(xb, w.w_in[:, off_u:off_u + dm.pw])
    ext_s[:, POOL_WMAX:POOL_WMAX + seq_len, :] = u.reshape(nseq, seq_len, dm.pw)
    p = lax.broadcasted_iota(jnp.int32, (nseq, seq_len, dm.pgc), 1)
    pooled = []
    for gi, win in enumerate(POOL_WINDOWS):
        cs = slice(gi * dm.pgc, (gi + 1) * dm.pgc)
        cur = ext_s[:, POOL_WMAX:POOL_WMAX + seq_len, cs]
        acc = cur
        for j in range(1, win):
            acc = acc + ext_s[:, POOL_WMAX - j:POOL_WMAX - j + seq_len, cs]
        cnt = jnp.minimum(win, p + (pos0 + 1)).astype(F32)
        pg = (acc / cnt - cur).reshape(t, dm.pgc)
        pg = _mm(pg.astype(BF16), w.w_pg[gi]) * w.pool_scale[:, cs]
        pooled.append(pg.astype(BF16))
    branch_b = _mm(jnp.concatenate(pooled, axis=1), w.w_bp[...])

    gate_a = jax.nn.sigmoid(_mm(xb, w.w_in[:, off_ga:off_ga + dm.d]) + w.b_gates[:, 0:dm.d])
    merged = gate_a * branch_a
    gate_b = jax.nn.sigmoid(_mm(xb, w.w_in[:, off_gb:off_gb + dm.d]) + w.b_gates[:, dm.d:2 * dm.d])
    merged = merged + gate_b * branch_b
    mix = _mm(merged.astype(BF16), w.w_out[...])
    h1 = _layer_norm(dm.alpha * x + mix, w.ln_g[...], w.ln_b[...])
    logits = _mm(h1.astype(BF16), w.w_rt[...]) + w.b_rt[...]
    return h1, logits.T[0:N_EXPERTS, :]


N_MIXW = len(MixW._fields)


def _prompt_mixer_kernel(*refs, dm, tile, n_tiles):
    x_ref = refs[0]
    w = MixW(*refs[1:1 + N_MIXW])
    h1_ref, lg_ref, st_ref, buf_ref, q_s, k_s, v_s, gl_s, b_s, o_s, ext_s, st_t = refs[1 + N_MIXW:]
    lt = pl.program_id(1)

    @pl.when(lt == 0)
    def _():
        st_t[...] = jnp.zeros(st_t.shape, F32)
        ext_s[:, 0:POOL_WMAX, :] = jnp.zeros((1, POOL_WMAX, dm.pw), F32)

    x = x_ref[...]
    xb = x.astype(BF16)
    _project_qkv(xb, w, dm, q_s, k_s, v_s, gl_s)
    _chunk_cumsum(gl_s, b_s, tile, GLA_CHUNK)
    _gla_tile(q_s, k_s, v_s, b_s, o_s, st_t, dm=dm, rows=tile, chunk=GLA_CHUNK, group=GLA_GROUP)

    @pl.when(lt == n_tiles - 1)
    def _():
        for h in range(GLA_HEADS):
            st_ref[0, h] = st_t[h].T

    h1, logits_t = _mixer_tail(x, xb, o_s, ext_s, lt * tile, w, dm, nseq=1, seq_len=tile)
    _to_row_tiles(h1_ref, h1, tile)
    lg_ref[...] = logits_t
    ext_s[:, 0:POOL_WMAX, :] = ext_s[:, tile:tile + POOL_WMAX, :]

    @pl.when(lt == n_tiles - 1)
    def _():
        buf_ref[...] = ext_s[:, 1:POOL_WMAX, :]


def _sample_proj_kernel(*refs, dm):
    x_ref = refs[0]
    w = MixW(*refs[1:1 + N_MIXW])
    q_ref, k_ref, v_ref, gl_ref = refs[1 + N_MIXW:]
    _project_qkv(x_ref[...].astype(BF16), w, dm, q_ref, k_ref, v_ref, gl_ref)


def _sample_gla_kernel(q_ref, k_ref, v_ref, gl_ref, s0_ref, o_ref, st_ref, b_s, *, dm, nseq, seq_len):
    _gla_seqs(q_ref, k_ref, v_ref, gl_ref, b_s, o_ref, s0_ref, st_ref, dm=dm, nseq=nseq, chunk=seq_len,
              unroll=GLA_SEQ_UNROLL)


def _sample_tail_kernel(*refs, dm, nseq, seq_len):
    x_ref, o_ref, hist_ref = refs[0:3]
    w = MixW(*refs[3:3 + N_MIXW])
    h1_ref, lg_ref, buf_ref, ext_s = refs[3 + N_MIXW:]
    ext_s[:, 0:1, :] = jnp.zeros((nseq, 1, dm.pw), F32)
    ext_s[:, 1:POOL_WMAX, :] = hist_ref[...]
    x = x_ref[...]
    h1, logits_t = _mixer_tail(x, x.astype(BF16), o_ref, ext_s, POOL_BUF, w, dm, nseq=nseq, seq_len=seq_len)
    _to_row_tiles(h1_ref, h1, nseq * seq_len)
    lg_ref[...] = logits_t
    buf_ref[...] = ext_s[:, seq_len + 1:seq_len + POOL_WMAX, :]


def _const_spec(arr):
    nd = arr.ndim
    return pl.BlockSpec(arr.shape, lambda *_: (0,) * nd, pipeline_mode=pl.Buffered(1))


def _prompt_mixer(x, w, dm):
    bsz, seq, d = x.shape
    tile = PROMPT_TILE
    n_tiles = seq // tile
    m = bsz * seq
    scratch = [pltpu.VMEM((tile, dm.key), F32), pltpu.VMEM((tile, dm.key), F32), pltpu.VMEM((tile, dm.val), F32),
               pltpu.VMEM((tile, dm.key), F32), pltpu.VMEM((tile, dm.key), F32), pltpu.VMEM((tile, dm.val), F32),
               pltpu.VMEM((1, POOL_WMAX + tile, dm.pw), F32),
               pltpu.VMEM((GLA_HEADS, dm.dv, dm.dk), F32)]
    return pl.pallas_call(
        functools.partial(_prompt_mixer_kernel, dm=dm, tile=tile, n_tiles=n_tiles),
        grid=(bsz, n_tiles),
        in_specs=[pl.BlockSpec((tile, d), lambda b, t: (b * n_tiles + t, 0))] + [_const_spec(a) for a in w],
        out_specs=[pl.BlockSpec((tile * SUBLANES, LANES), lambda b, t: (b * n_tiles + t, 0)),
                   pl.BlockSpec((N_EXPERTS, tile), lambda b, t: (0, b * n_tiles + t)),
                   pl.BlockSpec((1, GLA_HEADS, dm.dk, dm.dv), lambda b, t: (b, 0, 0, 0)),
                   pl.BlockSpec((1, POOL_BUF, dm.pw), lambda b, t: (b, 0, 0))],
        out_shape=[jax.ShapeDtypeStruct((m * SUBLANES, LANES), F32),
                   jax.ShapeDtypeStruct((N_EXPERTS, m), F32),
                   jax.ShapeDtypeStruct((bsz, GLA_HEADS, dm.dk, dm.dv), F32),
                   jax.ShapeDtypeStruct((bsz, POOL_BUF, dm.pw), F32)],
        scratch_shapes=scratch,
        compiler_params=pltpu.CompilerParams(dimension_semantics=("arbitrary", "arbitrary"),
                                             vmem_limit_bytes=VMEM_LIMIT_BYTES),
        name="prompt_mixer",
    )(x.reshape(m, d), *w)


def _sample_mixer(x, s0, hist, w, dm):
    bsz, seq, d = x.shape
    m = bsz * seq
    x2 = x.reshape(m, d)
    tile = MIX_TILE
    cparams = pltpu.CompilerParams(dimension_semantics=("arbitrary",), vmem_limit_bytes=VMEM_LIMIT_BYTES)
    row = lambda n: pl.BlockSpec((tile, n), lambda i: (i, 0))
    q, k, v, gl = pl.pallas_call(
        functools.partial(_sample_proj_kernel, dm=dm),
        grid=(m // tile,),
        in_specs=[row(d)] + [_const_spec(a) for a in w],
        out_specs=[row(dm.key), row(dm.key), row(dm.val), row(dm.key)],
        out_shape=[jax.ShapeDtypeStruct((m, n), F32) for n in (dm.key, dm.key, dm.val, dm.key)],
        compiler_params=cparams,
        name="sample_proj",
    )(x2, *w)

    nseq = GLA_SEQS
    rows = nseq * seq
    grow = lambda n: pl.BlockSpec((rows, n), lambda i: (i, 0))
    st_spec = pl.BlockSpec((nseq, GLA_HEADS, dm.dk, dm.dv), lambda i: (i, 0, 0, 0))
    o, st = pl.pallas_call(
        functools.partial(_sample_gla_kernel, dm=dm, nseq=nseq, seq_len=seq),
        grid=(bsz // nseq,),
        in_specs=[grow(dm.key), grow(dm.key), grow(dm.val), grow(dm.key), st_spec],
        out_specs=[grow(dm.val), st_spec],
        out_shape=[jax.ShapeDtypeStruct((m, dm.val), F32), jax.ShapeDtypeStruct(s0.shape, F32)],
        scratch_shapes=[pltpu.VMEM((rows, dm.key), F32)],
        compiler_params=cparams,
        name="sample_gla",
    )(q, k, v, gl, s0)

    tseq = tile // seq
    hist_spec = pl.BlockSpec((tseq, POOL_BUF, dm.pw), lambda i: (i, 0, 0))
    h1, lg, buf = pl.pallas_call(
        functools.partial(_sample_tail_kernel, dm=dm, nseq=tseq, seq_len=seq),
        grid=(m // tile,),
        in_specs=[row(d), row(dm.val), hist_spec] + [_const_spec(a) for a in w],
        out_specs=[pl.BlockSpec((tile * SUBLANES, LANES), lambda i: (i, 0)),
                   pl.BlockSpec((N_EXPERTS, tile), lambda i: (0, i)), hist_spec],
        out_shape=[jax.ShapeDtypeStruct((m * SUBLANES, LANES), F32),
                   jax.ShapeDtypeStruct((N_EXPERTS, m), F32),
                   jax.ShapeDtypeStruct((bsz, POOL_BUF, dm.pw), F32)],
        scratch_shapes=[pltpu.VMEM((tseq, POOL_WMAX + seq, dm.pw), F32)],
        compiler_params=cparams,
        name="sample_tail",
    )(x2, o, hist, *w)
    return h1, lg, st, buf


def _router_kernel(lga_ref, lgb_ref, w_ref, s_ref, meta_ref, c_ref, carry, *, tile, n_tiles_a):
    i = pl.program_id(0)

    @pl.when(i == 0)
    def _():
        carry[...] = jnp.zeros(carry.shape, F32)

    eio = lax.broadcasted_iota(jnp.int32, (N_EXPERTS, tile), 0)
    work = jnp.where(i < n_tiles_a, lga_ref[...], lgb_ref[...])
    vals, hots = [], []
    for k in range(TOP_K):
        mx = jnp.max(work, axis=0, keepdims=True)
        idx = jnp.min(jnp.where(work == mx, eio, N_EXPERTS), axis=0, keepdims=True)
        hot = eio == idx
        vals.append(mx)
        hots.append(hot)
        work = jnp.where(hot, -jnp.inf, work)
    ex = [jnp.exp(v - vals[0]) for v in vals]
    den = ex[0]
    for k in range(1, TOP_K):
        den = den + ex[k]
    for k in range(TOP_K):
        w_ref[k:k + 1, :] = ex[k] / den
    sel = hots[0]
    for k in range(1, TOP_K):
        sel = sel | hots[k]
    r = lax.broadcasted_iota(jnp.int32, (tile, tile), 0)
    c = lax.broadcasted_iota(jnp.int32, (tile, tile), 1)
    earlier_token = (r < c).astype(BF16)
    rank_in_tile = _mm(sel.astype(BF16), earlier_token)
    cnt = jnp.broadcast_to(jnp.sum(sel.astype(F32), axis=1, keepdims=True), (N_EXPERTS, LANES))
    er = lax.broadcasted_iota(jnp.int32, (N_EXPERTS, N_EXPERTS), 0)
    ec = lax.broadcasted_iota(jnp.int32, (N_EXPERTS, N_EXPERTS), 1)
    lower = (ec < er).astype(BF16)
    cnt_hi = jnp.floor(cnt * (1.0 / 256.0))
    first_slot = _mm(lower, (cnt - 256.0 * cnt_hi).astype(BF16)) + 256.0 * _mm(lower, cnt_hi.astype(BF16))
    slot = rank_in_tile + first_slot[:, 0:1]
    for k in range(TOP_K):
        s_ref[0, k:k + 1, :] = jnp.sum(jnp.where(hots[k], slot, 0.0), axis=0, keepdims=True).astype(jnp.int32)
    meta_ref[0, 0:N_EXPERTS, :] = cnt
    meta_ref[0, N_EXPERTS:2 * N_EXPERTS, :] = first_slot
    meta_ref[0, 2 * N_EXPERTS:3 * N_EXPERTS, :] = carry[...]
    carry[...] = carry[...] + cnt
    c_ref[...] = carry[...]


def _for_each_run(meta, fn):
    def body(e, carry):
        cnt = meta[e]
        first = meta[N_EXPERTS + e]
        dst = meta[2 * N_EXPERTS + e]
        for bit in range(RUN_BITS):
            @pl.when(((cnt >> bit) & 1) == 1)
            def _(bit=bit):
                done = cnt & ((1 << bit) - 1)
                fn(first + done, dst + done, 1 << bit)
        return carry
    lax.fori_loop(0, N_EXPERTS, body, 0)


def _rows(ref, first, n):
    return ref.at[pl.ds(pl.multiple_of(first * SUBLANES, SUBLANES), n * SUBLANES), :]


def _dispatch_kernel(fill_start_ref, fill_cnt_ref, tail_start_ref, meta_hbm, slot_hbm, h1a_ref, h1b_ref, xs_hbm,
                     meta_s0, meta_s1, slot_s0, slot_s1, stage, zeros_s, sem_m, sem_s, sem_r, sem_z,
                     *, tile, n_tiles, n_tiles_a):
    i = pl.program_id(0)
    sl = i % 2
    meta_s = (meta_s0, meta_s1)
    slot_s = (slot_s0, slot_s1)

    def index_copies(blk, s):
        return (pltpu.make_async_copy(meta_hbm.at[blk], meta_s[s], sem_m.at[s]),
                pltpu.make_async_copy(slot_hbm.at[blk], slot_s[s], sem_s.at[s]))

    def wait_stage(s):
        pltpu.make_async_copy(stage.at[s], stage.at[s], sem_r.at[s]).wait()

    @pl.when(i == 0)
    def _():
        for cp in index_copies(0, 0):
            cp.start()
        zeros_s[...] = jnp.zeros(zeros_s.shape, F32)

        def fill_copy(off, bit):
            n = 1 << bit
            return pltpu.make_async_copy(zeros_s.at[pl.ds(0, n * SUBLANES), :], _rows(xs_hbm, off, n), sem_z)

        def for_each_piece(fn):
            def body(e, carry):
                cnt = fill_cnt_ref[e]
                for bit in range(ZERO_FILL_BITS):
                    @pl.when(((cnt >> bit) & 1) == 1)
                    def _(bit=bit):
                        fn(fill_copy(fill_start_ref[e] + (cnt & ((1 << bit) - 1)), bit))
                return carry
            lax.fori_loop(0, N_EXPERTS, body, 0)

        top = ZERO_FILL_BITS - 1
        n_tail = (xs_hbm.shape[0] // SUBLANES - tail_start_ref[0]) >> top

        def for_each_tail_piece(fn):
            def body(j, carry):
                fn(fill_copy(tail_start_ref[0] + (j << top), top))
                return carry
            lax.fori_loop(0, n_tail, body, 0)

        for_each_piece(lambda cp: cp.start())
        for_each_tail_piece(lambda cp: cp.start())
        for_each_piece(lambda cp: cp.wait())
        for_each_tail_piece(lambda cp: cp.wait())

    def group_rows(h1_ref, s_):
        def group(g, carry):
            for u in range(DMA_UNROLL):
                t = g * DMA_UNROLL + u
                row = h1_ref[pl.ds(pl.multiple_of(t * SUBLANES, SUBLANES), SUBLANES), :]
                for k in range(TOP_K):
                    s = slot_s[s_][k * tile + t]
                    stage[s_, pl.ds(pl.multiple_of(s * SUBLANES, SUBLANES), SUBLANES), :] = row
            return carry
        lax.fori_loop(0, tile // DMA_UNROLL, group, 0)

    for s_ in range(2):
        @pl.when((sl == 1 - s_) & (i + 1 < n_tiles))
        def _(s_=s_):
            for cp in index_copies(i + 1, s_):
                cp.start()

    for s_ in range(2):
        @pl.when(sl == s_)
        def _(s_=s_):
            for cp in index_copies(i, s_):
                cp.wait()

            @pl.when(i >= 2)
            def _():
                wait_stage(s_)

            @pl.when(i < n_tiles_a)
            def _():
                group_rows(h1a_ref, s_)

            @pl.when(i >= n_tiles_a)
            def _():
                group_rows(h1b_ref, s_)

            _for_each_run(meta_s[s_], lambda s0, d0, n: pltpu.make_async_copy(
                _rows(stage.at[s_], s0, n), _rows(xs_hbm, d0, n), sem_r.at[s_]).start())

    @pl.when(i == n_tiles - 1)
    def _():
        @pl.when(n_tiles > 1)
        def _():
            wait_stage(1 - sl)
        wait_stage(sl)


def _expert_kernel(be_ref, nused_ref, next_e_ref, xs_ref, wgu_hbm, bgu_ref, wdn_hbm, bdn_ref, out_ref,
                   wgu_f, wdn_f, wgu_b, wdn_b, sem_w, *, rows, d_ff):
    i = pl.program_id(0)

    def weight_copies(e):
        return (pltpu.make_async_copy(wgu_hbm.at[e], wgu_f, sem_w.at[0]),
                pltpu.make_async_copy(wdn_hbm.at[e], wdn_f, sem_w.at[1]))

    @pl.when(i == 0)
    def _():
        for cp in weight_copies(be_ref[0]):
            cp.start()

    @pl.when((i < nused_ref[0]) & ((i == 0) | (be_ref[i] != be_ref[jnp.maximum(i - 1, 0)])))
    def _():
        for cp in weight_copies(be_ref[i]):
            cp.wait()
        wgu_b[...] = wgu_f[...].astype(BF16)
        wdn_b[...] = wdn_f[...].astype(BF16)

        @pl.when(next_e_ref[i] >= 0)
        def _():
            for cp in weight_copies(next_e_ref[i]):
                cp.start()

    @pl.when(i < nused_ref[0])
    def _():
        xb = _from_row_tiles(xs_ref, rows).astype(BF16)
        hh = _mm(xb, wgu_b[...]) + bgu_ref[0]
        gate = jnp.minimum(hh[:, 0:d_ff], SWIGLU_LIMIT)
        up = jnp.clip(hh[:, d_ff:2 * d_ff], -SWIGLU_LIMIT, SWIGLU_LIMIT)
        act = (up + 1.0) * (gate * jax.nn.sigmoid(GLU_ALPHA * gate))
        res = _mm(act.astype(BF16), wdn_b[...]) + bdn_ref[0]
        _to_row_tiles(out_ref, res, rows)

    @pl.when(i >= nused_ref[0])
    def _():
        out_ref[...] = jnp.zeros(out_ref.shape, F32)


def _combine_kernel(meta_hbm, slot_hbm, res_hbm, h1a_ref, h1b_ref, gw_ref, g_ref, b_ref, ya_ref, yb_ref,
                    meta_s0, meta_s1, slot_s0, slot_s1, stage, gbuf, sem_m, sem_s, sem_g,
                    *, tile, alpha, n_tiles, n_tiles_a):
    i = pl.program_id(0)
    sl = i % 2
    meta_s = (meta_s0, meta_s1)
    slot_s = (slot_s0, slot_s1)

    def index_copies(blk, s):
        return (pltpu.make_async_copy(meta_hbm.at[blk], meta_s[s], sem_m.at[s]),
                pltpu.make_async_copy(slot_hbm.at[blk], slot_s[s], sem_s.at[s]))

    def fetch_runs(s):
        _for_each_run(meta_s[s], lambda s0, d0, n: pltpu.make_async_copy(
            _rows(res_hbm, d0, n), _rows(stage.at[s], s0, n), sem_g.at[s]).start())

    @pl.when(i == 0)
    def _():
        for cp in index_copies(0, 0):
            cp.start()
        for cp in index_copies(0, 0):
            cp.wait()
        fetch_runs(0)

        @pl.when(n_tiles > 1)
        def _():
            for cp in index_copies(1, 1):
                cp.start()

    def regroup(s_):
        def group(g, carry):
            for u in range(DMA_UNROLL):
                t = g * DMA_UNROLL + u
                for k in range(TOP_K):
                    s = slot_s[s_][k * tile + t]
                    gbuf[k, pl.ds(pl.multiple_of(t * SUBLANES, SUBLANES), SUBLANES), :] = (
                        stage[s_, pl.ds(pl.multiple_of(s * SUBLANES, SUBLANES), SUBLANES), :])
            return carry
        lax.fori_loop(0, tile // DMA_UNROLL, group, 0)

    for s_ in range(2):
        @pl.when((sl == 1 - s_) & (i + 1 < n_tiles))
        def _(s_=s_):
            for cp in index_copies(i + 1, s_):
                cp.wait()
            fetch_runs(s_)

    for s_ in range(2):
        @pl.when(sl == s_)
        def _(s_=s_):
            pltpu.make_async_copy(stage.at[s_], stage.at[s_], sem_g.at[s_]).wait()
            regroup(s_)

    for s_ in range(2):
        @pl.when((sl == s_) & (i + 2 < n_tiles))
        def _(s_=s_):
            for cp in index_copies(i + 2, s_):
                cp.start()

    gw = gw_ref[...]
    z = alpha * jnp.where(i < n_tiles_a, _from_row_tiles(h1a_ref, tile), _from_row_tiles(h1b_ref, tile))
    for k in range(TOP_K):
        z = z + gw[:, k:k + 1] * _from_row_tiles(gbuf.at[k], tile)
    y = _layer_norm(z, g_ref[...], b_ref[...])

    @pl.when(i < n_tiles_a)
    def _():
        ya_ref[...] = y

    @pl.when(i >= n_tiles_a)
    def _():
        yb_ref[...] = y


def _moe(h1_a, h1_b, logits_a, logits_b, w_gu, b_gu, w_down, b_down, ln_g, ln_b, dm):
    m_a = logits_a.shape[1]
    m = m_a + logits_b.shape[1]
    rows = EXPERT_ROWS
    d_ff = w_down.shape[1]
    tile = MOE_TILE
    n_tiles = m // tile
    n_tiles_a = m_a // tile
    cparams = pltpu.CompilerParams(dimension_semantics=("arbitrary",), vmem_limit_bytes=VMEM_LIMIT_BYTES)

    def two_group_specs(block, lane_axis):
        pick = (lambda j: (0, j)) if lane_axis else (lambda j: (j, 0))
        return [pl.BlockSpec(block, lambda i, *_: pick(jnp.minimum(i, n_tiles_a - 1))),
                pl.BlockSpec(block, lambda i, *_: pick(jnp.maximum(i - n_tiles_a, 0)))]

    kblk = lambda: pl.BlockSpec((TOP_K, tile), lambda i: (0, i))
    gate_w, slot, meta, counts = pl.pallas_call(
        functools.partial(_router_kernel, tile=tile, n_tiles_a=n_tiles_a),
        grid=(n_tiles,),
        in_specs=two_group_specs((N_EXPERTS, tile), True),
        out_specs=[kblk(), pl.BlockSpec((1, TOP_K, tile), lambda i: (i, 0, 0)),
                   pl.BlockSpec((1, 3 * N_EXPERTS, LANES), lambda i: (i, 0, 0)),
                   pl.BlockSpec((N_EXPERTS, LANES), lambda i: (0, 0))],
        out_shape=[jax.ShapeDtypeStruct((TOP_K, m), F32), jax.ShapeDtypeStruct((n_tiles, TOP_K, tile), jnp.int32),
                   jax.ShapeDtypeStruct((n_tiles, 3 * N_EXPERTS, LANES), F32),
                   jax.ShapeDtypeStruct((N_EXPERTS, LANES), F32)],
        scratch_shapes=[pltpu.VMEM((N_EXPERTS, LANES), F32)],
        compiler_params=cparams,
        name="moe_router",
    )(logits_a, logits_b)

    counts = counts[:, 0].astype(jnp.int32)
    padded = (counts + rows - 1) // rows * rows
    pad_end = jnp.cumsum(padded)
    pad_start = pad_end - padded
    n_blocks = -(-(m * TOP_K + N_EXPERTS * (rows - 1)) // rows)
    n_rows = n_blocks * rows
    block_row0 = jnp.arange(n_blocks, dtype=jnp.int32) * rows
    block_e = jnp.minimum(jnp.sum((pad_end[None, :] <= block_row0[:, None]).astype(jnp.int32), axis=1), N_EXPERTS - 1)
    n_used = (pad_end[-1] // rows).astype(jnp.int32).reshape(1)
    meta = meta[:, :, 0].astype(jnp.int32)
    tile_meta = jnp.concatenate([meta[:, 0:2 * N_EXPERTS], meta[:, 2 * N_EXPERTS:] + pad_start[None, :],
                                 jnp.zeros((n_tiles, LANES - 3 * N_EXPERTS), jnp.int32)], axis=1)
    slot_t = slot.reshape(n_tiles, TOP_K * tile)

    h1_specs = two_group_specs((tile * SUBLANES, LANES), False)
    any_spec = pl.BlockSpec(memory_space=pl.ANY)
    index_scratch = [pltpu.SMEM((LANES,), jnp.int32), pltpu.SMEM((LANES,), jnp.int32),
                     pltpu.SMEM((TOP_K * tile,), jnp.int32), pltpu.SMEM((TOP_K * tile,), jnp.int32),
                     pltpu.VMEM((2, TOP_K * tile * SUBLANES, LANES), F32)]
    xs = pl.pallas_call(
        functools.partial(_dispatch_kernel, tile=tile, n_tiles=n_tiles, n_tiles_a=n_tiles_a),
        grid_spec=pltpu.PrefetchScalarGridSpec(
            num_scalar_prefetch=3,
            grid=(n_tiles,),
            in_specs=[any_spec, any_spec] + h1_specs,
            out_specs=any_spec,
            scratch_shapes=index_scratch + [
                pltpu.VMEM(((1 << (ZERO_FILL_BITS - 1)) * SUBLANES, LANES), F32),
                pltpu.SemaphoreType.DMA((2,)), pltpu.SemaphoreType.DMA((2,)), pltpu.SemaphoreType.DMA((2,)),
                pltpu.SemaphoreType.DMA]),
        out_shape=jax.ShapeDtypeStruct((n_rows * SUBLANES, LANES), F32),
        compiler_params=cparams,
        name="moe_dispatch",
    )(pad_start + counts, padded - counts, pad_end[-1:], tile_meta, slot_t, h1_a, h1_b)

    blk = jnp.arange(n_blocks, dtype=jnp.int32)
    run_start = (blk < n_used[0]) & ((blk == 0) | (block_e != jnp.roll(block_e, 1)))
    next_start = jnp.min(jnp.where((blk[None, :] > blk[:, None]) & run_start[None, :], blk[None, :], n_blocks), axis=1)
    next_e = jnp.where(next_start < n_blocks, block_e[jnp.minimum(next_start, n_blocks - 1)], -1).astype(jnp.int32)

    last = lambda i, nu: jnp.minimum(i, nu[0] - 1)
    res = pl.pallas_call(
        functools.partial(_expert_kernel, rows=rows, d_ff=d_ff),
        grid_spec=pltpu.PrefetchScalarGridSpec(
            num_scalar_prefetch=3,
            grid=(n_blocks,),
            in_specs=[pl.BlockSpec((rows * SUBLANES, LANES), lambda i, be, nu, ne: (last(i, nu), 0)),
                      any_spec,
                      pl.BlockSpec((1, 1, 2 * d_ff), lambda i, be, nu, ne: (be[i], 0, 0)),
                      any_spec,
                      pl.BlockSpec((1, 1, dm.d), lambda i, be, nu, ne: (be[i], 0, 0))],
            out_specs=pl.BlockSpec((rows * SUBLANES, LANES), lambda i, be, nu, ne: (i, 0)),
            scratch_shapes=[pltpu.VMEM((dm.d, 2 * d_ff), F32), pltpu.VMEM((d_ff, dm.d), F32),
                            pltpu.VMEM((dm.d, 2 * d_ff), BF16), pltpu.VMEM((d_ff, dm.d), BF16),
                            pltpu.SemaphoreType.DMA((2,))]),
        out_shape=jax.ShapeDtypeStruct((n_rows * SUBLANES, LANES), F32),
        compiler_params=cparams,
        name="moe_experts",
    )(block_e, n_used, next_e, xs, w_gu, b_gu[:, None, :], w_down, b_down[:, None, :])

    ya, yb = pl.pallas_call(
        functools.partial(_combine_kernel, tile=tile, alpha=dm.alpha, n_tiles=n_tiles, n_tiles_a=n_tiles_a),
        grid=(n_tiles,),
        in_specs=[any_spec, any_spec, any_spec] + h1_specs + [
                  pl.BlockSpec((tile, TOP_K), lambda i: (i, 0)),
                  pl.BlockSpec((1, dm.d), lambda i: (0, 0)),
                  pl.BlockSpec((1, dm.d), lambda i: (0, 0))],
        out_specs=[pl.BlockSpec((tile, dm.d), lambda i: (jnp.minimum(i, n_tiles_a - 1), 0)),
                   pl.BlockSpec((tile, dm.d), lambda i: (jnp.maximum(i - n_tiles_a, 0), 0))],
        out_shape=[jax.ShapeDtypeStruct((m_a, dm.d), F32), jax.ShapeDtypeStruct((m - m_a, dm.d), F32)],
        scratch_shapes=index_scratch + [
            pltpu.VMEM((TOP_K, tile * SUBLANES, LANES), F32),
            pltpu.SemaphoreType.DMA((2,)), pltpu.SemaphoreType.DMA((2,)), pltpu.SemaphoreType.DMA((2,))],
        compiler_params=cparams,
        name="moe_combine",
    )(tile_meta, slot_t, res, h1_a, h1_b, gate_w.T, ln_g[None, :], ln_b[None, :])
    return ya, yb


def _pad_cols(a, n):
    return jnp.pad(a, ((0, 0), (0, n - a.shape[1])))


def _regroup_kernel(w_ref, o_ref, *, o_lr, o_tail):
    n_in = w_ref.shape[1]
    n_main = n_in - (o_tail - o_lr)
    o_ref[:, 0:o_lr] = w_ref[:, 0:o_lr].astype(BF16)
    o_ref[:, o_lr:n_main] = w_ref[:, o_tail:n_in].astype(BF16)
    lr = w_ref[:, o_lr:o_tail].astype(BF16)
    o_ref[:, n_main:n_main + LANES] = jnp.concatenate(
        [lr, jnp.zeros((lr.shape[0], LANES - (o_tail - o_lr)), BF16)], axis=1)


def _regroup_w_in(w_in, o_lr, o_tail):
    d, n_in = w_in.shape
    n_out = n_in - (o_tail - o_lr) + LANES
    rows = REGROUP_ROWS
    return pl.pallas_call(
        functools.partial(_regroup_kernel, o_lr=o_lr, o_tail=o_tail),
        grid=(d // rows,),
        in_specs=[pl.BlockSpec((rows, n_in), lambda i: (i, 0))],
        out_specs=pl.BlockSpec((rows, n_out), lambda i: (i, 0)),
        out_shape=jax.ShapeDtypeStruct((d, n_out), BF16),
        compiler_params=pltpu.CompilerParams(dimension_semantics=("arbitrary",), vmem_limit_bytes=VMEM_LIMIT_BYTES),
        name="regroup_w_in",
    )(w_in)


def _mixer_weights(w_in, w_gk2, b_gk, gla_norm_w, w_branch_gla, w_pool_grp, pool_scale, w_branch_pool, b_gates,
                   w_out, ln_g, ln_b, w_router, b_router, dm):
    o_lr = 2 * dm.key + dm.val
    o_tail = o_lr + GATE_RANK
    return MixW(
        w_in=_regroup_w_in(w_in, o_lr, o_tail),
        w_gk2=jnp.pad(w_gk2, ((0, LANES - GATE_RANK), (0, 0))).astype(BF16),
        b_gk=b_gk[None, :],
        gla_norm=gla_norm_w[None, :],
        w_ba=w_branch_gla.astype(BF16),
        w_pg=w_pool_grp.astype(BF16),
        pool_scale=pool_scale[None, :],
        w_bp=w_branch_pool.astype(BF16),
        b_gates=b_gates[None, :],
        w_out=w_out.astype(BF16),
        ln_g=ln_g[None, :],
        ln_b=ln_b[None, :],
        w_rt=_pad_cols(w_router, LANES).astype(BF16),
        b_rt=_pad_cols(b_router[None, :], LANES),
    )


def kernel(x_prompt, x_sample, state_gla, state_pool, w_in, w_gk2, b_gk, gla_norm_w, w_branch_gla, w_pool_grp,
           pool_scale, w_branch_pool, b_gates, w_out, ln1_g, ln1_b, w_router, b_router, w_gu, b_gu, w_down, b_down,
           ln2_g, ln2_b):
    depth = w_in.shape[0]
    bp, lp, d = x_prompt.shape
    bs, ls, _ = x_sample.shape
    assert d == SUBLANES * LANES and lp % PROMPT_TILE == 0 and (bs * ls) % MIX_TILE == 0 and MIX_TILE % ls == 0
    assert bs % GLA_SEQS == 0 and ls % SUBLANES == 0
    assert (bp * lp) % MOE_TILE == 0 and (bs * ls) % MOE_TILE == 0
    dm = _dims(d, depth)
    yp, ys = x_prompt, x_sample
    gla_p, pool_p, gla_s, pool_s = [], [], [], []
    for l in range(depth):
        w = _mixer_weights(w_in[l], w_gk2[l], b_gk[l], gla_norm_w[l], w_branch_gla[l], w_pool_grp[l], pool_scale[l],
                           w_branch_pool[l], b_gates[l], w_out[l], ln1_g[l], ln1_b[l], w_router[l], b_router[l], dm)
        h1p, lgp, sp, bufp = _prompt_mixer(yp, w, dm)
        h1s, lgs, ss, bufs = _sample_mixer(ys, state_gla[l], state_pool[l], w, dm)
        yp, ys = _moe(h1p, h1s, lgp, lgs, w_gu[l], b_gu[l], w_down[l], b_down[l], ln2_g[l], ln2_b[l], dm)
        yp = yp.reshape(bp, lp, d)
        ys = ys.reshape(bs, ls, d)
        gla_p.append(sp.astype(state_gla.dtype))
        pool_p.append(bufp.astype(state_pool.dtype))
        gla_s.append(ss.astype(state_gla.dtype))
        pool_s.append(bufs.astype(state_pool.dtype))
    stack = lambda xs: xs[0][None] if len(xs) == 1 else jnp.stack(xs, 0)
    return (yp, ys, stack(gla_p), stack(pool_p), stack(gla_s), stack(pool_s))
```

```python
import functools
from typing import NamedTuple

import jax
import jax.numpy as jnp
from jax import lax
from jax.experimental import pallas as pl
from jax.experimental.pallas import tpu as pltpu

F32 = jnp.float32
BF16 = jnp.bfloat16

GLA_HEADS = 4
GATE_RANK = 16
GATE_NORMALIZER = 16.0
GLA_CHUNK = 64
RMS_EPS = 1e-6
POOL_WINDOWS = (2, 4, 8, 16)
POOL_WMAX = 16
POOL_BUF = POOL_WMAX - 1
N_EXPERTS = 32
TOP_K = 4
SWIGLU_LIMIT = 7.0
GLU_ALPHA = 1.702
LN_EPS = 1e-5

LANES = 128
SUBLANES = 8
VMEM_LIMIT_BYTES = 56 * 1024 * 1024

PROMPT_TILE = 512
MIX_TILE = 256
GLA_GROUP = 128
GLA_SEQS = 8
GLA_SEQ_UNROLL = 2
EXPERT_ROWS = 512
MOE_TILE = 512
CUMSUM_BLOCK = 256
DMA_UNROLL = 8
REGROUP_ROWS = 256
ZERO_FILL_BITS = (EXPERT_ROWS - 1).bit_length()
RUN_BITS = MOE_TILE.bit_length()


class Dims(NamedTuple):
    d: int
    dk: int
    dv: int
    key: int
    val: int
    pw: int
    pgc: int
    alpha: float


def _dims(d, depth):
    return Dims(d=d, dk=d // 8, dv=d // 4, key=d // 2, val=d, pw=d // 2, pgc=d // 8, alpha=(2.0 * depth) ** 0.25)


class MixW(NamedTuple):
    w_in: object
    w_gk2: object
    b_gk: object
    gla_norm: object
    w_ba: object
    w_pg: object
    pool_scale: object
    w_bp: object
    b_gates: object
    w_out: object
    ln_g: object
    ln_b: object
    w_rt: object
    b_rt: object


def _mm(a, b):
    return jnp.dot(a, b, preferred_element_type=F32)


def _layer_norm(z, g, b):
    mu = jnp.mean(z, axis=-1, keepdims=True)
    zc = z - mu
    var = jnp.mean(zc * zc, axis=-1, keepdims=True)
    return zc * lax.rsqrt(var + LN_EPS) * g + b


def _to_row_tiles(ref, val, rows):
    for c in range(SUBLANES):
        ref[pl.ds(c, rows, stride=SUBLANES), :] = val[:, c * LANES:(c + 1) * LANES]


def _from_row_tiles(ref, rows):
    return jnp.concatenate([ref[pl.ds(c, rows, stride=SUBLANES), :] for c in range(SUBLANES)], axis=1)


def _project_qkv(xb, w, dm, q_s, k_s, v_s, gl_s):
    q_s[...] = _mm(xb, w.w_in[:, 0:dm.key]) * (dm.dk ** -0.5)
    k_s[...] = _mm(xb, w.w_in[:, dm.key:2 * dm.key])
    v_s[...] = _mm(xb, w.w_in[:, 2 * dm.key:2 * dm.key + dm.val])
    off_lr = 2 * dm.key + 2 * dm.val + dm.pw + 2 * dm.d
    lr = _mm(xb, w.w_in[:, off_lr:off_lr + LANES])
    gk = _mm(lr.astype(BF16), w.w_gk2[...]) + w.b_gk[...]
    gl_s[...] = (jnp.minimum(gk, 0.0) - jnp.log1p(jnp.exp(-jnp.abs(gk)))) / GATE_NORMALIZER


def _chunk_cumsum(gl_s, b_s, rows, chunk):
    blk = min(rows, CUMSUM_BLOCK)
    r = lax.broadcasted_iota(jnp.int32, (blk, blk), 0)
    c = lax.broadcasted_iota(jnp.int32, (blk, blk), 1)
    tri = ((r // chunk == c // chunk) & (c <= r)).astype(BF16)
    for b0 in range(0, rows, blk):
        gl = gl_s[b0:b0 + blk, :]
        hi = gl.astype(BF16)
        lo = (gl - hi.astype(F32)).astype(BF16)
        b_s[b0:b0 + blk, :] = _mm(tri, hi) + _mm(tri, lo)


NT_DIMS = (((1,), (1,)), ((), ()))
TN_DIMS = (((0,), (0,)), ((), ()))


def _gla_tile(q_s, k_s, v_s, b_s, o_s, st_t, *, dm, rows, chunk, group):
    n_chunks = rows // chunk
    gr = lax.broadcasted_iota(jnp.int32, (group, group), 0)
    gc = lax.broadcasted_iota(jnp.int32, (group, group), 1)
    mask = (gr // chunk == gc // chunk) & (gc <= gr)

    def per_chunk_row(b, row):
        return jnp.concatenate([jnp.broadcast_to(b[c * chunk + row:c * chunk + row + 1, :], (chunk, b.shape[1]))
                                for c in range(n_chunks)], axis=0)

    for h in range(GLA_HEADS):
        ks = slice(h * dm.dk, (h + 1) * dm.dk)
        vs = slice(h * dm.dv, (h + 1) * dm.dv)
        b = b_s[:, ks]
        b_ref = per_chunk_row(b, chunk // 2)
        b_last = per_chunk_row(b, chunk - 1)
        q = q_s[:, ks]
        k = k_s[:, ks]
        vb = v_s[:, vs].astype(BF16)
        qa = (q * jnp.exp(b - b_ref)).astype(BF16)
        ka = (k * jnp.exp(b_ref - b)).astype(BF16)
        qe = (q * jnp.exp(b)).astype(BF16)
        kd = (k * jnp.exp(b_last - b)).astype(BF16)
        for g0 in range(0, rows, group):
            a = lax.dot_general(qa[g0:g0 + group], ka[g0:g0 + group], NT_DIMS, preferred_element_type=F32)
            a = jnp.where(mask, a, 0.0)
            o_s[g0:g0 + group, vs] = _mm(a.astype(BF16), vb[g0:g0 + group])
        s_t = st_t[h]
        for c in range(n_chunks):
            r0 = c * chunk
            o_s[r0:r0 + chunk, vs] += lax.dot_general(qe[r0:r0 + chunk], s_t.astype(BF16), NT_DIMS,
                                                      preferred_element_type=F32)
            decay = jnp.exp(b[r0 + chunk - 1:r0 + chunk, :])
            s_t = s_t * decay + lax.dot_general(vb[r0:r0 + chunk], kd[r0:r0 + chunk], TN_DIMS,
                                                preferred_element_type=F32)
        st_t[h] = s_t


def _gla_seqs(q_s, k_s, v_s, gl_s, b_s, o_s, s0_ref, st_ref, *, dm, nseq, chunk, unroll):
    _chunk_cumsum(gl_s, b_s, nseq * chunk, chunk)
    ri = lax.broadcasted_iota(jnp.int32, (chunk, chunk), 0)
    ci = lax.broadcasted_iota(jnp.int32, (chunk, chunk), 1)
    causal = ci <= ri
    nt, tn = NT_DIMS, TN_DIMS

    def one_seq(seq):
        r0 = pl.multiple_of(seq * chunk, chunk)
        for h in range(GLA_HEADS):
            ks = slice(h * dm.dk, (h + 1) * dm.dk)
            vs = slice(h * dm.dv, (h + 1) * dm.dv)
            bc = b_s[pl.ds(r0, chunk), ks]
            b_ref = bc[chunk // 2:chunk // 2 + 1, :]
            b_last = bc[chunk - 1:chunk, :]
            qc = q_s[pl.ds(r0, chunk), ks]
            kc = k_s[pl.ds(r0, chunk), ks]
            vc = v_s[pl.ds(r0, chunk), vs].astype(BF16)
            s_old = s0_ref[seq, h]
            a = lax.dot_general((qc * jnp.exp(bc - b_ref)).astype(BF16), (kc * jnp.exp(b_ref - bc)).astype(BF16),
                                nt, preferred_element_type=F32)
            a = jnp.where(causal, a, 0.0)
            o = _mm(a.astype(BF16), vc) + _mm((qc * jnp.exp(bc)).astype(BF16), s_old.astype(BF16))
            o_s[pl.ds(r0, chunk), vs] = o
            e_col = jnp.broadcast_to(jnp.exp(b_last), (dm.dk, dm.dk)).T
            decay = jnp.concatenate([e_col] * (dm.dv // dm.dk), axis=1)
            kv = lax.dot_general((kc * jnp.exp(b_last - bc)).astype(BF16), vc, tn, preferred_element_type=F32)
            st_ref[seq, h] = decay * s_old + kv

    def step(g, carry):
        for u in range(unroll):
            one_seq(g * unroll + u)
        return carry

    lax.fori_loop(0, nseq // unroll, step, 0)


def _mixer_tail(x, xb, o_s, ext_s, pos0, w, dm, *, nseq, seq_len):
    t = nseq * seq_len
    off_g = 2 * dm.key + dm.val
    off_u = off_g + dm.val
    off_ga = off_u + dm.pw
    off_gb = off_ga + dm.d

    g = _mm(xb, w.w_in[:, off_g:off_g + dm.val])
    parts = []
    for h in range(GLA_HEADS):
        vs = slice(h * dm.dv, (h + 1) * dm.dv)
        oh = o_s[:, vs]
        ms = jnp.mean(oh * oh, axis=-1, keepdims=True)
        on = oh * lax.rsqrt(ms + RMS_EPS) * w.gla_norm[...]
        gh = g[:, vs]
        parts.append((on * (gh * jax.nn.sigmoid(gh))).astype(BF16))
    branch_a = _mm(jnp.concatenate(parts, axis=1), w.w_ba[...])

    u = _mm(xb, w.w_in[:, off_u:off_u + dm.pw])
    ext_s[:, POOL_WMAX:POOL_WMAX + seq_len, :] = u.reshape(nseq, seq_len, dm.pw)
    p = lax.broadcasted_iota(jnp.int32, (nseq, seq_len, dm.pgc), 1)
    pooled = []
    for gi, win in enumerate(POOL_WINDOWS):
        cs = slice(gi * dm.pgc, (gi + 1) * dm.pgc)
        cur = ext_s[:, POOL_WMAX:POOL_WMAX + seq_len, cs]
        acc = cur
        for j in range(1, win):
            acc = acc + ext_s[:, POOL_WMAX - j:POOL_WMAX - j + seq_len, cs]
        cnt = jnp.minimum(win, p + (pos0 + 1)).astype(F32)
        pg = (acc / cnt - cur).reshape(t, dm.pgc)
        pg = _mm(pg.astype(BF16), w.w_pg[gi]) * w.pool_scale[:, cs]
        pooled.append(pg.astype(BF16))
    branch_b = _mm(jnp.concatenate(pooled, axis=1), w.w_bp[...])

    gate_a = jax.nn.sigmoid(_mm(xb, w.w_in[:, off_ga:off_ga + dm.d]) + w.b_gates[:, 0:dm.d])
    merged = gate_a * branch_a
    gate_b = jax.nn.sigmoid(_mm(xb, w.w_in[:, off_gb:off_gb + dm.d]) + w.b_gates[:, dm.d:2 * dm.d])
    merged = merged + gate_b * branch_b
    mix = _mm(merged.astype(BF16), w.w_out[...])
    h1 = _layer_norm(dm.alpha * x + mix, w.ln_g[...], w.ln_b[...])
    logits = _mm(h1.astype(BF16), w.w_rt[...]) + w.b_rt[...]
    return h1, logits.T[0:N_EXPERTS, :]


N_MIXW = len(MixW._fields)


def _prompt_mixer_kernel(*refs, dm, tile, n_tiles):
    x_ref = refs[0]
    w = MixW(*refs[1:1 + N_MIXW])
    h1_ref, lg_ref, st_ref, buf_ref, q_s, k_s, v_s, gl_s, b_s, o_s, ext_s, st_t = refs[1 + N_MIXW:]
    lt = pl.program_id(1)

    @pl.when(lt == 0)
    def _():
        st_t[...] = jnp.zeros(st_t.shape, F32)
        ext_s[:, 0:POOL_WMAX, :] = jnp.zeros((1, POOL_WMAX, dm.pw), F32)

    x = x_ref[...]
    xb = x.astype(BF16)
    _project_qkv(xb, w, dm, q_s, k_s, v_s, gl_s)
    _chunk_cumsum(gl_s, b_s, tile, GLA_CHUNK)
    _gla_tile(q_s, k_s, v_s, b_s, o_s, st_t, dm=dm, rows=tile, chunk=GLA_CHUNK, group=GLA_GROUP)

    @pl.when(lt == n_tiles - 1)
    def _():
        for h in range(GLA_HEADS):
            st_ref[0, h] = st_t[h].T

    h1, logits_t = _mixer_tail(x, xb, o_s, ext_s, lt * tile, w, dm, nseq=1, seq_len=tile)
    _to_row_tiles(h1_ref, h1, tile)
    lg_ref[...] = logits_t
    ext_s[:, 0:POOL_WMAX, :] = ext_s[:, tile:tile + POOL_WMAX, :]

    @pl.when(lt == n_tiles - 1)
    def _():
        buf_ref[...] = ext_s[:, 1:POOL_WMAX, :]


def _sample_proj_kernel(*refs, dm):
    x_ref = refs[0]
    w = MixW(*refs[1:1 + N_MIXW])
    q_ref, k_ref, v_ref, gl_ref = refs[1 + N_MIXW:]
    _project_qkv(x_ref[...].astype(BF16), w, dm, q_ref, k_ref, v_ref, gl_ref)


def _sample_gla_kernel(q_ref, k_ref, v_ref, gl_ref, s0_ref, o_ref, st_ref, b_s, *, dm, nseq, seq_len):
    _gla_seqs(q_ref, k_ref, v_ref, gl_ref, b_s, o_ref, s0_ref, st_ref, dm=dm, nseq=nseq, chunk=seq_len,
              unroll=GLA_SEQ_UNROLL)


def _sample_tail_kernel(*refs, dm, nseq, seq_len):
    x_ref, o_ref, hist_ref = refs[0:3]
    w = MixW(*refs[3:3 + N_MIXW])
    h1_ref, lg_ref, buf_ref, ext_s = refs[3 + N_MIXW:]
    ext_s[:, 0:1, :] = jnp.zeros((nseq, 1, dm.pw), F32)
    ext_s[:, 1:POOL_WMAX, :] = hist_ref[...]
    x = x_ref[...]
    h1, logits_t = _mixer_tail(x, x.astype(BF16), o_ref, ext_s, POOL_BUF, w, dm, nseq=nseq, seq_len=seq_len)
    _to_row_tiles(h1_ref, h1, nseq * seq_len)
    lg_ref[...] = logits_t
    buf_ref[...] = ext_s[:, seq_len + 1:seq_len + POOL_WMAX, :]


def _const_spec(arr):
    nd = arr.ndim
    return pl.BlockSpec(arr.shape, lambda *_: (0,) * nd, pipeline_mode=pl.Buffered(1))


def _prompt_mixer(x, w, dm):
    bsz, seq, d = x.shape
    tile = PROMPT_TILE
    n_tiles = seq // tile
    m = bsz * seq
    scratch = [pltpu.VMEM((tile, dm.key), F32), pltpu.VMEM((tile, dm.key), F32), pltpu.VMEM((tile, dm.val), F32),
               pltpu.VMEM((tile, dm.key), F32), pltpu.VMEM((tile, dm.key), F32), pltpu.VMEM((tile, dm.val), F32),
               pltpu.VMEM((1, POOL_WMAX + tile, dm.pw), F32),
               pltpu.VMEM((GLA_HEADS, dm.dv, dm.dk), F32)]
    return pl.pallas_call(
        functools.partial(_prompt_mixer_kernel, dm=dm, tile=tile, n_tiles=n_tiles),
        grid=(bsz, n_tiles),
        in_specs=[pl.BlockSpec((tile, d), lambda b, t: (b * n_tiles + t, 0))] + [_const_spec(a) for a in w],
        out_specs=[pl.BlockSpec((tile * SUBLANES, LANES), lambda b, t: (b * n_tiles + t, 0)),
                   pl.BlockSpec((N_EXPERTS, tile), lambda b, t: (0, b * n_tiles + t)),
                   pl.BlockSpec((1, GLA_HEADS, dm.dk, dm.dv), lambda b, t: (b, 0, 0, 0)),
                   pl.BlockSpec((1, POOL_BUF, dm.pw), lambda b, t: (b, 0, 0))],
        out_shape=[jax.ShapeDtypeStruct((m * SUBLANES, LANES), F32),
                   jax.ShapeDtypeStruct((N_EXPERTS, m), F32),
                   jax.ShapeDtypeStruct((bsz, GLA_HEADS, dm.dk, dm.dv), F32),
                   jax.ShapeDtypeStruct((bsz, POOL_BUF, dm.pw), F32)],
        scratch_shapes=scratch,
        compiler_params=pltpu.CompilerParams(dimension_semantics=("arbitrary", "arbitrary"),
                                             vmem_limit_bytes=VMEM_LIMIT_BYTES),
        name="prompt_mixer",
    )(x.reshape(m, d), *w)


def _sample_mixer(x, s0, hist, w, dm):
    bsz, seq, d = x.shape
    m = bsz * seq
    x2 = x.reshape(m, d)
    tile = MIX_TILE
    cparams = pltpu.CompilerParams(dimension_semantics=("arbitrary",), vmem_limit_bytes=VMEM_LIMIT_BYTES)
    row = lambda n: pl.BlockSpec((tile, n), lambda i: (i, 0))
    q, k, v, gl = pl.pallas_call(
        functools.partial(_sample_proj_kernel, dm=dm),
        grid=(m // tile,),
        in_specs=[row(d)] + [_const_spec(a) for a in w],
        out_specs=[row(dm.key), row(dm.key), row(dm.val), row(dm.key)],
        out_shape=[jax.ShapeDtypeStruct((m, n), F32) for n in (dm.key, dm.key, dm.val, dm.key)],
        compiler_params=cparams,
        name="sample_proj",
    )(x2, *w)

    nseq = GLA_SEQS
    rows = nseq * seq
    grow = lambda n: pl.BlockSpec((rows, n), lambda i: (i, 0))
    st_spec = pl.BlockSpec((nseq, GLA_HEADS, dm.dk, dm.dv), lambda i: (i, 0, 0, 0))
    o, st = pl.pallas_call(
        functools.partial(_sample_gla_kernel, dm=dm, nseq=nseq, seq_len=seq),
        grid=(bsz // nseq,),
        in_specs=[grow(dm.key), grow(dm.key), grow(dm.val), grow(dm.key), st_spec],
        out_specs=[grow(dm.val), st_spec],
        out_shape=[jax.ShapeDtypeStruct((m, dm.val), F32), jax.ShapeDtypeStruct(s0.shape, F32)],
        scratch_shapes=[pltpu.VMEM((rows, dm.key), F32)],
        compiler_params=cparams,
        name="sample_gla",
    )(q, k, v, gl, s0)

    tseq = tile // seq
    hist_spec = pl.BlockSpec((tseq, POOL_BUF, dm.pw), lambda i: (i, 0, 0))
    h1, lg, buf = pl.pallas_call(
        functools.partial(_sample_tail_kernel, dm=dm, nseq=tseq, seq_len=seq),
        grid=(m // tile,),
        in_specs=[row(d), row(dm.val), hist_spec] + [_const_spec(a) for a in w],
        out_specs=[pl.BlockSpec((tile * SUBLANES, LANES), lambda i: (i, 0)),
                   pl.BlockSpec((N_EXPERTS, tile), lambda i: (0, i)), hist_spec],
        out_shape=[jax.ShapeDtypeStruct((m * SUBLANES, LANES), F32),
                   jax.ShapeDtypeStruct((N_EXPERTS, m), F32),
                   jax.ShapeDtypeStruct((bsz, POOL_BUF, dm.pw), F32)],
        scratch_shapes=[pltpu.VMEM((tseq, POOL_WMAX + seq, dm.pw), F32)],
        compiler_params=cparams,
        name="sample_tail",
    )(x2, o, hist, *w)
    return h1, lg, st, buf


def _router_kernel(lga_ref, lgb_ref, w_ref, s_ref, meta_ref, c_ref, carry, *, tile, n_tiles_a):
    i = pl.program_id(0)

    @pl.when(i == 0)
    def _():
        carry[...] = jnp.zeros(carry.shape, F32)

    eio = lax.broadcasted_iota(jnp.int32, (N_EXPERTS, tile), 0)
    work = jnp.where(i < n_tiles_a, lga_ref[...], lgb_ref[...])
    vals, hots = [], []
    for k in range(TOP_K):
        mx = jnp.max(work, axis=0, keepdims=True)
        idx = jnp.min(jnp.where(work == mx, eio, N_EXPERTS), axis=0, keepdims=True)
        hot = eio == idx
        vals.append(mx)
        hots.append(hot)
        work = jnp.where(hot, -jnp.inf, work)
    ex = [jnp.exp(v - vals[0]) for v in vals]
    den = ex[0]
    for k in range(1, TOP_K):
        den = den + ex[k]
    for k in range(TOP_K):
        w_ref[k:k + 1, :] = ex[k] / den
    sel = hots[0]
    for k in range(1, TOP_K):
        sel = sel | hots[k]
    r = lax.broadcasted_iota(jnp.int32, (tile, tile), 0)
    c = lax.broadcasted_iota(jnp.int32, (tile, tile), 1)
    earlier_token = (r < c).astype(BF16)
    rank_in_tile = _mm(sel.astype(BF16), earlier_token)
    cnt = jnp.broadcast_to(jnp.sum(sel.astype(F32), axis=1, keepdims=True), (N_EXPERTS, LANES))
    er = lax.broadcasted_iota(jnp.int32, (N_EXPERTS, N_EXPERTS), 0)
    ec = lax.broadcasted_iota(jnp.int32, (N_EXPERTS, N_EXPERTS), 1)
    lower = (ec < er).astype(BF16)
    cnt_hi = jnp.floor(cnt * (1.0 / 256.0))
    first_slot = _mm(lower, (cnt - 256.0 * cnt_hi).astype(BF16)) + 256.0 * _mm(lower, cnt_hi.astype(BF16))
    slot = rank_in_tile + first_slot[:, 0:1]
    for k in range(TOP_K):
        s_ref[0, k:k + 1, :] = jnp.sum(jnp.where(hots[k], slot, 0.0), axis=0, keepdims=True).astype(jnp.int32)
    meta_ref[0, 0:N_EXPERTS, :] = cnt
    meta_ref[0, N_EXPERTS:2 * N_EXPERTS, :] = first_slot
    meta_ref[0, 2 * N_EXPERTS:3 * N_EXPERTS, :] = carry[...]
    carry[...] = carry[...] + cnt
    c_ref[...] = carry[...]


def _for_each_run(meta, fn):
    def body(e, carry):
        cnt = meta[e]
        first = meta[N_EXPERTS + e]
        dst = meta[2 * N_EXPERTS + e]
        for bit in range(RUN_BITS):
            @pl.when(((cnt >> bit) & 1) == 1)
            def _(bit=bit):
                done = cnt & ((1 << bit) - 1)
                fn(first + done, dst + done, 1 << bit)
        return carry
    lax.fori_loop(0, N_EXPERTS, body, 0)


def _rows(ref, first, n):
    return ref.at[pl.ds(pl.multiple_of(first * SUBLANES, SUBLANES), n * SUBLANES), :]


def _dispatch_kernel(fill_start_ref, fill_cnt_ref, tail_start_ref, meta_hbm, slot_hbm, h1a_ref, h1b_ref, xs_hbm,
                     meta_s0, meta_s1, slot_s0, slot_s1, stage, zeros_s, sem_m, sem_s, sem_r, sem_z,
                     *, tile, n_tiles, n_tiles_a):
    i = pl.program_id(0)
    sl = i % 2
    meta_s = (meta_s0, meta_s1)
    slot_s = (slot_s0, slot_s1)

    def index_copies(blk, s):
        return (pltpu.make_async_copy(meta_hbm.at[blk], meta_s[s], sem_m.at[s]),
                pltpu.make_async_copy(slot_hbm.at[blk], slot_s[s], sem_s.at[s]))

    def wait_stage(s):
        pltpu.make_async_copy(stage.at[s], stage.at[s], sem_r.at[s]).wait()

    @pl.when(i == 0)
    def _():
        for cp in index_copies(0, 0):
            cp.start()
        zeros_s[...] = jnp.zeros(zeros_s.shape, F32)

        def fill_copy(off, bit):
            n = 1 << bit
            return pltpu.make_async_copy(zeros_s.at[pl.ds(0, n * SUBLANES), :], _rows(xs_hbm, off, n), sem_z)

        def for_each_piece(fn):
            def body(e, carry):
                cnt = fill_cnt_ref[e]
                for bit in range(ZERO_FILL_BITS):
                    @pl.when(((cnt >> bit) & 1) == 1)
                    def _(bit=bit):
                        fn(fill_copy(fill_start_ref[e] + (cnt & ((1 << bit) - 1)), bit))
                return carry
            lax.fori_loop(0, N_EXPERTS, body, 0)

        top = ZERO_FILL_BITS - 1
        n_tail = (xs_hbm.shape[0] // SUBLANES - tail_start_ref[0]) >> top

        def for_each_tail_piece(fn):
            def body(j, carry):
                fn(fill_copy(tail_start_ref[0] + (j << top), top))
                return carry
            lax.fori_loop(0, n_tail, body, 0)

        for_each_piece(lambda cp: cp.start())
        for_each_tail_piece(lambda cp: cp.start())
        for_each_piece(lambda cp: cp.wait())
        for_each_tail_piece(lambda cp: cp.wait())

    def group_rows(h1_ref, s_):
        def group(g, carry):
            for u in range(DMA_UNROLL):
                t = g * DMA_UNROLL + u
                row = h1_ref[pl.ds(pl.multiple_of(t * SUBLANES, SUBLANES), SUBLANES), :]
                for k in range(TOP_K):
                    s = slot_s[s_][k * tile + t]
                    stage[s_, pl.ds(pl.multiple_of(s * SUBLANES, SUBLANES), SUBLANES), :] = row
            return carry
        lax.fori_loop(0, tile // DMA_UNROLL, group, 0)

    for s_ in range(2):
        @pl.when((sl == 1 - s_) & (i + 1 < n_tiles))
        def _(s_=s_):
            for cp in index_copies(i + 1, s_):
                cp.start()

    for s_ in range(2):
        @pl.when(sl == s_)
        def _(s_=s_):
            for cp in index_copies(i, s_):
                cp.wait()

            @pl.when(i >= 2)
            def _():
                wait_stage(s_)

            @pl.when(i < n_tiles_a)
            def _():
                group_rows(h1a_ref, s_)

            @pl.when(i >= n_tiles_a)
            def _():
                group_rows(h1b_ref, s_)

            _for_each_run(meta_s[s_], lambda s0, d0, n: pltpu.make_async_copy(
                _rows(stage.at[s_], s0, n), _rows(xs_hbm, d0, n), sem_r.at[s_]).start())

    @pl.when(i == n_tiles - 1)
    def _():
        @pl.when(n_tiles > 1)
        def _():
            wait_stage(1 - sl)
        wait_stage(sl)


def _expert_kernel(be_ref, nused_ref, next_e_ref, xs_ref, wgu_hbm, bgu_ref, wdn_hbm, bdn_ref, out_ref,
                   wgu_f, wdn_f, wgu_b, wdn_b, sem_w, *, rows, d_ff):
    i = pl.program_id(0)

    def weight_copies(e):
        return (pltpu.make_async_copy(wgu_hbm.at[e], wgu_f, sem_w.at[0]),
                pltpu.make_async_copy(wdn_hbm.at[e], wdn_f, sem_w.at[1]))

    @pl.when(i == 0)
    def _():
        for cp in weight_copies(be_ref[0]):
            cp.start()

    @pl.when((i < nused_ref[0]) & ((i == 0) | (be_ref[i] != be_ref[jnp.maximum(i - 1, 0)])))
    def _():
        for cp in weight_copies(be_ref[i]):
            cp.wait()
        wgu_b[...] = wgu_f[...].astype(BF16)
        wdn_b[...] = wdn_f[...].astype(BF16)

        @pl.when(next_e_ref[i] >= 0)
        def _():
            for cp in weight_copies(next_e_ref[i]):
                cp.start()

    @pl.when(i < nused_ref[0])
    def _():
        xb = _from_row_tiles(xs_ref, rows).astype(BF16)
        hh = _mm(xb, wgu_b[...]) + bgu_ref[0]
        gate = jnp.minimum(hh[:, 0:d_ff], SWIGLU_LIMIT)
        up = jnp.clip(hh[:, d_ff:2 * d_ff], -SWIGLU_LIMIT, SWIGLU_LIMIT)
        act = (up + 1.0) * (gate * jax.nn.sigmoid(GLU_ALPHA * gate))
        res = _mm(act.astype(BF16), wdn_b[...]) + bdn_ref[0]
        _to_row_tiles(out_ref, res, rows)

    @pl.when(i >= nused_ref[0])
    def _():
        out_ref[...] = jnp.zeros(out_ref.shape, F32)


def _combine_kernel(meta_hbm, slot_hbm, res_hbm, h1a_ref, h1b_ref, gw_ref, g_ref, b_ref, ya_ref, yb_ref,
                    meta_s0, meta_s1, slot_s0, slot_s1, stage, gbuf, sem_m, sem_s, sem_g,
                    *, tile, alpha, n_tiles, n_tiles_a):
    i = pl.program_id(0)
    sl = i % 2
    meta_s = (meta_s0, meta_s1)
    slot_s = (slot_s0, slot_s1)

    def index_copies(blk, s):
        return (pltpu.make_async_copy(meta_hbm.at[blk], meta_s[s], sem_m.at[s]),
                pltpu.make_async_copy(slot_hbm.at[blk], slot_s[s], sem_s.at[s]))

    def fetch_runs(s):
        _for_each_run(meta_s[s], lambda s0, d0, n: pltpu.make_async_copy(
            _rows(res_hbm, d0, n), _rows(stage.at[s], s0, n), sem_g.at[s]).start())

    @pl.when(i == 0)
    def _():
        for cp in index_copies(0, 0):
            cp.start()
        for cp in index_copies(0, 0):
            cp.wait()
        fetch_runs(0)

        @pl.when(n_tiles > 1)
        def _():
            for cp in index_copies(1, 1):
                cp.start()

    def regroup(s_):
        def group(g, carry):
            for u in range(DMA_UNROLL):
                t = g * DMA_UNROLL + u
                for k in range(TOP_K):
                    s = slot_s[s_][k * tile + t]
                    gbuf[k, pl.ds(pl.multiple_of(t * SUBLANES, SUBLANES), SUBLANES), :] = (
                        stage[s_, pl.ds(pl.multiple_of(s * SUBLANES, SUBLANES), SUBLANES), :])
            return carry
        lax.fori_loop(0, tile // DMA_UNROLL, group, 0)

    for s_ in range(2):
        @pl.when((sl == 1 - s_) & (i + 1 < n_tiles))
        def _(s_=s_):
            for cp in index_copies(i + 1, s_):
                cp.wait()
            fetch_runs(s_)

    for s_ in range(2):
        @pl.when(sl == s_)
        def _(s_=s_):
            pltpu.make_async_copy(stage.at[s_], stage.at[s_], sem_g.at[s_]).wait()
            regroup(s_)

    for s_ in range(2):
        @pl.when((sl == s_) & (i + 2 < n_tiles))
        def _(s_=s_):
            for cp in index_copies(i + 2, s_):
                cp.start()

    gw = gw_ref[...]
    z = alpha * jnp.where(i < n_tiles_a, _from_row_tiles(h1a_ref, tile), _from_row_tiles(h1b_ref, tile))
    for k in range(TOP_K):
        z = z + gw[:, k:k + 1] * _from_row_tiles(gbuf.at[k], tile)
    y = _layer_norm(z, g_ref[...], b_ref[...])

    @pl.when(i < n_tiles_a)
    def _():
        ya_ref[...] = y

    @pl.when(i >= n_tiles_a)
    def _():
        yb_ref[...] = y


def _moe(h1_a, h1_b, logits_a, logits_b, w_gu, b_gu, w_down, b_down, ln_g, ln_b, dm):
    m_a = logits_a.shape[1]
    m = m_a + logits_b.shape[1]
    rows = EXPERT_ROWS
    d_ff = w_down.shape[1]
    tile = MOE_TILE
    n_tiles = m // tile
    n_tiles_a = m_a // tile
    cparams = pltpu.CompilerParams(dimension_semantics=("arbitrary",), vmem_limit_bytes=VMEM_LIMIT_BYTES)

    def two_group_specs(block, lane_axis):
        pick = (lambda j: (0, j)) if lane_axis else (lambda j: (j, 0))
        return [pl.BlockSpec(block, lambda i, *_: pick(jnp.minimum(i, n_tiles_a - 1))),
                pl.BlockSpec(block, lambda i, *_: pick(jnp.maximum(i - n_tiles_a, 0)))]

    kblk = lambda: pl.BlockSpec((TOP_K, tile), lambda i: (0, i))
    gate_w, slot, meta, counts = pl.pallas_call(
        functools.partial(_router_kernel, tile=tile, n_tiles_a=n_tiles_a),
        grid=(n_tiles,),
        in_specs=two_group_specs((N_EXPERTS, tile), True),
        out_specs=[kblk(), pl.BlockSpec((1, TOP_K, tile), lambda i: (i, 0, 0)),
                   pl.BlockSpec((1, 3 * N_EXPERTS, LANES), lambda i: (i, 0, 0)),
                   pl.BlockSpec((N_EXPERTS, LANES), lambda i: (0, 0))],
        out_shape=[jax.ShapeDtypeStruct((TOP_K, m), F32), jax.ShapeDtypeStruct((n_tiles, TOP_K, tile), jnp.int32),
                   jax.ShapeDtypeStruct((n_tiles, 3 * N_EXPERTS, LANES), F32),
                   jax.ShapeDtypeStruct((N_EXPERTS, LANES), F32)],
        scratch_shapes=[pltpu.VMEM((N_EXPERTS, LANES), F32)],
        compiler_params=cparams,
        name="moe_router",
    )(logits_a, logits_b)

    counts = counts[:, 0].astype(jnp.int32)
    padded = (counts + rows - 1) // rows * rows
    pad_end = jnp.cumsum(padded)
    pad_start = pad_end - padded
    n_blocks = -(-(m * TOP_K + N_EXPERTS * (rows - 1)) // rows)
    n_rows = n_blocks * rows
    block_row0 = jnp.arange(n_blocks, dtype=jnp.int32) * rows
    block_e = jnp.minimum(jnp.sum((pad_end[None, :] <= block_row0[:, None]).astype(jnp.int32), axis=1), N_EXPERTS - 1)
    n_used = (pad_end[-1] // rows).astype(jnp.int32).reshape(1)
    meta = meta[:, :, 0].astype(jnp.int32)
    tile_meta = jnp.concatenate([meta[:, 0:2 * N_EXPERTS], meta[:, 2 * N_EXPERTS:] + pad_start[None, :],
                                 jnp.zeros((n_tiles, LANES - 3 * N_EXPERTS), jnp.int32)], axis=1)
    slot_t = slot.reshape(n_tiles, TOP_K * tile)

    h1_specs = two_group_specs((tile * SUBLANES, LANES), False)
    any_spec = pl.BlockSpec(memory_space=pl.ANY)
    index_scratch = [pltpu.SMEM((LANES,), jnp.int32), pltpu.SMEM((LANES,), jnp.int32),
                     pltpu.SMEM((TOP_K * tile,), jnp.int32), pltpu.SMEM((TOP_K * tile,), jnp.int32),
                     pltpu.VMEM((2, TOP_K * tile * SUBLANES, LANES), F32)]
    xs = pl.pallas_call(
        functools.partial(_dispatch_kernel, tile=tile, n_tiles=n_tiles, n_tiles_a=n_tiles_a),
        grid_spec=pltpu.PrefetchScalarGridSpec(
            num_scalar_prefetch=3,
            grid=(n_tiles,),
            in_specs=[any_spec, any_spec] + h1_specs,
            out_specs=any_spec,
            scratch_shapes=index_scratch + [
                pltpu.VMEM(((1 << (ZERO_FILL_BITS - 1)) * SUBLANES, LANES), F32),
                pltpu.SemaphoreType.DMA((2,)), pltpu.SemaphoreType.DMA((2,)), pltpu.SemaphoreType.DMA((2,)),
                pltpu.SemaphoreType.DMA]),
        out_shape=jax.ShapeDtypeStruct((n_rows * SUBLANES, LANES), F32),
        compiler_params=cparams,
        name="moe_dispatch",
    )(pad_start + counts, padded - counts, pad_end[-1:], tile_meta, slot_t, h1_a, h1_b)

    blk = jnp.arange(n_blocks, dtype=jnp.int32)
    run_start = (blk < n_used[0]) & ((blk == 0) | (block_e != jnp.roll(block_e, 1)))
    next_start = jnp.min(jnp.where((blk[None, :] > blk[:, None]) & run_start[None, :], blk[None, :], n_blocks), axis=1)
    next_e = jnp.where(next_start < n_blocks, block_e[jnp.minimum(next_start, n_blocks - 1)], -1).astype(jnp.int32)

    last = lambda i, nu: jnp.minimum(i, nu[0] - 1)
    res = pl.pallas_call(
        functools.partial(_expert_kernel, rows=rows, d_ff=d_ff),
        grid_spec=pltpu.PrefetchScalarGridSpec(
            num_scalar_prefetch=3,
            grid=(n_blocks,),
            in_specs=[pl.BlockSpec((rows * SUBLANES, LANES), lambda i, be, nu, ne: (last(i, nu), 0)),
                      any_spec,
                      pl.BlockSpec((1, 1, 2 * d_ff), lambda i, be, nu, ne: (be[i], 0, 0)),
                      any_spec,
                      pl.BlockSpec((1, 1, dm.d), lambda i, be, nu, ne: (be[i], 0, 0))],
            out_specs=pl.BlockSpec((rows * SUBLANES, LANES), lambda i, be, nu, ne: (i, 0)),
            scratch_shapes=[pltpu.VMEM((dm.d, 2 * d_ff), F32), pltpu.VMEM((d_ff, dm.d), F32),
                            pltpu.VMEM((dm.d, 2 * d_ff), BF16), pltpu.VMEM((d_ff, dm.d), BF16),
                            pltpu.SemaphoreType.DMA((2,))]),
        out_shape=jax.ShapeDtypeStruct((n_rows * SUBLANES, LANES), F32),
        compiler_params=cparams,
        name="moe_experts",
    )(block_e, n_used, next_e, xs, w_gu, b_gu[:, None, :], w_down, b_down[:, None, :])

    ya, yb = pl.pallas_call(
        functools.partial(_combine_kernel, tile=tile, alpha=dm.alpha, n_tiles=n_tiles, n_tiles_a=n_tiles_a),
        grid=(n_tiles,),
        in_specs=[any_spec, any_spec, any_spec] + h1_specs + [
                  pl.BlockSpec((tile, TOP_K), lambda i: (i, 0)),
                  pl.BlockSpec((1, dm.d), lambda i: (0, 0)),
                  pl.BlockSpec((1, dm.d), lambda i: (0, 0))],
        out_specs=[pl.BlockSpec((tile, dm.d), lambda i: (jnp.minimum(i, n_tiles_a - 1), 0)),
                   pl.BlockSpec((tile, dm.d), lambda i: (jnp.maximum(i - n_tiles_a, 0), 0))],
        out_shape=[jax.ShapeDtypeStruct((m_a, dm.d), F32), jax.ShapeDtypeStruct((m - m_a, dm.d), F32)],
        scratch_shapes=index_scratch + [
            pltpu.VMEM((TOP_K, tile * SUBLANES, LANES), F32),
            pltpu.SemaphoreType.DMA((2,)), pltpu.SemaphoreType.DMA((2,)), pltpu.SemaphoreType.DMA((2,))],
        compiler_params=cparams,
        name="moe_combine",
    )(tile_meta, slot_t, res, h1_a, h1_b, gate_w.T, ln_g[None, :], ln_b[None, :])
    return ya, yb


def _pad_cols(a, n):
    return jnp.pad(a, ((0, 0), (0, n - a.shape[1])))


def _regroup_kernel(w_ref, o_ref, *, o_lr, o_tail):
    n_in = w_ref.shape[1]
    n_main = n_in - (o_tail - o_lr)
    o_ref[:, 0:o_lr] = w_ref[:, 0:o_lr].astype(BF16)
    o_ref[:, o_lr:n_main] = w_ref[:, o_tail:n_in].astype(BF16)
    lr = w_ref[:, o_lr:o_tail].astype(BF16)
    o_ref[:, n_main:n_main + LANES] = jnp.concatenate(
        [lr, jnp.zeros((lr.shape[0], LANES - (o_tail - o_lr)), BF16)], axis=1)


def _regroup_w_in(w_in_all, layer, o_lr, o_tail):
    _, d, n_in = w_in_all.shape
    n_out = n_in - (o_tail - o_lr) + LANES
    rows = REGROUP_ROWS
    return pl.pallas_call(
        functools.partial(_regroup_kernel, o_lr=o_lr, o_tail=o_tail),
        grid=(d // rows,),
        in_specs=[pl.BlockSpec((None, rows, n_in), lambda i: (layer, i, 0))],
        out_specs=pl.BlockSpec((rows, n_out), lambda i: (i, 0)),
        out_shape=jax.ShapeDtypeStruct((d, n_out), BF16),
        compiler_params=pltpu.CompilerParams(dimension_semantics=("arbitrary",), vmem_limit_bytes=VMEM_LIMIT_BYTES),
        name="regroup_w_in",
    )(w_in_all)


def _mixer_weights(w_in_all, layer, w_gk2, b_gk, gla_norm_w, w_branch_gla, w_pool_grp, pool_scale, w_branch_pool,
                   b_gates, w_out, ln_g, ln_b, w_router, b_router, dm):
    o_lr = 2 * dm.key + dm.val
    o_tail = o_lr + GATE_RANK
    return MixW(
        w_in=_regroup_w_in(w_in_all, layer, o_lr, o_tail),
        w_gk2=jnp.pad(w_gk2, ((0, LANES - GATE_RANK), (0, 0))).astype(BF16),
        b_gk=b_gk[None, :],
        gla_norm=gla_norm_w[None, :],
        w_ba=w_branch_gla.astype(BF16),
        w_pg=w_pool_grp.astype(BF16),
        pool_scale=pool_scale[None, :],
        w_bp=w_branch_pool.astype(BF16),
        b_gates=b_gates[None, :],
        w_out=w_out.astype(BF16),
        ln_g=ln_g[None, :],
        ln_b=ln_b[None, :],
        w_rt=_pad_cols(w_router, LANES).astype(BF16),
        b_rt=_pad_cols(b_router[None, :], LANES),
    )


def kernel(x_prompt, x_sample, state_gla, state_pool, w_in, w_gk2, b_gk, gla_norm_w, w_branch_gla, w_pool_grp,
           pool_scale, w_branch_pool, b_gates, w_out, ln1_g, ln1_b, w_router, b_router, w_gu, b_gu, w_down, b_down,
           ln2_g, ln2_b):
    depth = w_in.shape[0]
    bp, lp, d = x_prompt.shape
    bs, ls, _ = x_sample.shape
    assert d == SUBLANES * LANES and lp % PROMPT_TILE == 0 and (bs * ls) % MIX_TILE == 0 and MIX_TILE % ls == 0
    assert bs % GLA_SEQS == 0 and ls % SUBLANES == 0
    assert (bp * lp) % MOE_TILE == 0 and (bs * ls) % MOE_TILE == 0
    dm = _dims(d, depth)
    yp, ys = x_prompt, x_sample
    gla_p, pool_p, gla_s, pool_s = [], [], [], []
    for l in range(depth):
        w = _mixer_weights(w_in, l, w_gk2[l], b_gk[l], gla_norm_w[l], w_branch_gla[l], w_pool_grp[l], pool_scale[l],
                           w_branch_pool[l], b_gates[l], w_out[l], ln1_g[l], ln1_b[l], w_router[l], b_router[l], dm)
        h1p, lgp, sp, bufp = _prompt_mixer(yp, w, dm)
        h1s, lgs, ss, bufs = _sample_mixer(ys, state_gla[l], state_pool[l], w, dm)
        yp, ys = _moe(h1p, h1s, lgp, lgs, w_gu[l], b_gu[l], w_down[l], b_down[l], ln2_g[l], ln2_b[l], dm)
        yp = yp.reshape(bp, lp, d)
        ys = ys.reshape(bs, ls, d)
        gla_p.append(sp.astype(state_gla.dtype))
        pool_p.append(bufp.astype(state_pool.dtype))
        gla_s.append(ss.astype(state_gla.dtype))
        pool_s.append(bufs.astype(state_pool.dtype))
    stack = lambda xs: xs[0][None] if len(xs) == 1 else jnp.stack(xs, 0)
    return (yp, ys, stack(gla_p), stack(pool_p), stack(gla_s), stack(pool_s))
```

```python
import functools
from typing import NamedTuple

import jax
import jax.numpy as jnp
from jax import lax
from jax.experimental import pallas as pl
from jax.experimental.pallas import tpu as pltpu

F32 = jnp.float32
BF16 = jnp.bfloat16

GLA_HEADS = 4
GATE_RANK = 16
GATE_NORMALIZER = 16.0
GLA_CHUNK = 64
RMS_EPS = 1e-6
POOL_WINDOWS = (2, 4, 8, 16)
POOL_WMAX = 16
POOL_BUF = POOL_WMAX - 1
N_EXPERTS = 32
TOP_K = 4
SWIGLU_LIMIT = 7.0
GLU_ALPHA = 1.702
LN_EPS = 1e-5

LANES = 128
SUBLANES = 8
VMEM_LIMIT_BYTES = 56 * 1024 * 1024

PROMPT_TILE = 512
MIX_TILE = 256
GLA_GROUP = 128
GLA_SEQS = 8
GLA_SEQ_UNROLL = 2
EXPERT_ROWS = 512
MOE_TILE = 512
CUMSUM_BLOCK = 256
DMA_UNROLL = 8
REGROUP_ROWS = 256
ZERO_FILL_BITS = (EXPERT_ROWS - 1).bit_length()
RUN_BITS = MOE_TILE.bit_length()


class Dims(NamedTuple):
    d: int
    dk: int
    dv: int
    key: int
    val: int
    pw: int
    pgc: int
    alpha: float


def _dims(d, depth):
    return Dims(d=d, dk=d // 8, dv=d // 4, key=d // 2, val=d, pw=d // 2, pgc=d // 8, alpha=(2.0 * depth) ** 0.25)


class MixW(NamedTuple):
    w_in: object
    w_gk2: object
    b_gk: object
    gla_norm: object
    w_ba: object
    w_pg: object
    pool_scale: object
    w_bp: object
    b_gates: object
    w_out: object
    ln_g: object
    ln_b: object
    w_rt: object
    b_rt: object


def _mm(a, b):
    return jnp.dot(a, b, preferred_element_type=F32)


def _layer_norm(z, g, b):
    mu = jnp.mean(z, axis=-1, keepdims=True)
    zc = z - mu
    var = jnp.mean(zc * zc, axis=-1, keepdims=True)
    return zc * lax.rsqrt(var + LN_EPS) * g + b


def _to_row_tiles(ref, val, rows):
    for c in range(SUBLANES):
        ref[pl.ds(c, rows, stride=SUBLANES), :] = val[:, c * LANES:(c + 1) * LANES]


def _from_row_tiles(ref, rows):
    return jnp.concatenate([ref[pl.ds(c, rows, stride=SUBLANES), :] for c in range(SUBLANES)], axis=1)


def _project_qkv(xb, w, dm, q_s, k_s, v_s, gl_s):
    off_lr = 2 * dm.key + 2 * dm.val + dm.pw + 2 * dm.d
    lr = _mm(xb, w.w_in[:, off_lr:off_lr + LANES])
    q_s[...] = _mm(xb, w.w_in[:, 0:dm.key]) * (dm.dk ** -0.5)
    gk = _mm(lr.astype(BF16), w.w_gk2[...]) + w.b_gk[...]
    k_s[...] = _mm(xb, w.w_in[:, dm.key:2 * dm.key])
    gl_s[...] = (jnp.minimum(gk, 0.0) - jnp.log1p(jnp.exp(-jnp.abs(gk)))) / GATE_NORMALIZER
    v_s[...] = _mm(xb, w.w_in[:, 2 * dm.key:2 * dm.key + dm.val])


def _chunk_cumsum(gl_s, b_s, rows, chunk):
    blk = min(rows, CUMSUM_BLOCK)
    r = lax.broadcasted_iota(jnp.int32, (blk, blk), 0)
    c = lax.broadcasted_iota(jnp.int32, (blk, blk), 1)
    tri = ((r // chunk == c // chunk) & (c <= r)).astype(BF16)
    for b0 in range(0, rows, blk):
        gl = gl_s[b0:b0 + blk, :]
        hi = gl.astype(BF16)
        lo = (gl - hi.astype(F32)).astype(BF16)
        b_s[b0:b0 + blk, :] = _mm(tri, hi) + _mm(tri, lo)


NT_DIMS = (((1,), (1,)), ((), ()))
TN_DIMS = (((0,), (0,)), ((), ()))


def _gla_tile(q_s, k_s, v_s, b_s, o_s, st_t, *, dm, rows, chunk, group):
    n_chunks = rows // chunk
    gr = lax.broadcasted_iota(jnp.int32, (group, group), 0)
    gc = lax.broadcasted_iota(jnp.int32, (group, group), 1)
    mask = (gr // chunk == gc // chunk) & (gc <= gr)

    def per_chunk_row(b, row):
        return jnp.concatenate([jnp.broadcast_to(b[c * chunk + row:c * chunk + row + 1, :], (chunk, b.shape[1]))
                                for c in range(n_chunks)], axis=0)

    for h in range(GLA_HEADS):
        ks = slice(h * dm.dk, (h + 1) * dm.dk)
        vs = slice(h * dm.dv, (h + 1) * dm.dv)
        b = b_s[:, ks]
        b_ref = per_chunk_row(b, chunk // 2)
        b_last = per_chunk_row(b, chunk - 1)
        q = q_s[:, ks]
        k = k_s[:, ks]
        vb = v_s[:, vs].astype(BF16)
        qa = (q * jnp.exp(b - b_ref)).astype(BF16)
        ka = (k * jnp.exp(b_ref - b)).astype(BF16)
        qe = (q * jnp.exp(b)).astype(BF16)
        kd = (k * jnp.exp(b_last - b)).astype(BF16)
        for g0 in range(0, rows, group):
            a = lax.dot_general(qa[g0:g0 + group], ka[g0:g0 + group], NT_DIMS, preferred_element_type=F32)
            a = jnp.where(mask, a, 0.0)
            o_s[g0:g0 + group, vs] = _mm(a.astype(BF16), vb[g0:g0 + group])
        s_t = st_t[h]
        for c in range(n_chunks):
            r0 = c * chunk
            o_s[r0:r0 + chunk, vs] += lax.dot_general(qe[r0:r0 + chunk], s_t.astype(BF16), NT_DIMS,
                                                      preferred_element_type=F32)
            decay = jnp.exp(b[r0 + chunk - 1:r0 + chunk, :])
            s_t = s_t * decay + lax.dot_general(vb[r0:r0 + chunk], kd[r0:r0 + chunk], TN_DIMS,
                                                preferred_element_type=F32)
        st_t[h] = s_t


def _gla_seqs(q_s, k_s, v_s, gl_s, b_s, o_s, s0_ref, st_ref, *, dm, nseq, chunk, unroll):
    _chunk_cumsum(gl_s, b_s, nseq * chunk, chunk)
    ri = lax.broadcasted_iota(jnp.int32, (chunk, chunk), 0)
    ci = lax.broadcasted_iota(jnp.int32, (chunk, chunk), 1)
    causal = ci <= ri
    nt, tn = NT_DIMS, TN_DIMS

    def one_seq(seq):
        r0 = pl.multiple_of(seq * chunk, chunk)
        for h in range(GLA_HEADS):
            ks = slice(h * dm.dk, (h + 1) * dm.dk)
            vs = slice(h * dm.dv, (h + 1) * dm.dv)
            bc = b_s[pl.ds(r0, chunk), ks]
            b_ref = bc[chunk // 2:chunk // 2 + 1, :]
            b_last = bc[chunk - 1:chunk, :]
            qc = q_s[pl.ds(r0, chunk), ks]
            kc = k_s[pl.ds(r0, chunk), ks]
            vc = v_s[pl.ds(r0, chunk), vs].astype(BF16)
            s_old = s0_ref[seq, h]
            a = lax.dot_general((qc * jnp.exp(bc - b_ref)).astype(BF16), (kc * jnp.exp(b_ref - bc)).astype(BF16),
                                nt, preferred_element_type=F32)
            a = jnp.where(causal, a, 0.0)
            o = _mm(a.astype(BF16), vc) + _mm((qc * jnp.exp(bc)).astype(BF16), s_old.astype(BF16))
            o_s[pl.ds(r0, chunk), vs] = o
            e_col = jnp.broadcast_to(jnp.exp(b_last), (dm.dk, dm.dk)).T
            decay = jnp.concatenate([e_col] * (dm.dv // dm.dk), axis=1)
            kv = lax.dot_general((kc * jnp.exp(b_last - bc)).astype(BF16), vc, tn, preferred_element_type=F32)
            st_ref[seq, h] = decay * s_old + kv

    def step(g, carry):
        for u in range(unroll):
            one_seq(g * unroll + u)
        return carry

    lax.fori_loop(0, nseq // unroll, step, 0)


def _mixer_tail(x, xb, o_s, ext_s, pos0, w, dm, *, nseq, seq_len):
    t = nseq * seq_len
    off_g = 2 * dm.key + dm.val
    off_u = off_g + dm.val
    off_ga = off_u + dm.pw
    off_gb = off_ga + dm.d

    u = _mm(xb, w.w_in[:, off_u:off_u + dm.pw])
    ext_s[:, POOL_WMAX:POOL_WMAX + seq_len, :] = u.reshape(nseq, seq_len, dm.pw)
    g = _mm(xb, w.w_in[:, off_g:off_g + dm.val])
    parts = []
    for h in range(GLA_HEADS):
        vs = slice(h * dm.dv, (h + 1) * dm.dv)
        oh = o_s[:, vs]
        ms = jnp.mean(oh * oh, axis=-1, keepdims=True)
        on = oh * lax.rsqrt(ms + RMS_EPS) * w.gla_norm[...]
        gh = g[:, vs]
        parts.append((on * (gh * jax.nn.sigmoid(gh))).astype(BF16))
    gate_a = jax.nn.sigmoid(_mm(xb, w.w_in[:, off_ga:off_ga + dm.d]) + w.b_gates[:, 0:dm.d])
    branch_a = _mm(jnp.concatenate(parts, axis=1), w.w_ba[...])

    p = lax.broadcasted_iota(jnp.int32, (nseq, seq_len, dm.pgc), 1)
    pooled = []
    for gi, win in enumerate(POOL_WINDOWS):
        cs = slice(gi * dm.pgc, (gi + 1) * dm.pgc)
        cur = ext_s[:, POOL_WMAX:POOL_WMAX + seq_len, cs]
        acc = cur
        for j in range(1, win):
            acc = acc + ext_s[:, POOL_WMAX - j:POOL_WMAX - j + seq_len, cs]
        cnt = jnp.minimum(win, p + (pos0 + 1)).astype(F32)
        pg = (acc / cnt - cur).reshape(t, dm.pgc)
        pg = _mm(pg.astype(BF16), w.w_pg[gi]) * w.pool_scale[:, cs]
        pooled.append(pg.astype(BF16))
    gate_b = jax.nn.sigmoid(_mm(xb, w.w_in[:, off_gb:off_gb + dm.d]) + w.b_gates[:, dm.d:2 * dm.d])
    branch_b = _mm(jnp.concatenate(pooled, axis=1), w.w_bp[...])
    merged = gate_a * branch_a + gate_b * branch_b
    mix = _mm(merged.astype(BF16), w.w_out[...])
    h1 = _layer_norm(dm.alpha * x + mix, w.ln_g[...], w.ln_b[...])
    logits = _mm(h1.astype(BF16), w.w_rt[...]) + w.b_rt[...]
    return h1, logits.T[0:N_EXPERTS, :]


N_MIXW = len(MixW._fields)


def _prompt_mixer_kernel(*refs, dm, tile, n_tiles):
    x_ref = refs[0]
    w = MixW(*refs[1:1 + N_MIXW])
    h1_ref, lg_ref, st_ref, buf_ref, q_s, k_s, v_s, gl_s, b_s, o_s, ext_s, st_t = refs[1 + N_MIXW:]
    lt = pl.program_id(1)

    @pl.when(lt == 0)
    def _():
        st_t[...] = jnp.zeros(st_t.shape, F32)
        ext_s[:, 0:POOL_WMAX, :] = jnp.zeros((1, POOL_WMAX, dm.pw), F32)

    x = x_ref[...]
    xb = x.astype(BF16)
    _project_qkv(xb, w, dm, q_s, k_s, v_s, gl_s)
    _chunk_cumsum(gl_s, b_s, tile, GLA_CHUNK)
    _gla_tile(q_s, k_s, v_s, b_s, o_s, st_t, dm=dm, rows=tile, chunk=GLA_CHUNK, group=GLA_GROUP)

    @pl.when(lt == n_tiles - 1)
    def _():
        for h in range(GLA_HEADS):
            st_ref[0, h] = st_t[h].T

    h1, logits_t = _mixer_tail(x, xb, o_s, ext_s, lt * tile, w, dm, nseq=1, seq_len=tile)
    _to_row_tiles(h1_ref, h1, tile)
    lg_ref[...] = logits_t
    ext_s[:, 0:POOL_WMAX, :] = ext_s[:, tile:tile + POOL_WMAX, :]

    @pl.when(lt == n_tiles - 1)
    def _():
        buf_ref[...] = ext_s[:, 1:POOL_WMAX, :]


def _sample_proj_kernel(*refs, dm):
    x_ref = refs[0]
    w = MixW(*refs[1:1 + N_MIXW])
    q_ref, k_ref, v_ref, gl_ref = refs[1 + N_MIXW:]
    _project_qkv(x_ref[...].astype(BF16), w, dm, q_ref, k_ref, v_ref, gl_ref)


def _sample_gla_kernel(q_ref, k_ref, v_ref, gl_ref, s0_ref, o_ref, st_ref, b_s, *, dm, nseq, seq_len):
    _gla_seqs(q_ref, k_ref, v_ref, gl_ref, b_s, o_ref, s0_ref, st_ref, dm=dm, nseq=nseq, chunk=seq_len,
              unroll=GLA_SEQ_UNROLL)


def _sample_tail_kernel(*refs, dm, nseq, seq_len):
    x_ref, o_ref, hist_ref = refs[0:3]
    w = MixW(*refs[3:3 + N_MIXW])
    h1_ref, lg_ref, buf_ref, ext_s = refs[3 + N_MIXW:]
    ext_s[:, 0:1, :] = jnp.zeros((nseq, 1, dm.pw), F32)
    ext_s[:, 1:POOL_WMAX, :] = hist_ref[...]
    x = x_ref[...]
    h1, logits_t = _mixer_tail(x, x.astype(BF16), o_ref, ext_s, POOL_BUF, w, dm, nseq=nseq, seq_len=seq_len)
    _to_row_tiles(h1_ref, h1, nseq * seq_len)
    lg_ref[...] = logits_t
    buf_ref[...] = ext_s[:, seq_len + 1:seq_len + POOL_WMAX, :]


def _const_spec(arr):
    nd = arr.ndim
    return pl.BlockSpec(arr.shape, lambda *_: (0,) * nd, pipeline_mode=pl.Buffered(1))


def _prompt_mixer(x, w, dm):
    bsz, seq, d = x.shape
    tile = PROMPT_TILE
    n_tiles = seq // tile
    m = bsz * seq
    scratch = [pltpu.VMEM((tile, dm.key), F32), pltpu.VMEM((tile, dm.key), F32), pltpu.VMEM((tile, dm.val), F32),
               pltpu.VMEM((tile, dm.key), F32), pltpu.VMEM((tile, dm.key), F32), pltpu.VMEM((tile, dm.val), F32),
               pltpu.VMEM((1, POOL_WMAX + tile, dm.pw), F32),
               pltpu.VMEM((GLA_HEADS, dm.dv, dm.dk), F32)]
    return pl.pallas_call(
        functools.partial(_prompt_mixer_kernel, dm=dm, tile=tile, n_tiles=n_tiles),
        grid=(bsz, n_tiles),
        in_specs=[pl.BlockSpec((tile, d), lambda b, t: (b * n_tiles + t, 0))] + [_const_spec(a) for a in w],
        out_specs=[pl.BlockSpec((tile * SUBLANES, LANES), lambda b, t: (b * n_tiles + t, 0)),
                   pl.BlockSpec((N_EXPERTS, tile), lambda b, t: (0, b * n_tiles + t)),
                   pl.BlockSpec((1, GLA_HEADS, dm.dk, dm.dv), lambda b, t: (b, 0, 0, 0)),
                   pl.BlockSpec((1, POOL_BUF, dm.pw), lambda b, t: (b, 0, 0))],
        out_shape=[jax.ShapeDtypeStruct((m * SUBLANES, LANES), F32),
                   jax.ShapeDtypeStruct((N_EXPERTS, m), F32),
                   jax.ShapeDtypeStruct((bsz, GLA_HEADS, dm.dk, dm.dv), F32),
                   jax.ShapeDtypeStruct((bsz, POOL_BUF, dm.pw), F32)],
        scratch_shapes=scratch,
        compiler_params=pltpu.CompilerParams(dimension_semantics=("arbitrary", "arbitrary"),
                                             vmem_limit_bytes=VMEM_LIMIT_BYTES),
        name="prompt_mixer",
    )(x.reshape(m, d), *w)


def _sample_mixer(x, s0, hist, w, dm):
    bsz, seq, d = x.shape
    m = bsz * seq
    x2 = x.reshape(m, d)
    tile = MIX_TILE
    cparams = pltpu.CompilerParams(dimension_semantics=("arbitrary",), vmem_limit_bytes=VMEM_LIMIT_BYTES)
    row = lambda n: pl.BlockSpec((tile, n), lambda i: (i, 0))
    q, k, v, gl = pl.pallas_call(
        functools.partial(_sample_proj_kernel, dm=dm),
        grid=(m // tile,),
        in_specs=[row(d)] + [_const_spec(a) for a in w],
        out_specs=[row(dm.key), row(dm.key), row(dm.val), row(dm.key)],
        out_shape=[jax.ShapeDtypeStruct((m, n), F32) for n in (dm.key, dm.key, dm.val, dm.key)],
        compiler_params=cparams,
        name="sample_proj",
    )(x2, *w)

    nseq = GLA_SEQS
    rows = nseq * seq
    grow = lambda n: pl.BlockSpec((rows, n), lambda i: (i, 0))
    st_spec = pl.BlockSpec((nseq, GLA_HEADS, dm.dk, dm.dv), lambda i: (i, 0, 0, 0))
    o, st = pl.pallas_call(
        functools.partial(_sample_gla_kernel, dm=dm, nseq=nseq, seq_len=seq),
        grid=(bsz // nseq,),
        in_specs=[grow(dm.key), grow(dm.key), grow(dm.val), grow(dm.key), st_spec],
        out_specs=[grow(dm.val), st_spec],
        out_shape=[jax.ShapeDtypeStruct((m, dm.val), F32), jax.ShapeDtypeStruct(s0.shape, F32)],
        scratch_shapes=[pltpu.VMEM((rows, dm.key), F32)],
        compiler_params=cparams,
        name="sample_gla",
    )(q, k, v, gl, s0)

    tseq = tile // seq
    hist_spec = pl.BlockSpec((tseq, POOL_BUF, dm.pw), lambda i: (i, 0, 0))
    h1, lg, buf = pl.pallas_call(
        functools.partial(_sample_tail_kernel, dm=dm, nseq=tseq, seq_len=seq),
        grid=(m // tile,),
        in_specs=[row(d), row(dm.val), hist_spec] + [_const_spec(a) for a in w],
        out_specs=[pl.BlockSpec((tile * SUBLANES, LANES), lambda i: (i, 0)),
                   pl.BlockSpec((N_EXPERTS, tile), lambda i: (0, i)), hist_spec],
        out_shape=[jax.ShapeDtypeStruct((m * SUBLANES, LANES), F32),
                   jax.ShapeDtypeStruct((N_EXPERTS, m), F32),
                   jax.ShapeDtypeStruct((bsz, POOL_BUF, dm.pw), F32)],
        scratch_shapes=[pltpu.VMEM((tseq, POOL_WMAX + seq, dm.pw), F32)],
        compiler_params=cparams,
        name="sample_tail",
    )(x2, o, hist, *w)
    return h1, lg, st, buf


def _router_kernel(lga_ref, lgb_ref, w_ref, s_ref, meta_ref, c_ref, carry, *, tile, n_tiles_a):
    i = pl.program_id(0)

    @pl.when(i == 0)
    def _():
        carry[...] = jnp.zeros(carry.shape, F32)

    eio = lax.broadcasted_iota(jnp.int32, (N_EXPERTS, tile), 0)
    work = jnp.where(i < n_tiles_a, lga_ref[...], lgb_ref[...])
    vals, hots = [], []
    for k in range(TOP_K):
        mx = jnp.max(work, axis=0, keepdims=True)
        idx = jnp.min(jnp.where(work == mx, eio, N_EXPERTS), axis=0, keepdims=True)
        hot = eio == idx
        vals.append(mx)
        hots.append(hot)
        work = jnp.where(hot, -jnp.inf, work)
    ex = [jnp.exp(v - vals[0]) for v in vals]
    den = ex[0]
    for k in range(1, TOP_K):
        den = den + ex[k]
    for k in range(TOP_K):
        w_ref[k:k + 1, :] = ex[k] / den
    sel = hots[0]
    for k in range(1, TOP_K):
        sel = sel | hots[k]
    r = lax.broadcasted_iota(jnp.int32, (tile, tile), 0)
    c = lax.broadcasted_iota(jnp.int32, (tile, tile), 1)
    earlier_token = (r < c).astype(BF16)
    rank_in_tile = _mm(sel.astype(BF16), earlier_token)
    cnt = jnp.broadcast_to(jnp.sum(sel.astype(F32), axis=1, keepdims=True), (N_EXPERTS, LANES))
    er = lax.broadcasted_iota(jnp.int32, (N_EXPERTS, N_EXPERTS), 0)
    ec = lax.broadcasted_iota(jnp.int32, (N_EXPERTS, N_EXPERTS), 1)
    lower = (ec < er).astype(BF16)
    cnt_hi = jnp.floor(cnt * (1.0 / 256.0))
    first_slot = _mm(lower, (cnt - 256.0 * cnt_hi).astype(BF16)) + 256.0 * _mm(lower, cnt_hi.astype(BF16))
    slot = rank_in_tile + first_slot[:, 0:1]
    for k in range(TOP_K):
        s_ref[0, k:k + 1, :] = jnp.sum(jnp.where(hots[k], slot, 0.0), axis=0, keepdims=True).astype(jnp.int32)
    meta_ref[0, 0:N_EXPERTS, :] = cnt
    meta_ref[0, N_EXPERTS:2 * N_EXPERTS, :] = first_slot
    meta_ref[0, 2 * N_EXPERTS:3 * N_EXPERTS, :] = carry[...]
    carry[...] = carry[...] + cnt
    c_ref[...] = carry[...]


def _for_each_run(meta, fn):
    def body(e, carry):
        cnt = meta[e]
        first = meta[N_EXPERTS + e]
        dst = meta[2 * N_EXPERTS + e]
        for bit in range(RUN_BITS):
            @pl.when(((cnt >> bit) & 1) == 1)
            def _(bit=bit):
                done = cnt & ((1 << bit) - 1)
                fn(first + done, dst + done, 1 << bit)
        return carry
    lax.fori_loop(0, N_EXPERTS, body, 0)


def _rows(ref, first, n):
    return ref.at[pl.ds(pl.multiple_of(first * SUBLANES, SUBLANES), n * SUBLANES), :]


def _dispatch_kernel(fill_start_ref, fill_cnt_ref, tail_start_ref, meta_hbm, slot_hbm, h1a_ref, h1b_ref, xs_hbm,
                     meta_s0, meta_s1, slot_s0, slot_s1, stage, zeros_s, sem_m, sem_s, sem_r, sem_z,
                     *, tile, n_tiles, n_tiles_a):
    i = pl.program_id(0)
    sl = i % 2
    meta_s = (meta_s0, meta_s1)
    slot_s = (slot_s0, slot_s1)

    def index_copies(blk, s):
        return (pltpu.make_async_copy(meta_hbm.at[blk], meta_s[s], sem_m.at[s]),
                pltpu.make_async_copy(slot_hbm.at[blk], slot_s[s], sem_s.at[s]))

    def wait_stage(s):
        pltpu.make_async_copy(stage.at[s], stage.at[s], sem_r.at[s]).wait()

    @pl.when(i == 0)
    def _():
        for cp in index_copies(0, 0):
            cp.start()
        zeros_s[...] = jnp.zeros(zeros_s.shape, F32)

        def fill_copy(off, bit):
            n = 1 << bit
            return pltpu.make_async_copy(zeros_s.at[pl.ds(0, n * SUBLANES), :], _rows(xs_hbm, off, n), sem_z)

        def for_each_piece(fn):
            def body(e, carry):
                cnt = fill_cnt_ref[e]
                for bit in range(ZERO_FILL_BITS):
                    @pl.when(((cnt >> bit) & 1) == 1)
                    def _(bit=bit):
                        fn(fill_copy(fill_start_ref[e] + (cnt & ((1 << bit) - 1)), bit))
                return carry
            lax.fori_loop(0, N_EXPERTS, body, 0)

        top = ZERO_FILL_BITS - 1
        n_tail = (xs_hbm.shape[0] // SUBLANES - tail_start_ref[0]) >> top

        def for_each_tail_piece(fn):
            def body(j, carry):
                fn(fill_copy(tail_start_ref[0] + (j << top), top))
                return carry
            lax.fori_loop(0, n_tail, body, 0)

        for_each_piece(lambda cp: cp.start())
        for_each_tail_piece(lambda cp: cp.start())
        for_each_piece(lambda cp: cp.wait())
        for_each_tail_piece(lambda cp: cp.wait())

    def group_rows(h1_ref, s_):
        def group(g, carry):
            for u in range(DMA_UNROLL):
                t = g * DMA_UNROLL + u
                row = h1_ref[pl.ds(pl.multiple_of(t * SUBLANES, SUBLANES), SUBLANES), :]
                for k in range(TOP_K):
                    s = slot_s[s_][k * tile + t]
                    stage[s_, pl.ds(pl.multiple_of(s * SUBLANES, SUBLANES), SUBLANES), :] = row
            return carry
        lax.fori_loop(0, tile // DMA_UNROLL, group, 0)

    for s_ in range(2):
        @pl.when((sl == 1 - s_) & (i + 1 < n_tiles))
        def _(s_=s_):
            for cp in index_copies(i + 1, s_):
                cp.start()

    for s_ in range(2):
        @pl.when(sl == s_)
        def _(s_=s_):
            for cp in index_copies(i, s_):
                cp.wait()

            @pl.when(i >= 2)
            def _():
                wait_stage(s_)

            @pl.when(i < n_tiles_a)
            def _():
                group_rows(h1a_ref, s_)

            @pl.when(i >= n_tiles_a)
            def _():
                group_rows(h1b_ref, s_)

            _for_each_run(meta_s[s_], lambda s0, d0, n: pltpu.make_async_copy(
                _rows(stage.at[s_], s0, n), _rows(xs_hbm, d0, n), sem_r.at[s_]).start())

    @pl.when(i == n_tiles - 1)
    def _():
        @pl.when(n_tiles > 1)
        def _():
            wait_stage(1 - sl)
        wait_stage(sl)


def _expert_kernel(be_ref, nused_ref, next_e_ref, xs_ref, wgu_hbm, bgu_ref, wdn_hbm, bdn_ref, out_ref,
                   wgu_f, wdn_f, wgu_b, wdn_b, sem_w, *, rows, d_ff):
    i = pl.program_id(0)

    def weight_copies(e):
        return (pltpu.make_async_copy(wgu_hbm.at[e], wgu_f, sem_w.at[0]),
                pltpu.make_async_copy(wdn_hbm.at[e], wdn_f, sem_w.at[1]))

    @pl.when(i == 0)
    def _():
        for cp in weight_copies(be_ref[0]):
            cp.start()

    @pl.when((i < nused_ref[0]) & ((i == 0) | (be_ref[i] != be_ref[jnp.maximum(i - 1, 0)])))
    def _():
        for cp in weight_copies(be_ref[i]):
            cp.wait()
        wgu_b[...] = wgu_f[...].astype(BF16)
        wdn_b[...] = wdn_f[...].astype(BF16)

        @pl.when(next_e_ref[i] >= 0)
        def _():
            for cp in weight_copies(next_e_ref[i]):
                cp.start()

    @pl.when(i < nused_ref[0])
    def _():
        xb = _from_row_tiles(xs_ref, rows).astype(BF16)
        hh = _mm(xb, wgu_b[...]) + bgu_ref[0]
        gate = jnp.minimum(hh[:, 0:d_ff], SWIGLU_LIMIT)
        up = jnp.clip(hh[:, d_ff:2 * d_ff], -SWIGLU_LIMIT, SWIGLU_LIMIT)
        act = (up + 1.0) * (gate * jax.nn.sigmoid(GLU_ALPHA * gate))
        res = _mm(act.astype(BF16), wdn_b[...]) + bdn_ref[0]
        _to_row_tiles(out_ref, res, rows)

    @pl.when(i >= nused_ref[0])
    def _():
        out_ref[...] = jnp.zeros(out_ref.shape, F32)


def _combine_kernel(meta_hbm, slot_hbm, res_hbm, h1a_ref, h1b_ref, gw_ref, g_ref, b_ref, ya_ref, yb_ref,
                    meta_s0, meta_s1, slot_s0, slot_s1, stage, gbuf, sem_m, sem_s, sem_g,
                    *, tile, alpha, n_tiles, n_tiles_a):
    i = pl.program_id(0)
    sl = i % 2
    meta_s = (meta_s0, meta_s1)
    slot_s = (slot_s0, slot_s1)

    def index_copies(blk, s):
        return (pltpu.make_async_copy(meta_hbm.at[blk], meta_s[s], sem_m.at[s]),
                pltpu.make_async_copy(slot_hbm.at[blk], slot_s[s], sem_s.at[s]))

    def fetch_runs(s):
        _for_each_run(meta_s[s], lambda s0, d0, n: pltpu.make_async_copy(
            _rows(res_hbm, d0, n), _rows(stage.at[s], s0, n), sem_g.at[s]).start())

    @pl.when(i == 0)
    def _():
        for cp in index_copies(0, 0):
            cp.start()
        for cp in index_copies(0, 0):
            cp.wait()
        fetch_runs(0)

        @pl.when(n_tiles > 1)
        def _():
            for cp in index_copies(1, 1):
                cp.start()

    def regroup(s_):
        def group(g, carry):
            for u in range(DMA_UNROLL):
                t = g * DMA_UNROLL + u
                for k in range(TOP_K):
                    s = slot_s[s_][k * tile + t]
                    gbuf[k, pl.ds(pl.multiple_of(t * SUBLANES, SUBLANES), SUBLANES), :] = (
                        stage[s_, pl.ds(pl.multiple_of(s * SUBLANES, SUBLANES), SUBLANES), :])
            return carry
        lax.fori_loop(0, tile // DMA_UNROLL, group, 0)

    for s_ in range(2):
        @pl.when((sl == 1 - s_) & (i + 1 < n_tiles))
        def _(s_=s_):
            for cp in index_copies(i + 1, s_):
                cp.wait()
            fetch_runs(s_)

    for s_ in range(2):
        @pl.when(sl == s_)
        def _(s_=s_):
            pltpu.make_async_copy(stage.at[s_], stage.at[s_], sem_g.at[s_]).wait()
            regroup(s_)

    for s_ in range(2):
        @pl.when((sl == s_) & (i + 2 < n_tiles))
        def _(s_=s_):
            for cp in index_copies(i + 2, s_):
                cp.start()

    gw = gw_ref[...]
    z = alpha * jnp.where(i < n_tiles_a, _from_row_tiles(h1a_ref, tile), _from_row_tiles(h1b_ref, tile))
    for k in range(TOP_K):
        z = z + gw[:, k:k + 1] * _from_row_tiles(gbuf.at[k], tile)
    y = _layer_norm(z, g_ref[...], b_ref[...])

    @pl.when(i < n_tiles_a)
    def _():
        ya_ref[...] = y

    @pl.when(i >= n_tiles_a)
    def _():
        yb_ref[...] = y


def _moe(h1_a, h1_b, logits_a, logits_b, w_gu, b_gu, w_down, b_down, ln_g, ln_b, dm):
    m_a = logits_a.shape[1]
    m = m_a + logits_b.shape[1]
    rows = EXPERT_ROWS
    d_ff = w_down.shape[1]
    tile = MOE_TILE
    n_tiles = m // tile
    n_tiles_a = m_a // tile
    cparams = pltpu.CompilerParams(dimension_semantics=("arbitrary",), vmem_limit_bytes=VMEM_LIMIT_BYTES)

    def two_group_specs(block, lane_axis):
        pick = (lambda j: (0, j)) if lane_axis else (lambda j: (j, 0))
        return [pl.BlockSpec(block, lambda i, *_: pick(jnp.minimum(i, n_tiles_a - 1))),
                pl.BlockSpec(block, lambda i, *_: pick(jnp.maximum(i - n_tiles_a, 0)))]

    kblk = lambda: pl.BlockSpec((TOP_K, tile), lambda i: (0, i))
    gate_w, slot, meta, counts = pl.pallas_call(
        functools.partial(_router_kernel, tile=tile, n_tiles_a=n_tiles_a),
        grid=(n_tiles,),
        in_specs=two_group_specs((N_EXPERTS, tile), True),
        out_specs=[kblk(), pl.BlockSpec((1, TOP_K, tile), lambda i: (i, 0, 0)),
                   pl.BlockSpec((1, 3 * N_EXPERTS, LANES), lambda i: (i, 0, 0)),
                   pl.BlockSpec((N_EXPERTS, LANES), lambda i: (0, 0))],
        out_shape=[jax.ShapeDtypeStruct((TOP_K, m), F32), jax.ShapeDtypeStruct((n_tiles, TOP_K, tile), jnp.int32),
                   jax.ShapeDtypeStruct((n_tiles, 3 * N_EXPERTS, LANES), F32),
                   jax.ShapeDtypeStruct((N_EXPERTS, LANES), F32)],
        scratch_shapes=[pltpu.VMEM((N_EXPERTS, LANES), F32)],
        compiler_params=cparams,
        name="moe_router",
    )(logits_a, logits_b)

    counts = counts[:, 0].astype(jnp.int32)
    padded = (counts + rows - 1) // rows * rows
    pad_end = jnp.cumsum(padded)
    pad_start = pad_end - padded
    n_blocks = -(-(m * TOP_K + N_EXPERTS * (rows - 1)) // rows)
    n_rows = n_blocks * rows
    block_row0 = jnp.arange(n_blocks, dtype=jnp.int32) * rows
    block_e = jnp.minimum(jnp.sum((pad_end[None, :] <= block_row0[:, None]).astype(jnp.int32), axis=1), N_EXPERTS - 1)
    n_used = (pad_end[-1] // rows).astype(jnp.int32).reshape(1)
    meta = meta[:, :, 0].astype(jnp.int32)
    tile_meta = jnp.concatenate([meta[:, 0:2 * N_EXPERTS], meta[:, 2 * N_EXPERTS:] + pad_start[None, :],
                                 jnp.zeros((n_tiles, LANES - 3 * N_EXPERTS), jnp.int32)], axis=1)
    slot_t = slot.reshape(n_tiles, TOP_K * tile)

    h1_specs = two_group_specs((tile * SUBLANES, LANES), False)
    any_spec = pl.BlockSpec(memory_space=pl.ANY)
    index_scratch = [pltpu.SMEM((LANES,), jnp.int32), pltpu.SMEM((LANES,), jnp.int32),
                     pltpu.SMEM((TOP_K * tile,), jnp.int32), pltpu.SMEM((TOP_K * tile,), jnp.int32),
                     pltpu.VMEM((2, TOP_K * tile * SUBLANES, LANES), F32)]
    xs = pl.pallas_call(
        functools.partial(_dispatch_kernel, tile=tile, n_tiles=n_tiles, n_tiles_a=n_tiles_a),
        grid_spec=pltpu.PrefetchScalarGridSpec(
            num_scalar_prefetch=3,
            grid=(n_tiles,),
            in_specs=[any_spec, any_spec] + h1_specs,
            out_specs=any_spec,
            scratch_shapes=index_scratch + [
                pltpu.VMEM(((1 << (ZERO_FILL_BITS - 1)) * SUBLANES, LANES), F32),
                pltpu.SemaphoreType.DMA((2,)), pltpu.SemaphoreType.DMA((2,)), pltpu.SemaphoreType.DMA((2,)),
                pltpu.SemaphoreType.DMA]),
        out_shape=jax.ShapeDtypeStruct((n_rows * SUBLANES, LANES), F32),
        compiler_params=cparams,
        name="moe_dispatch",
    )(pad_start + counts, padded - counts, pad_end[-1:], tile_meta, slot_t, h1_a, h1_b)

    blk = jnp.arange(n_blocks, dtype=jnp.int32)
    run_start = (blk < n_used[0]) & ((blk == 0) | (block_e != jnp.roll(block_e, 1)))
    next_start = jnp.min(jnp.where((blk[None, :] > blk[:, None]) & run_start[None, :], blk[None, :], n_blocks), axis=1)
    next_e = jnp.where(next_start < n_blocks, block_e[jnp.minimum(next_start, n_blocks - 1)], -1).astype(jnp.int32)

    last = lambda i, nu: jnp.minimum(i, nu[0] - 1)
    res = pl.pallas_call(
        functools.partial(_expert_kernel, rows=rows, d_ff=d_ff),
        grid_spec=pltpu.PrefetchScalarGridSpec(
            num_scalar_prefetch=3,
            grid=(n_blocks,),
            in_specs=[pl.BlockSpec((rows * SUBLANES, LANES), lambda i, be, nu, ne: (last(i, nu), 0)),
                      any_spec,
                      pl.BlockSpec((1, 1, 2 * d_ff), lambda i, be, nu, ne: (be[i], 0, 0)),
                      any_spec,
                      pl.BlockSpec((1, 1, dm.d), lambda i, be, nu, ne: (be[i], 0, 0))],
            out_specs=pl.BlockSpec((rows * SUBLANES, LANES), lambda i, be, nu, ne: (i, 0)),
            scratch_shapes=[pltpu.VMEM((dm.d, 2 * d_ff), F32), pltpu.VMEM((d_ff, dm.d), F32),
                            pltpu.VMEM((dm.d, 2 * d_ff), BF16), pltpu.VMEM((d_ff, dm.d), BF16),
                            pltpu.SemaphoreType.DMA((2,))]),
        out_shape=jax.ShapeDtypeStruct((n_rows * SUBLANES, LANES), F32),
        compiler_params=cparams,
        name="moe_experts",
    )(block_e, n_used, next_e, xs, w_gu, b_gu[:, None, :], w_down, b_down[:, None, :])

    ya, yb = pl.pallas_call(
        functools.partial(_combine_kernel, tile=tile, alpha=dm.alpha, n_tiles=n_tiles, n_tiles_a=n_tiles_a),
        grid=(n_tiles,),
        in_specs=[any_spec, any_spec, any_spec] + h1_specs + [
                  pl.BlockSpec((tile, TOP_K), lambda i: (i, 0)),
                  pl.BlockSpec((1, dm.d), lambda i: (0, 0)),
                  pl.BlockSpec((1, dm.d), lambda i: (0, 0))],
        out_specs=[pl.BlockSpec((tile, dm.d), lambda i: (jnp.minimum(i, n_tiles_a - 1), 0)),
                   pl.BlockSpec((tile, dm.d), lambda i: (jnp.maximum(i - n_tiles_a, 0), 0))],
        out_shape=[jax.ShapeDtypeStruct((m_a, dm.d), F32), jax.ShapeDtypeStruct((m - m_a, dm.d), F32)],
        scratch_shapes=index_scratch + [
            pltpu.VMEM((TOP_K, tile * SUBLANES, LANES), F32),
            pltpu.SemaphoreType.DMA((2,)), pltpu.SemaphoreType.DMA((2,)), pltpu.SemaphoreType.DMA((2,))],
        compiler_params=cparams,
        name="moe_combine",
    )(tile_meta, slot_t, res, h1_a, h1_b, gate_w.T, ln_g[None, :], ln_b[None, :])
    return ya, yb


def _pad_cols(a, n):
    return jnp.pad(a, ((0, 0), (0, n - a.shape[1])))


def _regroup_kernel(w_ref, o_ref, *, o_lr, o_tail):
    n_in = w_ref.shape[1]
    n_main = n_in - (o_tail - o_lr)
    o_ref[:, 0:o_lr] = w_ref[:, 0:o_lr].astype(BF16)
    o_ref[:, o_lr:n_main] = w_ref[:, o_tail:n_in].astype(BF16)
    lr = w_ref[:, o_lr:o_tail].astype(BF16)
    o_ref[:, n_main:n_main + LANES] = jnp.concatenate(
        [lr, jnp.zeros((lr.shape[0], LANES - (o_tail - o_lr)), BF16)], axis=1)


def _regroup_w_in(w_in_all, layer, o_lr, o_tail):
    _, d, n_in = w_in_all.shape
    n_out = n_in - (o_tail - o_lr) + LANES
    rows = REGROUP_ROWS
    return pl.pallas_call(
        functools.partial(_regroup_kernel, o_lr=o_lr, o_tail=o_tail),
        grid=(d // rows,),
        in_specs=[pl.BlockSpec((None, rows, n_in), lambda i: (layer, i, 0))],
        out_specs=pl.BlockSpec((rows, n_out), lambda i: (i, 0)),
        out_shape=jax.ShapeDtypeStruct((d, n_out), BF16),
        compiler_params=pltpu.CompilerParams(dimension_semantics=("arbitrary",), vmem_limit_bytes=VMEM_LIMIT_BYTES),
        name="regroup_w_in",
    )(w_in_all)


def _mixer_weights(w_in_all, layer, w_gk2, b_gk, gla_norm_w, w_branch_gla, w_pool_grp, pool_scale, w_branch_pool,
                   b_gates, w_out, ln_g, ln_b, w_router, b_router, dm):
    o_lr = 2 * dm.key + dm.val
    o_tail = o_lr + GATE_RANK
    return MixW(
        w_in=_regroup_w_in(w_in_all, layer, o_lr, o_tail),
        w_gk2=jnp.pad(w_gk2, ((0, LANES - GATE_RANK), (0, 0))).astype(BF16),
        b_gk=b_gk[None, :],
        gla_norm=gla_norm_w[None, :],
        w_ba=w_branch_gla.astype(BF16),
        w_pg=w_pool_grp.astype(BF16),
        pool_scale=pool_scale[None, :],
        w_bp=w_branch_pool.astype(BF16),
        b_gates=b_gates[None, :],
        w_out=w_out.astype(BF16),
        ln_g=ln_g[None, :],
        ln_b=ln_b[None, :],
        w_rt=_pad_cols(w_router, LANES).astype(BF16),
        b_rt=_pad_cols(b_router[None, :], LANES),
    )


def kernel(x_prompt, x_sample, state_gla, state_pool, w_in, w_gk2, b_gk, gla_norm_w, w_branch_gla, w_pool_grp,
           pool_scale, w_branch_pool, b_gates, w_out, ln1_g, ln1_b, w_router, b_router, w_gu, b_gu, w_down, b_down,
           ln2_g, ln2_b):
    depth = w_in.shape[0]
    bp, lp, d = x_prompt.shape
    bs, ls, _ = x_sample.shape
    assert d == SUBLANES * LANES and lp % PROMPT_TILE == 0 and (bs * ls) % MIX_TILE == 0 and MIX_TILE % ls == 0
    assert bs % GLA_SEQS == 0 and ls % SUBLANES == 0
    assert (bp * lp) % MOE_TILE == 0 and (bs * ls) % MOE_TILE == 0
    dm = _dims(d, depth)
    yp, ys = x_prompt, x_sample
    gla_p, pool_p, gla_s, pool_s = [], [], [], []
    for l in range(depth):
        w = _mixer_weights(w_in, l, w_gk2[l], b_gk[l], gla_norm_w[l], w_branch_gla[l], w_pool_grp[l], pool_scale[l],
                           w_branch_pool[l], b_gates[l], w_out[l], ln1_g[l], ln1_b[l], w_router[l], b_router[l], dm)
        h1p, lgp, sp, bufp = _prompt_mixer(yp, w, dm)
        h1s, lgs, ss, bufs = _sample_mixer(ys, state_gla[l], state_pool[l], w, dm)
        yp, ys = _moe(h1p, h1s, lgp, lgs, w_gu[l], b_gu[l], w_down[l], b_down[l], ln2_g[l], ln2_b[l], dm)
        yp = yp.reshape(bp, lp, d)
        ys = ys.reshape(bs, ls, d)
        gla_p.append(sp.astype(state_gla.dtype))
        pool_p.append(bufp.astype(state_pool.dtype))
        gla_s.append(ss.astype(state_gla.dtype))
        pool_s.append(bufs.astype(state_pool.dtype))
    stack = lambda xs: xs[0][None] if len(xs) == 1 else jnp.stack(xs, 0)
    return (yp, ys, stack(gla_p), stack(pool_p), stack(gla_s), stack(pool_s))
```

```python
import functools
from typing import NamedTuple

import jax
import jax.numpy as jnp
from jax import lax
from jax.experimental import pallas as pl
from jax.experimental.pallas import tpu as pltpu

F32 = jnp.float32
BF16 = jnp.bfloat16

GLA_HEADS = 4
GATE_RANK = 16
GATE_NORMALIZER = 16.0
GLA_CHUNK = 64
RMS_EPS = 1e-6
POOL_WINDOWS = (2, 4, 8, 16)
POOL_WMAX = 16
POOL_BUF = POOL_WMAX - 1
N_EXPERTS = 32
TOP_K = 4
SWIGLU_LIMIT = 7.0
GLU_ALPHA = 1.702
LN_EPS = 1e-5

LANES = 128
SUBLANES = 8
VMEM_LIMIT_BYTES = 56 * 1024 * 1024

PROMPT_TILE = 512
MIX_TILE = 256
GLA_GROUP = 128
GLA_SEQS = 8
GLA_SEQ_UNROLL = 8
EXPERT_ROWS = 512
MOE_TILE = 512
CUMSUM_BLOCK = 256
DMA_UNROLL = 8
REGROUP_ROWS = 256
ZERO_FILL_BITS = (EXPERT_ROWS - 1).bit_length()
RUN_BITS = MOE_TILE.bit_length()


class Dims(NamedTuple):
    d: int
    dk: int
    dv: int
    key: int
    val: int
    pw: int
    pgc: int
    alpha: float


def _dims(d, depth):
    return Dims(d=d, dk=d // 8, dv=d // 4, key=d // 2, val=d, pw=d // 2, pgc=d // 8, alpha=(2.0 * depth) ** 0.25)


class MixW(NamedTuple):
    w_in: object
    w_gk2: object
    b_gk: object
    gla_norm: object
    w_ba: object
    w_pg: object
    pool_scale: object
    w_bp: object
    b_gates: object
    w_out: object
    ln_g: object
    ln_b: object
    w_rt: object
    b_rt: object


def _mm(a, b):
    return jnp.dot(a, b, preferred_element_type=F32)


def _layer_norm(z, g, b):
    mu = jnp.mean(z, axis=-1, keepdims=True)
    zc = z - mu
    var = jnp.mean(zc * zc, axis=-1, keepdims=True)
    return zc * lax.rsqrt(var + LN_EPS) * g + b


def _to_row_tiles(ref, val, rows):
    for c in range(SUBLANES):
        ref[pl.ds(c, rows, stride=SUBLANES), :] = val[:, c * LANES:(c + 1) * LANES]


def _from_row_tiles(ref, rows):
    return jnp.concatenate([ref[pl.ds(c, rows, stride=SUBLANES), :] for c in range(SUBLANES)], axis=1)


def _project_qkv(xb_s, w, dm, q_s, k_s, v_s, gl_s):
    xb = xb_s[...]
    off_lr = 2 * dm.key + 2 * dm.val + dm.pw + 2 * dm.d
    lr = _mm(xb, w.w_in[:, off_lr:off_lr + LANES])
    q_s[...] = _mm(xb, w.w_in[:, 0:dm.key]) * (dm.dk ** -0.5)
    gk = _mm(lr.astype(BF16), w.w_gk2[...]) + w.b_gk[...]
    k_s[...] = _mm(xb, w.w_in[:, dm.key:2 * dm.key])
    gl_s[...] = (jnp.minimum(gk, 0.0) - jnp.log1p(jnp.exp(-jnp.abs(gk)))) / GATE_NORMALIZER
    v_s[...] = _mm(xb, w.w_in[:, 2 * dm.key:2 * dm.key + dm.val])


def _chunk_cumsum(gl_s, b_s, rows, chunk):
    blk = min(rows, CUMSUM_BLOCK)
    r = lax.broadcasted_iota(jnp.int32, (blk, blk), 0)
    c = lax.broadcasted_iota(jnp.int32, (blk, blk), 1)
    tri = ((r // chunk == c // chunk) & (c <= r)).astype(BF16)
    for b0 in range(0, rows, blk):
        gl = gl_s[b0:b0 + blk, :]
        hi = gl.astype(BF16)
        lo = (gl - hi.astype(F32)).astype(BF16)
        b_s[b0:b0 + blk, :] = _mm(tri, hi) + _mm(tri, lo)


NT_DIMS = (((1,), (1,)), ((), ()))
TN_DIMS = (((0,), (0,)), ((), ()))


def _gla_tile(q_s, k_s, v_s, b_s, o_s, st_t, *, dm, rows, chunk, group):
    n_chunks = rows // chunk
    gr = lax.broadcasted_iota(jnp.int32, (group, group), 0)
    gc = lax.broadcasted_iota(jnp.int32, (group, group), 1)
    mask = (gr // chunk == gc // chunk) & (gc <= gr)

    def per_chunk_row(b, row):
        return jnp.concatenate([jnp.broadcast_to(b[c * chunk + row:c * chunk + row + 1, :], (chunk, b.shape[1]))
                                for c in range(n_chunks)], axis=0)

    for h in range(GLA_HEADS):
        ks = slice(h * dm.dk, (h + 1) * dm.dk)
        vs = slice(h * dm.dv, (h + 1) * dm.dv)
        b = b_s[:, ks]
        b_ref = per_chunk_row(b, chunk // 2)
        b_last = per_chunk_row(b, chunk - 1)
        q = q_s[:, ks]
        k = k_s[:, ks]
        vb = v_s[:, vs].astype(BF16)
        qa = (q * jnp.exp(b - b_ref)).astype(BF16)
        ka = (k * jnp.exp(b_ref - b)).astype(BF16)
        qe = (q * jnp.exp(b)).astype(BF16)
        kd = (k * jnp.exp(b_last - b)).astype(BF16)
        for g0 in range(0, rows, group):
            a = lax.dot_general(qa[g0:g0 + group], ka[g0:g0 + group], NT_DIMS, preferred_element_type=F32)
            a = jnp.where(mask, a, 0.0)
            o_s[g0:g0 + group, vs] = _mm(a.astype(BF16), vb[g0:g0 + group])
        s_t = st_t[h]
        for c in range(n_chunks):
            r0 = c * chunk
            o_s[r0:r0 + chunk, vs] += lax.dot_general(qe[r0:r0 + chunk], s_t.astype(BF16), NT_DIMS,
                                                      preferred_element_type=F32)
            decay = jnp.exp(b[r0 + chunk - 1:r0 + chunk, :])
            s_t = s_t * decay + lax.dot_general(vb[r0:r0 + chunk], kd[r0:r0 + chunk], TN_DIMS,
                                                preferred_element_type=F32)
        st_t[h] = s_t


def _gla_seqs(q_s, k_s, v_s, gl_s, b_s, o_s, s0_ref, st_ref, *, dm, nseq, chunk, unroll):
    _chunk_cumsum(gl_s, b_s, nseq * chunk, chunk)
    ri = lax.broadcasted_iota(jnp.int32, (chunk, chunk), 0)
    ci = lax.broadcasted_iota(jnp.int32, (chunk, chunk), 1)
    causal = ci <= ri
    nt, tn = NT_DIMS, TN_DIMS

    def one_seq(seq):
        r0 = seq * chunk if isinstance(seq, int) else pl.multiple_of(seq * chunk, chunk)
        for h in range(GLA_HEADS):
            ks = slice(h * dm.dk, (h + 1) * dm.dk)
            vs = slice(h * dm.dv, (h + 1) * dm.dv)
            bc = b_s[pl.ds(r0, chunk), ks]
            b_ref = bc[chunk // 2:chunk // 2 + 1, :]
            b_last = bc[chunk - 1:chunk, :]
            qc = q_s[pl.ds(r0, chunk), ks]
            kc = k_s[pl.ds(r0, chunk), ks]
            vc = v_s[pl.ds(r0, chunk), vs].astype(BF16)
            s_old = s0_ref[seq, h]
            a = lax.dot_general((qc * jnp.exp(bc - b_ref)).astype(BF16), (kc * jnp.exp(b_ref - bc)).astype(BF16),
                                nt, preferred_element_type=F32)
            a = jnp.where(causal, a, 0.0)
            o = _mm(a.astype(BF16), vc) + _mm((qc * jnp.exp(bc)).astype(BF16), s_old.astype(BF16))
            o_s[pl.ds(r0, chunk), vs] = o
            e_col = jnp.broadcast_to(jnp.exp(b_last), (dm.dk, dm.dk)).T
            decay = jnp.concatenate([e_col] * (dm.dv // dm.dk), axis=1)
            kv = lax.dot_general((kc * jnp.exp(b_last - bc)).astype(BF16), vc, tn, preferred_element_type=F32)
            st_ref[seq, h] = decay * s_old + kv

    def step(g, carry):
        for u in range(unroll):
            one_seq(g * unroll + u)
        return carry

    if unroll == nseq:
        for seq in range(nseq):
            one_seq(seq)
    else:
        lax.fori_loop(0, nseq // unroll, step, 0)


def _mixer_tail(x, xb_s, o_s, ext_s, pos0, w, dm, *, nseq, seq_len):
    t = nseq * seq_len
    off_g = 2 * dm.key + dm.val
    off_u = off_g + dm.val
    off_ga = off_u + dm.pw
    off_gb = off_ga + dm.d
    xb = xb_s[...]

    u = _mm(xb, w.w_in[:, off_u:off_u + dm.pw])
    ext_s[:, POOL_WMAX:POOL_WMAX + seq_len, :] = u.reshape(nseq, seq_len, dm.pw)
    g = _mm(xb, w.w_in[:, off_g:off_g + dm.val])
    parts = []
    for h in range(GLA_HEADS):
        vs = slice(h * dm.dv, (h + 1) * dm.dv)
        oh = o_s[:, vs]
        ms = jnp.mean(oh * oh, axis=-1, keepdims=True)
        on = oh * lax.rsqrt(ms + RMS_EPS) * w.gla_norm[...]
        gh = g[:, vs]
        parts.append((on * (gh * jax.nn.sigmoid(gh))).astype(BF16))
    gate_a = jax.nn.sigmoid(_mm(xb, w.w_in[:, off_ga:off_ga + dm.d]) + w.b_gates[:, 0:dm.d])
    branch_a = _mm(jnp.concatenate(parts, axis=1), w.w_ba[...])

    p = lax.broadcasted_iota(jnp.int32, (nseq, seq_len, dm.pgc), 1)
    pooled = []
    for gi, win in enumerate(POOL_WINDOWS):
        cs = slice(gi * dm.pgc, (gi + 1) * dm.pgc)
        cur = ext_s[:, POOL_WMAX:POOL_WMAX + seq_len, cs]
        acc = cur
        for j in range(1, win):
            acc = acc + ext_s[:, POOL_WMAX - j:POOL_WMAX - j + seq_len, cs]
        cnt = jnp.minimum(win, p + (pos0 + 1)).astype(F32)
        pg = (acc / cnt - cur).reshape(t, dm.pgc)
        pg = _mm(pg.astype(BF16), w.w_pg[gi]) * w.pool_scale[:, cs]
        pooled.append(pg.astype(BF16))
    gate_b = jax.nn.sigmoid(_mm(xb, w.w_in[:, off_gb:off_gb + dm.d]) + w.b_gates[:, dm.d:2 * dm.d])
    branch_b = _mm(jnp.concatenate(pooled, axis=1), w.w_bp[...])
    merged = (gate_a * branch_a + gate_b * branch_b).astype(BF16)
    half = t // 2
    mix = [_mm(merged[p * half:(p + 1) * half], w.w_out[...]) for p in range(2)]
    h1, logits = [], []
    for p in range(2):
        h1.append(_layer_norm(dm.alpha * x[p * half:(p + 1) * half] + mix[p], w.ln_g[...], w.ln_b[...]))
        logits.append(_mm(h1[p].astype(BF16), w.w_rt[...]) + w.b_rt[...])
    return jnp.concatenate(h1, axis=0), jnp.concatenate(logits, axis=0).T[0:N_EXPERTS, :]


N_MIXW = len(MixW._fields)


def _prompt_mixer_kernel(*refs, dm, tile, n_tiles):
    x_ref = refs[0]
    w = MixW(*refs[1:1 + N_MIXW])
    h1_ref, lg_ref, st_ref, buf_ref, q_s, k_s, v_s, gl_s, b_s, o_s, ext_s, st_t, xb_s = refs[1 + N_MIXW:]
    lt = pl.program_id(1)

    @pl.when(lt == 0)
    def _():
        st_t[...] = jnp.zeros(st_t.shape, F32)
        ext_s[:, 0:POOL_WMAX, :] = jnp.zeros((1, POOL_WMAX, dm.pw), F32)

    x = x_ref[...]
    xb_s[...] = x.astype(BF16)
    _project_qkv(xb_s, w, dm, q_s, k_s, v_s, gl_s)
    _chunk_cumsum(gl_s, b_s, tile, GLA_CHUNK)
    _gla_tile(q_s, k_s, v_s, b_s, o_s, st_t, dm=dm, rows=tile, chunk=GLA_CHUNK, group=GLA_GROUP)

    @pl.when(lt == n_tiles - 1)
    def _():
        for h in range(GLA_HEADS):
            st_ref[0, h] = st_t[h].T

    h1, logits_t = _mixer_tail(x, xb_s, o_s, ext_s, lt * tile, w, dm, nseq=1, seq_len=tile)
    _to_row_tiles(h1_ref, h1, tile)
    lg_ref[...] = logits_t
    ext_s[:, 0:POOL_WMAX, :] = ext_s[:, tile:tile + POOL_WMAX, :]

    @pl.when(lt == n_tiles - 1)
    def _():
        buf_ref[...] = ext_s[:, 1:POOL_WMAX, :]


def _sample_proj_kernel(*refs, dm):
    x_ref = refs[0]
    w = MixW(*refs[1:1 + N_MIXW])
    q_ref, k_ref, v_ref, gl_ref, xb_s = refs[1 + N_MIXW:]
    xb_s[...] = x_ref[...].astype(BF16)
    _project_qkv(xb_s, w, dm, q_ref, k_ref, v_ref, gl_ref)


def _sample_gla_kernel(q_ref, k_ref, v_ref, gl_ref, s0_ref, o_ref, st_ref, b_s, *, dm, nseq, seq_len):
    _gla_seqs(q_ref, k_ref, v_ref, gl_ref, b_s, o_ref, s0_ref, st_ref, dm=dm, nseq=nseq, chunk=seq_len,
              unroll=GLA_SEQ_UNROLL)


def _sample_tail_kernel(*refs, dm, nseq, seq_len):
    x_ref, o_ref, hist_ref = refs[0:3]
    w = MixW(*refs[3:3 + N_MIXW])
    h1_ref, lg_ref, buf_ref, ext_s, xb_s = refs[3 + N_MIXW:]
    ext_s[:, 0:1, :] = jnp.zeros((nseq, 1, dm.pw), F32)
    ext_s[:, 1:POOL_WMAX, :] = hist_ref[...]
    x = x_ref[...]
    xb_s[...] = x.astype(BF16)
    h1, logits_t = _mixer_tail(x, xb_s, o_ref, ext_s, POOL_BUF, w, dm, nseq=nseq, seq_len=seq_len)
    _to_row_tiles(h1_ref, h1, nseq * seq_len)
    lg_ref[...] = logits_t
    buf_ref[...] = ext_s[:, seq_len + 1:seq_len + POOL_WMAX, :]


def _const_spec(arr):
    nd = arr.ndim
    return pl.BlockSpec(arr.shape, lambda *_: (0,) * nd, pipeline_mode=pl.Buffered(1))


def _prompt_mixer(x, w, dm):
    bsz, seq, d = x.shape
    tile = PROMPT_TILE
    n_tiles = seq // tile
    m = bsz * seq
    scratch = [pltpu.VMEM((tile, dm.key), F32), pltpu.VMEM((tile, dm.key), F32), pltpu.VMEM((tile, dm.val), F32),
               pltpu.VMEM((tile, dm.key), F32), pltpu.VMEM((tile, dm.key), F32), pltpu.VMEM((tile, dm.val), F32),
               pltpu.VMEM((1, POOL_WMAX + tile, dm.pw), F32),
               pltpu.VMEM((GLA_HEADS, dm.dv, dm.dk), F32),
               pltpu.VMEM((tile, d), BF16)]
    return pl.pallas_call(
        functools.partial(_prompt_mixer_kernel, dm=dm, tile=tile, n_tiles=n_tiles),
        grid=(bsz, n_tiles),
        in_specs=[pl.BlockSpec((tile, d), lambda b, t: (b * n_tiles + t, 0))] + [_const_spec(a) for a in w],
        out_specs=[pl.BlockSpec((tile * SUBLANES, LANES), lambda b, t: (b * n_tiles + t, 0)),
                   pl.BlockSpec((N_EXPERTS, tile), lambda b, t: (0, b * n_tiles + t)),
                   pl.BlockSpec((1, GLA_HEADS, dm.dk, dm.dv), lambda b, t: (b, 0, 0, 0)),
                   pl.BlockSpec((1, POOL_BUF, dm.pw), lambda b, t: (b, 0, 0))],
        out_shape=[jax.ShapeDtypeStruct((m * SUBLANES, LANES), F32),
                   jax.ShapeDtypeStruct((N_EXPERTS, m), F32),
                   jax.ShapeDtypeStruct((bsz, GLA_HEADS, dm.dk, dm.dv), F32),
                   jax.ShapeDtypeStruct((bsz, POOL_BUF, dm.pw), F32)],
        scratch_shapes=scratch,
        compiler_params=pltpu.CompilerParams(dimension_semantics=("arbitrary", "arbitrary"),
                                             vmem_limit_bytes=VMEM_LIMIT_BYTES),
        name="prompt_mixer",
    )(x.reshape(m, d), *w)


def _sample_mixer(x, s0, hist, w, dm):
    bsz, seq, d = x.shape
    m = bsz * seq
    x2 = x.reshape(m, d)
    tile = MIX_TILE
    cparams = pltpu.CompilerParams(dimension_semantics=("arbitrary",), vmem_limit_bytes=VMEM_LIMIT_BYTES)
    row = lambda n: pl.BlockSpec((tile, n), lambda i: (i, 0))
    q, k, v, gl = pl.pallas_call(
        functools.partial(_sample_proj_kernel, dm=dm),
        grid=(m // tile,),
        in_specs=[row(d)] + [_const_spec(a) for a in w],
        out_specs=[row(dm.key), row(dm.key), row(dm.val), row(dm.key)],
        out_shape=[jax.ShapeDtypeStruct((m, n), F32) for n in (dm.key, dm.key, dm.val, dm.key)],
        scratch_shapes=[pltpu.VMEM((tile, d), BF16)],
        compiler_params=cparams,
        name="sample_proj",
    )(x2, *w)

    nseq = GLA_SEQS
    rows = nseq * seq
    grow = lambda n: pl.BlockSpec((rows, n), lambda i: (i, 0))
    st_spec = pl.BlockSpec((nseq, GLA_HEADS, dm.dk, dm.dv), lambda i: (i, 0, 0, 0))
    o, st = pl.pallas_call(
        functools.partial(_sample_gla_kernel, dm=dm, nseq=nseq, seq_len=seq),
        grid=(bsz // nseq,),
        in_specs=[grow(dm.key), grow(dm.key), grow(dm.val), grow(dm.key), st_spec],
        out_specs=[grow(dm.val), st_spec],
        out_shape=[jax.ShapeDtypeStruct((m, dm.val), F32), jax.ShapeDtypeStruct(s0.shape, F32)],
        scratch_shapes=[pltpu.VMEM((rows, dm.key), F32)],
        compiler_params=cparams,
        name="sample_gla",
    )(q, k, v, gl, s0)

    tseq = tile // seq
    hist_spec = pl.BlockSpec((tseq, POOL_BUF, dm.pw), lambda i: (i, 0, 0))
    h1, lg, buf = pl.pallas_call(
        functools.partial(_sample_tail_kernel, dm=dm, nseq=tseq, seq_len=seq),
        grid=(m // tile,),
        in_specs=[row(d), row(dm.val), hist_spec] + [_const_spec(a) for a in w],
        out_specs=[pl.BlockSpec((tile * SUBLANES, LANES), lambda i: (i, 0)),
                   pl.BlockSpec((N_EXPERTS, tile), lambda i: (0, i)), hist_spec],
        out_shape=[jax.ShapeDtypeStruct((m * SUBLANES, LANES), F32),
                   jax.ShapeDtypeStruct((N_EXPERTS, m), F32),
                   jax.ShapeDtypeStruct((bsz, POOL_BUF, dm.pw), F32)],
        scratch_shapes=[pltpu.VMEM((tseq, POOL_WMAX + seq, dm.pw), F32), pltpu.VMEM((tile, d), BF16)],
        compiler_params=cparams,
        name="sample_tail",
    )(x2, o, hist, *w)
    return h1, lg, st, buf


def _router_kernel(lga_ref, lgb_ref, w_ref, s_ref, meta_ref, c_ref, carry, *, tile, n_tiles_a):
    i = pl.program_id(0)

    @pl.when(i == 0)
    def _():
        carry[...] = jnp.zeros(carry.shape, F32)

    eio = lax.broadcasted_iota(jnp.int32, (N_EXPERTS, tile), 0)
    work = jnp.where(i < n_tiles_a, lga_ref[...], lgb_ref[...])
    vals, hots = [], []
    for k in range(TOP_K):
        mx = jnp.max(work, axis=0, keepdims=True)
        idx = jnp.min(jnp.where(work == mx, eio, N_EXPERTS), axis=0, keepdims=True)
        hot = eio == idx
        vals.append(mx)
        hots.append(hot)
        work = jnp.where(hot, -jnp.inf, work)
    ex = [jnp.exp(v - vals[0]) for v in vals]
    den = ex[0]
    for k in range(1, TOP_K):
        den = den + ex[k]
    for k in range(TOP_K):
        w_ref[k:k + 1, :] = ex[k] / den
    sel = hots[0]
    for k in range(1, TOP_K):
        sel = sel | hots[k]
    r = lax.broadcasted_iota(jnp.int32, (tile, tile), 0)
    c = lax.broadcasted_iota(jnp.int32, (tile, tile), 1)
    earlier_token = (r < c).astype(BF16)
    rank_in_tile = _mm(sel.astype(BF16), earlier_token)
    cnt = jnp.broadcast_to(jnp.sum(sel.astype(F32), axis=1, keepdims=True), (N_EXPERTS, LANES))
    er = lax.broadcasted_iota(jnp.int32, (N_EXPERTS, N_EXPERTS), 0)
    ec = lax.broadcasted_iota(jnp.int32, (N_EXPERTS, N_EXPERTS), 1)
    lower = (ec < er).astype(BF16)
    cnt_hi = jnp.floor(cnt * (1.0 / 256.0))
    first_slot = _mm(lower, (cnt - 256.0 * cnt_hi).astype(BF16)) + 256.0 * _mm(lower, cnt_hi.astype(BF16))
    slot = rank_in_tile + first_slot[:, 0:1]
    for k in range(TOP_K):
        s_ref[0, k:k + 1, :] = jnp.sum(jnp.where(hots[k], slot, 0.0), axis=0, keepdims=True).astype(jnp.int32)
    meta_ref[0, 0:N_EXPERTS, :] = cnt
    meta_ref[0, N_EXPERTS:2 * N_EXPERTS, :] = first_slot
    meta_ref[0, 2 * N_EXPERTS:3 * N_EXPERTS, :] = carry[...]
    carry[...] = carry[...] + cnt
    c_ref[...] = carry[...]


def _for_each_run(meta, fn):
    def body(e, carry):
        cnt = meta[e]
        first = meta[N_EXPERTS + e]
        dst = meta[2 * N_EXPERTS + e]
        for bit in range(RUN_BITS):
            @pl.when(((cnt >> bit) & 1) == 1)
            def _(bit=bit):
                done = cnt & ((1 << bit) - 1)
                fn(first + done, dst + done, 1 << bit)
        return carry
    lax.fori_loop(0, N_EXPERTS, body, 0)


def _rows(ref, first, n):
    return ref.at[pl.ds(pl.multiple_of(first * SUBLANES, SUBLANES), n * SUBLANES), :]


def _dispatch_kernel(fill_start_ref, fill_cnt_ref, tail_start_ref, meta_hbm, slot_hbm, h1a_ref, h1b_ref, xs_hbm,
                     meta_s0, meta_s1, slot_s0, slot_s1, stage, zeros_s, sem_m, sem_s, sem_r, sem_z,
                     *, tile, n_tiles, n_tiles_a):
    i = pl.program_id(0)
    sl = i % 2
    meta_s = (meta_s0, meta_s1)
    slot_s = (slot_s0, slot_s1)

    def index_copies(blk, s):
        return (pltpu.make_async_copy(meta_hbm.at[blk], meta_s[s], sem_m.at[s]),
                pltpu.make_async_copy(slot_hbm.at[blk], slot_s[s], sem_s.at[s]))

    def wait_stage(s):
        pltpu.make_async_copy(stage.at[s], stage.at[s], sem_r.at[s]).wait()

    @pl.when(i == 0)
    def _():
        for cp in index_copies(0, 0):
            cp.start()
        zeros_s[...] = jnp.zeros(zeros_s.shape, F32)

        def fill_copy(off, bit):
            n = 1 << bit
            return pltpu.make_async_copy(zeros_s.at[pl.ds(0, n * SUBLANES), :], _rows(xs_hbm, off, n), sem_z)

        def for_each_piece(fn):
            def body(e, carry):
                cnt = fill_cnt_ref[e]
                for bit in range(ZERO_FILL_BITS):
                    @pl.when(((cnt >> bit) & 1) == 1)
                    def _(bit=bit):
                        fn(fill_copy(fill_start_ref[e] + (cnt & ((1 << bit) - 1)), bit))
                return carry
            lax.fori_loop(0, N_EXPERTS, body, 0)

        top = ZERO_FILL_BITS - 1
        n_tail = (xs_hbm.shape[0] // SUBLANES - tail_start_ref[0]) >> top

        def for_each_tail_piece(fn):
            def body(j, carry):
                fn(fill_copy(tail_start_ref[0] + (j << top), top))
                return carry
            lax.fori_loop(0, n_tail, body, 0)

        for_each_piece(lambda cp: cp.start())
        for_each_tail_piece(lambda cp: cp.start())
        for_each_piece(lambda cp: cp.wait())
        for_each_tail_piece(lambda cp: cp.wait())

    def group_rows(h1_ref, s_):
        def group(g, carry):
            for u in range(DMA_UNROLL):
                t = g * DMA_UNROLL + u
                row = h1_ref[pl.ds(pl.multiple_of(t * SUBLANES, SUBLANES), SUBLANES), :]
                for k in range(TOP_K):
                    s = slot_s[s_][k * tile + t]
                    stage[s_, pl.ds(pl.multiple_of(s * SUBLANES, SUBLANES), SUBLANES), :] = row
            return carry
        lax.fori_loop(0, tile // DMA_UNROLL, group, 0)

    for s_ in range(2):
        @pl.when((sl == 1 - s_) & (i + 1 < n_tiles))
        def _(s_=s_):
            for cp in index_copies(i + 1, s_):
                cp.start()

    for s_ in range(2):
        @pl.when(sl == s_)
        def _(s_=s_):
            for cp in index_copies(i, s_):
                cp.wait()

            @pl.when(i >= 2)
            def _():
                wait_stage(s_)

            @pl.when(i < n_tiles_a)
            def _():
                group_rows(h1a_ref, s_)

            @pl.when(i >= n_tiles_a)
            def _():
                group_rows(h1b_ref, s_)

            _for_each_run(meta_s[s_], lambda s0, d0, n: pltpu.make_async_copy(
                _rows(stage.at[s_], s0, n), _rows(xs_hbm, d0, n), sem_r.at[s_]).start())

    @pl.when(i == n_tiles - 1)
    def _():
        @pl.when(n_tiles > 1)
        def _():
            wait_stage(1 - sl)
        wait_stage(sl)


def _expert_kernel(be_ref, nused_ref, next_e_ref, valid_ref, xs_ref, wgu_hbm, bgu_ref, wdn_hbm, bdn_ref, out_ref,
                   wgu_f, wdn_f, wgu_b, wdn_b, sem_w, *, rows, d_ff):
    i = pl.program_id(0)

    def weight_copies(e):
        return (pltpu.make_async_copy(wgu_hbm.at[e], wgu_f, sem_w.at[0]),
                pltpu.make_async_copy(wdn_hbm.at[e], wdn_f, sem_w.at[1]))

    @pl.when(i == 0)
    def _():
        for cp in weight_copies(be_ref[0]):
            cp.start()

    @pl.when((i < nused_ref[0]) & ((i == 0) | (be_ref[i] != be_ref[jnp.maximum(i - 1, 0)])))
    def _():
        for cp in weight_copies(be_ref[i]):
            cp.wait()
        wgu_b[...] = wgu_f[...].astype(BF16)
        wdn_b[...] = wdn_f[...].astype(BF16)

        @pl.when(next_e_ref[i] >= 0)
        def _():
            for cp in weight_copies(next_e_ref[i]):
                cp.start()

    def ffn(n):
        head = lambda ref: ref.at[pl.ds(0, n * SUBLANES), :]
        xb = _from_row_tiles(head(xs_ref), n).astype(BF16)
        hh = _mm(xb, wgu_b[...]) + bgu_ref[0]
        gate = jnp.minimum(hh[:, 0:d_ff], SWIGLU_LIMIT)
        up = jnp.clip(hh[:, d_ff:2 * d_ff], -SWIGLU_LIMIT, SWIGLU_LIMIT)
        act = (up + 1.0) * (gate * jax.nn.sigmoid(GLU_ALPHA * gate))
        res = _mm(act.astype(BF16), wdn_b[...]) + bdn_ref[0]
        _to_row_tiles(head(out_ref), res, n)

    half = rows // 2

    @pl.when((i < nused_ref[0]) & (valid_ref[i] > half))
    def _():
        ffn(rows)

    @pl.when((i < nused_ref[0]) & (valid_ref[i] <= half))
    def _():
        ffn(half)
        out_ref[pl.ds(half * SUBLANES, half * SUBLANES), :] = jnp.zeros((half * SUBLANES, LANES), F32)

    @pl.when(i >= nused_ref[0])
    def _():
        out_ref[...] = jnp.zeros(out_ref.shape, F32)


def _combine_kernel(meta_hbm, slot_hbm, res_hbm, h1a_ref, h1b_ref, gw_ref, g_ref, b_ref, ya_ref, yb_ref,
                    meta_s0, meta_s1, slot_s0, slot_s1, stage, gbuf, sem_m, sem_s, sem_g,
                    *, tile, alpha, n_tiles, n_tiles_a):
    i = pl.program_id(0)
    sl = i % 2
    meta_s = (meta_s0, meta_s1)
    slot_s = (slot_s0, slot_s1)

    def index_copies(blk, s):
        return (pltpu.make_async_copy(meta_hbm.at[blk], meta_s[s], sem_m.at[s]),
                pltpu.make_async_copy(slot_hbm.at[blk], slot_s[s], sem_s.at[s]))

    def fetch_runs(s):
        _for_each_run(meta_s[s], lambda s0, d0, n: pltpu.make_async_copy(
            _rows(res_hbm, d0, n), _rows(stage.at[s], s0, n), sem_g.at[s]).start())

    @pl.when(i == 0)
    def _():
        for cp in index_copies(0, 0):
            cp.start()
        for cp in index_copies(0, 0):
            cp.wait()
        fetch_runs(0)

        @pl.when(n_tiles > 1)
        def _():
            for cp in index_copies(1, 1):
                cp.start()

    def regroup(s_):
        def group(g, carry):
            for u in range(DMA_UNROLL):
                t = g * DMA_UNROLL + u
                for k in range(TOP_K):
                    s = slot_s[s_][k * tile + t]
                    gbuf[k, pl.ds(pl.multiple_of(t * SUBLANES, SUBLANES), SUBLANES), :] = (
                        stage[s_, pl.ds(pl.multiple_of(s * SUBLANES, SUBLANES), SUBLANES), :])
            return carry
        lax.fori_loop(0, tile // DMA_UNROLL, group, 0)

    for s_ in range(2):
        @pl.when((sl == 1 - s_) & (i + 1 < n_tiles))
        def _(s_=s_):
            for cp in index_copies(i + 1, s_):
                cp.wait()
            fetch_runs(s_)

    for s_ in range(2):
        @pl.when(sl == s_)
        def _(s_=s_):
            pltpu.make_async_copy(stage.at[s_], stage.at[s_], sem_g.at[s_]).wait()
            regroup(s_)

    for s_ in range(2):
        @pl.when((sl == s_) & (i + 2 < n_tiles))
        def _(s_=s_):
            for cp in index_copies(i + 2, s_):
                cp.start()

    gw = gw_ref[...]
    z = alpha * jnp.where(i < n_tiles_a, _from_row_tiles(h1a_ref, tile), _from_row_tiles(h1b_ref, tile))
    for k in range(TOP_K):
        z = z + gw[:, k:k + 1] * _from_row_tiles(gbuf.at[k], tile)
    y = _layer_norm(z, g_ref[...], b_ref[...])

    @pl.when(i < n_tiles_a)
    def _():
        ya_ref[...] = y

    @pl.when(i >= n_tiles_a)
    def _():
        yb_ref[...] = y


def _moe(h1_a, h1_b, logits_a, logits_b, w_gu, b_gu, w_down, b_down, ln_g, ln_b, dm):
    m_a = logits_a.shape[1]
    m = m_a + logits_b.shape[1]
    rows = EXPERT_ROWS
    d_ff = w_down.shape[1]
    tile = MOE_TILE
    n_tiles = m // tile
    n_tiles_a = m_a // tile
    cparams = pltpu.CompilerParams(dimension_semantics=("arbitrary",), vmem_limit_bytes=VMEM_LIMIT_BYTES)

    def two_group_specs(block, lane_axis):
        pick = (lambda j: (0, j)) if lane_axis else (lambda j: (j, 0))
        return [pl.BlockSpec(block, lambda i, *_: pick(jnp.minimum(i, n_tiles_a - 1))),
                pl.BlockSpec(block, lambda i, *_: pick(jnp.maximum(i - n_tiles_a, 0)))]

    kblk = lambda: pl.BlockSpec((TOP_K, tile), lambda i: (0, i))
    gate_w, slot, meta, counts = pl.pallas_call(
        functools.partial(_router_kernel, tile=tile, n_tiles_a=n_tiles_a),
        grid=(n_tiles,),
        in_specs=two_group_specs((N_EXPERTS, tile), True),
        out_specs=[kblk(), pl.BlockSpec((1, TOP_K, tile), lambda i: (i, 0, 0)),
                   pl.BlockSpec((1, 3 * N_EXPERTS, LANES), lambda i: (i, 0, 0)),
                   pl.BlockSpec((N_EXPERTS, LANES), lambda i: (0, 0))],
        out_shape=[jax.ShapeDtypeStruct((TOP_K, m), F32), jax.ShapeDtypeStruct((n_tiles, TOP_K, tile), jnp.int32),
                   jax.ShapeDtypeStruct((n_tiles, 3 * N_EXPERTS, LANES), F32),
                   jax.ShapeDtypeStruct((N_EXPERTS, LANES), F32)],
        scratch_shapes=[pltpu.VMEM((N_EXPERTS, LANES), F32)],
        compiler_params=cparams,
        name="moe_router",
    )(logits_a, logits_b)

    counts = counts[:, 0].astype(jnp.int32)
    padded = (counts + rows - 1) // rows * rows
    pad_end = jnp.cumsum(padded)
    pad_start = pad_end - padded
    n_blocks = -(-(m * TOP_K + N_EXPERTS * (rows - 1)) // rows)
    n_rows = n_blocks * rows
    block_row0 = jnp.arange(n_blocks, dtype=jnp.int32) * rows
    block_e = jnp.minimum(jnp.sum((pad_end[None, :] <= block_row0[:, None]).astype(jnp.int32), axis=1), N_EXPERTS - 1)
    n_used = (pad_end[-1] // rows).astype(jnp.int32).reshape(1)
    meta = meta[:, :, 0].astype(jnp.int32)
    tile_meta = jnp.concatenate([meta[:, 0:2 * N_EXPERTS], meta[:, 2 * N_EXPERTS:] + pad_start[None, :],
                                 jnp.zeros((n_tiles, LANES - 3 * N_EXPERTS), jnp.int32)], axis=1)
    slot_t = slot.reshape(n_tiles, TOP_K * tile)

    h1_specs = two_group_specs((tile * SUBLANES, LANES), False)
    any_spec = pl.BlockSpec(memory_space=pl.ANY)
    index_scratch = [pltpu.SMEM((LANES,), jnp.int32), pltpu.SMEM((LANES,), jnp.int32),
                     pltpu.SMEM((TOP_K * tile,), jnp.int32), pltpu.SMEM((TOP_K * tile,), jnp.int32),
                     pltpu.VMEM((2, TOP_K * tile * SUBLANES, LANES), F32)]
    xs = pl.pallas_call(
        functools.partial(_dispatch_kernel, tile=tile, n_tiles=n_tiles, n_tiles_a=n_tiles_a),
        grid_spec=pltpu.PrefetchScalarGridSpec(
            num_scalar_prefetch=3,
            grid=(n_tiles,),
            in_specs=[any_spec, any_spec] + h1_specs,
            out_specs=any_spec,
            scratch_shapes=index_scratch + [
                pltpu.VMEM(((1 << (ZERO_FILL_BITS - 1)) * SUBLANES, LANES), F32),
                pltpu.SemaphoreType.DMA((2,)), pltpu.SemaphoreType.DMA((2,)), pltpu.SemaphoreType.DMA((2,)),
                pltpu.SemaphoreType.DMA]),
        out_shape=jax.ShapeDtypeStruct((n_rows * SUBLANES, LANES), F32),
        compiler_params=cparams,
        name="moe_dispatch",
    )(pad_start + counts, padded - counts, pad_end[-1:], tile_meta, slot_t, h1_a, h1_b)

    blk = jnp.arange(n_blocks, dtype=jnp.int32)
    run_start = (blk < n_used[0]) & ((blk == 0) | (block_e != jnp.roll(block_e, 1)))
    next_start = jnp.min(jnp.where((blk[None, :] > blk[:, None]) & run_start[None, :], blk[None, :], n_blocks), axis=1)
    next_e = jnp.where(next_start < n_blocks, block_e[jnp.minimum(next_start, n_blocks - 1)], -1).astype(jnp.int32)

    block_valid = jnp.clip(counts[block_e] - (block_row0 - pad_start[block_e]), 0, rows).astype(jnp.int32)

    last = lambda i, nu: jnp.minimum(i, nu[0] - 1)
    res = pl.pallas_call(
        functools.partial(_expert_kernel, rows=rows, d_ff=d_ff),
        grid_spec=pltpu.PrefetchScalarGridSpec(
            num_scalar_prefetch=4,
            grid=(n_blocks,),
            in_specs=[pl.BlockSpec((rows * SUBLANES, LANES), lambda i, be, nu, ne, bv: (last(i, nu), 0)),
                      any_spec,
                      pl.BlockSpec((1, 1, 2 * d_ff), lambda i, be, nu, ne, bv: (be[i], 0, 0)),
                      any_spec,
                      pl.BlockSpec((1, 1, dm.d), lambda i, be, nu, ne, bv: (be[i], 0, 0))],
            out_specs=pl.BlockSpec((rows * SUBLANES, LANES), lambda i, be, nu, ne, bv: (i, 0)),
            scratch_shapes=[pltpu.VMEM((dm.d, 2 * d_ff), F32), pltpu.VMEM((d_ff, dm.d), F32),
                            pltpu.VMEM((dm.d, 2 * d_ff), BF16), pltpu.VMEM((d_ff, dm.d), BF16),
                            pltpu.SemaphoreType.DMA((2,))]),
        out_shape=jax.ShapeDtypeStruct((n_rows * SUBLANES, LANES), F32),
        compiler_params=cparams,
        name="moe_experts",
    )(block_e, n_used, next_e, block_valid, xs, w_gu, b_gu[:, None, :], w_down, b_down[:, None, :])

    ya, yb = pl.pallas_call(
        functools.partial(_combine_kernel, tile=tile, alpha=dm.alpha, n_tiles=n_tiles, n_tiles_a=n_tiles_a),
        grid=(n_tiles,),
        in_specs=[any_spec, any_spec, any_spec] + h1_specs + [
                  pl.BlockSpec((tile, TOP_K), lambda i: (i, 0)),
                  pl.BlockSpec((1, dm.d), lambda i: (0, 0)),
                  pl.BlockSpec((1, dm.d), lambda i: (0, 0))],
        out_specs=[pl.BlockSpec((tile, dm.d), lambda i: (jnp.minimum(i, n_tiles_a - 1), 0)),
                   pl.BlockSpec((tile, dm.d), lambda i: (jnp.maximum(i - n_tiles_a, 0), 0))],
        out_shape=[jax.ShapeDtypeStruct((m_a, dm.d), F32), jax.ShapeDtypeStruct((m - m_a, dm.d), F32)],
        scratch_shapes=index_scratch + [
            pltpu.VMEM((TOP_K, tile * SUBLANES, LANES), F32),
            pltpu.SemaphoreType.DMA((2,)), pltpu.SemaphoreType.DMA((2,)), pltpu.SemaphoreType.DMA((2,))],
        compiler_params=cparams,
        name="moe_combine",
    )(tile_meta, slot_t, res, h1_a, h1_b, gate_w.T, ln_g[None, :], ln_b[None, :])
    return ya, yb


def _pad_cols(a, n):
    return jnp.pad(a, ((0, 0), (0, n - a.shape[1])))


def _regroup_kernel(w_ref, o_ref, *, o_lr, o_tail):
    n_in = w_ref.shape[1]
    n_main = n_in - (o_tail - o_lr)
    o_ref[:, 0:o_lr] = w_ref[:, 0:o_lr].astype(BF16)
    o_ref[:, o_lr:n_main] = w_ref[:, o_tail:n_in].astype(BF16)
    lr = w_ref[:, o_lr:o_tail].astype(BF16)
    o_ref[:, n_main:n_main + LANES] = jnp.concatenate(
        [lr, jnp.zeros((lr.shape[0], LANES - (o_tail - o_lr)), BF16)], axis=1)


def _regroup_w_in(w_in_all, layer, o_lr, o_tail):
    _, d, n_in = w_in_all.shape
    n_out = n_in - (o_tail - o_lr) + LANES
    rows = REGROUP_ROWS
    return pl.pallas_call(
        functools.partial(_regroup_kernel, o_lr=o_lr, o_tail=o_tail),
        grid=(d // rows,),
        in_specs=[pl.BlockSpec((None, rows, n_in), lambda i: (layer, i, 0))],
        out_specs=pl.BlockSpec((rows, n_out), lambda i: (i, 0)),
        out_shape=jax.ShapeDtypeStruct((d, n_out), BF16),
        compiler_params=pltpu.CompilerParams(dimension_semantics=("arbitrary",), vmem_limit_bytes=VMEM_LIMIT_BYTES),
        name="regroup_w_in",
    )(w_in_all)


def _mixer_weights(w_in_all, layer, w_gk2, b_gk, gla_norm_w, w_branch_gla, w_pool_grp, pool_scale, w_branch_pool,
                   b_gates, w_out, ln_g, ln_b, w_router, b_router, dm):
    o_lr = 2 * dm.key + dm.val
    o_tail = o_lr + GATE_RANK
    return MixW(
        w_in=_regroup_w_in(w_in_all, layer, o_lr, o_tail),
        w_gk2=jnp.pad(w_gk2, ((0, LANES - GATE_RANK), (0, 0))).astype(BF16),
        b_gk=b_gk[None, :],
        gla_norm=gla_norm_w[None, :],
        w_ba=w_branch_gla.astype(BF16),
        w_pg=w_pool_grp.astype(BF16),
        pool_scale=pool_scale[None, :],
        w_bp=w_branch_pool.astype(BF16),
        b_gates=b_gates[None, :],
        w_out=w_out.astype(BF16),
        ln_g=ln_g[None, :],
        ln_b=ln_b[None, :],
        w_rt=_pad_cols(w_router, LANES).astype(BF16),
        b_rt=_pad_cols(b_router[None, :], LANES),
    )


def kernel(x_prompt, x_sample, state_gla, state_pool, w_in, w_gk2, b_gk, gla_norm_w, w_branch_gla, w_pool_grp,
           pool_scale, w_branch_pool, b_gates, w_out, ln1_g, ln1_b, w_router, b_router, w_gu, b_gu, w_down, b_down,
           ln2_g, ln2_b):
    depth = w_in.shape[0]
    bp, lp, d = x_prompt.shape
    bs, ls, _ = x_sample.shape
    assert d == SUBLANES * LANES and lp % PROMPT_TILE == 0 and (bs * ls) % MIX_TILE == 0 and MIX_TILE % ls == 0
    assert bs % GLA_SEQS == 0 and ls % SUBLANES == 0
    assert (bp * lp) % MOE_TILE == 0 and (bs * ls) % MOE_TILE == 0
    dm = _dims(d, depth)
    yp, ys = x_prompt, x_sample
    gla_p, pool_p, gla_s, pool_s = [], [], [], []
    for l in range(depth):
        w = _mixer_weights(w_in, l, w_gk2[l], b_gk[l], gla_norm_w[l], w_branch_gla[l], w_pool_grp[l], pool_scale[l],
                           w_branch_pool[l], b_gates[l], w_out[l], ln1_g[l], ln1_b[l], w_router[l], b_router[l], dm)
        h1p, lgp, sp, bufp = _prompt_mixer(yp, w, dm)
        h1s, lgs, ss, bufs = _sample_mixer(ys, state_gla[l], state_pool[l], w, dm)
        yp, ys = _moe(h1p, h1s, lgp, lgs, w_gu[l], b_gu[l], w_down[l], b_down[l], ln2_g[l], ln2_b[l], dm)
        yp = yp.reshape(bp, lp, d)
        ys = ys.reshape(bs, ls, d)
        gla_p.append(sp.astype(state_gla.dtype))
        pool_p.append(bufp.astype(state_pool.dtype))
        gla_s.append(ss.astype(state_gla.dtype))
        pool_s.append(bufs.astype(state_pool.dtype))
    stack = lambda xs: xs[0][None] if len(xs) == 1 else jnp.stack(xs, 0)
    return (yp, ys, stack(gla_p), stack(pool_p), stack(gla_s), stack(pool_s))
```

```python
import functools
from typing import NamedTuple

import jax
import jax.numpy as jnp
from jax import lax
from jax.experimental import pallas as pl
from jax.experimental.pallas import tpu as pltpu

F32 = jnp.float32
BF16 = jnp.bfloat16

GLA_HEADS = 4
GATE_RANK = 16
GATE_NORMALIZER = 16.0
GLA_CHUNK = 64
RMS_EPS = 1e-6
POOL_WINDOWS = (2, 4, 8, 16)
POOL_WMAX = 16
POOL_BUF = POOL_WMAX - 1
N_EXPERTS = 32
TOP_K = 4
SWIGLU_LIMIT = 7.0
GLU_ALPHA = 1.702
LN_EPS = 1e-5

LANES = 128
SUBLANES = 8
VMEM_LIMIT_BYTES = 56 * 1024 * 1024

PROMPT_TILE = 512
MIX_TILE = 256
GLA_GROUP = 128
GLA_SEQS = 8
GLA_SEQ_UNROLL = 8
EXPERT_ROWS = 512
MOE_TILE = 512
CUMSUM_BLOCK = 256
DMA_UNROLL = 8
REGROUP_ROWS = 256
ZERO_FILL_BITS = (EXPERT_ROWS - 1).bit_length()
RUN_BITS = MOE_TILE.bit_length()


class Dims(NamedTuple):
    d: int
    dk: int
    dv: int
    key: int
    val: int
    pw: int
    pgc: int
    alpha: float


def _dims(d, depth):
    return Dims(d=d, dk=d // 8, dv=d // 4, key=d // 2, val=d, pw=d // 2, pgc=d // 8, alpha=(2.0 * depth) ** 0.25)


class MixW(NamedTuple):
    w_in: object
    w_gk2: object
    b_gk: object
    gla_norm: object
    w_ba: object
    w_pg: object
    pool_scale: object
    w_bp: object
    b_gates: object
    w_out: object
    ln_g: object
    ln_b: object
    w_rt: object
    b_rt: object


def _mm(a, b):
    return jnp.dot(a, b, preferred_element_type=F32)


def _layer_norm(z, g, b):
    mu = jnp.mean(z, axis=-1, keepdims=True)
    zc = z - mu
    var = jnp.mean(zc * zc, axis=-1, keepdims=True)
    return zc * lax.rsqrt(var + LN_EPS) * g + b


def _to_row_tiles(ref, val, rows):
    for c in range(SUBLANES):
        ref[pl.ds(c, rows, stride=SUBLANES), :] = val[:, c * LANES:(c + 1) * LANES]


def _from_row_tiles(ref, rows):
    return jnp.concatenate([ref[pl.ds(c, rows, stride=SUBLANES), :] for c in range(SUBLANES)], axis=1)


def _project_qkv(xb_s, w, dm, q_s, k_s, v_s, gl_s):
    xb = xb_s[...]
    off_lr = 2 * dm.key + 2 * dm.val + dm.pw + 2 * dm.d
    lr = _mm(xb, w.w_in[:, off_lr:off_lr + LANES])
    q_s[...] = _mm(xb, w.w_in[:, 0:dm.key]) * (dm.dk ** -0.5)
    gk = _mm(lr.astype(BF16), w.w_gk2[...]) + w.b_gk[...]
    k_s[...] = _mm(xb, w.w_in[:, dm.key:2 * dm.key])
    gl_s[...] = (jnp.minimum(gk, 0.0) - jnp.log1p(jnp.exp(-jnp.abs(gk)))) / GATE_NORMALIZER
    v_s[...] = _mm(xb, w.w_in[:, 2 * dm.key:2 * dm.key + dm.val])


def _chunk_cumsum(gl_s, b_s, rows, chunk):
    blk = min(rows, CUMSUM_BLOCK)
    r = lax.broadcasted_iota(jnp.int32, (blk, blk), 0)
    c = lax.broadcasted_iota(jnp.int32, (blk, blk), 1)
    tri = ((r // chunk == c // chunk) & (c <= r)).astype(BF16)
    for b0 in range(0, rows, blk):
        gl = gl_s[b0:b0 + blk, :]
        hi = gl.astype(BF16)
        lo = (gl - hi.astype(F32)).astype(BF16)
        b_s[b0:b0 + blk, :] = _mm(tri, hi) + _mm(tri, lo)


NT_DIMS = (((1,), (1,)), ((), ()))
TN_DIMS = (((0,), (0,)), ((), ()))


def _gla_tile(q_s, k_s, v_s, b_s, o_s, st_t, *, dm, rows, chunk, group):
    n_chunks = rows // chunk
    gr = lax.broadcasted_iota(jnp.int32, (group, group), 0)
    gc = lax.broadcasted_iota(jnp.int32, (group, group), 1)
    mask = (gr // chunk == gc // chunk) & (gc <= gr)

    def per_chunk_row(b, row):
        return jnp.concatenate([jnp.broadcast_to(b[c * chunk + row:c * chunk + row + 1, :], (chunk, b.shape[1]))
                                for c in range(n_chunks)], axis=0)

    for h in range(GLA_HEADS):
        ks = slice(h * dm.dk, (h + 1) * dm.dk)
        vs = slice(h * dm.dv, (h + 1) * dm.dv)
        b = b_s[:, ks]
        b_ref = per_chunk_row(b, chunk // 2)
        b_last = per_chunk_row(b, chunk - 1)
        q = q_s[:, ks]
        k = k_s[:, ks]
        vb = v_s[:, vs].astype(BF16)
        qa = (q * jnp.exp(b - b_ref)).astype(BF16)
        ka = (k * jnp.exp(b_ref - b)).astype(BF16)
        qe = (q * jnp.exp(b)).astype(BF16)
        kd = (k * jnp.exp(b_last - b)).astype(BF16)
        for g0 in range(0, rows, group):
            a = lax.dot_general(qa[g0:g0 + group], ka[g0:g0 + group], NT_DIMS, preferred_element_type=F32)
            a = jnp.where(mask, a, 0.0)
            o_s[g0:g0 + group, vs] = _mm(a.astype(BF16), vb[g0:g0 + group])
        s_t = st_t[h]
        for c in range(n_chunks):
            r0 = c * chunk
            o_s[r0:r0 + chunk, vs] += lax.dot_general(qe[r0:r0 + chunk], s_t.astype(BF16), NT_DIMS,
                                                      preferred_element_type=F32)
            decay = jnp.exp(b[r0 + chunk - 1:r0 + chunk, :])
            s_t = s_t * decay + lax.dot_general(vb[r0:r0 + chunk], kd[r0:r0 + chunk], TN_DIMS,
                                                preferred_element_type=F32)
        st_t[h] = s_t


def _gla_seqs(q_s, k_s, v_s, gl_s, b_s, o_s, s0_ref, st_ref, *, dm, nseq, chunk, unroll):
    _chunk_cumsum(gl_s, b_s, nseq * chunk, chunk)
    ri = lax.broadcasted_iota(jnp.int32, (chunk, chunk), 0)
    ci = lax.broadcasted_iota(jnp.int32, (chunk, chunk), 1)
    causal = ci <= ri
    nt, tn = NT_DIMS, TN_DIMS

    def one_seq(seq):
        r0 = seq * chunk if isinstance(seq, int) else pl.multiple_of(seq * chunk, chunk)
        for h in range(GLA_HEADS):
            ks = slice(h * dm.dk, (h + 1) * dm.dk)
            vs = slice(h * dm.dv, (h + 1) * dm.dv)
            bc = b_s[pl.ds(r0, chunk), ks]
            b_ref = bc[chunk // 2:chunk // 2 + 1, :]
            b_last = bc[chunk - 1:chunk, :]
            qc = q_s[pl.ds(r0, chunk), ks]
            kc = k_s[pl.ds(r0, chunk), ks]
            vc = v_s[pl.ds(r0, chunk), vs].astype(BF16)
            s_old = s0_ref[seq, h]
            a = lax.dot_general((qc * jnp.exp(bc - b_ref)).astype(BF16), (kc * jnp.exp(b_ref - bc)).astype(BF16),
                                nt, preferred_element_type=F32)
            a = jnp.where(causal, a, 0.0)
            o = _mm(a.astype(BF16), vc) + _mm((qc * jnp.exp(bc)).astype(BF16), s_old.astype(BF16))
            o_s[pl.ds(r0, chunk), vs] = o
            e_col = jnp.broadcast_to(jnp.exp(b_last), (dm.dk, dm.dk)).T
            decay = jnp.concatenate([e_col] * (dm.dv // dm.dk), axis=1)
            kv = lax.dot_general((kc * jnp.exp(b_last - bc)).astype(BF16), vc, tn, preferred_element_type=F32)
            st_ref[seq, h] = decay * s_old + kv

    def step(g, carry):
        for u in range(unroll):
            one_seq(g * unroll + u)
        return carry

    if unroll == nseq:
        for seq in range(nseq):
            one_seq(seq)
    else:
        lax.fori_loop(0, nseq // unroll, step, 0)


def _mixer_tail(x, xb_s, o_s, ext_s, pos0, w, dm, *, nseq, seq_len):
    t = nseq * seq_len
    off_g = 2 * dm.key + dm.val
    off_u = off_g + dm.val
    off_ga = off_u + dm.pw
    off_gb = off_ga + dm.d
    xb = xb_s[...]

    u = _mm(xb, w.w_in[:, off_u:off_u + dm.pw])
    ext_s[:, POOL_WMAX:POOL_WMAX + seq_len, :] = u.reshape(nseq, seq_len, dm.pw)
    g = _mm(xb, w.w_in[:, off_g:off_g + dm.val])
    parts = []
    for h in range(GLA_HEADS):
        vs = slice(h * dm.dv, (h + 1) * dm.dv)
        oh = o_s[:, vs]
        ms = jnp.mean(oh * oh, axis=-1, keepdims=True)
        on = oh * lax.rsqrt(ms + RMS_EPS) * w.gla_norm[...]
        gh = g[:, vs]
        parts.append((on * (gh * jax.nn.sigmoid(gh))).astype(BF16))
    gate_a = jax.nn.sigmoid(_mm(xb, w.w_in[:, off_ga:off_ga + dm.d]) + w.b_gates[:, 0:dm.d])
    branch_a = _mm(jnp.concatenate(parts, axis=1), w.w_ba[...])

    p = lax.broadcasted_iota(jnp.int32, (nseq, seq_len, dm.pgc), 1)
    pooled = []
    for gi, win in enumerate(POOL_WINDOWS):
        cs = slice(gi * dm.pgc, (gi + 1) * dm.pgc)
        cur = ext_s[:, POOL_WMAX:POOL_WMAX + seq_len, cs]
        acc = cur
        for j in range(1, win):
            acc = acc + ext_s[:, POOL_WMAX - j:POOL_WMAX - j + seq_len, cs]
        cnt = jnp.minimum(win, p + (pos0 + 1)).astype(F32)
        pg = (acc / cnt - cur).reshape(t, dm.pgc)
        pg = _mm(pg.astype(BF16), w.w_pg[gi]) * w.pool_scale[:, cs]
        pooled.append(pg.astype(BF16))
    gate_b = jax.nn.sigmoid(_mm(xb, w.w_in[:, off_gb:off_gb + dm.d]) + w.b_gates[:, dm.d:2 * dm.d])
    branch_b = _mm(jnp.concatenate(pooled, axis=1), w.w_bp[...])
    merged = (gate_a * branch_a + gate_b * branch_b).astype(BF16)
    half = t // 2
    mix = [_mm(merged[p * half:(p + 1) * half], w.w_out[...]) for p in range(2)]
    h1, logits = [], []
    for p in range(2):
        h1.append(_layer_norm(dm.alpha * x[p * half:(p + 1) * half] + mix[p], w.ln_g[...], w.ln_b[...]))
        logits.append(_mm(h1[p].astype(BF16), w.w_rt[...]) + w.b_rt[...])
    return jnp.concatenate(h1, axis=0), jnp.concatenate(logits, axis=0).T[0:N_EXPERTS, :]


N_MIXW = len(MixW._fields)


def _prompt_mixer_kernel(*refs, dm, tile, n_tiles):
    x_ref = refs[0]
    w = MixW(*refs[1:1 + N_MIXW])
    h1_ref, lg_ref, st_ref, buf_ref, q_s, k_s, v_s, gl_s, b_s, o_s, ext_s, st_t, xb_s = refs[1 + N_MIXW:]
    lt = pl.program_id(1)

    @pl.when(lt == 0)
    def _():
        st_t[...] = jnp.zeros(st_t.shape, F32)
        ext_s[:, 0:POOL_WMAX, :] = jnp.zeros((1, POOL_WMAX, dm.pw), F32)

    x = x_ref[...]
    xb_s[...] = x.astype(BF16)
    _project_qkv(xb_s, w, dm, q_s, k_s, v_s, gl_s)
    _chunk_cumsum(gl_s, b_s, tile, GLA_CHUNK)
    _gla_tile(q_s, k_s, v_s, b_s, o_s, st_t, dm=dm, rows=tile, chunk=GLA_CHUNK, group=GLA_GROUP)

    @pl.when(lt == n_tiles - 1)
    def _():
        for h in range(GLA_HEADS):
            st_ref[0, h] = st_t[h].T

    h1, logits_t = _mixer_tail(x, xb_s, o_s, ext_s, lt * tile, w, dm, nseq=1, seq_len=tile)
    _to_row_tiles(h1_ref, h1, tile)
    lg_ref[...] = logits_t
    ext_s[:, 0:POOL_WMAX, :] = ext_s[:, tile:tile + POOL_WMAX, :]

    @pl.when(lt == n_tiles - 1)
    def _():
        buf_ref[...] = ext_s[:, 1:POOL_WMAX, :]


def _sample_proj_kernel(*refs, dm):
    x_ref = refs[0]
    w = MixW(*refs[1:1 + N_MIXW])
    q_ref, k_ref, v_ref, gl_ref, xb_s = refs[1 + N_MIXW:]
    xb_s[...] = x_ref[...].astype(BF16)
    _project_qkv(xb_s, w, dm, q_ref, k_ref, v_ref, gl_ref)


def _sample_gla_kernel(q_ref, k_ref, v_ref, gl_ref, s0_ref, o_ref, st_ref, b_s, *, dm, nseq, seq_len):
    _gla_seqs(q_ref, k_ref, v_ref, gl_ref, b_s, o_ref, s0_ref, st_ref, dm=dm, nseq=nseq, chunk=seq_len,
              unroll=GLA_SEQ_UNROLL)


def _sample_tail_kernel(*refs, dm, nseq, seq_len):
    x_ref, o_ref, hist_ref = refs[0:3]
    w = MixW(*refs[3:3 + N_MIXW])
    h1_ref, lg_ref, buf_ref, ext_s, xb_s = refs[3 + N_MIXW:]
    ext_s[:, 0:1, :] = jnp.zeros((nseq, 1, dm.pw), F32)
    ext_s[:, 1:POOL_WMAX, :] = hist_ref[...]
    x = x_ref[...]
    xb_s[...] = x.astype(BF16)
    h1, logits_t = _mixer_tail(x, xb_s, o_ref, ext_s, POOL_BUF, w, dm, nseq=nseq, seq_len=seq_len)
    _to_row_tiles(h1_ref, h1, nseq * seq_len)
    lg_ref[...] = logits_t
    buf_ref[...] = ext_s[:, seq_len + 1:seq_len + POOL_WMAX, :]


def _const_spec(arr):
    nd = arr.ndim
    return pl.BlockSpec(arr.shape, lambda *_: (0,) * nd, pipeline_mode=pl.Buffered(1))


def _prompt_mixer(x, w, dm):
    bsz, seq, d = x.shape
    tile = PROMPT_TILE
    n_tiles = seq // tile
    m = bsz * seq
    scratch = [pltpu.VMEM((tile, dm.key), F32), pltpu.VMEM((tile, dm.key), F32), pltpu.VMEM((tile, dm.val), F32),
               pltpu.VMEM((tile, dm.key), F32), pltpu.VMEM((tile, dm.key), F32), pltpu.VMEM((tile, dm.val), F32),
               pltpu.VMEM((1, POOL_WMAX + tile, dm.pw), F32),
               pltpu.VMEM((GLA_HEADS, dm.dv, dm.dk), F32),
               pltpu.VMEM((tile, d), BF16)]
    return pl.pallas_call(
        functools.partial(_prompt_mixer_kernel, dm=dm, tile=tile, n_tiles=n_tiles),
        grid=(bsz, n_tiles),
        in_specs=[pl.BlockSpec((tile, d), lambda b, t: (b * n_tiles + t, 0))] + [_const_spec(a) for a in w],
        out_specs=[pl.BlockSpec((tile * SUBLANES, LANES), lambda b, t: (b * n_tiles + t, 0)),
                   pl.BlockSpec((N_EXPERTS, tile), lambda b, t: (0, b * n_tiles + t)),
                   pl.BlockSpec((1, GLA_HEADS, dm.dk, dm.dv), lambda b, t: (b, 0, 0, 0)),
                   pl.BlockSpec((1, POOL_BUF, dm.pw), lambda b, t: (b, 0, 0))],
        out_shape=[jax.ShapeDtypeStruct((m * SUBLANES, LANES), F32),
                   jax.ShapeDtypeStruct((N_EXPERTS, m), F32),
                   jax.ShapeDtypeStruct((bsz, GLA_HEADS, dm.dk, dm.dv), F32),
                   jax.ShapeDtypeStruct((bsz, POOL_BUF, dm.pw), F32)],
        scratch_shapes=scratch,
        compiler_params=pltpu.CompilerParams(dimension_semantics=("arbitrary", "arbitrary"),
                                             vmem_limit_bytes=VMEM_LIMIT_BYTES),
        name="prompt_mixer",
    )(x.reshape(m, d), *w)


def _sample_mixer(x, s0, hist, w, dm):
    bsz, seq, d = x.shape
    m = bsz * seq
    x2 = x.reshape(m, d)
    tile = MIX_TILE
    cparams = pltpu.CompilerParams(dimension_semantics=("arbitrary",), vmem_limit_bytes=VMEM_LIMIT_BYTES)
    row = lambda n: pl.BlockSpec((tile, n), lambda i: (i, 0))
    q, k, v, gl = pl.pallas_call(
        functools.partial(_sample_proj_kernel, dm=dm),
        grid=(m // tile,),
        in_specs=[row(d)] + [_const_spec(a) for a in w],
        out_specs=[row(dm.key), row(dm.key), row(dm.val), row(dm.key)],
        out_shape=[jax.ShapeDtypeStruct((m, n), F32) for n in (dm.key, dm.key, dm.val, dm.key)],
        scratch_shapes=[pltpu.VMEM((tile, d), BF16)],
        compiler_params=cparams,
        name="sample_proj",
    )(x2, *w)

    nseq = GLA_SEQS
    rows = nseq * seq
    grow = lambda n: pl.BlockSpec((rows, n), lambda i: (i, 0))
    st_spec = pl.BlockSpec((nseq, GLA_HEADS, dm.dk, dm.dv), lambda i: (i, 0, 0, 0))
    o, st = pl.pallas_call(
        functools.partial(_sample_gla_kernel, dm=dm, nseq=nseq, seq_len=seq),
        grid=(bsz // nseq,),
        in_specs=[grow(dm.key), grow(dm.key), grow(dm.val), grow(dm.key), st_spec],
        out_specs=[grow(dm.val), st_spec],
        out_shape=[jax.ShapeDtypeStruct((m, dm.val), F32), jax.ShapeDtypeStruct(s0.shape, F32)],
        scratch_shapes=[pltpu.VMEM((rows, dm.key), F32)],
        compiler_params=cparams,
        name="sample_gla",
    )(q, k, v, gl, s0)

    tseq = tile // seq
    hist_spec = pl.BlockSpec((tseq, POOL_BUF, dm.pw), lambda i: (i, 0, 0))
    h1, lg, buf = pl.pallas_call(
        functools.partial(_sample_tail_kernel, dm=dm, nseq=tseq, seq_len=seq),
        grid=(m // tile,),
        in_specs=[row(d), row(dm.val), hist_spec] + [_const_spec(a) for a in w],
        out_specs=[pl.BlockSpec((tile * SUBLANES, LANES), lambda i: (i, 0)),
                   pl.BlockSpec((N_EXPERTS, tile), lambda i: (0, i)), hist_spec],
        out_shape=[jax.ShapeDtypeStruct((m * SUBLANES, LANES), F32),
                   jax.ShapeDtypeStruct((N_EXPERTS, m), F32),
                   jax.ShapeDtypeStruct((bsz, POOL_BUF, dm.pw), F32)],
        scratch_shapes=[pltpu.VMEM((tseq, POOL_WMAX + seq, dm.pw), F32), pltpu.VMEM((tile, d), BF16)],
        compiler_params=cparams,
        name="sample_tail",
    )(x2, o, hist, *w)
    return h1, lg, st, buf


def _router_kernel(lga_ref, lgb_ref, w_ref, s_ref, meta_ref, c_ref, carry, *, tile, n_tiles_a):
    i = pl.program_id(0)

    @pl.when(i == 0)
    def _():
        carry[...] = jnp.zeros(carry.shape, F32)

    eio = lax.broadcasted_iota(jnp.int32, (N_EXPERTS, tile), 0)
    work = jnp.where(i < n_tiles_a, lga_ref[...], lgb_ref[...])
    vals, hots = [], []
    for k in range(TOP_K):
        mx = jnp.max(work, axis=0, keepdims=True)
        idx = jnp.min(jnp.where(work == mx, eio, N_EXPERTS), axis=0, keepdims=True)
        hot = eio == idx
        vals.append(mx)
        hots.append(hot)
        work = jnp.where(hot, -jnp.inf, work)
    ex = [jnp.exp(v - vals[0]) for v in vals]
    den = ex[0]
    for k in range(1, TOP_K):
        den = den + ex[k]
    for k in range(TOP_K):
        w_ref[k:k + 1, :] = ex[k] / den
    sel = hots[0]
    for k in range(1, TOP_K):
        sel = sel | hots[k]
    r = lax.broadcasted_iota(jnp.int32, (tile, tile), 0)
    c = lax.broadcasted_iota(jnp.int32, (tile, tile), 1)
    earlier_token = (r < c).astype(BF16)
    rank_in_tile = _mm(sel.astype(BF16), earlier_token)
    cnt = jnp.broadcast_to(jnp.sum(sel.astype(F32), axis=1, keepdims=True), (N_EXPERTS, LANES))
    er = lax.broadcasted_iota(jnp.int32, (N_EXPERTS, N_EXPERTS), 0)
    ec = lax.broadcasted_iota(jnp.int32, (N_EXPERTS, N_EXPERTS), 1)
    lower = (ec < er).astype(BF16)
    cnt_hi = jnp.floor(cnt * (1.0 / 256.0))
    first_slot = _mm(lower, (cnt - 256.0 * cnt_hi).astype(BF16)) + 256.0 * _mm(lower, cnt_hi.astype(BF16))
    slot = rank_in_tile + first_slot[:, 0:1]
    for k in range(TOP_K):
        s_ref[0, k:k + 1, :] = jnp.sum(jnp.where(hots[k], slot, 0.0), axis=0, keepdims=True).astype(jnp.int32)
    meta_ref[0, 0:N_EXPERTS, :] = cnt
    meta_ref[0, N_EXPERTS:2 * N_EXPERTS, :] = first_slot
    meta_ref[0, 2 * N_EXPERTS:3 * N_EXPERTS, :] = carry[...]
    carry[...] = carry[...] + cnt
    c_ref[...] = carry[...]


def _for_each_run(meta, fn):
    def body(e, carry):
        cnt = meta[e]
        first = meta[N_EXPERTS + e]
        dst = meta[2 * N_EXPERTS + e]
        for bit in range(RUN_BITS):
            @pl.when(((cnt >> bit) & 1) == 1)
            def _(bit=bit):
                done = cnt & ((1 << bit) - 1)
                fn(first + done, dst + done, 1 << bit)
        return carry
    lax.fori_loop(0, N_EXPERTS, body, 0)


def _rows(ref, first, n):
    return ref.at[pl.ds(pl.multiple_of(first * SUBLANES, SUBLANES), n * SUBLANES), :]


def _dispatch_kernel(fill_start_ref, fill_cnt_ref, tail_start_ref, meta_hbm, slot_hbm, h1a_ref, h1b_ref, xs_hbm,
                     meta_s0, meta_s1, slot_s0, slot_s1, stage, zeros_s, sem_m, sem_s, sem_r, sem_z,
                     *, tile, n_tiles, n_tiles_a):
    i = pl.program_id(0)
    sl = i % 2
    meta_s = (meta_s0, meta_s1)
    slot_s = (slot_s0, slot_s1)

    def index_copies(blk, s):
        return (pltpu.make_async_copy(meta_hbm.at[blk], meta_s[s], sem_m.at[s]),
                pltpu.make_async_copy(slot_hbm.at[blk], slot_s[s], sem_s.at[s]))

    def wait_stage(s):
        pltpu.make_async_copy(stage.at[s], stage.at[s], sem_r.at[s]).wait()

    @pl.when(i == 0)
    def _():
        for cp in index_copies(0, 0):
            cp.start()
        zeros_s[...] = jnp.zeros(zeros_s.shape, F32)

        def fill_copy(off, bit):
            n = 1 << bit
            return pltpu.make_async_copy(zeros_s.at[pl.ds(0, n * SUBLANES), :], _rows(xs_hbm, off, n), sem_z)

        def for_each_piece(fn):
            def body(e, carry):
                cnt = fill_cnt_ref[e]
                for bit in range(ZERO_FILL_BITS):
                    @pl.when(((cnt >> bit) & 1) == 1)
                    def _(bit=bit):
                        fn(fill_copy(fill_start_ref[e] + (cnt & ((1 << bit) - 1)), bit))
                return carry
            lax.fori_loop(0, N_EXPERTS, body, 0)

        top = ZERO_FILL_BITS - 1
        n_tail = (xs_hbm.shape[0] // SUBLANES - tail_start_ref[0]) >> top

        def for_each_tail_piece(fn):
            def body(j, carry):
                fn(fill_copy(tail_start_ref[0] + (j << top), top))
                return carry
            lax.fori_loop(0, n_tail, body, 0)

        for_each_piece(lambda cp: cp.start())
        for_each_tail_piece(lambda cp: cp.start())
        for_each_piece(lambda cp: cp.wait())
        for_each_tail_piece(lambda cp: cp.wait())

    def group_rows(h1_ref, s_):
        def group(g, carry):
            for u in range(DMA_UNROLL):
                t = g * DMA_UNROLL + u
                row = h1_ref[pl.ds(pl.multiple_of(t * SUBLANES, SUBLANES), SUBLANES), :]
                for k in range(TOP_K):
                    s = slot_s[s_][k * tile + t]
                    stage[s_, pl.ds(pl.multiple_of(s * SUBLANES, SUBLANES), SUBLANES), :] = row
            return carry
        lax.fori_loop(0, tile // DMA_UNROLL, group, 0)

    for s_ in range(2):
        @pl.when((sl == 1 - s_) & (i + 1 < n_tiles))
        def _(s_=s_):
            for cp in index_copies(i + 1, s_):
                cp.start()

    for s_ in range(2):
        @pl.when(sl == s_)
        def _(s_=s_):
            for cp in index_copies(i, s_):
                cp.wait()

            @pl.when(i >= 2)
            def _():
                wait_stage(s_)

            @pl.when(i < n_tiles_a)
            def _():
                group_rows(h1a_ref, s_)

            @pl.when(i >= n_tiles_a)
            def _():
                group_rows(h1b_ref, s_)

            _for_each_run(meta_s[s_], lambda s0, d0, n: pltpu.make_async_copy(
                _rows(stage.at[s_], s0, n), _rows(xs_hbm, d0, n), sem_r.at[s_]).start())

    @pl.when(i == n_tiles - 1)
    def _():
        @pl.when(n_tiles > 1)
        def _():
            wait_stage(1 - sl)
        wait_stage(sl)


def _expert_kernel(be_ref, nused_ref, next_e_ref, valid_ref, xs_ref, wgu_hbm, bgu_ref, wdn_hbm, bdn_ref, out_ref,
                   wgu_f, wdn_f, wgu_b, wdn_b, sem_w, *, rows, d_ff):
    i = pl.program_id(0)

    def weight_copies(e):
        return (pltpu.make_async_copy(wgu_hbm.at[e], wgu_f, sem_w.at[0]),
                pltpu.make_async_copy(wdn_hbm.at[e], wdn_f, sem_w.at[1]))

    @pl.when(i == 0)
    def _():
        for cp in weight_copies(be_ref[0]):
            cp.start()

    @pl.when((i < nused_ref[0]) & ((i == 0) | (be_ref[i] != be_ref[jnp.maximum(i - 1, 0)])))
    def _():
        for cp in weight_copies(be_ref[i]):
            cp.wait()
        wgu_b[...] = wgu_f[...].astype(BF16)
        wdn_b[...] = wdn_f[...].astype(BF16)

        @pl.when(next_e_ref[i] >= 0)
        def _():
            for cp in weight_copies(next_e_ref[i]):
                cp.start()

    def ffn(n):
        head = lambda ref: ref.at[pl.ds(0, n * SUBLANES), :]
        xb = _from_row_tiles(head(xs_ref), n).astype(BF16)
        hh = _mm(xb, wgu_b[...]) + bgu_ref[0]
        gate = jnp.minimum(hh[:, 0:d_ff], SWIGLU_LIMIT)
        up = jnp.clip(hh[:, d_ff:2 * d_ff], -SWIGLU_LIMIT, SWIGLU_LIMIT)
        act = (up + 1.0) * (gate * jax.nn.sigmoid(GLU_ALPHA * gate))
        res = _mm(act.astype(BF16), wdn_b[...]) + bdn_ref[0]
        _to_row_tiles(head(out_ref), res, n)

    half = rows // 2

    @pl.when((i < nused_ref[0]) & (valid_ref[i] > half))
    def _():
        ffn(rows)

    @pl.when((i < nused_ref[0]) & (valid_ref[i] <= half))
    def _():
        ffn(half)
        out_ref[pl.ds(half * SUBLANES, half * SUBLANES), :] = jnp.zeros((half * SUBLANES, LANES), F32)

    @pl.when(i >= nused_ref[0])
    def _():
        out_ref[...] = jnp.zeros(out_ref.shape, F32)


def _combine_kernel(meta_hbm, slot_hbm, res_hbm, h1a_ref, h1b_ref, gw_ref, g_ref, b_ref, ya_ref, yb_ref,
                    meta_s0, meta_s1, slot_s0, slot_s1, stage, gbuf, sem_m, sem_s, sem_g,
                    *, tile, alpha, n_tiles, n_tiles_a):
    i = pl.program_id(0)
    sl = i % 2
    meta_s = (meta_s0, meta_s1)
    slot_s = (slot_s0, slot_s1)

    def index_copies(blk, s):
        return (pltpu.make_async_copy(meta_hbm.at[blk], meta_s[s], sem_m.at[s]),
                pltpu.make_async_copy(slot_hbm.at[blk], slot_s[s], sem_s.at[s]))

    def fetch_runs(s):
        _for_each_run(meta_s[s], lambda s0, d0, n: pltpu.make_async_copy(
            _rows(res_hbm, d0, n), _rows(stage.at[s], s0, n), sem_g.at[s]).start())

    @pl.when(i == 0)
    def _():
        for cp in index_copies(0, 0):
            cp.start()
        for cp in index_copies(0, 0):
            cp.wait()
        fetch_runs(0)

        @pl.when(n_tiles > 1)
        def _():
            for cp in index_copies(1, 1):
                cp.start()

    def regroup(s_):
        def group(g, carry):
            for u in range(DMA_UNROLL):
                t = g * DMA_UNROLL + u
                for k in range(TOP_K):
                    s = slot_s[s_][k * tile + t]
                    gbuf[k, pl.ds(pl.multiple_of(t * SUBLANES, SUBLANES), SUBLANES), :] = (
                        stage[s_, pl.ds(pl.multiple_of(s * SUBLANES, SUBLANES), SUBLANES), :])
            return carry
        lax.fori_loop(0, tile // DMA_UNROLL, group, 0)

    for s_ in range(2):
        @pl.when((sl == 1 - s_) & (i + 1 < n_tiles))
        def _(s_=s_):
            for cp in index_copies(i + 1, s_):
                cp.wait()
            fetch_runs(s_)

    for s_ in range(2):
        @pl.when(sl == s_)
        def _(s_=s_):
            pltpu.make_async_copy(stage.at[s_], stage.at[s_], sem_g.at[s_]).wait()
            regroup(s_)

    for s_ in range(2):
        @pl.when((sl == s_) & (i + 2 < n_tiles))
        def _(s_=s_):
            for cp in index_copies(i + 2, s_):
                cp.start()

    gw = gw_ref[...]
    z = alpha * jnp.where(i < n_tiles_a, _from_row_tiles(h1a_ref, tile), _from_row_tiles(h1b_ref, tile))
    for k in range(TOP_K):
        z = z + gw[:, k:k + 1] * _from_row_tiles(gbuf.at[k], tile)
    y = _layer_norm(z, g_ref[...], b_ref[...])

    @pl.when(i < n_tiles_a)
    def _():
        ya_ref[...] = y

    @pl.when(i >= n_tiles_a)
    def _():
        yb_ref[...] = y


def _moe(h1_a, h1_b, logits_a, logits_b, w_gu, b_gu, w_down, b_down, ln_g, ln_b, dm):
    m_a = logits_a.shape[1]
    m = m_a + logits_b.shape[1]
    rows = EXPERT_ROWS
    d_ff = w_down.shape[1]
    tile = MOE_TILE
    n_tiles = m // tile
    n_tiles_a = m_a // tile
    cparams = pltpu.CompilerParams(dimension_semantics=("arbitrary",), vmem_limit_bytes=VMEM_LIMIT_BYTES)

    def two_group_specs(block, lane_axis):
        pick = (lambda j: (0, j)) if lane_axis else (lambda j: (j, 0))
        return [pl.BlockSpec(block, lambda i, *_: pick(jnp.minimum(i, n_tiles_a - 1))),
                pl.BlockSpec(block, lambda i, *_: pick(jnp.maximum(i - n_tiles_a, 0)))]

    kblk = lambda: pl.BlockSpec((TOP_K, tile), lambda i: (0, i))
    gate_w, slot, meta, counts = pl.pallas_call(
        functools.partial(_router_kernel, tile=tile, n_tiles_a=n_tiles_a),
        grid=(n_tiles,),
        in_specs=two_group_specs((N_EXPERTS, tile), True),
        out_specs=[kblk(), pl.BlockSpec((1, TOP_K, tile), lambda i: (i, 0, 0)),
                   pl.BlockSpec((1, 3 * N_EXPERTS, LANES), lambda i: (i, 0, 0)),
                   pl.BlockSpec((N_EXPERTS, LANES), lambda i: (0, 0))],
        out_shape=[jax.ShapeDtypeStruct((TOP_K, m), F32), jax.ShapeDtypeStruct((n_tiles, TOP_K, tile), jnp.int32),
                   jax.ShapeDtypeStruct((n_tiles, 3 * N_EXPERTS, LANES), F32),
                   jax.ShapeDtypeStruct((N_EXPERTS, LANES), F32)],
        scratch_shapes=[pltpu.VMEM((N_EXPERTS, LANES), F32)],
        compiler_params=cparams,
        name="moe_router",
    )(logits_a, logits_b)

    counts = counts[:, 0].astype(jnp.int32)
    padded = (counts + rows - 1) // rows * rows
    pad_end = jnp.cumsum(padded)
    pad_start = pad_end - padded
    n_blocks = -(-(m * TOP_K + N_EXPERTS * (rows - 1)) // rows)
    n_rows = n_blocks * rows
    block_row0 = jnp.arange(n_blocks, dtype=jnp.int32) * rows
    block_e = jnp.minimum(jnp.sum((pad_end[None, :] <= block_row0[:, None]).astype(jnp.int32), axis=1), N_EXPERTS - 1)
    n_used = (pad_end[-1] // rows).astype(jnp.int32).reshape(1)
    meta = meta[:, :, 0].astype(jnp.int32)
    tile_meta = jnp.concatenate([meta[:, 0:2 * N_EXPERTS], meta[:, 2 * N_EXPERTS:] + pad_start[None, :],
                                 jnp.zeros((n_tiles, LANES - 3 * N_EXPERTS), jnp.int32)], axis=1)
    slot_t = slot.reshape(n_tiles, TOP_K * tile)

    h1_specs = two_group_specs((tile * SUBLANES, LANES), False)
    any_spec = pl.BlockSpec(memory_space=pl.ANY)
    index_scratch = [pltpu.SMEM((LANES,), jnp.int32), pltpu.SMEM((LANES,), jnp.int32),
                     pltpu.SMEM((TOP_K * tile,), jnp.int32), pltpu.SMEM((TOP_K * tile,), jnp.int32),
                     pltpu.VMEM((2, TOP_K * tile * SUBLANES, LANES), F32)]
    xs = pl.pallas_call(
        functools.partial(_dispatch_kernel, tile=tile, n_tiles=n_tiles, n_tiles_a=n_tiles_a),
        grid_spec=pltpu.PrefetchScalarGridSpec(
            num_scalar_prefetch=3,
            grid=(n_tiles,),
            in_specs=[any_spec, any_spec] + h1_specs,
            out_specs=any_spec,
            scratch_shapes=index_scratch + [
                pltpu.VMEM(((1 << (ZERO_FILL_BITS - 1)) * SUBLANES, LANES), F32),
                pltpu.SemaphoreType.DMA((2,)), pltpu.SemaphoreType.DMA((2,)), pltpu.SemaphoreType.DMA((2,)),
                pltpu.SemaphoreType.DMA]),
        out_shape=jax.ShapeDtypeStruct((n_rows * SUBLANES, LANES), F32),
        compiler_params=cparams,
        name="moe_dispatch",
    )(pad_start + counts, padded - counts, pad_end[-1:], tile_meta, slot_t, h1_a, h1_b)

    blk = jnp.arange(n_blocks, dtype=jnp.int32)
    run_start = (blk < n_used[0]) & ((blk == 0) | (block_e != jnp.roll(block_e, 1)))
    next_start = jnp.min(jnp.where((blk[None, :] > blk[:, None]) & run_start[None, :], blk[None, :], n_blocks), axis=1)
    next_e = jnp.where(next_start < n_blocks, block_e[jnp.minimum(next_start, n_blocks - 1)], -1).astype(jnp.int32)

    of_block = block_e[:, None] == jnp.arange(N_EXPERTS, dtype=jnp.int32)[None, :]
    region_end = jnp.sum(jnp.where(of_block, (pad_start + counts)[None, :], 0), axis=1)
    block_valid = jnp.clip(region_end - block_row0, 0, rows).astype(jnp.int32)

    last = lambda i, nu: jnp.minimum(i, nu[0] - 1)
    res = pl.pallas_call(
        functools.partial(_expert_kernel, rows=rows, d_ff=d_ff),
        grid_spec=pltpu.PrefetchScalarGridSpec(
            num_scalar_prefetch=4,
            grid=(n_blocks,),
            in_specs=[pl.BlockSpec((rows * SUBLANES, LANES), lambda i, be, nu, ne, bv: (last(i, nu), 0)),
                      any_spec,
                      pl.BlockSpec((1, 1, 2 * d_ff), lambda i, be, nu, ne, bv: (be[i], 0, 0)),
                      any_spec,
                      pl.BlockSpec((1, 1, dm.d), lambda i, be, nu, ne, bv: (be[i], 0, 0))],
            out_specs=pl.BlockSpec((rows * SUBLANES, LANES), lambda i, be, nu, ne, bv: (i, 0)),
            scratch_shapes=[pltpu.VMEM((dm.d, 2 * d_ff), F32), pltpu.VMEM((d_ff, dm.d), F32),
                            pltpu.VMEM((dm.d, 2 * d_ff), BF16), pltpu.VMEM((d_ff, dm.d), BF16),
                            pltpu.SemaphoreType.DMA((2,))]),
        out_shape=jax.ShapeDtypeStruct((n_rows * SUBLANES, LANES), F32),
        compiler_params=cparams,
        name="moe_experts",
    )(block_e, n_used, next_e, block_valid, xs, w_gu, b_gu[:, None, :], w_down, b_down[:, None, :])

    ya, yb = pl.pallas_call(
        functools.partial(_combine_kernel, tile=tile, alpha=dm.alpha, n_tiles=n_tiles, n_tiles_a=n_tiles_a),
        grid=(n_tiles,),
        in_specs=[any_spec, any_spec, any_spec] + h1_specs + [
                  pl.BlockSpec((tile, TOP_K), lambda i: (i, 0)),
                  pl.BlockSpec((1, dm.d), lambda i: (0, 0)),
                  pl.BlockSpec((1, dm.d), lambda i: (0, 0))],
        out_specs=[pl.BlockSpec((tile, dm.d), lambda i: (jnp.minimum(i, n_tiles_a - 1), 0)),
                   pl.BlockSpec((tile, dm.d), lambda i: (jnp.maximum(i - n_tiles_a, 0), 0))],
        out_shape=[jax.ShapeDtypeStruct((m_a, dm.d), F32), jax.ShapeDtypeStruct((m - m_a, dm.d), F32)],
        scratch_shapes=index_scratch + [
            pltpu.VMEM((TOP_K, tile * SUBLANES, LANES), F32),
            pltpu.SemaphoreType.DMA((2,)), pltpu.SemaphoreType.DMA((2,)), pltpu.SemaphoreType.DMA((2,))],
        compiler_params=cparams,
        name="moe_combine",
    )(tile_meta, slot_t, res, h1_a, h1_b, gate_w.T, ln_g[None, :], ln_b[None, :])
    return ya, yb


def _pad_cols(a, n):
    return jnp.pad(a, ((0, 0), (0, n - a.shape[1])))


def _regroup_kernel(w_ref, o_ref, *, o_lr, o_tail):
    n_in = w_ref.shape[1]
    n_main = n_in - (o_tail - o_lr)
    o_ref[:, 0:o_lr] = w_ref[:, 0:o_lr].astype(BF16)
    o_ref[:, o_lr:n_main] = w_ref[:, o_tail:n_in].astype(BF16)
    lr = w_ref[:, o_lr:o_tail].astype(BF16)
    o_ref[:, n_main:n_main + LANES] = jnp.concatenate(
        [lr, jnp.zeros((lr.shape[0], LANES - (o_tail - o_lr)), BF16)], axis=1)


def _regroup_w_in(w_in_all, layer, o_lr, o_tail):
    _, d, n_in = w_in_all.shape
    n_out = n_in - (o_tail - o_lr) + LANES
    rows = REGROUP_ROWS
    return pl.pallas_call(
        functools.partial(_regroup_kernel, o_lr=o_lr, o_tail=o_tail),
        grid=(d // rows,),
        in_specs=[pl.BlockSpec((None, rows, n_in), lambda i: (layer, i, 0))],
        out_specs=pl.BlockSpec((rows, n_out), lambda i: (i, 0)),
        out_shape=jax.ShapeDtypeStruct((d, n_out), BF16),
        compiler_params=pltpu.CompilerParams(dimension_semantics=("arbitrary",), vmem_limit_bytes=VMEM_LIMIT_BYTES),
        name="regroup_w_in",
    )(w_in_all)


def _mixer_weights(w_in_all, layer, w_gk2, b_gk, gla_norm_w, w_branch_gla, w_pool_grp, pool_scale, w_branch_pool,
                   b_gates, w_out, ln_g, ln_b, w_router, b_router, dm):
    o_lr = 2 * dm.key + dm.val
    o_tail = o_lr + GATE_RANK
    return MixW(
        w_in=_regroup_w_in(w_in_all, layer, o_lr, o_tail),
        w_gk2=jnp.pad(w_gk2, ((0, LANES - GATE_RANK), (0, 0))).astype(BF16),
        b_gk=b_gk[None, :],
        gla_norm=gla_norm_w[None, :],
        w_ba=w_branch_gla.astype(BF16),
        w_pg=w_pool_grp.astype(BF16),
        pool_scale=pool_scale[None, :],
        w_bp=w_branch_pool.astype(BF16),
        b_gates=b_gates[None, :],
        w_out=w_out.astype(BF16),
        ln_g=ln_g[None, :],
        ln_b=ln_b[None, :],
        w_rt=_pad_cols(w_router, LANES).astype(BF16),
        b_rt=_pad_cols(b_router[None, :], LANES),
    )


def kernel(x_prompt, x_sample, state_gla, state_pool, w_in, w_gk2, b_gk, gla_norm_w, w_branch_gla, w_pool_grp,
           pool_scale, w_branch_pool, b_gates, w_out, ln1_g, ln1_b, w_router, b_router, w_gu, b_gu, w_down, b_down,
           ln2_g, ln2_b):
    depth = w_in.shape[0]
    bp, lp, d = x_prompt.shape
    bs, ls, _ = x_sample.shape
    assert d == SUBLANES * LANES and lp % PROMPT_TILE == 0 and (bs * ls) % MIX_TILE == 0 and MIX_TILE % ls == 0
    assert bs % GLA_SEQS == 0 and ls % SUBLANES == 0
    assert (bp * lp) % MOE_TILE == 0 and (bs * ls) % MOE_TILE == 0
    dm = _dims(d, depth)
    yp, ys = x_prompt, x_sample
    gla_p, pool_p, gla_s, pool_s = [], [], [], []
    for l in range(depth):
        w = _mixer_weights(w_in, l, w_gk2[l], b_gk[l], gla_norm_w[l], w_branch_gla[l], w_pool_grp[l], pool_scale[l],
                           w_branch_pool[l], b_gates[l], w_out[l], ln1_g[l], ln1_b[l], w_router[l], b_router[l], dm)
        h1p, lgp, sp, bufp = _prompt_mixer(yp, w, dm)
        h1s, lgs, ss, bufs = _sample_mixer(ys, state_gla[l], state_pool[l], w, dm)
        yp, ys = _moe(h1p, h1s, lgp, lgs, w_gu[l], b_gu[l], w_down[l], b_down[l], ln2_g[l], ln2_b[l], dm)
        yp = yp.reshape(bp, lp, d)
        ys = ys.reshape(bs, ls, d)
        gla_p.append(sp.astype(state_gla.dtype))
        pool_p.append(bufp.astype(state_pool.dtype))
        gla_s.append(ss.astype(state_gla.dtype))
        pool_s.append(bufs.astype(state_pool.dtype))
    stack = lambda xs: xs[0][None] if len(xs) == 1 else jnp.stack(xs, 0)
    return (yp, ys, stack(gla_p), stack(pool_p), stack(gla_s), stack(pool_s))
```

```python
import functools
from typing import NamedTuple

import jax
import jax.numpy as jnp
from jax import lax
from jax.experimental import pallas as pl
from jax.experimental.pallas import tpu as pltpu

F32 = jnp.float32
BF16 = jnp.bfloat16

GLA_HEADS = 4
GATE_RANK = 16
GATE_NORMALIZER = 16.0
GLA_CHUNK = 64
RMS_EPS = 1e-6
POOL_WINDOWS = (2, 4, 8, 16)
POOL_WMAX = 16
POOL_BUF = POOL_WMAX - 1
N_EXPERTS = 32
TOP_K = 4
SWIGLU_LIMIT = 7.0
GLU_ALPHA = 1.702
LN_EPS = 1e-5

LANES = 128
SUBLANES = 8
VMEM_LIMIT_BYTES = 56 * 1024 * 1024

PROMPT_TILE = 512
MIX_TILE = 256
GLA_GROUP = 128
GLA_SEQS = 8
GLA_SEQ_UNROLL = 8
EXPERT_ROWS = 512
MOE_TILE = 512
CUMSUM_BLOCK = 256
DMA_UNROLL = 8
REGROUP_ROWS = 256
ZERO_FILL_BITS = (EXPERT_ROWS - 1).bit_length()
RUN_BITS = MOE_TILE.bit_length()


class Dims(NamedTuple):
    d: int
    dk: int
    dv: int
    key: int
    val: int
    pw: int
    pgc: int
    alpha: float


def _dims(d, depth):
    return Dims(d=d, dk=d // 8, dv=d // 4, key=d // 2, val=d, pw=d // 2, pgc=d // 8, alpha=(2.0 * depth) ** 0.25)


class MixW(NamedTuple):
    w_in: object
    w_gk2: object
    b_gk: object
    gla_norm: object
    w_ba: object
    w_pg: object
    pool_scale: object
    w_bp: object
    b_gates: object
    w_out: object
    ln_g: object
    ln_b: object
    w_rt: object
    b_rt: object


def _mm(a, b):
    return jnp.dot(a, b, preferred_element_type=F32)


def _layer_norm(z, g, b):
    mu = jnp.mean(z, axis=-1, keepdims=True)
    zc = z - mu
    var = jnp.mean(zc * zc, axis=-1, keepdims=True)
    return zc * lax.rsqrt(var + LN_EPS) * g + b


def _to_row_tiles(ref, val, rows):
    for c in range(SUBLANES):
        ref[pl.ds(c, rows, stride=SUBLANES), :] = val[:, c * LANES:(c + 1) * LANES]


def _from_row_tiles(ref, rows):
    return jnp.concatenate([ref[pl.ds(c, rows, stride=SUBLANES), :] for c in range(SUBLANES)], axis=1)


def _project_qkv(xb_s, w, dm, q_s, k_s, v_s, gl_s):
    xb = xb_s[...]
    off_lr = 2 * dm.key + 2 * dm.val + dm.pw + 2 * dm.d
    lr = _mm(xb, w.w_in[:, off_lr:off_lr + LANES])
    q_s[...] = _mm(xb, w.w_in[:, 0:dm.key]) * (dm.dk ** -0.5)
    gk = _mm(lr.astype(BF16), w.w_gk2[...]) + w.b_gk[...]
    k_s[...] = _mm(xb, w.w_in[:, dm.key:2 * dm.key])
    gl_s[...] = (jnp.minimum(gk, 0.0) - jnp.log1p(jnp.exp(-jnp.abs(gk)))) / GATE_NORMALIZER
    v_s[...] = _mm(xb, w.w_in[:, 2 * dm.key:2 * dm.key + dm.val])


def _chunk_cumsum(gl_s, b_s, rows, chunk):
    blk = min(rows, CUMSUM_BLOCK)
    r = lax.broadcasted_iota(jnp.int32, (blk, blk), 0)
    c = lax.broadcasted_iota(jnp.int32, (blk, blk), 1)
    tri = ((r // chunk == c // chunk) & (c <= r)).astype(BF16)
    for b0 in range(0, rows, blk):
        gl = gl_s[b0:b0 + blk, :]
        hi = gl.astype(BF16)
        lo = (gl - hi.astype(F32)).astype(BF16)
        b_s[b0:b0 + blk, :] = _mm(tri, hi) + _mm(tri, lo)


NT_DIMS = (((1,), (1,)), ((), ()))
TN_DIMS = (((0,), (0,)), ((), ()))


def _gla_tile(q_s, k_s, v_s, b_s, o_s, st_t, *, dm, rows, chunk, group):
    n_chunks = rows // chunk
    gr = lax.broadcasted_iota(jnp.int32, (group, group), 0)
    gc = lax.broadcasted_iota(jnp.int32, (group, group), 1)
    mask = (gr // chunk == gc // chunk) & (gc <= gr)

    def per_chunk_row(b, row):
        return jnp.concatenate([jnp.broadcast_to(b[c * chunk + row:c * chunk + row + 1, :], (chunk, b.shape[1]))
                                for c in range(n_chunks)], axis=0)

    for h in range(GLA_HEADS):
        ks = slice(h * dm.dk, (h + 1) * dm.dk)
        vs = slice(h * dm.dv, (h + 1) * dm.dv)
        b = b_s[:, ks]
        b_ref = per_chunk_row(b, chunk // 2)
        b_last = per_chunk_row(b, chunk - 1)
        q = q_s[:, ks]
        k = k_s[:, ks]
        vb = v_s[:, vs].astype(BF16)
        qa = (q * jnp.exp(b - b_ref)).astype(BF16)
        ka = (k * jnp.exp(b_ref - b)).astype(BF16)
        qe = (q * jnp.exp(b)).astype(BF16)
        kd = (k * jnp.exp(b_last - b)).astype(BF16)
        for g0 in range(0, rows, group):
            a = lax.dot_general(qa[g0:g0 + group], ka[g0:g0 + group], NT_DIMS, preferred_element_type=F32)
            a = jnp.where(mask, a, 0.0)
            o_s[g0:g0 + group, vs] = _mm(a.astype(BF16), vb[g0:g0 + group])
        s_t = st_t[h]
        for c in range(n_chunks):
            r0 = c * chunk
            o_s[r0:r0 + chunk, vs] += lax.dot_general(qe[r0:r0 + chunk], s_t.astype(BF16), NT_DIMS,
                                                      preferred_element_type=F32)
            decay = jnp.exp(b[r0 + chunk - 1:r0 + chunk, :])
            s_t = s_t * decay + lax.dot_general(vb[r0:r0 + chunk], kd[r0:r0 + chunk], TN_DIMS,
                                                preferred_element_type=F32)
        st_t[h] = s_t


def _gla_seqs(q_s, k_s, v_s, gl_s, b_s, o_s, s0_ref, st_ref, *, dm, nseq, chunk, unroll):
    _chunk_cumsum(gl_s, b_s, nseq * chunk, chunk)
    ri = lax.broadcasted_iota(jnp.int32, (chunk, chunk), 0)
    ci = lax.broadcasted_iota(jnp.int32, (chunk, chunk), 1)
    causal = ci <= ri
    nt, tn = NT_DIMS, TN_DIMS

    def one_seq(seq):
        r0 = seq * chunk if isinstance(seq, int) else pl.multiple_of(seq * chunk, chunk)
        for h in range(GLA_HEADS):
            ks = slice(h * dm.dk, (h + 1) * dm.dk)
            vs = slice(h * dm.dv, (h + 1) * dm.dv)
            bc = b_s[pl.ds(r0, chunk), ks]
            b_ref = bc[chunk // 2:chunk // 2 + 1, :]
            b_last = bc[chunk - 1:chunk, :]
            qc = q_s[pl.ds(r0, chunk), ks]
            kc = k_s[pl.ds(r0, chunk), ks]
            vc = v_s[pl.ds(r0, chunk), vs].astype(BF16)
            s_old = s0_ref[seq, h]
            a = lax.dot_general((qc * jnp.exp(bc - b_ref)).astype(BF16), (kc * jnp.exp(b_ref - bc)).astype(BF16),
                                nt, preferred_element_type=F32)
            a = jnp.where(causal, a, 0.0)
            o = _mm(a.astype(BF16), vc) + _mm((qc * jnp.exp(bc)).astype(BF16), s_old.astype(BF16))
            o_s[pl.ds(r0, chunk), vs] = o
            e_col = jnp.broadcast_to(jnp.exp(b_last), (dm.dk, dm.dk)).T
            decay = jnp.concatenate([e_col] * (dm.dv // dm.dk), axis=1)
            kv = lax.dot_general((kc * jnp.exp(b_last - bc)).astype(BF16), vc, tn, preferred_element_type=F32)
            st_ref[seq, h] = decay * s_old + kv

    def step(g, carry):
        for u in range(unroll):
            one_seq(g * unroll + u)
        return carry

    if unroll == nseq:
        for seq in range(nseq):
            one_seq(seq)
    else:
        lax.fori_loop(0, nseq // unroll, step, 0)


def _mixer_tail(x, xb_s, o_s, ext_s, pos0, w, dm, *, nseq, seq_len):
    t = nseq * seq_len
    off_g = 2 * dm.key + dm.val
    off_u = off_g + dm.val
    off_ga = off_u + dm.pw
    off_gb = off_ga + dm.d
    xb = xb_s[...]

    u = _mm(xb, w.w_in[:, off_u:off_u + dm.pw])
    ext_s[:, POOL_WMAX:POOL_WMAX + seq_len, :] = u.reshape(nseq, seq_len, dm.pw)
    g = _mm(xb, w.w_in[:, off_g:off_g + dm.val])
    parts = []
    for h in range(GLA_HEADS):
        vs = slice(h * dm.dv, (h + 1) * dm.dv)
        oh = o_s[:, vs]
        ms = jnp.mean(oh * oh, axis=-1, keepdims=True)
        on = oh * lax.rsqrt(ms + RMS_EPS) * w.gla_norm[...]
        gh = g[:, vs]
        parts.append((on * (gh * jax.nn.sigmoid(gh))).astype(BF16))
    gate_a = jax.nn.sigmoid(_mm(xb, w.w_in[:, off_ga:off_ga + dm.d]) + w.b_gates[:, 0:dm.d])
    branch_a = _mm(jnp.concatenate(parts, axis=1), w.w_ba[...])

    p = lax.broadcasted_iota(jnp.int32, (nseq, seq_len, dm.pgc), 1)
    pooled = []
    for gi, win in enumerate(POOL_WINDOWS):
        cs = slice(gi * dm.pgc, (gi + 1) * dm.pgc)
        cur = ext_s[:, POOL_WMAX:POOL_WMAX + seq_len, cs]
        acc = cur
        for j in range(1, win):
            acc = acc + ext_s[:, POOL_WMAX - j:POOL_WMAX - j + seq_len, cs]
        cnt = jnp.minimum(win, p + (pos0 + 1)).astype(F32)
        pg = (acc / cnt - cur).reshape(t, dm.pgc)
        pg = _mm(pg.astype(BF16), w.w_pg[gi]) * w.pool_scale[:, cs]
        pooled.append(pg.astype(BF16))
    gate_b = jax.nn.sigmoid(_mm(xb, w.w_in[:, off_gb:off_gb + dm.d]) + w.b_gates[:, dm.d:2 * dm.d])
    branch_b = _mm(jnp.concatenate(pooled, axis=1), w.w_bp[...])
    merged = (gate_a * branch_a + gate_b * branch_b).astype(BF16)
    half = t // 2
    mix = [_mm(merged[p * half:(p + 1) * half], w.w_out[...]) for p in range(2)]
    h1, logits = [], []
    for p in range(2):
        h1.append(_layer_norm(dm.alpha * x[p * half:(p + 1) * half] + mix[p], w.ln_g[...], w.ln_b[...]))
        logits.append(_mm(h1[p].astype(BF16), w.w_rt[...]) + w.b_rt[...])
    return jnp.concatenate(h1, axis=0), jnp.concatenate(logits, axis=0).T[0:N_EXPERTS, :]


N_MIXW = len(MixW._fields)


def _prompt_mixer_kernel(*refs, dm, tile, n_tiles):
    x_ref = refs[0]
    w = MixW(*refs[1:1 + N_MIXW])
    h1_ref, lg_ref, st_ref, buf_ref, q_s, k_s, v_s, gl_s, b_s, o_s, ext_s, st_t, xb_s = refs[1 + N_MIXW:]
    lt = pl.program_id(1)

    @pl.when(lt == 0)
    def _():
        st_t[...] = jnp.zeros(st_t.shape, F32)
        ext_s[:, 0:POOL_WMAX, :] = jnp.zeros((1, POOL_WMAX, dm.pw), F32)

    x = x_ref[...]
    xb_s[...] = x.astype(BF16)
    _project_qkv(xb_s, w, dm, q_s, k_s, v_s, gl_s)
    _chunk_cumsum(gl_s, b_s, tile, GLA_CHUNK)
    _gla_tile(q_s, k_s, v_s, b_s, o_s, st_t, dm=dm, rows=tile, chunk=GLA_CHUNK, group=GLA_GROUP)

    @pl.when(lt == n_tiles - 1)
    def _():
        for h in range(GLA_HEADS):
            st_ref[0, h] = st_t[h].T

    h1, logits_t = _mixer_tail(x, xb_s, o_s, ext_s, lt * tile, w, dm, nseq=1, seq_len=tile)
    _to_row_tiles(h1_ref, h1, tile)
    lg_ref[...] = logits_t
    ext_s[:, 0:POOL_WMAX, :] = ext_s[:, tile:tile + POOL_WMAX, :]

    @pl.when(lt == n_tiles - 1)
    def _():
        buf_ref[...] = ext_s[:, 1:POOL_WMAX, :]


def _sample_proj_kernel(*refs, dm):
    x_ref = refs[0]
    w = MixW(*refs[1:1 + N_MIXW])
    q_ref, k_ref, v_ref, gl_ref, xb_s = refs[1 + N_MIXW:]
    xb_s[...] = x_ref[...].astype(BF16)
    _project_qkv(xb_s, w, dm, q_ref, k_ref, v_ref, gl_ref)


def _sample_gla_kernel(q_ref, k_ref, v_ref, gl_ref, s0_ref, o_ref, st_ref, b_s, *, dm, nseq, seq_len):
    _gla_seqs(q_ref, k_ref, v_ref, gl_ref, b_s, o_ref, s0_ref, st_ref, dm=dm, nseq=nseq, chunk=seq_len,
              unroll=GLA_SEQ_UNROLL)


def _sample_tail_kernel(*refs, dm, nseq, seq_len):
    x_ref, o_ref, hist_ref = refs[0:3]
    w = MixW(*refs[3:3 + N_MIXW])
    h1_ref, lg_ref, buf_ref, ext_s, xb_s = refs[3 + N_MIXW:]
    ext_s[:, 0:1, :] = jnp.zeros((nseq, 1, dm.pw), F32)
    ext_s[:, 1:POOL_WMAX, :] = hist_ref[...]
    x = x_ref[...]
    xb_s[...] = x.astype(BF16)
    h1, logits_t = _mixer_tail(x, xb_s, o_ref, ext_s, POOL_BUF, w, dm, nseq=nseq, seq_len=seq_len)
    _to_row_tiles(h1_ref, h1, nseq * seq_len)
    lg_ref[...] = logits_t
    buf_ref[...] = ext_s[:, seq_len + 1:seq_len + POOL_WMAX, :]


def _const_spec(arr):
    nd = arr.ndim
    return pl.BlockSpec(arr.shape, lambda *_: (0,) * nd, pipeline_mode=pl.Buffered(1))


def _prompt_mixer(x, w, dm):
    bsz, seq, d = x.shape
    tile = PROMPT_TILE
    n_tiles = seq // tile
    m = bsz * seq
    scratch = [pltpu.VMEM((tile, dm.key), F32), pltpu.VMEM((tile, dm.key), F32), pltpu.VMEM((tile, dm.val), F32),
               pltpu.VMEM((tile, dm.key), F32), pltpu.VMEM((tile, dm.key), F32), pltpu.VMEM((tile, dm.val), F32),
               pltpu.VMEM((1, POOL_WMAX + tile, dm.pw), F32),
               pltpu.VMEM((GLA_HEADS, dm.dv, dm.dk), F32),
               pltpu.VMEM((tile, d), BF16)]
    return pl.pallas_call(
        functools.partial(_prompt_mixer_kernel, dm=dm, tile=tile, n_tiles=n_tiles),
        grid=(bsz, n_tiles),
        in_specs=[pl.BlockSpec((tile, d), lambda b, t: (b * n_tiles + t, 0))] + [_const_spec(a) for a in w],
        out_specs=[pl.BlockSpec((tile * SUBLANES, LANES), lambda b, t: (b * n_tiles + t, 0)),
                   pl.BlockSpec((N_EXPERTS, tile), lambda b, t: (0, b * n_tiles + t)),
                   pl.BlockSpec((1, GLA_HEADS, dm.dk, dm.dv), lambda b, t: (b, 0, 0, 0)),
                   pl.BlockSpec((1, POOL_BUF, dm.pw), lambda b, t: (b, 0, 0))],
        out_shape=[jax.ShapeDtypeStruct((m * SUBLANES, LANES), F32),
                   jax.ShapeDtypeStruct((N_EXPERTS, m), F32),
                   jax.ShapeDtypeStruct((bsz, GLA_HEADS, dm.dk, dm.dv), F32),
                   jax.ShapeDtypeStruct((bsz, POOL_BUF, dm.pw), F32)],
        scratch_shapes=scratch,
        compiler_params=pltpu.CompilerParams(dimension_semantics=("arbitrary", "arbitrary"),
                                             vmem_limit_bytes=VMEM_LIMIT_BYTES),
        name="prompt_mixer",
    )(x.reshape(m, d), *w)


def _sample_mixer(x, s0, hist, w, dm):
    bsz, seq, d = x.shape
    m = bsz * seq
    x2 = x.reshape(m, d)
    tile = MIX_TILE
    cparams = pltpu.CompilerParams(dimension_semantics=("arbitrary",), vmem_limit_bytes=VMEM_LIMIT_BYTES)
    row = lambda n: pl.BlockSpec((tile, n), lambda i: (i, 0))
    q, k, v, gl = pl.pallas_call(
        functools.partial(_sample_proj_kernel, dm=dm),
        grid=(m // tile,),
        in_specs=[row(d)] + [_const_spec(a) for a in w],
        out_specs=[row(dm.key), row(dm.key), row(dm.val), row(dm.key)],
        out_shape=[jax.ShapeDtypeStruct((m, n), F32) for n in (dm.key, dm.key, dm.val, dm.key)],
        scratch_shapes=[pltpu.VMEM((tile, d), BF16)],
        compiler_params=cparams,
        name="sample_proj",
    )(x2, *w)

    nseq = GLA_SEQS
    rows = nseq * seq
    grow = lambda n: pl.BlockSpec((rows, n), lambda i: (i, 0))
    st_spec = pl.BlockSpec((nseq, GLA_HEADS, dm.dk, dm.dv), lambda i: (i, 0, 0, 0))
    o, st = pl.pallas_call(
        functools.partial(_sample_gla_kernel, dm=dm, nseq=nseq, seq_len=seq),
        grid=(bsz // nseq,),
        in_specs=[grow(dm.key), grow(dm.key), grow(dm.val), grow(dm.key), st_spec],
        out_specs=[grow(dm.val), st_spec],
        out_shape=[jax.ShapeDtypeStruct((m, dm.val), F32), jax.ShapeDtypeStruct(s0.shape, F32)],
        scratch_shapes=[pltpu.VMEM((rows, dm.key), F32)],
        compiler_params=cparams,
        name="sample_gla",
    )(q, k, v, gl, s0)

    tseq = tile // seq
    hist_spec = pl.BlockSpec((tseq, POOL_BUF, dm.pw), lambda i: (i, 0, 0))
    h1, lg, buf = pl.pallas_call(
        functools.partial(_sample_tail_kernel, dm=dm, nseq=tseq, seq_len=seq),
        grid=(m // tile,),
        in_specs=[row(d), row(dm.val), hist_spec] + [_const_spec(a) for a in w],
        out_specs=[pl.BlockSpec((tile * SUBLANES, LANES), lambda i: (i, 0)),
                   pl.BlockSpec((N_EXPERTS, tile), lambda i: (0, i)), hist_spec],
        out_shape=[jax.ShapeDtypeStruct((m * SUBLANES, LANES), F32),
                   jax.ShapeDtypeStruct((N_EXPERTS, m), F32),
                   jax.ShapeDtypeStruct((bsz, POOL_BUF, dm.pw), F32)],
        scratch_shapes=[pltpu.VMEM((tseq, POOL_WMAX + seq, dm.pw), F32), pltpu.VMEM((tile, d), BF16)],
        compiler_params=cparams,
        name="sample_tail",
    )(x2, o, hist, *w)
    return h1, lg, st, buf


def _router_kernel(lga_ref, lgb_ref, w_ref, s_ref, meta_ref, c_ref, carry, *, tile, n_tiles_a):
    i = pl.program_id(0)

    @pl.when(i == 0)
    def _():
        carry[...] = jnp.zeros(carry.shape, F32)

    eio = lax.broadcasted_iota(jnp.int32, (N_EXPERTS, tile), 0)
    work = jnp.where(i < n_tiles_a, lga_ref[...], lgb_ref[...])
    vals, hots = [], []
    for k in range(TOP_K):
        mx = jnp.max(work, axis=0, keepdims=True)
        idx = jnp.min(jnp.where(work == mx, eio, N_EXPERTS), axis=0, keepdims=True)
        hot = eio == idx
        vals.append(mx)
        hots.append(hot)
        work = jnp.where(hot, -jnp.inf, work)
    ex = [jnp.exp(v - vals[0]) for v in vals]
    den = ex[0]
    for k in range(1, TOP_K):
        den = den + ex[k]
    for k in range(TOP_K):
        w_ref[k:k + 1, :] = ex[k] / den
    sel = hots[0]
    for k in range(1, TOP_K):
        sel = sel | hots[k]
    r = lax.broadcasted_iota(jnp.int32, (tile, tile), 0)
    c = lax.broadcasted_iota(jnp.int32, (tile, tile), 1)
    earlier_token = (r < c).astype(BF16)
    rank_in_tile = _mm(sel.astype(BF16), earlier_token)
    cnt = jnp.broadcast_to(jnp.sum(sel.astype(F32), axis=1, keepdims=True), (N_EXPERTS, LANES))
    er = lax.broadcasted_iota(jnp.int32, (N_EXPERTS, N_EXPERTS), 0)
    ec = lax.broadcasted_iota(jnp.int32, (N_EXPERTS, N_EXPERTS), 1)
    lower = (ec < er).astype(BF16)
    cnt_hi = jnp.floor(cnt * (1.0 / 256.0))
    first_slot = _mm(lower, (cnt - 256.0 * cnt_hi).astype(BF16)) + 256.0 * _mm(lower, cnt_hi.astype(BF16))
    slot = rank_in_tile + first_slot[:, 0:1]
    for k in range(TOP_K):
        s_ref[0, k:k + 1, :] = jnp.sum(jnp.where(hots[k], slot, 0.0), axis=0, keepdims=True).astype(jnp.int32)
    meta_ref[0, 0:N_EXPERTS, :] = cnt
    meta_ref[0, N_EXPERTS:2 * N_EXPERTS, :] = first_slot
    meta_ref[0, 2 * N_EXPERTS:3 * N_EXPERTS, :] = carry[...]
    carry[...] = carry[...] + cnt
    c_ref[...] = carry[...]


def _for_each_run(meta, fn):
    def body(e, carry):
        cnt = meta[e]
        first = meta[N_EXPERTS + e]
        dst = meta[2 * N_EXPERTS + e]
        for bit in range(RUN_BITS):
            @pl.when(((cnt >> bit) & 1) == 1)
            def _(bit=bit):
                done = cnt & ((1 << bit) - 1)
                fn(first + done, dst + done, 1 << bit)
        return carry
    lax.fori_loop(0, N_EXPERTS, body, 0)


def _rows(ref, first, n):
    return ref.at[pl.ds(pl.multiple_of(first * SUBLANES, SUBLANES), n * SUBLANES), :]


def _dispatch_kernel(fill_start_ref, fill_cnt_ref, tail_start_ref, meta_hbm, slot_hbm, h1a_ref, h1b_ref, xs_hbm,
                     meta_s0, meta_s1, slot_s0, slot_s1, stage, zeros_s, sem_m, sem_s, sem_r, sem_z,
                     *, tile, n_tiles, n_tiles_a):
    i = pl.program_id(0)
    sl = i % 2
    meta_s = (meta_s0, meta_s1)
    slot_s = (slot_s0, slot_s1)

    def index_copies(blk, s):
        return (pltpu.make_async_copy(meta_hbm.at[blk], meta_s[s], sem_m.at[s]),
                pltpu.make_async_copy(slot_hbm.at[blk], slot_s[s], sem_s.at[s]))

    def wait_stage(s):
        pltpu.make_async_copy(stage.at[s], stage.at[s], sem_r.at[s]).wait()

    @pl.when(i == 0)
    def _():
        for cp in index_copies(0, 0):
            cp.start()
        zeros_s[...] = jnp.zeros(zeros_s.shape, F32)

        def fill_copy(off, bit):
            n = 1 << bit
            return pltpu.make_async_copy(zeros_s.at[pl.ds(0, n * SUBLANES), :], _rows(xs_hbm, off, n), sem_z)

        def for_each_piece(fn):
            def body(e, carry):
                cnt = fill_cnt_ref[e]
                for bit in range(ZERO_FILL_BITS):
                    @pl.when(((cnt >> bit) & 1) == 1)
                    def _(bit=bit):
                        fn(fill_copy(fill_start_ref[e] + (cnt & ((1 << bit) - 1)), bit))
                return carry
            lax.fori_loop(0, N_EXPERTS, body, 0)

        top = ZERO_FILL_BITS - 1
        n_tail = (xs_hbm.shape[0] // SUBLANES - tail_start_ref[0]) >> top

        def for_each_tail_piece(fn):
            def body(j, carry):
                fn(fill_copy(tail_start_ref[0] + (j << top), top))
                return carry
            lax.fori_loop(0, n_tail, body, 0)

        for_each_piece(lambda cp: cp.start())
        for_each_tail_piece(lambda cp: cp.start())
        for_each_piece(lambda cp: cp.wait())
        for_each_tail_piece(lambda cp: cp.wait())

    def group_rows(h1_ref, s_):
        def group(g, carry):
            for u in range(DMA_UNROLL):
                t = g * DMA_UNROLL + u
                row = h1_ref[pl.ds(pl.multiple_of(t * SUBLANES, SUBLANES), SUBLANES), :]
                for k in range(TOP_K):
                    s = slot_s[s_][k * tile + t]
                    stage[s_, pl.ds(pl.multiple_of(s * SUBLANES, SUBLANES), SUBLANES), :] = row
            return carry
        lax.fori_loop(0, tile // DMA_UNROLL, group, 0)

    for s_ in range(2):
        @pl.when((sl == 1 - s_) & (i + 1 < n_tiles))
        def _(s_=s_):
            for cp in index_copies(i + 1, s_):
                cp.start()

    for s_ in range(2):
        @pl.when(sl == s_)
        def _(s_=s_):
            for cp in index_copies(i, s_):
                cp.wait()

            @pl.when(i >= 2)
            def _():
                wait_stage(s_)

            @pl.when(i < n_tiles_a)
            def _():
                group_rows(h1a_ref, s_)

            @pl.when(i >= n_tiles_a)
            def _():
                group_rows(h1b_ref, s_)

            _for_each_run(meta_s[s_], lambda s0, d0, n: pltpu.make_async_copy(
                _rows(stage.at[s_], s0, n), _rows(xs_hbm, d0, n), sem_r.at[s_]).start())

    @pl.when(i == n_tiles - 1)
    def _():
        @pl.when(n_tiles > 1)
        def _():
            wait_stage(1 - sl)
        wait_stage(sl)


def _expert_kernel(be_ref, nused_ref, next_e_ref, valid_ref, xs_ref, wgu_hbm, bgu_ref, wdn_hbm, bdn_ref, out_ref,
                   wgu_f, wdn_f, wgu_b, wdn_b, sem_w, *, rows, d_ff):
    i = pl.program_id(0)

    def weight_copies(e):
        return (pltpu.make_async_copy(wgu_hbm.at[e], wgu_f, sem_w.at[0]),
                pltpu.make_async_copy(wdn_hbm.at[e], wdn_f, sem_w.at[1]))

    @pl.when(i == 0)
    def _():
        for cp in weight_copies(be_ref[0]):
            cp.start()

    @pl.when((i < nused_ref[0]) & ((i == 0) | (be_ref[i] != be_ref[jnp.maximum(i - 1, 0)])))
    def _():
        for cp in weight_copies(be_ref[i]):
            cp.wait()
        wgu_b[...] = wgu_f[...].astype(BF16)
        wdn_b[...] = wdn_f[...].astype(BF16)

        @pl.when(next_e_ref[i] >= 0)
        def _():
            for cp in weight_copies(next_e_ref[i]):
                cp.start()

    def ffn(n):
        head = lambda ref: ref.at[pl.ds(0, n * SUBLANES), :]
        xb = _from_row_tiles(head(xs_ref), n).astype(BF16)
        hh = _mm(xb, wgu_b[...]) + bgu_ref[0]
        gate = jnp.minimum(hh[:, 0:d_ff], SWIGLU_LIMIT)
        up = jnp.clip(hh[:, d_ff:2 * d_ff], -SWIGLU_LIMIT, SWIGLU_LIMIT)
        act = (up + 1.0) * (gate * jax.nn.sigmoid(GLU_ALPHA * gate))
        res = _mm(act.astype(BF16), wdn_b[...]) + bdn_ref[0]
        _to_row_tiles(head(out_ref), res, n)

    prefixes = (rows, rows // 2, rows // 4)
    for j, n in enumerate(prefixes):
        lower = prefixes[j + 1] if j + 1 < len(prefixes) else 0

        @pl.when((i < nused_ref[0]) & (valid_ref[i] <= n) & (valid_ref[i] > lower))
        def _(n=n):
            ffn(n)
            if n < rows:
                out_ref[pl.ds(n * SUBLANES, (rows - n) * SUBLANES), :] = jnp.zeros(((rows - n) * SUBLANES, LANES), F32)

    @pl.when(i >= nused_ref[0])
    def _():
        out_ref[...] = jnp.zeros(out_ref.shape, F32)


def _combine_kernel(meta_hbm, slot_hbm, res_hbm, h1a_ref, h1b_ref, gw_ref, g_ref, b_ref, ya_ref, yb_ref,
                    meta_s0, meta_s1, slot_s0, slot_s1, stage, gbuf, sem_m, sem_s, sem_g,
                    *, tile, alpha, n_tiles, n_tiles_a):
    i = pl.program_id(0)
    sl = i % 2
    meta_s = (meta_s0, meta_s1)
    slot_s = (slot_s0, slot_s1)

    def index_copies(blk, s):
        return (pltpu.make_async_copy(meta_hbm.at[blk], meta_s[s], sem_m.at[s]),
                pltpu.make_async_copy(slot_hbm.at[blk], slot_s[s], sem_s.at[s]))

    def fetch_runs(s):
        _for_each_run(meta_s[s], lambda s0, d0, n: pltpu.make_async_copy(
            _rows(res_hbm, d0, n), _rows(stage.at[s], s0, n), sem_g.at[s]).start())

    @pl.when(i == 0)
    def _():
        for cp in index_copies(0, 0):
            cp.start()
        for cp in index_copies(0, 0):
            cp.wait()
        fetch_runs(0)

        @pl.when(n_tiles > 1)
        def _():
            for cp in index_copies(1, 1):
                cp.start()

    def regroup(s_):
        def group(g, carry):
            for u in range(DMA_UNROLL):
                t = g * DMA_UNROLL + u
                for k in range(TOP_K):
                    s = slot_s[s_][k * tile + t]
                    gbuf[k, pl.ds(pl.multiple_of(t * SUBLANES, SUBLANES), SUBLANES), :] = (
                        stage[s_, pl.ds(pl.multiple_of(s * SUBLANES, SUBLANES), SUBLANES), :])
            return carry
        lax.fori_loop(0, tile // DMA_UNROLL, group, 0)

    for s_ in range(2):
        @pl.when((sl == 1 - s_) & (i + 1 < n_tiles))
        def _(s_=s_):
            for cp in index_copies(i + 1, s_):
                cp.wait()
            fetch_runs(s_)

    for s_ in range(2):
        @pl.when(sl == s_)
        def _(s_=s_):
            pltpu.make_async_copy(stage.at[s_], stage.at[s_], sem_g.at[s_]).wait()
            regroup(s_)

    for s_ in range(2):
        @pl.when((sl == s_) & (i + 2 < n_tiles))
        def _(s_=s_):
            for cp in index_copies(i + 2, s_):
                cp.start()

    gw = gw_ref[...]
    z = alpha * jnp.where(i < n_tiles_a, _from_row_tiles(h1a_ref, tile), _from_row_tiles(h1b_ref, tile))
    for k in range(TOP_K):
        z = z + gw[:, k:k + 1] * _from_row_tiles(gbuf.at[k], tile)
    y = _layer_norm(z, g_ref[...], b_ref[...])

    @pl.when(i < n_tiles_a)
    def _():
        ya_ref[...] = y

    @pl.when(i >= n_tiles_a)
    def _():
        yb_ref[...] = y


def _moe(h1_a, h1_b, logits_a, logits_b, w_gu, b_gu, w_down, b_down, ln_g, ln_b, dm):
    m_a = logits_a.shape[1]
    m = m_a + logits_b.shape[1]
    rows = EXPERT_ROWS
    d_ff = w_down.shape[1]
    tile = MOE_TILE
    n_tiles = m // tile
    n_tiles_a = m_a // tile
    cparams = pltpu.CompilerParams(dimension_semantics=("arbitrary",), vmem_limit_bytes=VMEM_LIMIT_BYTES)

    def two_group_specs(block, lane_axis):
        pick = (lambda j: (0, j)) if lane_axis else (lambda j: (j, 0))
        return [pl.BlockSpec(block, lambda i, *_: pick(jnp.minimum(i, n_tiles_a - 1))),
                pl.BlockSpec(block, lambda i, *_: pick(jnp.maximum(i - n_tiles_a, 0)))]

    kblk = lambda: pl.BlockSpec((TOP_K, tile), lambda i: (0, i))
    gate_w, slot, meta, counts = pl.pallas_call(
        functools.partial(_router_kernel, tile=tile, n_tiles_a=n_tiles_a),
        grid=(n_tiles,),
        in_specs=two_group_specs((N_EXPERTS, tile), True),
        out_specs=[kblk(), pl.BlockSpec((1, TOP_K, tile), lambda i: (i, 0, 0)),
                   pl.BlockSpec((1, 3 * N_EXPERTS, LANES), lambda i: (i, 0, 0)),
                   pl.BlockSpec((N_EXPERTS, LANES), lambda i: (0, 0))],
        out_shape=[jax.ShapeDtypeStruct((TOP_K, m), F32), jax.ShapeDtypeStruct((n_tiles, TOP_K, tile), jnp.int32),
                   jax.ShapeDtypeStruct((n_tiles, 3 * N_EXPERTS, LANES), F32),
                   jax.ShapeDtypeStruct((N_EXPERTS, LANES), F32)],
        scratch_shapes=[pltpu.VMEM((N_EXPERTS, LANES), F32)],
        compiler_params=cparams,
        name="moe_router",
    )(logits_a, logits_b)

    counts = counts[:, 0].astype(jnp.int32)
    padded = (counts + rows - 1) // rows * rows
    pad_end = jnp.cumsum(padded)
    pad_start = pad_end - padded
    n_blocks = -(-(m * TOP_K + N_EXPERTS * (rows - 1)) // rows)
    n_rows = n_blocks * rows
    block_row0 = jnp.arange(n_blocks, dtype=jnp.int32) * rows
    block_e = jnp.minimum(jnp.sum((pad_end[None, :] <= block_row0[:, None]).astype(jnp.int32), axis=1), N_EXPERTS - 1)
    n_used = (pad_end[-1] // rows).astype(jnp.int32).reshape(1)
    meta = meta[:, :, 0].astype(jnp.int32)
    tile_meta = jnp.concatenate([meta[:, 0:2 * N_EXPERTS], meta[:, 2 * N_EXPERTS:] + pad_start[None, :],
                                 jnp.zeros((n_tiles, LANES - 3 * N_EXPERTS), jnp.int32)], axis=1)
    slot_t = slot.reshape(n_tiles, TOP_K * tile)

    h1_specs = two_group_specs((tile * SUBLANES, LANES), False)
    any_spec = pl.BlockSpec(memory_space=pl.ANY)
    index_scratch = [pltpu.SMEM((LANES,), jnp.int32), pltpu.SMEM((LANES,), jnp.int32),
                     pltpu.SMEM((TOP_K * tile,), jnp.int32), pltpu.SMEM((TOP_K * tile,), jnp.int32),
                     pltpu.VMEM((2, TOP_K * tile * SUBLANES, LANES), F32)]
    xs = pl.pallas_call(
        functools.partial(_dispatch_kernel, tile=tile, n_tiles=n_tiles, n_tiles_a=n_tiles_a),
        grid_spec=pltpu.PrefetchScalarGridSpec(
            num_scalar_prefetch=3,
            grid=(n_tiles,),
            in_specs=[any_spec, any_spec] + h1_specs,
            out_specs=any_spec,
            scratch_shapes=index_scratch + [
                pltpu.VMEM(((1 << (ZERO_FILL_BITS - 1)) * SUBLANES, LANES), F32),
                pltpu.SemaphoreType.DMA((2,)), pltpu.SemaphoreType.DMA((2,)), pltpu.SemaphoreType.DMA((2,)),
                pltpu.SemaphoreType.DMA]),
        out_shape=jax.ShapeDtypeStruct((n_rows * SUBLANES, LANES), F32),
        compiler_params=cparams,
        name="moe_dispatch",
    )(pad_start + counts, padded - counts, pad_end[-1:], tile_meta, slot_t, h1_a, h1_b)

    blk = jnp.arange(n_blocks, dtype=jnp.int32)
    run_start = (blk < n_used[0]) & ((blk == 0) | (block_e != jnp.roll(block_e, 1)))
    next_start = jnp.min(jnp.where((blk[None, :] > blk[:, None]) & run_start[None, :], blk[None, :], n_blocks), axis=1)
    next_e = jnp.where(next_start < n_blocks, block_e[jnp.minimum(next_start, n_blocks - 1)], -1).astype(jnp.int32)

    of_block = block_e[:, None] == jnp.arange(N_EXPERTS, dtype=jnp.int32)[None, :]
    region_end = jnp.sum(jnp.where(of_block, (pad_start + counts)[None, :], 0), axis=1)
    block_valid = jnp.clip(region_end - block_row0, 0, rows).astype(jnp.int32)

    last = lambda i, nu: jnp.minimum(i, nu[0] - 1)
    res = pl.pallas_call(
        functools.partial(_expert_kernel, rows=rows, d_ff=d_ff),
        grid_spec=pltpu.PrefetchScalarGridSpec(
            num_scalar_prefetch=4,
            grid=(n_blocks,),
            in_specs=[pl.BlockSpec((rows * SUBLANES, LANES), lambda i, be, nu, ne, bv: (last(i, nu), 0)),
                      any_spec,
                      pl.BlockSpec((1, 1, 2 * d_ff), lambda i, be, nu, ne, bv: (be[i], 0, 0)),
                      any_spec,
                      pl.BlockSpec((1, 1, dm.d), lambda i, be, nu, ne, bv: (be[i], 0, 0))],
            out_specs=pl.BlockSpec((rows * SUBLANES, LANES), lambda i, be, nu, ne, bv: (i, 0)),
            scratch_shapes=[pltpu.VMEM((dm.d, 2 * d_ff), F32), pltpu.VMEM((d_ff, dm.d), F32),
                            pltpu.VMEM((dm.d, 2 * d_ff), BF16), pltpu.VMEM((d_ff, dm.d), BF16),
                            pltpu.SemaphoreType.DMA((2,))]),
        out_shape=jax.ShapeDtypeStruct((n_rows * SUBLANES, LANES), F32),
        compiler_params=cparams,
        name="moe_experts",
    )(block_e, n_used, next_e, block_valid, xs, w_gu, b_gu[:, None, :], w_down, b_down[:, None, :])

    ya, yb = pl.pallas_call(
        functools.partial(_combine_kernel, tile=tile, alpha=dm.alpha, n_tiles=n_tiles, n_tiles_a=n_tiles_a),
        grid=(n_tiles,),
        in_specs=[any_spec, any_spec, any_spec] + h1_specs + [
                  pl.BlockSpec((tile, TOP_K), lambda i: (i, 0)),
                  pl.BlockSpec((1, dm.d), lambda i: (0, 0)),
                  pl.BlockSpec((1, dm.d), lambda i: (0, 0))],
        out_specs=[pl.BlockSpec((tile, dm.d), lambda i: (jnp.minimum(i, n_tiles_a - 1), 0)),
                   pl.BlockSpec((tile, dm.d), lambda i: (jnp.maximum(i - n_tiles_a, 0), 0))],
        out_shape=[jax.ShapeDtypeStruct((m_a, dm.d), F32), jax.ShapeDtypeStruct((m - m_a, dm.d), F32)],
        scratch_shapes=index_scratch + [
            pltpu.VMEM((TOP_K, tile * SUBLANES, LANES), F32),
            pltpu.SemaphoreType.DMA((2,)), pltpu.SemaphoreType.DMA((2,)), pltpu.SemaphoreType.DMA((2,))],
        compiler_params=cparams,
        name="moe_combine",
    )(tile_meta, slot_t, res, h1_a, h1_b, gate_w.T, ln_g[None, :], ln_b[None, :])
    return ya, yb


def _pad_cols(a, n):
    return jnp.pad(a, ((0, 0), (0, n - a.shape[1])))


def _regroup_kernel(w_ref, o_ref, *, o_lr, o_tail):
    n_in = w_ref.shape[1]
    n_main = n_in - (o_tail - o_lr)
    o_ref[:, 0:o_lr] = w_ref[:, 0:o_lr].astype(BF16)
    o_ref[:, o_lr:n_main] = w_ref[:, o_tail:n_in].astype(BF16)
    lr = w_ref[:, o_lr:o_tail].astype(BF16)
    o_ref[:, n_main:n_main + LANES] = jnp.concatenate(
        [lr, jnp.zeros((lr.shape[0], LANES - (o_tail - o_lr)), BF16)], axis=1)


def _regroup_w_in(w_in_all, layer, o_lr, o_tail):
    _, d, n_in = w_in_all.shape
    n_out = n_in - (o_tail - o_lr) + LANES
    rows = REGROUP_ROWS
    return pl.pallas_call(
        functools.partial(_regroup_kernel, o_lr=o_lr, o_tail=o_tail),
        grid=(d // rows,),
        in_specs=[pl.BlockSpec((None, rows, n_in), lambda i: (layer, i, 0))],
        out_specs=pl.BlockSpec((rows, n_out), lambda i: (i, 0)),
        out_shape=jax.ShapeDtypeStruct((d, n_out), BF16),
        compiler_params=pltpu.CompilerParams(dimension_semantics=("arbitrary",), vmem_limit_bytes=VMEM_LIMIT_BYTES),
        name="regroup_w_in",
    )(w_in_all)


def _mixer_weights(w_in_all, layer, w_gk2, b_gk, gla_norm_w, w_branch_gla, w_pool_grp, pool_scale, w_branch_pool,
                   b_gates, w_out, ln_g, ln_b, w_router, b_router, dm):
    o_lr = 2 * dm.key + dm.val
    o_tail = o_lr + GATE_RANK
    return MixW(
        w_in=_regroup_w_in(w_in_all, layer, o_lr, o_tail),
        w_gk2=jnp.pad(w_gk2, ((0, LANES - GATE_RANK), (0, 0))).astype(BF16),
        b_gk=b_gk[None, :],
        gla_norm=gla_norm_w[None, :],
        w_ba=w_branch_gla.astype(BF16),
        w_pg=w_pool_grp.astype(BF16),
        pool_scale=pool_scale[None, :],
        w_bp=w_branch_pool.astype(BF16),
        b_gates=b_gates[None, :],
        w_out=w_out.astype(BF16),
        ln_g=ln_g[None, :],
        ln_b=ln_b[None, :],
        w_rt=_pad_cols(w_router, LANES).astype(BF16),
        b_rt=_pad_cols(b_router[None, :], LANES),
    )


def kernel(x_prompt, x_sample, state_gla, state_pool, w_in, w_gk2, b_gk, gla_norm_w, w_branch_gla, w_pool_grp,
           pool_scale, w_branch_pool, b_gates, w_out, ln1_g, ln1_b, w_router, b_router, w_gu, b_gu, w_down, b_down,
           ln2_g, ln2_b):
    depth = w_in.shape[0]
    bp, lp, d = x_prompt.shape
    bs, ls, _ = x_sample.shape
    assert d == SUBLANES * LANES and lp % PROMPT_TILE == 0 and (bs * ls) % MIX_TILE == 0 and MIX_TILE % ls == 0
    assert bs % GLA_SEQS == 0 and ls % SUBLANES == 0
    assert (bp * lp) % MOE_TILE == 0 and (bs * ls) % MOE_TILE == 0
    dm = _dims(d, depth)
    yp, ys = x_prompt, x_sample
    gla_p, pool_p, gla_s, pool_s = [], [], [], []
    for l in range(depth):
        w = _mixer_weights(w_in, l, w_gk2[l], b_gk[l], gla_norm_w[l], w_branch_gla[l], w_pool_grp[l], pool_scale[l],
                           w_branch_pool[l], b_gates[l], w_out[l], ln1_g[l], ln1_b[l], w_router[l], b_router[l], dm)
        h1p, lgp, sp, bufp = _prompt_mixer(yp, w, dm)
        h1s, lgs, ss, bufs = _sample_mixer(ys, state_gla[l], state_pool[l], w, dm)
        yp, ys = _moe(h1p, h1s, lgp, lgs, w_gu[l], b_gu[l], w_down[l], b_down[l], ln2_g[l], ln2_b[l], dm)
        yp = yp.reshape(bp, lp, d)
        ys = ys.reshape(bs, ls, d)
        gla_p.append(sp.astype(state_gla.dtype))
        pool_p.append(bufp.astype(state_pool.dtype))
        gla_s.append(ss.astype(state_gla.dtype))
        pool_s.append(bufs.astype(state_pool.dtype))
    stack = lambda xs: xs[0][None] if len(xs) == 1 else jnp.stack(xs, 0)
    return (yp, ys, stack(gla_p), stack(pool_p), stack(gla_s), stack(pool_s))
```

```python
import functools
from typing import NamedTuple

import jax
import jax.numpy as jnp
from jax import lax
from jax.experimental import pallas as pl
from jax.experimental.pallas import tpu as pltpu

F32 = jnp.float32
BF16 = jnp.bfloat16

GLA_HEADS = 4
GATE_RANK = 16
GATE_NORMALIZER = 16.0
GLA_CHUNK = 64
RMS_EPS = 1e-6
POOL_WINDOWS = (2, 4, 8, 16)
POOL_WMAX = 16
POOL_BUF = POOL_WMAX - 1
N_EXPERTS = 32
TOP_K = 4
SWIGLU_LIMIT = 7.0
GLU_ALPHA = 1.702
LN_EPS = 1e-5

LANES = 128
SUBLANES = 8
VMEM_LIMIT_BYTES = 56 * 1024 * 1024

PROMPT_TILE = 512
MIX_TILE = 256
GLA_GROUP = 128
GLA_SEQS = 8
GLA_SEQ_UNROLL = 8
EXPERT_ROWS = 512
FFN_SPLIT = 2
MOE_TILE = 512
CUMSUM_BLOCK = 256
DMA_UNROLL = 8
REGROUP_ROWS = 256
ZERO_FILL_BITS = (EXPERT_ROWS - 1).bit_length()
RUN_BITS = MOE_TILE.bit_length()


class Dims(NamedTuple):
    d: int
    dk: int
    dv: int
    key: int
    val: int
    pw: int
    pgc: int
    alpha: float


def _dims(d, depth):
    return Dims(d=d, dk=d // 8, dv=d // 4, key=d // 2, val=d, pw=d // 2, pgc=d // 8, alpha=(2.0 * depth) ** 0.25)


class MixW(NamedTuple):
    w_in: object
    w_gk2: object
    b_gk: object
    gla_norm: object
    w_ba: object
    w_pg: object
    pool_scale: object
    w_bp: object
    b_gates: object
    w_out: object
    ln_g: object
    ln_b: object
    w_rt: object
    b_rt: object


def _mm(a, b):
    return jnp.dot(a, b, preferred_element_type=F32)


def _layer_norm(z, g, b):
    mu = jnp.mean(z, axis=-1, keepdims=True)
    zc = z - mu
    var = jnp.mean(zc * zc, axis=-1, keepdims=True)
    return zc * lax.rsqrt(var + LN_EPS) * g + b


def _to_row_tiles(ref, val, rows):
    for c in range(SUBLANES):
        ref[pl.ds(c, rows, stride=SUBLANES), :] = val[:, c * LANES:(c + 1) * LANES]


def _from_row_tiles(ref, rows):
    return jnp.concatenate([ref[pl.ds(c, rows, stride=SUBLANES), :] for c in range(SUBLANES)], axis=1)


def _project_qkv(xb_s, w, dm, q_s, k_s, v_s, gl_s):
    xb = xb_s[...]
    off_lr = 2 * dm.key + 2 * dm.val + dm.pw + 2 * dm.d
    lr = _mm(xb, w.w_in[:, off_lr:off_lr + LANES])
    q_s[...] = _mm(xb, w.w_in[:, 0:dm.key]) * (dm.dk ** -0.5)
    gk = _mm(lr.astype(BF16), w.w_gk2[...]) + w.b_gk[...]
    k_s[...] = _mm(xb, w.w_in[:, dm.key:2 * dm.key])
    gl_s[...] = (jnp.minimum(gk, 0.0) - jnp.log1p(jnp.exp(-jnp.abs(gk)))) / GATE_NORMALIZER
    v_s[...] = _mm(xb, w.w_in[:, 2 * dm.key:2 * dm.key + dm.val])


def _chunk_cumsum(gl_s, b_s, rows, chunk):
    blk = min(rows, CUMSUM_BLOCK)
    r = lax.broadcasted_iota(jnp.int32, (blk, blk), 0)
    c = lax.broadcasted_iota(jnp.int32, (blk, blk), 1)
    tri = ((r // chunk == c // chunk) & (c <= r)).astype(BF16)
    for b0 in range(0, rows, blk):
        gl = gl_s[b0:b0 + blk, :]
        hi = gl.astype(BF16)
        lo = (gl - hi.astype(F32)).astype(BF16)
        b_s[b0:b0 + blk, :] = _mm(tri, hi) + _mm(tri, lo)


NT_DIMS = (((1,), (1,)), ((), ()))
TN_DIMS = (((0,), (0,)), ((), ()))


def _gla_tile(q_s, k_s, v_s, b_s, o_s, st_t, *, dm, rows, chunk, group):
    n_chunks = rows // chunk
    gr = lax.broadcasted_iota(jnp.int32, (group, group), 0)
    gc = lax.broadcasted_iota(jnp.int32, (group, group), 1)
    mask = (gr // chunk == gc // chunk) & (gc <= gr)

    def per_chunk_row(b, row):
        return jnp.concatenate([jnp.broadcast_to(b[c * chunk + row:c * chunk + row + 1, :], (chunk, b.shape[1]))
                                for c in range(n_chunks)], axis=0)

    for h in range(GLA_HEADS):
        ks = slice(h * dm.dk, (h + 1) * dm.dk)
        vs = slice(h * dm.dv, (h + 1) * dm.dv)
        b = b_s[:, ks]
        b_ref = per_chunk_row(b, chunk // 2)
        b_last = per_chunk_row(b, chunk - 1)
        q = q_s[:, ks]
        k = k_s[:, ks]
        vb = v_s[:, vs].astype(BF16)
        qa = (q * jnp.exp(b - b_ref)).astype(BF16)
        ka = (k * jnp.exp(b_ref - b)).astype(BF16)
        qe = (q * jnp.exp(b)).astype(BF16)
        kd = (k * jnp.exp(b_last - b)).astype(BF16)
        for g0 in range(0, rows, group):
            a = lax.dot_general(qa[g0:g0 + group], ka[g0:g0 + group], NT_DIMS, preferred_element_type=F32)
            a = jnp.where(mask, a, 0.0)
            o_s[g0:g0 + group, vs] = _mm(a.astype(BF16), vb[g0:g0 + group])
        s_t = st_t[h]
        for c in range(n_chunks):
            r0 = c * chunk
            o_s[r0:r0 + chunk, vs] += lax.dot_general(qe[r0:r0 + chunk], s_t.astype(BF16), NT_DIMS,
                                                      preferred_element_type=F32)
            decay = jnp.exp(b[r0 + chunk - 1:r0 + chunk, :])
            s_t = s_t * decay + lax.dot_general(vb[r0:r0 + chunk], kd[r0:r0 + chunk], TN_DIMS,
                                                preferred_element_type=F32)
        st_t[h] = s_t


def _gla_seqs(q_s, k_s, v_s, gl_s, b_s, o_s, s0_ref, st_ref, *, dm, nseq, chunk, unroll):
    _chunk_cumsum(gl_s, b_s, nseq * chunk, chunk)
    ri = lax.broadcasted_iota(jnp.int32, (chunk, chunk), 0)
    ci = lax.broadcasted_iota(jnp.int32, (chunk, chunk), 1)
    causal = ci <= ri
    nt, tn = NT_DIMS, TN_DIMS

    def one_seq(seq):
        r0 = seq * chunk if isinstance(seq, int) else pl.multiple_of(seq * chunk, chunk)
        for h in range(GLA_HEADS):
            ks = slice(h * dm.dk, (h + 1) * dm.dk)
            vs = slice(h * dm.dv, (h + 1) * dm.dv)
            bc = b_s[pl.ds(r0, chunk), ks]
            b_ref = bc[chunk // 2:chunk // 2 + 1, :]
            b_last = bc[chunk - 1:chunk, :]
            qc = q_s[pl.ds(r0, chunk), ks]
            kc = k_s[pl.ds(r0, chunk), ks]
            vc = v_s[pl.ds(r0, chunk), vs].astype(BF16)
            s_old = s0_ref[seq, h]
            a = lax.dot_general((qc * jnp.exp(bc - b_ref)).astype(BF16), (kc * jnp.exp(b_ref - bc)).astype(BF16),
                                nt, preferred_element_type=F32)
            a = jnp.where(causal, a, 0.0)
            o = _mm(a.astype(BF16), vc) + _mm((qc * jnp.exp(bc)).astype(BF16), s_old.astype(BF16))
            o_s[pl.ds(r0, chunk), vs] = o
            e_col = jnp.broadcast_to(jnp.exp(b_last), (dm.dk, dm.dk)).T
            decay = jnp.concatenate([e_col] * (dm.dv // dm.dk), axis=1)
            kv = lax.dot_general((kc * jnp.exp(b_last - bc)).astype(BF16), vc, tn, preferred_element_type=F32)
            st_ref[seq, h] = decay * s_old + kv

    def step(g, carry):
        for u in range(unroll):
            one_seq(g * unroll + u)
        return carry

    if unroll == nseq:
        for seq in range(nseq):
            one_seq(seq)
    else:
        lax.fori_loop(0, nseq // unroll, step, 0)


def _mixer_tail(x, xb_s, o_s, ext_s, pos0, w, dm, *, nseq, seq_len):
    t = nseq * seq_len
    off_g = 2 * dm.key + dm.val
    off_u = off_g + dm.val
    off_ga = off_u + dm.pw
    off_gb = off_ga + dm.d
    xb = xb_s[...]

    u = _mm(xb, w.w_in[:, off_u:off_u + dm.pw])
    ext_s[:, POOL_WMAX:POOL_WMAX + seq_len, :] = u.reshape(nseq, seq_len, dm.pw)
    g = _mm(xb, w.w_in[:, off_g:off_g + dm.val])
    parts = []
    for h in range(GLA_HEADS):
        vs = slice(h * dm.dv, (h + 1) * dm.dv)
        oh = o_s[:, vs]
        ms = jnp.mean(oh * oh, axis=-1, keepdims=True)
        on = oh * lax.rsqrt(ms + RMS_EPS) * w.gla_norm[...]
        gh = g[:, vs]
        parts.append((on * (gh * jax.nn.sigmoid(gh))).astype(BF16))
    gate_a = jax.nn.sigmoid(_mm(xb, w.w_in[:, off_ga:off_ga + dm.d]) + w.b_gates[:, 0:dm.d])
    branch_a = _mm(jnp.concatenate(parts, axis=1), w.w_ba[...])

    p = lax.broadcasted_iota(jnp.int32, (nseq, seq_len, dm.pgc), 1)
    pooled = []
    for gi, win in enumerate(POOL_WINDOWS):
        cs = slice(gi * dm.pgc, (gi + 1) * dm.pgc)
        cur = ext_s[:, POOL_WMAX:POOL_WMAX + seq_len, cs]
        acc = cur
        for j in range(1, win):
            acc = acc + ext_s[:, POOL_WMAX - j:POOL_WMAX - j + seq_len, cs]
        cnt = jnp.minimum(win, p + (pos0 + 1)).astype(F32)
        pg = (acc / cnt - cur).reshape(t, dm.pgc)
        pg = _mm(pg.astype(BF16), w.w_pg[gi]) * w.pool_scale[:, cs]
        pooled.append(pg.astype(BF16))
    gate_b = jax.nn.sigmoid(_mm(xb, w.w_in[:, off_gb:off_gb + dm.d]) + w.b_gates[:, dm.d:2 * dm.d])
    branch_b = _mm(jnp.concatenate(pooled, axis=1), w.w_bp[...])
    merged = (gate_a * branch_a + gate_b * branch_b).astype(BF16)
    half = t // 2
    mix = [_mm(merged[p * half:(p + 1) * half], w.w_out[...]) for p in range(2)]
    h1, logits = [], []
    for p in range(2):
        h1.append(_layer_norm(dm.alpha * x[p * half:(p + 1) * half] + mix[p], w.ln_g[...], w.ln_b[...]))
        logits.append(_mm(h1[p].astype(BF16), w.w_rt[...]) + w.b_rt[...])
    return jnp.concatenate(h1, axis=0), jnp.concatenate(logits, axis=0).T[0:N_EXPERTS, :]


N_MIXW = len(MixW._fields)


def _prompt_mixer_kernel(*refs, dm, tile, n_tiles):
    x_ref = refs[0]
    w = MixW(*refs[1:1 + N_MIXW])
    h1_ref, lg_ref, st_ref, buf_ref, q_s, k_s, v_s, gl_s, b_s, o_s, ext_s, st_t, xb_s = refs[1 + N_MIXW:]
    lt = pl.program_id(1)

    @pl.when(lt == 0)
    def _():
        st_t[...] = jnp.zeros(st_t.shape, F32)
        ext_s[:, 0:POOL_WMAX, :] = jnp.zeros((1, POOL_WMAX, dm.pw), F32)

    x = x_ref[...]
    xb_s[...] = x.astype(BF16)
    _project_qkv(xb_s, w, dm, q_s, k_s, v_s, gl_s)
    _chunk_cumsum(gl_s, b_s, tile, GLA_CHUNK)
    _gla_tile(q_s, k_s, v_s, b_s, o_s, st_t, dm=dm, rows=tile, chunk=GLA_CHUNK, group=GLA_GROUP)

    @pl.when(lt == n_tiles - 1)
    def _():
        for h in range(GLA_HEADS):
            st_ref[0, h] = st_t[h].T

    h1, logits_t = _mixer_tail(x, xb_s, o_s, ext_s, lt * tile, w, dm, nseq=1, seq_len=tile)
    _to_row_tiles(h1_ref, h1, tile)
    lg_ref[...] = logits_t
    ext_s[:, 0:POOL_WMAX, :] = ext_s[:, tile:tile + POOL_WMAX, :]

    @pl.when(lt == n_tiles - 1)
    def _():
        buf_ref[...] = ext_s[:, 1:POOL_WMAX, :]


def _sample_proj_kernel(*refs, dm):
    x_ref = refs[0]
    w = MixW(*refs[1:1 + N_MIXW])
    q_ref, k_ref, v_ref, gl_ref, xb_s = refs[1 + N_MIXW:]
    xb_s[...] = x_ref[...].astype(BF16)
    _project_qkv(xb_s, w, dm, q_ref, k_ref, v_ref, gl_ref)


def _sample_gla_kernel(q_ref, k_ref, v_ref, gl_ref, s0_ref, o_ref, st_ref, b_s, *, dm, nseq, seq_len):
    _gla_seqs(q_ref, k_ref, v_ref, gl_ref, b_s, o_ref, s0_ref, st_ref, dm=dm, nseq=nseq, chunk=seq_len,
              unroll=GLA_SEQ_UNROLL)


def _sample_tail_kernel(*refs, dm, nseq, seq_len):
    x_ref, o_ref, hist_ref = refs[0:3]
    w = MixW(*refs[3:3 + N_MIXW])
    h1_ref, lg_ref, buf_ref, ext_s, xb_s = refs[3 + N_MIXW:]
    ext_s[:, 0:1, :] = jnp.zeros((nseq, 1, dm.pw), F32)
    ext_s[:, 1:POOL_WMAX, :] = hist_ref[...]
    x = x_ref[...]
    xb_s[...] = x.astype(BF16)
    h1, logits_t = _mixer_tail(x, xb_s, o_ref, ext_s, POOL_BUF, w, dm, nseq=nseq, seq_len=seq_len)
    _to_row_tiles(h1_ref, h1, nseq * seq_len)
    lg_ref[...] = logits_t
    buf_ref[...] = ext_s[:, seq_len + 1:seq_len + POOL_WMAX, :]


def _const_spec(arr):
    nd = arr.ndim
    return pl.BlockSpec(arr.shape, lambda *_: (0,) * nd, pipeline_mode=pl.Buffered(1))


def _prompt_mixer(x, w, dm):
    bsz, seq, d = x.shape
    tile = PROMPT_TILE
    n_tiles = seq // tile
    m = bsz * seq
    scratch = [pltpu.VMEM((tile, dm.key), F32), pltpu.VMEM((tile, dm.key), F32), pltpu.VMEM((tile, dm.val), F32),
               pltpu.VMEM((tile, dm.key), F32), pltpu.VMEM((tile, dm.key), F32), pltpu.VMEM((tile, dm.val), F32),
               pltpu.VMEM((1, POOL_WMAX + tile, dm.pw), F32),
               pltpu.VMEM((GLA_HEADS, dm.dv, dm.dk), F32),
               pltpu.VMEM((tile, d), BF16)]
    return pl.pallas_call(
        functools.partial(_prompt_mixer_kernel, dm=dm, tile=tile, n_tiles=n_tiles),
        grid=(bsz, n_tiles),
        in_specs=[pl.BlockSpec((tile, d), lambda b, t: (b * n_tiles + t, 0))] + [_const_spec(a) for a in w],
        out_specs=[pl.BlockSpec((tile * SUBLANES, LANES), lambda b, t: (b * n_tiles + t, 0)),
                   pl.BlockSpec((N_EXPERTS, tile), lambda b, t: (0, b * n_tiles + t)),
                   pl.BlockSpec((1, GLA_HEADS, dm.dk, dm.dv), lambda b, t: (b, 0, 0, 0)),
                   pl.BlockSpec((1, POOL_BUF, dm.pw), lambda b, t: (b, 0, 0))],
        out_shape=[jax.ShapeDtypeStruct((m * SUBLANES, LANES), F32),
                   jax.ShapeDtypeStruct((N_EXPERTS, m), F32),
                   jax.ShapeDtypeStruct((bsz, GLA_HEADS, dm.dk, dm.dv), F32),
                   jax.ShapeDtypeStruct((bsz, POOL_BUF, dm.pw), F32)],
        scratch_shapes=scratch,
        compiler_params=pltpu.CompilerParams(dimension_semantics=("arbitrary", "arbitrary"),
                                             vmem_limit_bytes=VMEM_LIMIT_BYTES),
        name="prompt_mixer",
    )(x.reshape(m, d), *w)


def _sample_mixer(x, s0, hist, w, dm):
    bsz, seq, d = x.shape
    m = bsz * seq
    x2 = x.reshape(m, d)
    tile = MIX_TILE
    cparams = pltpu.CompilerParams(dimension_semantics=("arbitrary",), vmem_limit_bytes=VMEM_LIMIT_BYTES)
    row = lambda n: pl.BlockSpec((tile, n), lambda i: (i, 0))
    q, k, v, gl = pl.pallas_call(
        functools.partial(_sample_proj_kernel, dm=dm),
        grid=(m // tile,),
        in_specs=[row(d)] + [_const_spec(a) for a in w],
        out_specs=[row(dm.key), row(dm.key), row(dm.val), row(dm.key)],
        out_shape=[jax.ShapeDtypeStruct((m, n), F32) for n in (dm.key, dm.key, dm.val, dm.key)],
        scratch_shapes=[pltpu.VMEM((tile, d), BF16)],
        compiler_params=cparams,
        name="sample_proj",
    )(x2, *w)

    nseq = GLA_SEQS
    rows = nseq * seq
    grow = lambda n: pl.BlockSpec((rows, n), lambda i: (i, 0))
    st_spec = pl.BlockSpec((nseq, GLA_HEADS, dm.dk, dm.dv), lambda i: (i, 0, 0, 0))
    o, st = pl.pallas_call(
        functools.partial(_sample_gla_kernel, dm=dm, nseq=nseq, seq_len=seq),
        grid=(bsz // nseq,),
        in_specs=[grow(dm.key), grow(dm.key), grow(dm.val), grow(dm.key), st_spec],
        out_specs=[grow(dm.val), st_spec],
        out_shape=[jax.ShapeDtypeStruct((m, dm.val), F32), jax.ShapeDtypeStruct(s0.shape, F32)],
        scratch_shapes=[pltpu.VMEM((rows, dm.key), F32)],
        compiler_params=cparams,
        name="sample_gla",
    )(q, k, v, gl, s0)

    tseq = tile // seq
    hist_spec = pl.BlockSpec((tseq, POOL_BUF, dm.pw), lambda i: (i, 0, 0))
    h1, lg, buf = pl.pallas_call(
        functools.partial(_sample_tail_kernel, dm=dm, nseq=tseq, seq_len=seq),
        grid=(m // tile,),
        in_specs=[row(d), row(dm.val), hist_spec] + [_const_spec(a) for a in w],
        out_specs=[pl.BlockSpec((tile * SUBLANES, LANES), lambda i: (i, 0)),
                   pl.BlockSpec((N_EXPERTS, tile), lambda i: (0, i)), hist_spec],
        out_shape=[jax.ShapeDtypeStruct((m * SUBLANES, LANES), F32),
                   jax.ShapeDtypeStruct((N_EXPERTS, m), F32),
                   jax.ShapeDtypeStruct((bsz, POOL_BUF, dm.pw), F32)],
        scratch_shapes=[pltpu.VMEM((tseq, POOL_WMAX + seq, dm.pw), F32), pltpu.VMEM((tile, d), BF16)],
        compiler_params=cparams,
        name="sample_tail",
    )(x2, o, hist, *w)
    return h1, lg, st, buf


def _router_kernel(lga_ref, lgb_ref, w_ref, s_ref, meta_ref, c_ref, carry, *, tile, n_tiles_a):
    i = pl.program_id(0)

    @pl.when(i == 0)
    def _():
        carry[...] = jnp.zeros(carry.shape, F32)

    eio = lax.broadcasted_iota(jnp.int32, (N_EXPERTS, tile), 0)
    work = jnp.where(i < n_tiles_a, lga_ref[...], lgb_ref[...])
    vals, hots = [], []
    for k in range(TOP_K):
        mx = jnp.max(work, axis=0, keepdims=True)
        idx = jnp.min(jnp.where(work == mx, eio, N_EXPERTS), axis=0, keepdims=True)
        hot = eio == idx
        vals.append(mx)
        hots.append(hot)
        work = jnp.where(hot, -jnp.inf, work)
    ex = [jnp.exp(v - vals[0]) for v in vals]
    den = ex[0]
    for k in range(1, TOP_K):
        den = den + ex[k]
    for k in range(TOP_K):
        w_ref[k:k + 1, :] = ex[k] / den
    sel = hots[0]
    for k in range(1, TOP_K):
        sel = sel | hots[k]
    r = lax.broadcasted_iota(jnp.int32, (tile, tile), 0)
    c = lax.broadcasted_iota(jnp.int32, (tile, tile), 1)
    earlier_token = (r < c).astype(BF16)
    rank_in_tile = _mm(sel.astype(BF16), earlier_token)
    cnt = jnp.broadcast_to(jnp.sum(sel.astype(F32), axis=1, keepdims=True), (N_EXPERTS, LANES))
    er = lax.broadcasted_iota(jnp.int32, (N_EXPERTS, N_EXPERTS), 0)
    ec = lax.broadcasted_iota(jnp.int32, (N_EXPERTS, N_EXPERTS), 1)
    lower = (ec < er).astype(BF16)
    cnt_hi = jnp.floor(cnt * (1.0 / 256.0))
    first_slot = _mm(lower, (cnt - 256.0 * cnt_hi).astype(BF16)) + 256.0 * _mm(lower, cnt_hi.astype(BF16))
    slot = rank_in_tile + first_slot[:, 0:1]
    for k in range(TOP_K):
        s_ref[0, k:k + 1, :] = jnp.sum(jnp.where(hots[k], slot, 0.0), axis=0, keepdims=True).astype(jnp.int32)
    meta_ref[0, 0:N_EXPERTS, :] = cnt
    meta_ref[0, N_EXPERTS:2 * N_EXPERTS, :] = first_slot
    meta_ref[0, 2 * N_EXPERTS:3 * N_EXPERTS, :] = carry[...]
    carry[...] = carry[...] + cnt
    c_ref[...] = carry[...]


def _for_each_run(meta, fn):
    def body(e, carry):
        cnt = meta[e]
        first = meta[N_EXPERTS + e]
        dst = meta[2 * N_EXPERTS + e]
        for bit in range(RUN_BITS):
            @pl.when(((cnt >> bit) & 1) == 1)
            def _(bit=bit):
                done = cnt & ((1 << bit) - 1)
                fn(first + done, dst + done, 1 << bit)
        return carry
    lax.fori_loop(0, N_EXPERTS, body, 0)


def _rows(ref, first, n):
    return ref.at[pl.ds(pl.multiple_of(first * SUBLANES, SUBLANES), n * SUBLANES), :]


def _dispatch_kernel(fill_start_ref, fill_cnt_ref, tail_start_ref, meta_hbm, slot_hbm, h1a_ref, h1b_ref, xs_hbm,
                     meta_s0, meta_s1, slot_s0, slot_s1, stage, zeros_s, sem_m, sem_s, sem_r, sem_z,
                     *, tile, n_tiles, n_tiles_a):
    i = pl.program_id(0)
    sl = i % 2
    meta_s = (meta_s0, meta_s1)
    slot_s = (slot_s0, slot_s1)

    def index_copies(blk, s):
        return (pltpu.make_async_copy(meta_hbm.at[blk], meta_s[s], sem_m.at[s]),
                pltpu.make_async_copy(slot_hbm.at[blk], slot_s[s], sem_s.at[s]))

    def wait_stage(s):
        pltpu.make_async_copy(stage.at[s], stage.at[s], sem_r.at[s]).wait()

    @pl.when(i == 0)
    def _():
        for cp in index_copies(0, 0):
            cp.start()
        zeros_s[...] = jnp.zeros(zeros_s.shape, F32)

        def fill_copy(off, bit):
            n = 1 << bit
            return pltpu.make_async_copy(zeros_s.at[pl.ds(0, n * SUBLANES), :], _rows(xs_hbm, off, n), sem_z)

        def for_each_piece(fn):
            def body(e, carry):
                cnt = fill_cnt_ref[e]
                for bit in range(ZERO_FILL_BITS):
                    @pl.when(((cnt >> bit) & 1) == 1)
                    def _(bit=bit):
                        fn(fill_copy(fill_start_ref[e] + (cnt & ((1 << bit) - 1)), bit))
                return carry
            lax.fori_loop(0, N_EXPERTS, body, 0)

        top = ZERO_FILL_BITS - 1
        n_tail = (xs_hbm.shape[0] // SUBLANES - tail_start_ref[0]) >> top

        def for_each_tail_piece(fn):
            def body(j, carry):
                fn(fill_copy(tail_start_ref[0] + (j << top), top))
                return carry
            lax.fori_loop(0, n_tail, body, 0)

        for_each_piece(lambda cp: cp.start())
        for_each_tail_piece(lambda cp: cp.start())
        for_each_piece(lambda cp: cp.wait())
        for_each_tail_piece(lambda cp: cp.wait())

    def group_rows(h1_ref, s_):
        def group(g, carry):
            for u in range(DMA_UNROLL):
                t = g * DMA_UNROLL + u
                row = h1_ref[pl.ds(pl.multiple_of(t * SUBLANES, SUBLANES), SUBLANES), :]
                for k in range(TOP_K):
                    s = slot_s[s_][k * tile + t]
                    stage[s_, pl.ds(pl.multiple_of(s * SUBLANES, SUBLANES), SUBLANES), :] = row
            return carry
        lax.fori_loop(0, tile // DMA_UNROLL, group, 0)

    for s_ in range(2):
        @pl.when((sl == 1 - s_) & (i + 1 < n_tiles))
        def _(s_=s_):
            for cp in index_copies(i + 1, s_):
                cp.start()

    for s_ in range(2):
        @pl.when(sl == s_)
        def _(s_=s_):
            for cp in index_copies(i, s_):
                cp.wait()

            @pl.when(i >= 2)
            def _():
                wait_stage(s_)

            @pl.when(i < n_tiles_a)
            def _():
                group_rows(h1a_ref, s_)

            @pl.when(i >= n_tiles_a)
            def _():
                group_rows(h1b_ref, s_)

            _for_each_run(meta_s[s_], lambda s0, d0, n: pltpu.make_async_copy(
                _rows(stage.at[s_], s0, n), _rows(xs_hbm, d0, n), sem_r.at[s_]).start())

    @pl.when(i == n_tiles - 1)
    def _():
        @pl.when(n_tiles > 1)
        def _():
            wait_stage(1 - sl)
        wait_stage(sl)


def _expert_kernel(be_ref, nused_ref, next_e_ref, valid_ref, xs_ref, wgu_hbm, bgu_ref, wdn_hbm, bdn_ref, out_ref,
                   wgu_f, wdn_f, wgu_b, wdn_b, sem_w, *, rows, d_ff):
    i = pl.program_id(0)

    def weight_copies(e):
        return (pltpu.make_async_copy(wgu_hbm.at[e], wgu_f, sem_w.at[0]),
                pltpu.make_async_copy(wdn_hbm.at[e], wdn_f, sem_w.at[1]))

    @pl.when(i == 0)
    def _():
        for cp in weight_copies(be_ref[0]):
            cp.start()

    @pl.when((i < nused_ref[0]) & ((i == 0) | (be_ref[i] != be_ref[jnp.maximum(i - 1, 0)])))
    def _():
        for cp in weight_copies(be_ref[i]):
            cp.wait()
        wgu_b[...] = wgu_f[...].astype(BF16)
        wdn_b[...] = wdn_f[...].astype(BF16)

        @pl.when(next_e_ref[i] >= 0)
        def _():
            for cp in weight_copies(next_e_ref[i]):
                cp.start()

    def ffn(n):
        head = lambda ref: ref.at[pl.ds(0, n * SUBLANES), :]
        xb = _from_row_tiles(head(xs_ref), n).astype(BF16)
        width = d_ff // FFN_SPLIT
        cols = [slice(p * width, (p + 1) * width) for p in range(FFN_SPLIT)]
        ups = [slice(d_ff + p * width, d_ff + (p + 1) * width) for p in range(FFN_SPLIT)]
        pre = [(_mm(xb, wgu_b[:, cols[p]]) + bgu_ref[0, :, cols[p]], _mm(xb, wgu_b[:, ups[p]]) + bgu_ref[0, :, ups[p]])
               for p in range(FFN_SPLIT)]
        res = bdn_ref[0]
        for p in range(FFN_SPLIT):
            gate = jnp.minimum(pre[p][0], SWIGLU_LIMIT)
            up = jnp.clip(pre[p][1], -SWIGLU_LIMIT, SWIGLU_LIMIT)
            act = (up + 1.0) * (gate * jax.nn.sigmoid(GLU_ALPHA * gate))
            res = res + _mm(act.astype(BF16), wdn_b[cols[p], :])
        _to_row_tiles(head(out_ref), res, n)

    half = rows // 2

    @pl.when((i < nused_ref[0]) & (valid_ref[i] > half))
    def _():
        ffn(rows)

    @pl.when((i < nused_ref[0]) & (valid_ref[i] <= half))
    def _():
        ffn(half)
        out_ref[pl.ds(half * SUBLANES, half * SUBLANES), :] = jnp.zeros((half * SUBLANES, LANES), F32)

    @pl.when(i >= nused_ref[0])
    def _():
        out_ref[...] = jnp.zeros(out_ref.shape, F32)


def _combine_kernel(meta_hbm, slot_hbm, res_hbm, h1a_ref, h1b_ref, gw_ref, g_ref, b_ref, ya_ref, yb_ref,
                    meta_s0, meta_s1, slot_s0, slot_s1, stage, gbuf, sem_m, sem_s, sem_g,
                    *, tile, alpha, n_tiles, n_tiles_a):
    i = pl.program_id(0)
    sl = i % 2
    meta_s = (meta_s0, meta_s1)
    slot_s = (slot_s0, slot_s1)

    def index_copies(blk, s):
        return (pltpu.make_async_copy(meta_hbm.at[blk], meta_s[s], sem_m.at[s]),
                pltpu.make_async_copy(slot_hbm.at[blk], slot_s[s], sem_s.at[s]))

    def fetch_runs(s):
        _for_each_run(meta_s[s], lambda s0, d0, n: pltpu.make_async_copy(
            _rows(res_hbm, d0, n), _rows(stage.at[s], s0, n), sem_g.at[s]).start())

    @pl.when(i == 0)
    def _():
        for cp in index_copies(0, 0):
            cp.start()
        for cp in index_copies(0, 0):
            cp.wait()
        fetch_runs(0)

        @pl.when(n_tiles > 1)
        def _():
            for cp in index_copies(1, 1):
                cp.start()

    def regroup(s_):
        def group(g, carry):
            for u in range(DMA_UNROLL):
                t = g * DMA_UNROLL + u
                for k in range(TOP_K):
                    s = slot_s[s_][k * tile + t]
                    gbuf[k, pl.ds(pl.multiple_of(t * SUBLANES, SUBLANES), SUBLANES), :] = (
                        stage[s_, pl.ds(pl.multiple_of(s * SUBLANES, SUBLANES), SUBLANES), :])
            return carry
        lax.fori_loop(0, tile // DMA_UNROLL, group, 0)

    for s_ in range(2):
        @pl.when((sl == 1 - s_) & (i + 1 < n_tiles))
        def _(s_=s_):
            for cp in index_copies(i + 1, s_):
                cp.wait()
            fetch_runs(s_)

    for s_ in range(2):
        @pl.when(sl == s_)
        def _(s_=s_):
            pltpu.make_async_copy(stage.at[s_], stage.at[s_], sem_g.at[s_]).wait()
            regroup(s_)

    for s_ in range(2):
        @pl.when((sl == s_) & (i + 2 < n_tiles))
        def _(s_=s_):
            for cp in index_copies(i + 2, s_):
                cp.start()

    gw = gw_ref[...]
    z = alpha * jnp.where(i < n_tiles_a, _from_row_tiles(h1a_ref, tile), _from_row_tiles(h1b_ref, tile))
    for k in range(TOP_K):
        z = z + gw[:, k:k + 1] * _from_row_tiles(gbuf.at[k], tile)
    y = _layer_norm(z, g_ref[...], b_ref[...])

    @pl.when(i < n_tiles_a)
    def _():
        ya_ref[...] = y

    @pl.when(i >= n_tiles_a)
    def _():
        yb_ref[...] = y


def _moe(h1_a, h1_b, logits_a, logits_b, w_gu, b_gu, w_down, b_down, ln_g, ln_b, dm):
    m_a = logits_a.shape[1]
    m = m_a + logits_b.shape[1]
    rows = EXPERT_ROWS
    d_ff = w_down.shape[1]
    tile = MOE_TILE
    n_tiles = m // tile
    n_tiles_a = m_a // tile
    cparams = pltpu.CompilerParams(dimension_semantics=("arbitrary",), vmem_limit_bytes=VMEM_LIMIT_BYTES)

    def two_group_specs(block, lane_axis):
        pick = (lambda j: (0, j)) if lane_axis else (lambda j: (j, 0))
        return [pl.BlockSpec(block, lambda i, *_: pick(jnp.minimum(i, n_tiles_a - 1))),
                pl.BlockSpec(block, lambda i, *_: pick(jnp.maximum(i - n_tiles_a, 0)))]

    kblk = lambda: pl.BlockSpec((TOP_K, tile), lambda i: (0, i))
    gate_w, slot, meta, counts = pl.pallas_call(
        functools.partial(_router_kernel, tile=tile, n_tiles_a=n_tiles_a),
        grid=(n_tiles,),
        in_specs=two_group_specs((N_EXPERTS, tile), True),
        out_specs=[kblk(), pl.BlockSpec((1, TOP_K, tile), lambda i: (i, 0, 0)),
                   pl.BlockSpec((1, 3 * N_EXPERTS, LANES), lambda i: (i, 0, 0)),
                   pl.BlockSpec((N_EXPERTS, LANES), lambda i: (0, 0))],
        out_shape=[jax.ShapeDtypeStruct((TOP_K, m), F32), jax.ShapeDtypeStruct((n_tiles, TOP_K, tile), jnp.int32),
                   jax.ShapeDtypeStruct((n_tiles, 3 * N_EXPERTS, LANES), F32),
                   jax.ShapeDtypeStruct((N_EXPERTS, LANES), F32)],
        scratch_shapes=[pltpu.VMEM((N_EXPERTS, LANES), F32)],
        compiler_params=cparams,
        name="moe_router",
    )(logits_a, logits_b)

    counts = counts[:, 0].astype(jnp.int32)
    padded = (counts + rows - 1) // rows * rows
    pad_end = jnp.cumsum(padded)
    pad_start = pad_end - padded
    n_blocks = -(-(m * TOP_K + N_EXPERTS * (rows - 1)) // rows)
    n_rows = n_blocks * rows
    block_row0 = jnp.arange(n_blocks, dtype=jnp.int32) * rows
    block_e = jnp.minimum(jnp.sum((pad_end[None, :] <= block_row0[:, None]).astype(jnp.int32), axis=1), N_EXPERTS - 1)
    n_used = (pad_end[-1] // rows).astype(jnp.int32).reshape(1)
    meta = meta[:, :, 0].astype(jnp.int32)
    tile_meta = jnp.concatenate([meta[:, 0:2 * N_EXPERTS], meta[:, 2 * N_EXPERTS:] + pad_start[None, :],
                                 jnp.zeros((n_tiles, LANES - 3 * N_EXPERTS), jnp.int32)], axis=1)
    slot_t = slot.reshape(n_tiles, TOP_K * tile)

    h1_specs = two_group_specs((tile * SUBLANES, LANES), False)
    any_spec = pl.BlockSpec(memory_space=pl.ANY)
    index_scratch = [pltpu.SMEM((LANES,), jnp.int32), pltpu.SMEM((LANES,), jnp.int32),
                     pltpu.SMEM((TOP_K * tile,), jnp.int32), pltpu.SMEM((TOP_K * tile,), jnp.int32),
                     pltpu.VMEM((2, TOP_K * tile * SUBLANES, LANES), F32)]
    xs = pl.pallas_call(
        functools.partial(_dispatch_kernel, tile=tile, n_tiles=n_tiles, n_tiles_a=n_tiles_a),
        grid_spec=pltpu.PrefetchScalarGridSpec(
            num_scalar_prefetch=3,
            grid=(n_tiles,),
            in_specs=[any_spec, any_spec] + h1_specs,
            out_specs=any_spec,
            scratch_shapes=index_scratch + [
                pltpu.VMEM(((1 << (ZERO_FILL_BITS - 1)) * SUBLANES, LANES), F32),
                pltpu.SemaphoreType.DMA((2,)), pltpu.SemaphoreType.DMA((2,)), pltpu.SemaphoreType.DMA((2,)),
                pltpu.SemaphoreType.DMA]),
        out_shape=jax.ShapeDtypeStruct((n_rows * SUBLANES, LANES), F32),
        compiler_params=cparams,
        name="moe_dispatch",
    )(pad_start + counts, padded - counts, pad_end[-1:], tile_meta, slot_t, h1_a, h1_b)

    blk = jnp.arange(n_blocks, dtype=jnp.int32)
    run_start = (blk < n_used[0]) & ((blk == 0) | (block_e != jnp.roll(block_e, 1)))
    next_start = jnp.min(jnp.where((blk[None, :] > blk[:, None]) & run_start[None, :], blk[None, :], n_blocks), axis=1)
    next_e = jnp.where(next_start < n_blocks, block_e[jnp.minimum(next_start, n_blocks - 1)], -1).astype(jnp.int32)

    of_block = block_e[:, None] == jnp.arange(N_EXPERTS, dtype=jnp.int32)[None, :]
    region_end = jnp.sum(jnp.where(of_block, (pad_start + counts)[None, :], 0), axis=1)
    block_valid = jnp.clip(region_end - block_row0, 0, rows).astype(jnp.int32)

    last = lambda i, nu: jnp.minimum(i, nu[0] - 1)
    res = pl.pallas_call(
        functools.partial(_expert_kernel, rows=rows, d_ff=d_ff),
        grid_spec=pltpu.PrefetchScalarGridSpec(
            num_scalar_prefetch=4,
            grid=(n_blocks,),
            in_specs=[pl.BlockSpec((rows * SUBLANES, LANES), lambda i, be, nu, ne, bv: (last(i, nu), 0)),
                      any_spec,
                      pl.BlockSpec((1, 1, 2 * d_ff), lambda i, be, nu, ne, bv: (be[i], 0, 0)),
                      any_spec,
                      pl.BlockSpec((1, 1, dm.d), lambda i, be, nu, ne, bv: (be[i], 0, 0))],
            out_specs=pl.BlockSpec((rows * SUBLANES, LANES), lambda i, be, nu, ne, bv: (i, 0)),
            scratch_shapes=[pltpu.VMEM((dm.d, 2 * d_ff), F32), pltpu.VMEM((d_ff, dm.d), F32),
                            pltpu.VMEM((dm.d, 2 * d_ff), BF16), pltpu.VMEM((d_ff, dm.d), BF16),
                            pltpu.SemaphoreType.DMA((2,))]),
        out_shape=jax.ShapeDtypeStruct((n_rows * SUBLANES, LANES), F32),
        compiler_params=cparams,
        name="moe_experts",
    )(block_e, n_used, next_e, block_valid, xs, w_gu, b_gu[:, None, :], w_down, b_down[:, None, :])

    ya, yb = pl.pallas_call(
        functools.partial(_combine_kernel, tile=tile, alpha=dm.alpha, n_tiles=n_tiles, n_tiles_a=n_tiles_a),
        grid=(n_tiles,),
        in_specs=[any_spec, any_spec, any_spec] + h1_specs + [
                  pl.BlockSpec((tile, TOP_K), lambda i: (i, 0)),
                  pl.BlockSpec((1, dm.d), lambda i: (0, 0)),
                  pl.BlockSpec((1, dm.d), lambda i: (0, 0))],
        out_specs=[pl.BlockSpec((tile, dm.d), lambda i: (jnp.minimum(i, n_tiles_a - 1), 0)),
                   pl.BlockSpec((tile, dm.d), lambda i: (jnp.maximum(i - n_tiles_a, 0), 0))],
        out_shape=[jax.ShapeDtypeStruct((m_a, dm.d), F32), jax.ShapeDtypeStruct((m - m_a, dm.d), F32)],
        scratch_shapes=index_scratch + [
            pltpu.VMEM((TOP_K, tile * SUBLANES, LANES), F32),
            pltpu.SemaphoreType.DMA((2,)), pltpu.SemaphoreType.DMA((2,)), pltpu.SemaphoreType.DMA((2,))],
        compiler_params=cparams,
        name="moe_combine",
    )(tile_meta, slot_t, res, h1_a, h1_b, gate_w.T, ln_g[None, :], ln_b[None, :])
    return ya, yb


def _pad_cols(a, n):
    return jnp.pad(a, ((0, 0), (0, n - a.shape[1])))


def _regroup_kernel(w_ref, o_ref, *, o_lr, o_tail):
    n_in = w_ref.shape[1]
    n_main = n_in - (o_tail - o_lr)
    o_ref[:, 0:o_lr] = w_ref[:, 0:o_lr].astype(BF16)
    o_ref[:, o_lr:n_main] = w_ref[:, o_tail:n_in].astype(BF16)
    lr = w_ref[:, o_lr:o_tail].astype(BF16)
    o_ref[:, n_main:n_main + LANES] = jnp.concatenate(
        [lr, jnp.zeros((lr.shape[0], LANES - (o_tail - o_lr)), BF16)], axis=1)


def _regroup_w_in(w_in_all, layer, o_lr, o_tail):
    _, d, n_in = w_in_all.shape
    n_out = n_in - (o_tail - o_lr) + LANES
    rows = REGROUP_ROWS
    return pl.pallas_call(
        functools.partial(_regroup_kernel, o_lr=o_lr, o_tail=o_tail),
        grid=(d // rows,),
        in_specs=[pl.BlockSpec((None, rows, n_in), lambda i: (layer, i, 0))],
        out_specs=pl.BlockSpec((rows, n_out), lambda i: (i, 0)),
        out_shape=jax.ShapeDtypeStruct((d, n_out), BF16),
        compiler_params=pltpu.CompilerParams(dimension_semantics=("arbitrary",), vmem_limit_bytes=VMEM_LIMIT_BYTES),
        name="regroup_w_in",
    )(w_in_all)


def _mixer_weights(w_in_all, layer, w_gk2, b_gk, gla_norm_w, w_branch_gla, w_pool_grp, pool_scale, w_branch_pool,
                   b_gates, w_out, ln_g, ln_b, w_router, b_router, dm):
    o_lr = 2 * dm.key + dm.val
    o_tail = o_lr + GATE_RANK
    return MixW(
        w_in=_regroup_w_in(w_in_all, layer, o_lr, o_tail),
        w_gk2=jnp.pad(w_gk2, ((0, LANES - GATE_RANK), (0, 0))).astype(BF16),
        b_gk=b_gk[None, :],
        gla_norm=gla_norm_w[None, :],
        w_ba=w_branch_gla.astype(BF16),
        w_pg=w_pool_grp.astype(BF16),
        pool_scale=pool_scale[None, :],
        w_bp=w_branch_pool.astype(BF16),
        b_gates=b_gates[None, :],
        w_out=w_out.astype(BF16),
        ln_g=ln_g[None, :],
        ln_b=ln_b[None, :],
        w_rt=_pad_cols(w_router, LANES).astype(BF16),
        b_rt=_pad_cols(b_router[None, :], LANES),
    )


def kernel(x_prompt, x_sample, state_gla, state_pool, w_in, w_gk2, b_gk, gla_norm_w, w_branch_gla, w_pool_grp,
           pool_scale, w_branch_pool, b_gates, w_out, ln1_g, ln1_b, w_router, b_router, w_gu, b_gu, w_down, b_down,
           ln2_g, ln2_b):
    depth = w_in.shape[0]
    bp, lp, d = x_prompt.shape
    bs, ls, _ = x_sample.shape
    assert d == SUBLANES * LANES and lp % PROMPT_TILE == 0 and (bs * ls) % MIX_TILE == 0 and MIX_TILE % ls == 0
    assert bs % GLA_SEQS == 0 and ls % SUBLANES == 0
    assert (bp * lp) % MOE_TILE == 0 and (bs * ls) % MOE_TILE == 0
    dm = _dims(d, depth)
    yp, ys = x_prompt, x_sample
    gla_p, pool_p, gla_s, pool_s = [], [], [], []
    for l in range(depth):
        w = _mixer_weights(w_in, l, w_gk2[l], b_gk[l], gla_norm_w[l], w_branch_gla[l], w_pool_grp[l], pool_scale[l],
                           w_branch_pool[l], b_gates[l], w_out[l], ln1_g[l], ln1_b[l], w_router[l], b_router[l], dm)
        h1p, lgp, sp, bufp = _prompt_mixer(yp, w, dm)
        h1s, lgs, ss, bufs = _sample_mixer(ys, state_gla[l], state_pool[l], w, dm)
        yp, ys = _moe(h1p, h1s, lgp, lgs, w_gu[l], b_gu[l], w_down[l], b_down[l], ln2_g[l], ln2_b[l], dm)
        yp = yp.reshape(bp, lp, d)
        ys = ys.reshape(bs, ls, d)
        gla_p.append(sp.astype(state_gla.dtype))
        pool_p.append(bufp.astype(state_pool.dtype))
        gla_s.append(ss.astype(state_gla.dtype))
        pool_s.append(bufs.astype(state_pool.dtype))
    stack = lambda xs: xs[0][None] if len(xs) == 1 else jnp.stack(xs, 0)
    return (yp, ys, stack(gla_p), stack(pool_p), stack(gla_s), stack(pool_s))
```
